```python
import jax, jax.numpy as jnp
from jax import lax
import numpy as np

D_MODEL = 1024
BATCH = 8
SEQ = 4096
DEPTH = 2

N_META = 16
N_A_LAYERS = DEPTH // 2
N_B_LAYERS = DEPTH - N_A_LAYERS
POOL_WINDOWS = (2, 4, 8, 16)
N_POOL_GROUPS = len(POOL_WINDOWS)
POOL_GROUP_DIM = D_MODEL // N_POOL_GROUPS
N_HEADS = 16
HEAD_DIM = D_MODEL // N_HEADS
Q_BLOCK = 128
D_FF = ((8 * D_MODEL // 3 + 127) // 128) * 128
CONV_WIDTH = 3
RMS_EPS = 1e-6

kernel_name = "yoco_pool_stickbreak_convffn"


def rms_norm(x, g):
    xf = x.astype(jnp.float32)
    y = xf * lax.rsqrt(jnp.mean(xf * xf, axis=-1, keepdims=True) + RMS_EPS)
    return (y * g.astype(jnp.float32)).astype(x.dtype)


def multiscale_pool(h, w_groups, scale):
    b, l, d = h.shape
    hf = h.astype(jnp.float32)
    csum = jnp.concatenate([jnp.zeros((b, 1, d), jnp.float32), jnp.cumsum(hf, axis=1)], axis=1)
    hg = hf.reshape(b, l, N_POOL_GROUPS, POOL_GROUP_DIM)
    cg = csum.reshape(b, l + 1, N_POOL_GROUPS, POOL_GROUP_DIM)
    t = jnp.arange(l)
    diffs = []
    for g, w in enumerate(POOL_WINDOWS):
        lo = jnp.maximum(t + 1 - w, 0)
        count = (t + 1 - lo).astype(jnp.float32)
        cgg = cg[:, :, g]
        window_sum = cgg[:, 1:] - cgg[:, lo]
        diffs.append(window_sum / count[None, :, None] - hg[:, :, g])
    diff = jnp.stack(diffs, axis=2).astype(h.dtype)
    y = jnp.einsum('blgc,gcd->blgd', diff, w_groups).reshape(b, l, d)
    return y * scale


def causal_dwconv(u, w, bias):
    l = u.shape[1]
    up = jnp.pad(u, ((0, 0), (CONV_WIDTH - 1, 0), (0, 0)))
    out = bias + w[0] * up[:, 0:l]
    for k in range(1, CONV_WIDTH):
        out = out + w[k] * up[:, k:k + l]
    return out


def conv_ffn(h, w_up, conv_w, conv_b, w_down):
    u = causal_dwconv(h @ w_up, conv_w, conv_b)
    gate, val = jnp.split(u, 2, axis=-1)
    return (jax.nn.silu(gate) * val) @ w_down


def shared_kv(h, kv_norm, w_kv):
    b, l, _ = h.shape
    kv = rms_norm(h, kv_norm) @ w_kv
    k, v = jnp.split(kv, 2, axis=-1)
    k = k.reshape(b, l, N_HEADS, HEAD_DIM).transpose(0, 2, 1, 3)
    v = v.reshape(b, l, N_HEADS, HEAD_DIM).transpose(0, 2, 1, 3)
    return k, v


def stick_breaking_block(q_blk, pos_q, k, v):
    z = jnp.einsum('bhqd,bhsd->bhqs', q_blk, k).astype(jnp.float32) * (HEAD_DIM ** -0.5)
    pos_k = jnp.arange(k.shape[2])
    mask = pos_k[None, :] < pos_q[:, None]
    log_beta = jax.nn.log_sigmoid(z)
    log_1m_beta = jnp.where(mask, jax.nn.log_sigmoid(-z), 0.0)
    later = lax.cumsum(log_1m_beta, axis=3, reverse=True) - log_1m_beta
    a = jnp.where(mask, jnp.exp(log_beta + later), 0.0)
    return jnp.einsum('bhqs,bhsd->bhqd', a.astype(v.dtype), v)


def stick_breaking_attention(h, w_q, k, v, w_o):
    b, l, d = h.shape
    n_real = l - N_META
    n_blk = n_real // Q_BLOCK
    q = (h @ w_q).reshape(b, l, N_HEADS, HEAD_DIM).transpose(0, 2, 1, 3)
    o_meta = stick_breaking_block(q[:, :, :N_META], jnp.arange(N_META),
                                  k[:, :, :N_META], v[:, :, :N_META])
    q_real = q[:, :, N_META:].reshape(b, N_HEADS, n_blk, Q_BLOCK, HEAD_DIM).transpose(2, 0, 1, 3, 4)
    pos_real = (N_META + jnp.arange(n_real)).reshape(n_blk, Q_BLOCK)
    o_real = lax.map(lambda args: stick_breaking_block(args[0], args[1], k, v), (q_real, pos_real))
    o_real = o_real.transpose(1, 2, 0, 3, 4).reshape(b, N_HEADS, n_real, HEAD_DIM)
    o = jnp.concatenate([o_meta, o_real], axis=2).transpose(0, 2, 1, 3).reshape(b, l, d)
    return o @ w_o


def _fwd_setup_inputs(seed: int = 0) -> dict:
    key = jax.random.key(seed)
    ks = jax.random.split(key, 16)
    f32 = jnp.float32
    nrm = lambda k, shape, s: jax.random.normal(k, shape, f32) * s
    return {
        "x": nrm(ks[0], (BATCH, SEQ, D_MODEL), 1.0),
        "meta_tokens": nrm(ks[1], (N_META, D_MODEL), 1.0),
        "mix_norm": 1.0 + nrm(ks[2], (DEPTH, D_MODEL), 0.05),
        "ffn_norm": 1.0 + nrm(ks[3], (DEPTH, D_MODEL), 0.05),
        "pool_w": nrm(ks[4], (N_A_LAYERS, N_POOL_GROUPS, POOL_GROUP_DIM, POOL_GROUP_DIM), POOL_GROUP_DIM ** -0.5),
        "pool_scale": 1.0 + nrm(ks[5], (N_A_LAYERS, D_MODEL), 0.1),
        "kv_norm": 1.0 + nrm(ks[6], (D_MODEL,), 0.05),
        "w_kv": nrm(ks[7], (D_MODEL, 2 * D_MODEL), D_MODEL ** -0.5),
        "w_q": nrm(ks[8], (N_B_LAYERS, D_MODEL, D_MODEL), D_MODEL ** -0.5),
        "w_o": nrm(ks[9], (N_B_LAYERS, D_MODEL, D_MODEL), D_MODEL ** -0.5),
        "ffn_w_up": nrm(ks[10], (DEPTH, D_MODEL, 2 * D_FF), D_MODEL ** -0.5),
        "ffn_conv_w": nrm(ks[11], (DEPTH, CONV_WIDTH, 2 * D_FF), CONV_WIDTH ** -0.5),
        "ffn_conv_b": nrm(ks[12], (DEPTH, 2 * D_FF), 0.01),
        "ffn_w_down": nrm(ks[13], (DEPTH, D_FF, D_MODEL), D_FF ** -0.5),
        "final_norm": 1.0 + nrm(ks[14], (D_MODEL,), 0.05),
    }


def _fwd_reference(x, meta_tokens, mix_norm, ffn_norm, pool_w, pool_scale, kv_norm, w_kv,
              w_q, w_o, ffn_w_up, ffn_conv_w, ffn_conv_b, ffn_w_down, final_norm):
    b = x.shape[0]
    meta = jnp.broadcast_to(meta_tokens[None].astype(x.dtype), (b, N_META, D_MODEL))
    h = jnp.concatenate([meta, x], axis=1)
    k = v = None
    for layer in range(DEPTH):
        if layer < N_A_LAYERS:
            h = h + multiscale_pool(rms_norm(h, mix_norm[layer]), pool_w[layer], pool_scale[layer])
        else:
            if layer == N_A_LAYERS:
                k, v = shared_kv(h, kv_norm, w_kv)
            j = layer - N_A_LAYERS
            h = h + stick_breaking_attention(rms_norm(h, mix_norm[layer]), w_q[j], k, v, w_o[j])
        h = h + conv_ffn(rms_norm(h, ffn_norm[layer]), ffn_w_up[layer], ffn_conv_w[layer],
                         ffn_conv_b[layer], ffn_w_down[layer])
    return rms_norm(h, final_norm)[:, N_META:]


import jax as _jax
import jax.numpy as _jnp

TWIN_FORMAT = 'train_step'
FWD_PARAMS = ['x', 'meta_tokens', 'mix_norm', 'ffn_norm', 'pool_w', 'pool_scale', 'kv_norm', 'w_kv', 'w_q', 'w_o', 'ffn_w_up', 'ffn_conv_w', 'ffn_conv_b', 'ffn_w_down', 'final_norm']
TWIN_WEIGHTS = ['meta_tokens', 'mix_norm', 'ffn_norm', 'pool_w', 'pool_scale', 'kv_norm', 'w_kv', 'w_q', 'w_o', 'ffn_w_up', 'ffn_conv_w', 'ffn_conv_b', 'ffn_w_down', 'final_norm']
TWIN_DIFF_INPUT = 'x'
TWIN_INPUTS = ['x', 'meta_tokens', 'mix_norm', 'ffn_norm', 'pool_w', 'pool_scale', 'kv_norm', 'w_kv', 'w_q', 'w_o', 'ffn_w_up', 'ffn_conv_w', 'ffn_conv_b', 'ffn_w_down', 'final_norm', 'loss_target', 'm_meta_tokens', 'm_mix_norm', 'm_ffn_norm', 'm_pool_w', 'm_pool_scale', 'm_kv_norm', 'm_w_kv', 'm_w_q', 'm_w_o', 'm_ffn_w_up', 'm_ffn_conv_w', 'm_ffn_conv_b', 'm_ffn_w_down', 'm_final_norm', 'v_meta_tokens', 'v_mix_norm', 'v_ffn_norm', 'v_pool_w', 'v_pool_scale', 'v_kv_norm', 'v_w_kv', 'v_w_q', 'v_w_o', 'v_ffn_w_up', 'v_ffn_conv_w', 'v_ffn_conv_b', 'v_ffn_w_down', 'v_final_norm']
TWIN_OUTPUTS = ['loss', 'grad_x', 'grad_meta_tokens', 'grad_mix_norm', 'grad_ffn_norm', 'grad_pool_w', 'grad_pool_scale', 'grad_kv_norm', 'grad_w_kv', 'grad_w_q', 'grad_w_o', 'grad_ffn_w_up', 'grad_ffn_conv_w', 'grad_ffn_conv_b', 'grad_ffn_w_down', 'grad_final_norm', 'delta_meta_tokens', 'delta_mix_norm', 'delta_ffn_norm', 'delta_pool_w', 'delta_pool_scale', 'delta_kv_norm', 'delta_w_kv', 'delta_w_q', 'delta_w_o', 'delta_ffn_w_up', 'delta_ffn_conv_w', 'delta_ffn_conv_b', 'delta_ffn_w_down', 'delta_final_norm', 'new_m_meta_tokens', 'new_m_mix_norm', 'new_m_ffn_norm', 'new_m_pool_w', 'new_m_pool_scale', 'new_m_kv_norm', 'new_m_w_kv', 'new_m_w_q', 'new_m_w_o', 'new_m_ffn_w_up', 'new_m_ffn_conv_w', 'new_m_ffn_conv_b', 'new_m_ffn_w_down', 'new_m_final_norm', 'new_v_meta_tokens', 'new_v_mix_norm', 'new_v_ffn_norm', 'new_v_pool_w', 'new_v_pool_scale', 'new_v_kv_norm', 'new_v_w_kv', 'new_v_w_q', 'new_v_w_o', 'new_v_ffn_w_up', 'new_v_ffn_conv_w', 'new_v_ffn_conv_b', 'new_v_ffn_w_down', 'new_v_final_norm']
TWIN_LEAF_KINDS = {'loss': 'loss', 'grad_x': 'grad_x', 'grad_meta_tokens': 'grad_w', 'grad_mix_norm': 'grad_w', 'grad_ffn_norm': 'grad_w', 'grad_pool_w': 'grad_w', 'grad_pool_scale': 'grad_w', 'grad_kv_norm': 'grad_w', 'grad_w_kv': 'grad_w', 'grad_w_q': 'grad_w', 'grad_w_o': 'grad_w', 'grad_ffn_w_up': 'grad_w', 'grad_ffn_conv_w': 'grad_w', 'grad_ffn_conv_b': 'grad_w', 'grad_ffn_w_down': 'grad_w', 'grad_final_norm': 'grad_w', 'delta_meta_tokens': 'delta_w', 'delta_mix_norm': 'delta_w', 'delta_ffn_norm': 'delta_w', 'delta_pool_w': 'delta_w', 'delta_pool_scale': 'delta_w', 'delta_kv_norm': 'delta_w', 'delta_w_kv': 'delta_w', 'delta_w_q': 'delta_w', 'delta_w_o': 'delta_w', 'delta_ffn_w_up': 'delta_w', 'delta_ffn_conv_w': 'delta_w', 'delta_ffn_conv_b': 'delta_w', 'delta_ffn_w_down': 'delta_w', 'delta_final_norm': 'delta_w', 'new_m_meta_tokens': 'new_m', 'new_m_mix_norm': 'new_m', 'new_m_ffn_norm': 'new_m', 'new_m_pool_w': 'new_m', 'new_m_pool_scale': 'new_m', 'new_m_kv_norm': 'new_m', 'new_m_w_kv': 'new_m', 'new_m_w_q': 'new_m', 'new_m_w_o': 'new_m', 'new_m_ffn_w_up': 'new_m', 'new_m_ffn_conv_w': 'new_m', 'new_m_ffn_conv_b': 'new_m', 'new_m_ffn_w_down': 'new_m', 'new_m_final_norm': 'new_m', 'new_v_meta_tokens': 'new_v', 'new_v_mix_norm': 'new_v', 'new_v_ffn_norm': 'new_v', 'new_v_pool_w': 'new_v', 'new_v_pool_scale': 'new_v', 'new_v_kv_norm': 'new_v', 'new_v_w_kv': 'new_v', 'new_v_w_q': 'new_v', 'new_v_w_o': 'new_v', 'new_v_ffn_w_up': 'new_v', 'new_v_ffn_conv_w': 'new_v', 'new_v_ffn_conv_b': 'new_v', 'new_v_ffn_w_down': 'new_v', 'new_v_final_norm': 'new_v'}


def _forward(args):
    return _fwd_reference(*[args[k] for k in FWD_PARAMS])


def _output_shape():
    def fwd():
        inp = _fwd_setup_inputs(0)
        return _fwd_reference(*[inp[k] for k in FWD_PARAMS])
    out = _jax.eval_shape(fwd)
    return out.shape, out.dtype

N_MICROBATCH = 1
ADAM_LR = 0.001
ADAM_B1 = 0.9
ADAM_B2 = 0.999
ADAM_EPS = 1e-08
ADAM_WD = 0.01
ADAM_STEP = 10
PER_EXAMPLE_BATCH_AXIS = {'x': 0, 'loss_target': 0}
SHARED_INPUTS = []
_WEIGHT_DTYPES = {'meta_tokens': _jnp.float32, 'mix_norm': _jnp.float32, 'ffn_norm': _jnp.float32, 'pool_w': _jnp.float32, 'pool_scale': _jnp.float32, 'kv_norm': _jnp.float32, 'w_kv': _jnp.float32, 'w_q': _jnp.float32, 'w_o': _jnp.float32, 'ffn_w_up': _jnp.float32, 'ffn_conv_w': _jnp.float32, 'ffn_conv_b': _jnp.float32, 'ffn_w_down': _jnp.float32, 'final_norm': _jnp.float32}
MOMENT_SCALE = {'meta_tokens': 3.502251e-03, 'mix_norm': 1.234098e-01, 'ffn_norm': 1.081539e-01, 'pool_w': 1.512458e-01, 'pool_scale': 8.072223e-01, 'kv_norm': 8.262666e-02, 'w_kv': 5.926991e-02, 'w_q': 3.636003e-02, 'w_o': 7.582509e-02, 'ffn_w_up': 4.582212e-02, 'ffn_conv_w': 4.620482e-02, 'ffn_conv_b': 4.558464e-02, 'ffn_w_down': 7.519181e-02, 'final_norm': 3.219251e+01}


def _to_microbatches(a, axis):
    t = _jnp.moveaxis(a, axis, 0)
    t = t.reshape((N_MICROBATCH, t.shape[0] // N_MICROBATCH) + t.shape[1:])
    return _jnp.moveaxis(t, 1, axis + 1)


def setup_inputs(seed: int = 0) -> dict:
    inp = _fwd_setup_inputs(seed)
    key = _jax.random.fold_in(_jax.random.key(seed), 7919)
    shape, _ = _output_shape()
    out = dict(inp)
    out["loss_target"] = _jax.random.normal(_jax.random.fold_in(key, 0), shape, _jnp.float32)
    for i, name in enumerate(TWIN_WEIGHTS):
        w = inp[name].astype(_jnp.float32)
        if MOMENT_SCALE is None:
            s = _jnp.sqrt(_jnp.mean(_jnp.square(w)) + 1e-30)
        else:
            s = MOMENT_SCALE[name]
        km, kv = _jax.random.split(_jax.random.fold_in(key, i + 1))
        out[name] = w
        out["m_" + name] = s * _jax.random.normal(km, w.shape, _jnp.float32)
        out["v_" + name] = (s * s) * _jax.random.uniform(kv, w.shape, _jnp.float32, 0.5, 1.5)
    if N_MICROBATCH > 1:
        for name, axis in PER_EXAMPLE_BATCH_AXIS.items():
            out[name] = _to_microbatches(out[name], axis)
    return {'x': out['x'], 'meta_tokens': out['meta_tokens'], 'mix_norm': out['mix_norm'], 'ffn_norm': out['ffn_norm'], 'pool_w': out['pool_w'], 'pool_scale': out['pool_scale'], 'kv_norm': out['kv_norm'], 'w_kv': out['w_kv'], 'w_q': out['w_q'], 'w_o': out['w_o'], 'ffn_w_up': out['ffn_w_up'], 'ffn_conv_w': out['ffn_conv_w'], 'ffn_conv_b': out['ffn_conv_b'], 'ffn_w_down': out['ffn_w_down'], 'final_norm': out['final_norm'], 'loss_target': out['loss_target'], 'm_meta_tokens': out['m_meta_tokens'], 'm_mix_norm': out['m_mix_norm'], 'm_ffn_norm': out['m_ffn_norm'], 'm_pool_w': out['m_pool_w'], 'm_pool_scale': out['m_pool_scale'], 'm_kv_norm': out['m_kv_norm'], 'm_w_kv': out['m_w_kv'], 'm_w_q': out['m_w_q'], 'm_w_o': out['m_w_o'], 'm_ffn_w_up': out['m_ffn_w_up'], 'm_ffn_conv_w': out['m_ffn_conv_w'], 'm_ffn_conv_b': out['m_ffn_conv_b'], 'm_ffn_w_down': out['m_ffn_w_down'], 'm_final_norm': out['m_final_norm'], 'v_meta_tokens': out['v_meta_tokens'], 'v_mix_norm': out['v_mix_norm'], 'v_ffn_norm': out['v_ffn_norm'], 'v_pool_w': out['v_pool_w'], 'v_pool_scale': out['v_pool_scale'], 'v_kv_norm': out['v_kv_norm'], 'v_w_kv': out['v_w_kv'], 'v_w_q': out['v_w_q'], 'v_w_o': out['v_w_o'], 'v_ffn_w_up': out['v_ffn_w_up'], 'v_ffn_conv_w': out['v_ffn_conv_w'], 'v_ffn_conv_b': out['v_ffn_conv_b'], 'v_ffn_w_down': out['v_ffn_w_down'], 'v_final_norm': out['v_final_norm']}


def _loss(weights, diff, rest, loss_target):
    with _jax.named_scope("forward"):
        args = {**rest, TWIN_DIFF_INPUT: diff, **{k: w.astype(_WEIGHT_DTYPES[k]) for k, w in weights.items()}}
        y = _forward(args)
    with _jax.named_scope("loss_head"):
        err = _jnp.square(y.astype(_jnp.float32) - loss_target)
        return 0.5 * _jnp.sum(_jnp.mean(err, axis=-1)) if err.ndim else 0.5 * err


def _adamw(w, g, m, v):
    m = ADAM_B1 * m + (1.0 - ADAM_B1) * g
    v = ADAM_B2 * v + (1.0 - ADAM_B2) * _jnp.square(g)
    m_hat = m / (1.0 - ADAM_B1 ** ADAM_STEP)
    v_hat = v / (1.0 - ADAM_B2 ** ADAM_STEP)
    delta = -ADAM_LR * (m_hat / (_jnp.sqrt(v_hat) + ADAM_EPS) + ADAM_WD * w)
    return delta, m, v


def reference(x, meta_tokens, mix_norm, ffn_norm, pool_w, pool_scale, kv_norm, w_kv, w_q, w_o, ffn_w_up, ffn_conv_w, ffn_conv_b, ffn_w_down, final_norm, loss_target, m_meta_tokens, m_mix_norm, m_ffn_norm, m_pool_w, m_pool_scale, m_kv_norm, m_w_kv, m_w_q, m_w_o, m_ffn_w_up, m_ffn_conv_w, m_ffn_conv_b, m_ffn_w_down, m_final_norm, v_meta_tokens, v_mix_norm, v_ffn_norm, v_pool_w, v_pool_scale, v_kv_norm, v_w_kv, v_w_q, v_w_o, v_ffn_w_up, v_ffn_conv_w, v_ffn_conv_b, v_ffn_w_down, v_final_norm):
    given = dict(x=x, meta_tokens=meta_tokens, mix_norm=mix_norm, ffn_norm=ffn_norm, pool_w=pool_w, pool_scale=pool_scale, kv_norm=kv_norm, w_kv=w_kv, w_q=w_q, w_o=w_o, ffn_w_up=ffn_w_up, ffn_conv_w=ffn_conv_w, ffn_conv_b=ffn_conv_b, ffn_w_down=ffn_w_down, final_norm=final_norm, loss_target=loss_target, m_meta_tokens=m_meta_tokens, m_mix_norm=m_mix_norm, m_ffn_norm=m_ffn_norm, m_pool_w=m_pool_w, m_pool_scale=m_pool_scale, m_kv_norm=m_kv_norm, m_w_kv=m_w_kv, m_w_q=m_w_q, m_w_o=m_w_o, m_ffn_w_up=m_ffn_w_up, m_ffn_conv_w=m_ffn_conv_w, m_ffn_conv_b=m_ffn_conv_b, m_ffn_w_down=m_ffn_w_down, m_final_norm=m_final_norm, v_meta_tokens=v_meta_tokens, v_mix_norm=v_mix_norm, v_ffn_norm=v_ffn_norm, v_pool_w=v_pool_w, v_pool_scale=v_pool_scale, v_kv_norm=v_kv_norm, v_w_kv=v_w_kv, v_w_q=v_w_q, v_w_o=v_w_o, v_ffn_w_up=v_ffn_w_up, v_ffn_conv_w=v_ffn_conv_w, v_ffn_conv_b=v_ffn_conv_b, v_ffn_w_down=v_ffn_w_down, v_final_norm=v_final_norm)
    weights = {n: given[n] for n in TWIN_WEIGHTS}
    shared = {n: given[n] for n in SHARED_INPUTS}
    per_example = {n: given[n] for n in ['x']}
    grad_fn = _jax.value_and_grad(_loss, argnums=(0, 1))

    def one_microbatch(ex, loss_target):
        ex = dict(ex)
        diff = ex.pop(TWIN_DIFF_INPUT)
        return grad_fn(weights, diff, {**shared, **ex}, loss_target)

    if N_MICROBATCH == 1:
        loss, (grad_w, grad_x) = one_microbatch(per_example, given["loss_target"])
    else:
        def body(carry, xs):
            loss_sum, grad_sum = carry
            l_k, (gw_k, gx_k) = one_microbatch(xs[0], xs[1])
            with _jax.named_scope("update"):
                return (loss_sum + l_k, _jax.tree.map(_jnp.add, grad_sum, gw_k)), gx_k

        init = (_jnp.zeros((), _jnp.float32), _jax.tree.map(_jnp.zeros_like, weights))
        (loss, grad_w), grad_x = _jax.lax.scan(body, init, (per_example, given["loss_target"]))
    with _jax.named_scope("update"):
        delta_w, new_m, new_v = {}, {}, {}
        for n in TWIN_WEIGHTS:
            delta_w[n], new_m[n], new_v[n] = _adamw(weights[n], grad_w[n], given["m_" + n], given["v_" + n])
    return (loss, grad_x, *[grad_w[n] for n in TWIN_WEIGHTS], *[delta_w[n] for n in TWIN_WEIGHTS],
            *[new_m[n] for n in TWIN_WEIGHTS], *[new_v[n] for n in TWIN_WEIGHTS])
```

```python
import jax
import jax.numpy as jnp
from jax import lax
from jax.experimental import pallas as pl
from jax.experimental.pallas import tpu as pltpu

F32 = jnp.float32
BF16 = jnp.bfloat16

N_META = 16
N_HEADS = 16
HEAD_DIM = 64
POOL_WINDOWS = (2, 4, 8, 16)
N_GROUPS = 4
RMS_EPS = 1e-6
ADAM_LR = 0.001
ADAM_B1 = 0.9
ADAM_B2 = 0.999
ADAM_EPS = 1e-08
ADAM_WD = 0.01
ADAM_STEP = 10

LANES = 128
BLK = 128
N_CHIPS = 4
N_DEV = 8
VMEM_LIMIT = 56 * 1024 * 1024
MESH = pl.DeviceIdType.MESH


def _cparams(*sem):
    return pltpu.CompilerParams(dimension_semantics=tuple(sem) if sem else None,
                                vmem_limit_bytes=VMEM_LIMIT)


def _tile(n, pref):
    best = None
    t = LANES
    while t <= min(n, pref):
        if n % t == 0:
            best = t
        t += LANES
    assert best is not None, (n, pref)
    return best


def _row_tile(r, pref, mult):
    best = r
    for t in range(mult, min(r, pref) + 1, mult):
        if r % t == 0:
            best = t
    return best


_DIMS = {
    "nn": (((1,), (0,)), ((), ())),
    "nt": (((1,), (1,)), ((), ())),
    "tn": (((0,), (0,)), ((), ())),
}


def _mm(a, b, dims, *, tm, tn, name, out_dtype=F32, res=None):
    if dims == "tn":
        k, m = a.shape
    else:
        m, k = a.shape
    n = b.shape[0] if dims == "nt" else b.shape[1]
    tm = _tile(m, tm)
    tn = _tile(n, tn)
    a_spec = pl.BlockSpec((k, tm), lambda i, j: (0, i)) if dims == "tn" else pl.BlockSpec((tm, k), lambda i, j: (i, 0))
    b_spec = pl.BlockSpec((tn, k), lambda i, j: (j, 0)) if dims == "nt" else pl.BlockSpec((k, tn), lambda i, j: (0, j))
    o_spec = pl.BlockSpec((tm, tn), lambda i, j: (i, j))
    dn = _DIMS[dims]

    def body(*refs):
        if res is None:
            a_ref, b_ref, o_ref = refs
        else:
            a_ref, b_ref, r_ref, o_ref = refs
        acc = lax.dot_general(a_ref[...], b_ref[...], dn, preferred_element_type=F32)
        if res is not None:
            acc = acc + r_ref[...]
        o_ref[...] = acc.astype(out_dtype)

    ins = [a, b] + ([] if res is None else [res])
    specs = [a_spec, b_spec] + ([] if res is None else [o_spec])
    return pl.pallas_call(
        body, name=name, grid=(m // tm, n // tn), in_specs=specs, out_specs=o_spec,
        out_shape=jax.ShapeDtypeStruct((m, n), out_dtype),
        compiler_params=_cparams("parallel", "arbitrary"),
    )(*ins)


def _norm_fwd(x, gains, *, name):
    l, d = x.shape
    tr = _tile(l, 384)
    ng = len(gains)

    def body(*refs):
        x_ref = refs[0]
        g_refs = refs[1:1 + ng]
        o_refs = refs[1 + ng:1 + 2 * ng]
        r_ref = refs[1 + 2 * ng]
        xv = x_ref[...]
        r = lax.rsqrt(jnp.mean(xv * xv, axis=-1, keepdims=True) + RMS_EPS)
        xn = xv * r
        for g_ref, o_ref in zip(g_refs, o_refs):
            o_ref[...] = (xn * g_ref[...]).astype(BF16)
        r_ref[...] = r

    row = pl.BlockSpec((tr, d), lambda i: (i, 0))
    gspec = pl.BlockSpec((1, d), lambda i: (0, 0))
    outs = pl.pallas_call(
        body, name=name, grid=(l // tr,),
        in_specs=[row] + [gspec] * ng,
        out_specs=[row] * ng + [pl.BlockSpec((tr, 1), lambda i: (i, 0))],
        out_shape=[jax.ShapeDtypeStruct((l, d), BF16)] * ng + [jax.ShapeDtypeStruct((l, 1), F32)],
        compiler_params=_cparams("parallel"),
    )(x, *gains)
    return list(outs[:ng]), outs[ng]


def _rows8(v):
    r, d = v.shape
    return jnp.sum(v.reshape(r // 8, 8, d), axis=0)


def _norm_bwd(x, r, gains, dns, dres, *, name):
    l, d = x.shape
    tr = _tile(l, 384)
    ng = len(gains)
    nsteps = l // tr

    def body(*refs):
        x_ref, r_ref, dres_ref = refs[0], refs[1], refs[2]
        g_refs = refs[3:3 + ng]
        dn_refs = refs[3 + ng:3 + 2 * ng]
        dx_ref, dxb_ref = refs[3 + 2 * ng], refs[4 + 2 * ng]
        dg_refs = refs[5 + 2 * ng:5 + 3 * ng]
        acc_refs = refs[5 + 3 * ng:5 + 4 * ng]
        i = pl.program_id(0)

        @pl.when(i == 0)
        def _():
            for acc in acc_refs:
                acc[...] = jnp.zeros_like(acc)

        rv = r_ref[...]
        xn = x_ref[...] * rv
        total = dres_ref[...]
        for g_ref, dn_ref, acc in zip(g_refs, dn_refs, acc_refs):
            dn = dn_ref[...]
            acc[...] += _rows8(dn * xn)
            dxn = dn * g_ref[...]
            total = total + rv * (dxn - xn * jnp.mean(dxn * xn, axis=-1, keepdims=True))
        dx_ref[...] = total
        dxb_ref[...] = total.astype(BF16)

        @pl.when(i == nsteps - 1)
        def _():
            for dg_ref, acc in zip(dg_refs, acc_refs):
                dg_ref[...] = jnp.sum(acc[...], axis=0, keepdims=True)

    row = pl.BlockSpec((tr, d), lambda i: (i, 0))
    gspec = pl.BlockSpec((1, d), lambda i: (0, 0))
    outs = pl.pallas_call(
        body, name=name, grid=(nsteps,),
        in_specs=[row, pl.BlockSpec((tr, 1), lambda i: (i, 0)), row] + [gspec] * ng + [row] * ng,
        out_specs=[row, row] + [gspec] * ng,
        out_shape=[jax.ShapeDtypeStruct((l, d), F32), jax.ShapeDtypeStruct((l, d), BF16)]
        + [jax.ShapeDtypeStruct((1, d), F32)] * ng,
        scratch_shapes=[pltpu.VMEM((8, d), F32)] * ng,
        compiler_params=_cparams("arbitrary"),
    )(x, r, dres, *gains, *dns)
    return outs[0], outs[1], list(outs[2:])


def _loss_bwd(h, gain, tgt, seq, *, name):
    l, d = h.shape
    tr = _tile(l, 384)
    nsteps = l // tr

    def body(h_ref, g_ref, t_ref, loss_ref, dh_ref, dhb_ref, dg_ref, lacc, gacc):
        i = pl.program_id(0)

        @pl.when(i == 0)
        def _():
            lacc[...] = jnp.zeros_like(lacc)
            gacc[...] = jnp.zeros_like(gacc)

        xv = h_ref[...]
        g = g_ref[...]
        r = lax.rsqrt(jnp.mean(xv * xv, axis=-1, keepdims=True) + RMS_EPS)
        xn = xv * r
        rows = i * tr + lax.broadcasted_iota(jnp.int32, (tr, 1), 0)
        valid = (rows >= N_META) & (rows < N_META + seq)
        e = jnp.where(valid, xn * g - t_ref[...], 0.0)
        lacc[...] += _rows8(e * e)
        dy = e * (1.0 / d)
        gacc[...] += _rows8(dy * xn)
        dxn = dy * g
        dx = r * (dxn - xn * jnp.mean(dxn * xn, axis=-1, keepdims=True))
        dh_ref[...] = dx
        dhb_ref[...] = dx.astype(BF16)

        @pl.when(i == nsteps - 1)
        def _():
            loss_ref[...] = jnp.full((8, LANES), 0.5 / d * jnp.sum(lacc[...]), F32)
            dg_ref[...] = jnp.sum(gacc[...], axis=0, keepdims=True)

    row = pl.BlockSpec((tr, d), lambda i: (i, 0))
    gspec = pl.BlockSpec((1, d), lambda i: (0, 0))
    return pl.pallas_call(
        body, name=name, grid=(nsteps,),
        in_specs=[row, gspec, row],
        out_specs=[pl.BlockSpec((8, LANES), lambda i: (0, 0)), row, row, gspec],
        out_shape=[jax.ShapeDtypeStruct((8, LANES), F32), jax.ShapeDtypeStruct((l, d), F32),
                   jax.ShapeDtypeStruct((l, d), BF16), jax.ShapeDtypeStruct((1, d), F32)],
        scratch_shapes=[pltpu.VMEM((8, d), F32), pltpu.VMEM((8, d), F32)],
        compiler_params=_cparams("arbitrary"),
    )(h, gain, tgt)


def _shift_down(v, k, rows):
    return jnp.where(rows >= k, pltpu.roll(v, k, 0), 0.0)


def _shift_up(v, k, rows):
    l = v.shape[0]
    return jnp.where(rows < l - k, pltpu.roll(v, l - k, 0), 0.0)


def _pool_diff(n, w, rows):
    s = n
    for k in (1, 2, 4, 8):
        s = s + jnp.where(k < w, _shift_down(s, k, rows), 0.0)
    cnt = jnp.minimum(rows + 1, w).astype(F32)
    return s / cnt - n, cnt


def _pool_diff_fwd(h, r, gain, *, name):
    l, d = h.shape
    per_group = d // N_GROUPS // LANES

    def body(h_ref, r_ref, g_ref, o_ref):
        w = jnp.left_shift(2, pl.program_id(0) // per_group)
        rows = lax.broadcasted_iota(jnp.int32, (l, 1), 0)
        n = h_ref[...] * r_ref[...] * g_ref[...]
        diff, _ = _pool_diff(n, w, rows)
        o_ref[...] = diff.astype(BF16)

    col = pl.BlockSpec((l, LANES), lambda j: (0, j))
    return pl.pallas_call(
        body, name=name, grid=(d // LANES,),
        in_specs=[col, pl.BlockSpec((l, 1), lambda j: (0, 0)), pl.BlockSpec((1, LANES), lambda j: (0, j))],
        out_specs=col, out_shape=jax.ShapeDtypeStruct((l, d), BF16),
        compiler_params=_cparams("parallel"),
    )(h, r, gain)


def _pool_diff_bwd(ddiff, *, name):
    l, d = ddiff.shape
    per_group = d // N_GROUPS // LANES

    def body(dd_ref, o_ref):
        w = jnp.left_shift(2, pl.program_id(0) // per_group)
        rows = lax.broadcasted_iota(jnp.int32, (l, 1), 0)
        dd = dd_ref[...]
        s = dd / jnp.minimum(rows + 1, w).astype(F32)
        for k in (1, 2, 4, 8):
            s = s + jnp.where(k < w, _shift_up(s, k, rows), 0.0)
        o_ref[...] = s - dd

    col = pl.BlockSpec((l, LANES), lambda j: (0, j))
    return pl.pallas_call(
        body, name=name, grid=(d // LANES,), in_specs=[col], out_specs=col,
        out_shape=jax.ShapeDtypeStruct((l, d), F32), compiler_params=_cparams("parallel"),
    )(ddiff)


def _pool_mix_fwd(h, diff_b, w_pool, scale, *, name):
    l, d = h.shape
    gd = d // N_GROUPS
    tr = _tile(l, 1408)

    def body(h_ref, d_ref, w_ref, s_ref, o_ref):
        y = jnp.dot(d_ref[...], w_ref[0], preferred_element_type=F32)
        o_ref[...] = h_ref[...] + y * s_ref[...]

    blk = pl.BlockSpec((tr, gd), lambda i, g: (i, g))
    vec = pl.BlockSpec((1, gd), lambda i, g: (0, g))
    return pl.pallas_call(
        body, name=name, grid=(l // tr, N_GROUPS),
        in_specs=[blk, blk, pl.BlockSpec((1, gd, gd), lambda i, g: (g, 0, 0)), vec],
        out_specs=blk, out_shape=jax.ShapeDtypeStruct((l, d), F32),
        compiler_params=_cparams("parallel", "parallel"),
    )(h, diff_b, w_pool, scale)


def _pool_mix_bwd(diff_b, w_pool, scale, dh1, *, name):
    l, d = dh1.shape
    gd = d // N_GROUPS
    tr = _tile(l, 1408)
    nsteps = l // tr

    def body(d_ref, w_ref, s_ref, dy_ref, dd_ref, dw_ref, ds_ref, sacc):
        i = pl.program_id(1)

        @pl.when(i == 0)
        def _():
            dw_ref[...] = jnp.zeros_like(dw_ref)
            sacc[...] = jnp.zeros_like(sacc)

        diff_b_, wg, dy = d_ref[...], w_ref[0], dy_ref[...]
        yy = jnp.dot(diff_b_, wg, preferred_element_type=F32)
        sacc[...] += _rows8(dy * yy)
        dyy_b = (dy * s_ref[...]).astype(BF16)
        dw_ref[0] += lax.dot_general(diff_b_, dyy_b, _DIMS["tn"], preferred_element_type=F32)
        dd_ref[...] = lax.dot_general(dyy_b, wg, _DIMS["nt"], preferred_element_type=F32)

        @pl.when(i == nsteps - 1)
        def _():
            ds_ref[...] = jnp.sum(sacc[...], axis=0, keepdims=True)

    blk = pl.BlockSpec((tr, gd), lambda g, i: (i, g))
    vec = pl.BlockSpec((1, gd), lambda g, i: (0, g))
    wspec = pl.BlockSpec((1, gd, gd), lambda g, i: (g, 0, 0))
    return pl.pallas_call(
        body, name=name, grid=(N_GROUPS, nsteps),
        in_specs=[blk, wspec, vec, blk],
        out_specs=[blk, wspec, vec],
        out_shape=[jax.ShapeDtypeStruct((l, d), F32), jax.ShapeDtypeStruct((N_GROUPS, gd, gd), F32),
                   jax.ShapeDtypeStruct((1, d), F32)],
        scratch_shapes=[pltpu.VMEM((8, gd), F32)],
        compiler_params=_cparams("parallel", "arbitrary"),
    )(diff_b, w_pool, scale, dh1)


def _conv(u, w, b, rows):
    return b + w[0:1] * _shift_down(u, 2, rows) + w[1:2] * _shift_down(u, 1, rows) + w[2:3] * u


def _convgate_fwd(u, cw, cb, *, name):
    l, f2 = u.shape
    f = f2 // 2
    tc = _tile(f, LANES)
    nc = f // tc

    def body(ug_ref, uv_ref, wg_ref, wv_ref, bg_ref, bv_ref, a_ref):
        rows = lax.broadcasted_iota(jnp.int32, (l, 1), 0)
        gate = _conv(ug_ref[...], wg_ref[...], bg_ref[...], rows)
        val = _conv(uv_ref[...], wv_ref[...], bv_ref[...], rows)
        a_ref[...] = (gate * jax.nn.sigmoid(gate) * val).astype(BF16)

    def spec(rows_, off):
        return pl.BlockSpec((rows_, tc), lambda j: (0, j + off))

    return pl.pallas_call(
        body, name=name, grid=(nc,),
        in_specs=[spec(l, 0), spec(l, nc), spec(3, 0), spec(3, nc), spec(1, 0), spec(1, nc)],
        out_specs=spec(l, 0), out_shape=jax.ShapeDtypeStruct((l, f), BF16),
        compiler_params=_cparams("parallel"),
    )(u, u, cw, cw, cb, cb)


def _convgate_bwd(u, cw, cb, da, *, name):
    l, f2 = u.shape
    f = f2 // 2
    tc = _tile(f, LANES)
    nc = f // tc

    def body(ug_ref, uv_ref, wg_ref, wv_ref, bg_ref, bv_ref, da_ref, du_ref, dcw_ref, dcb_ref):
        part = pl.program_id(0)
        rows = lax.broadcasted_iota(jnp.int32, (l, 1), 0)
        ug, uv = ug_ref[...], uv_ref[...]
        wg, wv = wg_ref[...], wv_ref[...]
        gate = _conv(ug, wg, bg_ref[...], rows)
        val = _conv(uv, wv, bv_ref[...], rows)
        sg = jax.nn.sigmoid(gate)
        dav = da_ref[...]
        dgate = dav * val * (sg * (1.0 + gate * (1.0 - sg)))
        dval = dav * (gate * sg)
        is_gate = part == 0
        dc = jnp.where(is_gate, dgate, dval)
        uo = jnp.where(is_gate, ug, uv)
        wo = jnp.where(is_gate, wg, wv)
        dcb_ref[...] = jnp.sum(dc, axis=0, keepdims=True)
        dcw_ref[0:1, :] = jnp.sum(dc * _shift_down(uo, 2, rows), axis=0, keepdims=True)
        dcw_ref[1:2, :] = jnp.sum(dc * _shift_down(uo, 1, rows), axis=0, keepdims=True)
        dcw_ref[2:3, :] = jnp.sum(dc * uo, axis=0, keepdims=True)
        du = wo[2:3] * dc + wo[1:2] * _shift_up(dc, 1, rows) + wo[0:1] * _shift_up(dc, 2, rows)
        du_ref[...] = du.astype(BF16)

    def spec(rows_, off):
        return pl.BlockSpec((rows_, tc), lambda p, j: (0, j + off))

    def own(rows_):
        return pl.BlockSpec((rows_, tc), lambda p, j: (0, p * nc + j))

    return pl.pallas_call(
        body, name=name, grid=(2, nc),
        in_specs=[spec(l, 0), spec(l, nc), spec(3, 0), spec(3, nc), spec(1, 0), spec(1, nc), spec(l, 0)],
        out_specs=[own(l), own(3), own(1)],
        out_shape=[jax.ShapeDtypeStruct((l, f2), BF16), jax.ShapeDtypeStruct((3, f2), F32),
                   jax.ShapeDtypeStruct((1, f2), F32)],
        compiler_params=_cparams("parallel", "parallel"),
    )(u, u, cw, cw, cb, cb, da)


def _cumsum_mm(v, t_mat):
    hi = v.astype(BF16)
    lo = (v - hi.astype(F32)).astype(BF16)
    return (jnp.dot(hi, t_mat, preferred_element_type=F32)
            + jnp.dot(lo, t_mat, preferred_element_type=F32))


def _block_logits(qi, kj, mask):
    z = lax.dot_general(qi, kj, _DIMS["nt"], preferred_element_type=F32)
    sp = jnp.log(1.0 + jnp.exp(-jnp.abs(z)))
    lb = jnp.minimum(z, 0.0) - sp
    lm = jnp.where(mask, jnp.minimum(-z, 0.0) - sp, 0.0)
    return lb, lm


def _attn_fwd(q, k, v, *, name):
    nh, l, dh = q.shape
    nb = l // BLK
    qscale = HEAD_DIM ** -0.5

    def body(q_ref, k_ref, v_ref, o_ref, lt_ref):
        row = lax.broadcasted_iota(jnp.int32, (BLK, BLK), 0)
        col = lax.broadcasted_iota(jnp.int32, (BLK, BLK), 1)
        t_later = (row > col).astype(BF16)

        def qblock(i, _):
            r0 = pl.multiple_of(i * BLK, BLK)
            qi = q_ref[0, pl.ds(r0, BLK), :] * jnp.asarray(qscale, BF16)

            def kblock(jj, carry):
                acc, run = carry
                j = i - jj
                c0 = pl.multiple_of(j * BLK, BLK)
                mask = (col - row) < (r0 - c0)
                lb, lm = _block_logits(qi, k_ref[0, pl.ds(c0, BLK), :], mask)
                later = _cumsum_mm(lm, t_later)
                a = jnp.where(mask, jnp.exp(lb + later + run), 0.0)
                acc = acc + jnp.dot(a.astype(BF16), v_ref[0, pl.ds(c0, BLK), :], preferred_element_type=F32)
                run = run + jnp.sum(lm, axis=1, keepdims=True)
                return acc, run

            acc, run = lax.fori_loop(0, i + 1, kblock,
                                     (jnp.zeros((BLK, dh), F32), jnp.zeros((BLK, 1), F32)))
            o_ref[0, pl.ds(r0, BLK), :] = acc.astype(BF16)
            lt_ref[0, pl.ds(r0, BLK), :] = run
            return 0

        lax.fori_loop(0, nb, qblock, 0)

    head = pl.BlockSpec((1, l, dh), lambda h: (h, 0, 0))
    return pl.pallas_call(
        body, name=name, grid=(nh,),
        in_specs=[head, head, head],
        out_specs=[head, pl.BlockSpec((1, l, 1), lambda h: (h, 0, 0))],
        out_shape=[jax.ShapeDtypeStruct((nh, l, dh), BF16), jax.ShapeDtypeStruct((nh, l, 1), F32)],
        compiler_params=_cparams("parallel"),
    )(q, k, v)


def _attn_bwd(q, k, v, do, ltot, *, name):
    nh, l, dh = q.shape
    nb = l // BLK
    qscale = HEAD_DIM ** -0.5

    def body(q_ref, k_ref, v_ref, do_ref, lt_ref, dq_ref, dk_ref, dv_ref):
        row = lax.broadcasted_iota(jnp.int32, (BLK, BLK), 0)
        col = lax.broadcasted_iota(jnp.int32, (BLK, BLK), 1)
        t_incl = (row <= col).astype(BF16)
        t_excl = (row < col).astype(BF16)
        dk_ref[...] = jnp.zeros_like(dk_ref)
        dv_ref[...] = jnp.zeros_like(dv_ref)

        def qblock(i, _):
            r0 = pl.multiple_of(i * BLK, BLK)
            q_raw = q_ref[0, pl.ds(r0, BLK), :]
            qi = q_raw * jnp.asarray(qscale, BF16)
            doi = do_ref[0, pl.ds(r0, BLK), :]
            lt = lt_ref[0, pl.ds(r0, BLK), :]

            def kblock(j, carry):
                dq, pre_lm, pre_dl = carry
                c0 = pl.multiple_of(j * BLK, BLK)
                mask = (col - row) < (r0 - c0)
                kj = k_ref[0, pl.ds(c0, BLK), :]
                vj = v_ref[0, pl.ds(c0, BLK), :]
                lb, lm = _block_logits(qi, kj, mask)
                later = lt - pre_lm - _cumsum_mm(lm, t_incl)
                a = jnp.where(mask, jnp.exp(lb + later), 0.0)
                da = lax.dot_general(doi, vj, _DIMS["nt"], preferred_element_type=F32)
                dl = a * da
                prefix = pre_dl + _cumsum_mm(dl, t_excl)
                beta = jnp.exp(lb)
                dz = jnp.where(mask, dl * (1.0 - beta) - beta * prefix, 0.0)
                dz_b = (dz * qscale).astype(BF16)
                dq = dq + jnp.dot(dz_b, kj, preferred_element_type=F32)
                dk_ref[0, pl.ds(c0, BLK), :] += lax.dot_general(dz_b, q_raw, _DIMS["tn"], preferred_element_type=F32)
                dv_ref[0, pl.ds(c0, BLK), :] += lax.dot_general(a.astype(BF16), doi, _DIMS["tn"], preferred_element_type=F32)
                pre_lm = pre_lm + jnp.sum(lm, axis=1, keepdims=True)
                pre_dl = pre_dl + jnp.sum(dl, axis=1, keepdims=True)
                return dq, pre_lm, pre_dl

            zero_col = jnp.zeros((BLK, 1), F32)
            dq, _, _ = lax.fori_loop(0, i + 1, kblock, (jnp.zeros((BLK, dh), F32), zero_col, zero_col))
            dq_ref[0, pl.ds(r0, BLK), :] = dq
            return 0

        lax.fori_loop(0, nb, qblock, 0)

    head = pl.BlockSpec((1, l, dh), lambda h: (h, 0, 0))
    return pl.pallas_call(
        body, name=name, grid=(nh,),
        in_specs=[head, head, head, head, pl.BlockSpec((1, l, 1), lambda h: (h, 0, 0))],
        out_specs=[head, head, head],
        out_shape=[jax.ShapeDtypeStruct((nh, l, dh), F32)] * 3,
        compiler_params=_cparams("parallel"),
    )(q, k, v, do, ltot)


def _add_to_bf16(a, b, *, name):
    n, r, c = a.shape
    tr = _row_tile(r, 1536, 16)
    blk = pl.BlockSpec((1, tr, c), lambda i, j: (i, j, 0))

    def body(a_ref, b_ref, o_ref):
        o_ref[...] = (a_ref[...] + b_ref[...]).astype(BF16)

    return pl.pallas_call(body, name=name, grid=(n, r // tr), in_specs=[blk, blk], out_specs=blk,
                          out_shape=jax.ShapeDtypeStruct(a.shape, BF16),
                          compiler_params=_cparams("parallel", "parallel"))(a, b)


def _sum_slots(p, *, name):
    n, r, c = p.shape
    tr = _row_tile(r, 1536, 16)

    def body(p_ref, o_ref):
        acc = p_ref[0].astype(F32)
        for s in range(1, n):
            acc = acc + p_ref[s].astype(F32)
        o_ref[...] = acc

    return pl.pallas_call(body, name=name, grid=(r // tr,),
                          in_specs=[pl.BlockSpec((n, tr, c), lambda j: (0, j, 0))],
                          out_specs=pl.BlockSpec((tr, c), lambda j: (j, 0)),
                          out_shape=jax.ShapeDtypeStruct((r, c), F32),
                          compiler_params=_cparams("parallel"))(p)


def _adamw(w, g, m, v, *, name):
    r, c = w.shape
    tr = _row_tile(r, 512, 8)
    blk = pl.BlockSpec((tr, c), lambda i: (i, 0))

    def body(w_ref, g_ref, m_ref, v_ref, d_ref, mo_ref, vo_ref):
        gv = g_ref[...]
        mn = ADAM_B1 * m_ref[...] + (1.0 - ADAM_B1) * gv
        vn = ADAM_B2 * v_ref[...] + (1.0 - ADAM_B2) * (gv * gv)
        m_hat = mn / (1.0 - ADAM_B1 ** ADAM_STEP)
        v_hat = vn / (1.0 - ADAM_B2 ** ADAM_STEP)
        d_ref[...] = -ADAM_LR * (m_hat / (jnp.sqrt(v_hat) + ADAM_EPS) + ADAM_WD * w_ref[...])
        mo_ref[...] = mn
        vo_ref[...] = vn

    return pl.pallas_call(body, name=name, grid=(r // tr,), in_specs=[blk] * 4, out_specs=[blk] * 3,
                          out_shape=[jax.ShapeDtypeStruct((r, c), F32)] * 3,
                          compiler_params=_cparams("parallel"))(w, g, m, v)


_ANY = pl.BlockSpec(memory_space=pl.ANY)


def _place():
    x, y, c = lax.axis_index("x"), lax.axis_index("y"), lax.axis_index("c")
    chips = [(1 - x, y), (x, 1 - y), (1 - x, 1 - y)]
    return x, y, c, chips


def _gather_shards(wsh, *, name):
    _, r, cdim = wsh.shape

    def body(w_ref, out_ref, send_sems, recv_sems, local_sem):
        x, y, c, chips = _place()
        s = 2 * x + y
        sibling = (x, y, 1 - c)

        def copy(k, src, dst, to):
            return pltpu.make_async_remote_copy(src_ref=src, dst_ref=dst, send_sem=send_sems.at[k],
                                                recv_sem=recv_sems.at[k], device_id=to, device_id_type=MESH)

        mine = pltpu.make_async_copy(w_ref, out_ref.at[s], local_sem)
        mine.start()
        first = [copy(j, w_ref.at[c], out_ref.at[s, c], (*chip, c)) for j, chip in enumerate(chips)]
        for cp in first:
            cp.start()
        passed = []
        for j, (px, py) in enumerate(chips):
            sp = 2 * px + py
            copy(j, w_ref.at[c], out_ref.at[sp, c], sibling).wait_recv()
            fwd = copy(3 + j, out_ref.at[sp, c], out_ref.at[sp, c], sibling)
            fwd.start()
            passed.append(fwd)
        for j, (px, py) in enumerate(chips):
            sp = 2 * px + py
            copy(3 + j, w_ref.at[c], out_ref.at[sp, 1 - c], sibling).wait_recv()
        for cp in first + passed:
            cp.wait_send()
        mine.wait()

    return pl.pallas_call(
        body, name=name, in_specs=[_ANY], out_specs=_ANY,
        out_shape=jax.ShapeDtypeStruct((N_CHIPS, 2, r, cdim), wsh.dtype),
        scratch_shapes=[pltpu.SemaphoreType.DMA((6,)), pltpu.SemaphoreType.DMA((6,)), pltpu.SemaphoreType.DMA],
    )(wsh)


def _to_sibling(a, *, name):
    def body(a_ref, out_ref, send_sem, recv_sem):
        x, y, c, _ = _place()
        cp = pltpu.make_async_remote_copy(src_ref=a_ref, dst_ref=out_ref, send_sem=send_sem, recv_sem=recv_sem,
                                          device_id=(x, y, 1 - c), device_id_type=MESH)
        cp.start()
        cp.wait()

    return pl.pallas_call(
        body, name=name, in_specs=[_ANY], out_specs=_ANY,
        out_shape=jax.ShapeDtypeStruct(a.shape, a.dtype),
        scratch_shapes=[pltpu.SemaphoreType.DMA, pltpu.SemaphoreType.DMA],
    )(a)


def _scatter_rows(p, *, name):
    def body(p_ref, out_ref, send_sems, recv_sems, local_sem):
        x, y, c, chips = _place()
        s = 2 * x + y
        mine = pltpu.make_async_copy(p_ref.at[s], out_ref.at[s], local_sem)
        mine.start()
        sends = []
        for j, (px, py) in enumerate(chips):
            sp = 2 * px + py
            cp = pltpu.make_async_remote_copy(src_ref=p_ref.at[sp], dst_ref=out_ref.at[s], send_sem=send_sems.at[j],
                                              recv_sem=recv_sems.at[j], device_id=(px, py, c), device_id_type=MESH)
            cp.start()
            sends.append(cp)
        for j, (px, py) in enumerate(chips):
            sp = 2 * px + py
            pltpu.make_async_remote_copy(src_ref=p_ref.at[sp], dst_ref=out_ref.at[sp], send_sem=send_sems.at[j],
                                         recv_sem=recv_sems.at[j], device_id=(px, py, c),
                                         device_id_type=MESH).wait_recv()
        for cp in sends:
            cp.wait_send()
        mine.wait()

    return pl.pallas_call(
        body, name=name, in_specs=[_ANY], out_specs=_ANY,
        out_shape=jax.ShapeDtypeStruct(p.shape, p.dtype),
        scratch_shapes=[pltpu.SemaphoreType.DMA((3,)), pltpu.SemaphoreType.DMA((3,)), pltpu.SemaphoreType.DMA],
    )(p)


def _join_halves(half, *, name):
    def body(h_ref, out_ref, send_sem, recv_sem, local_sem):
        x, y, c, _ = _place()
        mine = pltpu.make_async_copy(h_ref, out_ref.at[c], local_sem)
        mine.start()
        cp = pltpu.make_async_remote_copy(src_ref=h_ref, dst_ref=out_ref.at[c], send_sem=send_sem, recv_sem=recv_sem,
                                          device_id=(x, y, 1 - c), device_id_type=MESH)
        cp.start()
        pltpu.make_async_remote_copy(src_ref=h_ref, dst_ref=out_ref.at[1 - c], send_sem=send_sem, recv_sem=recv_sem,
                                     device_id=(x, y, 1 - c), device_id_type=MESH).wait_recv()
        cp.wait_send()
        mine.wait()

    return pl.pallas_call(
        body, name=name, in_specs=[_ANY], out_specs=_ANY,
        out_shape=jax.ShapeDtypeStruct((2,) + half.shape, half.dtype),
        scratch_shapes=[pltpu.SemaphoreType.DMA, pltpu.SemaphoreType.DMA, pltpu.SemaphoreType.DMA],
    )(half)


def _all_devices(a, reduce, *, name):
    r, cdim = a.shape

    def body(a_ref, out_ref, *scratch):
        if reduce:
            buf, send_sems, recv_sems = scratch
        else:
            buf = out_ref
            send_sems, recv_sems = scratch
        x, y, c, _ = _place()
        me = 4 * x + 2 * y + c
        buf[me] = a_ref[...]
        peers = []
        for k in range(1, N_DEV):
            dx, dy, dc = (k >> 2) & 1, (k >> 1) & 1, k & 1
            peers.append((x ^ dx, y ^ dy, c ^ dc))
        sends = []
        for k, peer in enumerate(peers):
            cp = pltpu.make_async_remote_copy(src_ref=a_ref, dst_ref=buf.at[me], send_sem=send_sems.at[k],
                                              recv_sem=recv_sems.at[k], device_id=peer, device_id_type=MESH)
            cp.start()
            sends.append(cp)
        for k, (px, py, pc) in enumerate(peers):
            pltpu.make_async_remote_copy(src_ref=a_ref, dst_ref=buf.at[4 * px + 2 * py + pc],
                                         send_sem=send_sems.at[k], recv_sem=recv_sems.at[k],
                                         device_id=(px, py, pc), device_id_type=MESH).wait_recv()
        for cp in sends:
            cp.wait_send()
        if reduce:
            acc = buf[0]
            for k in range(1, N_DEV):
                acc = acc + buf[k]
            out_ref[...] = acc

    vm = pl.BlockSpec(memory_space=pltpu.VMEM)
    sems = [pltpu.SemaphoreType.DMA((N_DEV - 1,)), pltpu.SemaphoreType.DMA((N_DEV - 1,))]
    if reduce:
        out_shape = jax.ShapeDtypeStruct((r, cdim), F32)
        scratch = [pltpu.VMEM((N_DEV, r, cdim), F32)] + sems
    else:
        out_shape = jax.ShapeDtypeStruct((N_DEV, r, cdim), F32)
        scratch = sems
    return pl.pallas_call(
        body, name=name, in_specs=[vm], out_specs=vm, out_shape=out_shape, scratch_shapes=scratch,
        compiler_params=pltpu.CompilerParams(vmem_limit_bytes=VMEM_LIMIT),
    )(a)


def _flat_rows(parts, cols):
    flat = jnp.concatenate([p.reshape(-1) for p in parts])
    padded = -(-flat.size // (8 * cols)) * (8 * cols)
    return jnp.pad(flat, (0, padded - flat.size)).reshape(-1, cols)


def kernel(x, meta_tokens, mix_norm, ffn_norm, pool_w, pool_scale, kv_norm, w_kv, w_q, w_o, ffn_w_up, ffn_conv_w, ffn_conv_b, ffn_w_down, final_norm, loss_target, m_meta_tokens, m_mix_norm, m_ffn_norm, m_pool_w, m_pool_scale, m_kv_norm, m_w_kv, m_w_q, m_w_o, m_ffn_w_up, m_ffn_conv_w, m_ffn_conv_b, m_ffn_w_down, m_final_norm, v_meta_tokens, v_mix_norm, v_ffn_norm, v_pool_w, v_pool_scale, v_kv_norm, v_w_kv, v_w_q, v_w_o, v_ffn_w_up, v_ffn_conv_w, v_ffn_conv_b, v_ffn_w_down, v_final_norm):
    seq, d = x.shape[1], x.shape[2]
    l_real = N_META + seq
    lp = -(-l_real // BLK) * BLK
    f2 = ffn_w_up.shape[2] * N_CHIPS
    f = f2 // 2
    gd = d // N_GROUPS
    chip = 2 * lax.axis_index("x") + lax.axis_index("y")
    core = lax.axis_index("c")

    big_parts = [pool_w[0], w_kv, w_q[0], w_o[0], ffn_w_up, ffn_w_down]
    sizes = [p.size for p in big_parts]
    n_big = sum(sizes)
    cw = 1024
    assert n_big % (2 * 16 * cw) == 0
    rows_half = n_big // (2 * cw)
    wsh = _flat_rows([p.astype(BF16) for p in big_parts], cw).reshape(2, rows_half, cw)
    wall = _gather_shards(wsh, name="gather_weights").reshape(N_CHIPS, n_big)
    offs = [0]
    for sz in sizes:
        offs.append(offs[-1] + sz)

    def big(i, shape):
        return wall[:, offs[i]:offs[i + 1]].reshape((N_CHIPS,) + shape)

    wp_b = big(0, (N_GROUPS, gd // N_CHIPS, gd)).transpose(1, 0, 2, 3).reshape(N_GROUPS, gd, gd)
    wkv_b = big(1, (d, 2 * d // N_CHIPS)).transpose(1, 0, 2).reshape(d, 2 * d)
    wq_b = big(2, (d // N_CHIPS, d)).reshape(d, d)
    wo_b = big(3, (d // N_CHIPS, d)).reshape(d, d)
    wup_b = big(4, (2, d, f2 // N_CHIPS)).transpose(1, 2, 0, 3).reshape(2, d, f2)
    wdn_b = big(5, (2, f // N_CHIPS, d)).transpose(1, 0, 2, 3).reshape(2, f, d)

    small_parts = [meta_tokens, pool_scale, ffn_conv_w]
    ssizes = [p.size for p in small_parts]
    small = _flat_rows(small_parts, LANES)
    sall = _all_devices(small, False, name="gather_small")[::2].reshape(N_CHIPS, -1)
    meta_f = sall[:, :ssizes[0]].reshape(N_CHIPS, N_META, d // N_CHIPS).transpose(1, 0, 2).reshape(N_META, d)
    scale_f = sall[:, ssizes[0]:ssizes[0] + ssizes[1]].reshape(1, d)
    cw_f = sall[:, ssizes[0] + ssizes[1]:sum(ssizes)].reshape(N_CHIPS, 2, 3, f2 // N_CHIPS).transpose(1, 2, 0, 3).reshape(2, 3, f2)

    mix0, mix1 = mix_norm[0:1], mix_norm[1:2]
    fn0, fn1 = ffn_norm[0:1], ffn_norm[1:2]
    kvn_g = kv_norm.reshape(1, d)
    fin_g = final_norm.reshape(1, d)

    pad = lp - l_real
    h0 = jnp.concatenate([meta_f, x[0], jnp.zeros((pad, d), F32)], axis=0)
    tgt = jnp.pad(loss_target[0], ((N_META, pad), (0, 0)))

    _, r0 = _norm_fwd(h0, [mix0], name="norm_h0")
    diff_b = _pool_diff_fwd(h0, r0, mix0, name="pool_diff")
    h1 = _pool_mix_fwd(h0, diff_b, wp_b, scale_f, name="pool_mix")

    def ffn_fwd(h_in, layer, tag):
        (fb,), rf = _norm_fwd(h_in, [ffn_norm[layer:layer + 1]], name=f"norm_ffn{tag}")
        u = _mm(fb, wup_b[layer], "nn", tm=1408, tn=512, name=f"ffn_up{tag}")
        a = _convgate_fwd(u, cw_f[layer], ffn_conv_b[layer:layer + 1], name=f"convgate{tag}")
        h_out = _mm(a, wdn_b[layer], "nn", tm=384, tn=1024, res=h_in, name=f"ffn_down{tag}")
        return h_out, (fb, rf, u, a)

    h2, ffn0_saved = ffn_fwd(h1, 0, "0")
    (kvn_b, n1_b), r2 = _norm_fwd(h2, [kvn_g, mix1], name="norm_h2")
    kv_b = _mm(kvn_b, wkv_b, "nn", tm=1408, tn=512, out_dtype=BF16, name="kv_proj")
    q_b = _mm(n1_b, wq_b, "nn", tm=1408, tn=512, out_dtype=BF16, name="q_proj")

    def heads(t):
        return t.reshape(lp, N_HEADS, HEAD_DIM).transpose(1, 0, 2)

    def unheads(t):
        return t.transpose(1, 0, 2).reshape(lp, N_HEADS * HEAD_DIM)

    qh, kh, vh = heads(q_b), heads(kv_b[:, :d]), heads(kv_b[:, d:])
    oh, ltot = _attn_fwd(qh, kh, vh, name="attn_fwd")
    o_b = unheads(oh)
    h3 = _mm(o_b, wo_b, "nn", tm=384, tn=1024, res=h2, name="o_proj")
    h4, ffn1_saved = ffn_fwd(h3, 1, "1")

    loss8, dh4, dh4_b, dfin = _loss_bwd(h4, fin_g, tgt, seq, name="loss")

    def ffn_bwd(h_in, layer, saved, dh_out, dh_out_b, tag):
        fb, rf, u, a = saved
        dwdn = _mm(a, dh_out_b, "tn", tm=256, tn=1024, name=f"d_wdown{tag}")
        da = _mm(dh_out_b, wdn_b[layer], "nt", tm=384, tn=f, name=f"d_act{tag}")
        du_b, dcw, dcb = _convgate_bwd(u, cw_f[layer], ffn_conv_b[layer:layer + 1], da, name=f"convgate_bwd{tag}")
        dwup = _mm(fb, du_b, "tn", tm=1024, tn=512, name=f"d_wup{tag}")
        df = _mm(du_b, wup_b[layer], "nt", tm=384, tn=1024, name=f"d_ffn_in{tag}")
        dh_in, dh_in_b, (dfn,) = _norm_bwd(h_in, rf, [ffn_norm[layer:layer + 1]], [df], dh_out,
                                          name=f"norm_ffn_bwd{tag}")
        return dh_in, dh_in_b, dwup, dwdn, dcw, dcb, dfn

    dh3, dh3_b, dwup1, dwdn1, dcw1, dcb1, dfn1 = ffn_bwd(h3, 1, ffn1_saved, dh4, dh4_b, "1")

    dwo = _mm(o_b, dh3_b, "tn", tm=1024, tn=512, name="d_wo")
    do_b = _mm(dh3_b, wo_b, "nt", tm=384, tn=1024, out_dtype=BF16, name="d_attn_out")
    dqh, dkh, dvh = _attn_bwd(qh, kh, vh, heads(do_b), ltot, name="attn_bwd")
    dq_b = unheads(dqh).astype(BF16)
    dkv_b = jnp.concatenate([unheads(dkh), unheads(dvh)], axis=1).astype(BF16)
    dwq = _mm(n1_b, dq_b, "tn", tm=1024, tn=512, name="d_wq")
    dwkv = _mm(kvn_b, dkv_b, "tn", tm=1024, tn=512, name="d_wkv")
    dn1 = _mm(dq_b, wq_b, "nt", tm=384, tn=1024, name="d_n1")
    dkvn = _mm(dkv_b, wkv_b, "nt", tm=384, tn=1024, name="d_kvn")
    dh2, dh2_b, (dmix1, dkvg) = _norm_bwd(h2, r2, [mix1, kvn_g], [dn1, dkvn], dh3, name="norm_h2_bwd")

    dh1, dh1_b, dwup0, dwdn0, dcw0, dcb0, dfn0 = ffn_bwd(h1, 0, ffn0_saved, dh2, dh2_b, "0")

    ddiff, dwp, dscale = _pool_mix_bwd(diff_b, wp_b, scale_f, dh1, name="pool_mix_bwd")
    dn0 = _pool_diff_bwd(ddiff, name="pool_diff_bwd")
    dh0, _, (dmix0,) = _norm_bwd(h0, r0, [mix0], [dn0], dh1, name="norm_h0_bwd")

    grad_x = dh0[N_META:l_real][None]
    dmeta = dh0[:N_META]

    dwup = jnp.stack([dwup0, dwup1])
    dwdn = jnp.stack([dwdn0, dwdn1])
    g_parts = [
        dwp.reshape(N_GROUPS, N_CHIPS, gd // N_CHIPS, gd).transpose(1, 0, 2, 3),
        dwkv.reshape(d, N_CHIPS, 2 * d // N_CHIPS).transpose(1, 0, 2),
        dwq.reshape(N_CHIPS, d // N_CHIPS, d),
        dwo.reshape(N_CHIPS, d // N_CHIPS, d),
        dwup.reshape(2, d, N_CHIPS, f2 // N_CHIPS).transpose(2, 0, 1, 3),
        dwdn.reshape(2, N_CHIPS, f // N_CHIPS, d).transpose(1, 0, 2, 3),
    ]
    g_all = jnp.concatenate([p.reshape(N_CHIPS, -1) for p in g_parts], axis=1).reshape(N_CHIPS, 2, rows_half, cw)
    g_mine = lax.dynamic_index_in_dim(g_all, core, axis=1, keepdims=False)
    g_other = lax.dynamic_index_in_dim(g_all, 1 - core, axis=1, keepdims=False)
    from_sibling = _to_sibling(g_other, name="grads_to_sibling")
    chip_part = _add_to_bf16(g_mine, from_sibling, name="grads_chip_sum")
    slots = _scatter_rows(chip_part, name="grads_scatter")
    half = _sum_slots(slots, name="grads_sum")
    g_shard = _join_halves(half, name="grads_join").reshape(n_big)

    def gbig(i, shape):
        return g_shard[offs[i]:offs[i + 1]].reshape(shape)

    g_pool_w = gbig(0, pool_w.shape)
    g_w_kv = gbig(1, w_kv.shape)
    g_w_q = gbig(2, w_q.shape)
    g_w_o = gbig(3, w_o.shape)
    g_w_up = gbig(4, ffn_w_up.shape)
    g_w_dn = gbig(5, ffn_w_down.shape)

    dcw = jnp.stack([dcw0, dcw1])
    dcb = jnp.concatenate([dcb0, dcb1], axis=0)
    sg_parts = [jnp.concatenate([dmix0, dmix1], axis=0), jnp.concatenate([dfn0, dfn1], axis=0), dkvg, dfin, dcb,
                dmeta, dscale, dcw, loss8]
    sg_sizes = [p.size for p in sg_parts]
    sg = _all_devices(_flat_rows(sg_parts, LANES), True, name="reduce_small").reshape(-1)
    sg_offs = [0]
    for sz in sg_sizes:
        sg_offs.append(sg_offs[-1] + sz)

    def gsmall(i, shape):
        return sg[sg_offs[i]:sg_offs[i + 1]].reshape(shape)

    g_mix = gsmall(0, mix_norm.shape)
    g_ffn_norm = gsmall(1, ffn_norm.shape)
    g_kv_norm = gsmall(2, kv_norm.shape)
    g_final = gsmall(3, final_norm.shape)
    g_conv_b = gsmall(4, ffn_conv_b.shape)
    csh = d // N_CHIPS
    g_meta = lax.dynamic_slice_in_dim(gsmall(5, (N_META, d)), chip * csh, csh, axis=1)
    g_scale = lax.dynamic_slice_in_dim(gsmall(6, (1, d)), chip * csh, csh, axis=1)
    fsh = f2 // N_CHIPS
    g_conv_w = lax.dynamic_slice_in_dim(gsmall(7, (2, 3, f2)), chip * fsh, fsh, axis=2)
    loss = gsmall(8, (8 * LANES,))[0]

    weights = [meta_tokens, mix_norm, ffn_norm, pool_w, pool_scale, kv_norm, w_kv, w_q, w_o, ffn_w_up, ffn_conv_w,
               ffn_conv_b, ffn_w_down, final_norm]
    grads = [g_meta, g_mix, g_ffn_norm, g_pool_w, g_scale, g_kv_norm, g_w_kv, g_w_q, g_w_o, g_w_up, g_conv_w,
             g_conv_b, g_w_dn, g_final]
    ms = [m_meta_tokens, m_mix_norm, m_ffn_norm, m_pool_w, m_pool_scale, m_kv_norm, m_w_kv, m_w_q, m_w_o,
          m_ffn_w_up, m_ffn_conv_w, m_ffn_conv_b, m_ffn_w_down, m_final_norm]
    vs = [v_meta_tokens, v_mix_norm, v_ffn_norm, v_pool_w, v_pool_scale, v_kv_norm, v_w_kv, v_w_q, v_w_o,
          v_ffn_w_up, v_ffn_conv_w, v_ffn_conv_b, v_ffn_w_down, v_final_norm]
    names = ["meta", "mix", "ffnnorm", "poolw", "poolscale", "kvnorm", "wkv", "wq", "wo", "wup", "convw", "convb",
             "wdown", "final"]
    deltas, new_ms, new_vs = [], [], []
    for w, g, m, v, nm in zip(weights, grads, ms, vs, names):
        cols = w.shape[-1]
        if w.size % (8 * LANES) == 0 and w.ndim == 1:
            cols = LANES
        view = (w.size // cols, cols)
        dl, mn, vn = _adamw(w.reshape(view), g.reshape(view), m.reshape(view), v.reshape(view), name=f"adamw_{nm}")
        deltas.append(dl.reshape(w.shape))
        new_ms.append(mn.reshape(w.shape))
        new_vs.append(vn.reshape(w.shape))

    return (loss, grad_x, *grads, *deltas, *new_ms, *new_vs)
```

```python
import jax
import jax.numpy as jnp
from jax import lax
from jax.experimental import pallas as pl
from jax.experimental.pallas import tpu as pltpu

F32 = jnp.float32
BF16 = jnp.bfloat16

N_META = 16
N_HEADS = 16
HEAD_DIM = 64
POOL_WINDOWS = (2, 4, 8, 16)
N_GROUPS = 4
RMS_EPS = 1e-6
ADAM_LR = 0.001
ADAM_B1 = 0.9
ADAM_B2 = 0.999
ADAM_EPS = 1e-08
ADAM_WD = 0.01
ADAM_STEP = 10

LANES = 128
BLK = 128
ATT_CHUNKS = 3
ATT_BQ = ATT_CHUNKS * BLK
N_CHIPS = 4
N_DEV = 8
VMEM_LIMIT = 56 * 1024 * 1024
MESH = pl.DeviceIdType.MESH


def _cparams(*sem):
    return pltpu.CompilerParams(dimension_semantics=tuple(sem) if sem else None,
                                vmem_limit_bytes=VMEM_LIMIT)


def _tile(n, pref):
    best = None
    t = LANES
    while t <= min(n, pref):
        if n % t == 0:
            best = t
        t += LANES
    assert best is not None, (n, pref)
    return best


def _row_tile(r, pref, mult):
    best = r
    for t in range(mult, min(r, pref) + 1, mult):
        if r % t == 0:
            best = t
    return best


_DIMS = {
    "nn": (((1,), (0,)), ((), ())),
    "nt": (((1,), (1,)), ((), ())),
    "tn": (((0,), (0,)), ((), ())),
}


def _mm(a, b, dims, *, tm, tn, name, out_dtype=F32, res=None):
    if dims == "tn":
        k, m = a.shape
    else:
        m, k = a.shape
    n = b.shape[0] if dims == "nt" else b.shape[1]
    tm = _tile(m, tm)
    tn = _tile(n, tn)
    a_spec = pl.BlockSpec((k, tm), lambda i, j: (0, i)) if dims == "tn" else pl.BlockSpec((tm, k), lambda i, j: (i, 0))
    b_spec = pl.BlockSpec((tn, k), lambda i, j: (j, 0)) if dims == "nt" else pl.BlockSpec((k, tn), lambda i, j: (0, j))
    o_spec = pl.BlockSpec((tm, tn), lambda i, j: (i, j))
    dn = _DIMS[dims]

    def body(*refs):
        if res is None:
            a_ref, b_ref, o_ref = refs
        else:
            a_ref, b_ref, r_ref, o_ref = refs
        acc = lax.dot_general(a_ref[...], b_ref[...], dn, preferred_element_type=F32)
        if res is not None:
            acc = acc + r_ref[...]
        o_ref[...] = acc.astype(out_dtype)

    ins = [a, b] + ([] if res is None else [res])
    specs = [a_spec, b_spec] + ([] if res is None else [o_spec])
    return pl.pallas_call(
        body, name=name, grid=(m // tm, n // tn), in_specs=specs, out_specs=o_spec,
        out_shape=jax.ShapeDtypeStruct((m, n), out_dtype),
        compiler_params=_cparams("parallel", "arbitrary"),
    )(*ins)


def _norm_fwd(x, gains, *, name):
    l, d = x.shape
    tr = _tile(l, 384)
    ng = len(gains)

    def body(*refs):
        x_ref = refs[0]
        g_refs = refs[1:1 + ng]
        o_refs = refs[1 + ng:1 + 2 * ng]
        r_ref = refs[1 + 2 * ng]
        xv = x_ref[...]
        r = lax.rsqrt(jnp.mean(xv * xv, axis=-1, keepdims=True) + RMS_EPS)
        xn = xv * r
        for g_ref, o_ref in zip(g_refs, o_refs):
            o_ref[...] = (xn * g_ref[...]).astype(BF16)
        r_ref[...] = r

    row = pl.BlockSpec((tr, d), lambda i: (i, 0))
    gspec = pl.BlockSpec((1, d), lambda i: (0, 0))
    outs = pl.pallas_call(
        body, name=name, grid=(l // tr,),
        in_specs=[row] + [gspec] * ng,
        out_specs=[row] * ng + [pl.BlockSpec((tr, 1), lambda i: (i, 0))],
        out_shape=[jax.ShapeDtypeStruct((l, d), BF16)] * ng + [jax.ShapeDtypeStruct((l, 1), F32)],
        compiler_params=_cparams("parallel"),
    )(x, *gains)
    return list(outs[:ng]), outs[ng]


def _rows8(v):
    r, d = v.shape
    return jnp.sum(v.reshape(r // 8, 8, d), axis=0)


def _norm_bwd(x, r, gains, dns, dres, *, name):
    l, d = x.shape
    tr = _tile(l, 384)
    ng = len(gains)
    nsteps = l // tr

    def body(*refs):
        x_ref, r_ref, dres_ref = refs[0], refs[1], refs[2]
        g_refs = refs[3:3 + ng]
        dn_refs = refs[3 + ng:3 + 2 * ng]
        dx_ref, dxb_ref = refs[3 + 2 * ng], refs[4 + 2 * ng]
        dg_refs = refs[5 + 2 * ng:5 + 3 * ng]
        acc_refs = refs[5 + 3 * ng:5 + 4 * ng]
        i = pl.program_id(0)

        @pl.when(i == 0)
        def _():
            for acc in acc_refs:
                acc[...] = jnp.zeros_like(acc)

        rv = r_ref[...]
        xn = x_ref[...] * rv
        total = dres_ref[...]
        for g_ref, dn_ref, acc in zip(g_refs, dn_refs, acc_refs):
            dn = dn_ref[...]
            acc[...] += _rows8(dn * xn)
            dxn = dn * g_ref[...]
            total = total + rv * (dxn - xn * jnp.mean(dxn * xn, axis=-1, keepdims=True))
        dx_ref[...] = total
        dxb_ref[...] = total.astype(BF16)

        @pl.when(i == nsteps - 1)
        def _():
            for dg_ref, acc in zip(dg_refs, acc_refs):
                dg_ref[...] = jnp.sum(acc[...], axis=0, keepdims=True)

    row = pl.BlockSpec((tr, d), lambda i: (i, 0))
    gspec = pl.BlockSpec((1, d), lambda i: (0, 0))
    outs = pl.pallas_call(
        body, name=name, grid=(nsteps,),
        in_specs=[row, pl.BlockSpec((tr, 1), lambda i: (i, 0)), row] + [gspec] * ng + [row] * ng,
        out_specs=[row, row] + [gspec] * ng,
        out_shape=[jax.ShapeDtypeStruct((l, d), F32), jax.ShapeDtypeStruct((l, d), BF16)]
        + [jax.ShapeDtypeStruct((1, d), F32)] * ng,
        scratch_shapes=[pltpu.VMEM((8, d), F32)] * ng,
        compiler_params=_cparams("arbitrary"),
    )(x, r, dres, *gains, *dns)
    return outs[0], outs[1], list(outs[2:])


def _loss_bwd(h, gain, tgt, seq, *, name):
    l, d = h.shape
    tr = _tile(l, 384)
    nsteps = l // tr

    def body(h_ref, g_ref, t_ref, loss_ref, dh_ref, dhb_ref, dg_ref, lacc, gacc):
        i = pl.program_id(0)

        @pl.when(i == 0)
        def _():
            lacc[...] = jnp.zeros_like(lacc)
            gacc[...] = jnp.zeros_like(gacc)

        xv = h_ref[...]
        g = g_ref[...]
        r = lax.rsqrt(jnp.mean(xv * xv, axis=-1, keepdims=True) + RMS_EPS)
        xn = xv * r
        rows = i * tr + lax.broadcasted_iota(jnp.int32, (tr, 1), 0)
        valid = (rows >= N_META) & (rows < N_META + seq)
        e = jnp.where(valid, xn * g - t_ref[...], 0.0)
        lacc[...] += _rows8(e * e)
        dy = e * (1.0 / d)
        gacc[...] += _rows8(dy * xn)
        dxn = dy * g
        dx = r * (dxn - xn * jnp.mean(dxn * xn, axis=-1, keepdims=True))
        dh_ref[...] = dx
        dhb_ref[...] = dx.astype(BF16)

        @pl.when(i == nsteps - 1)
        def _():
            loss_ref[...] = jnp.full((8, LANES), 0.5 / d * jnp.sum(lacc[...]), F32)
            dg_ref[...] = jnp.sum(gacc[...], axis=0, keepdims=True)

    row = pl.BlockSpec((tr, d), lambda i: (i, 0))
    gspec = pl.BlockSpec((1, d), lambda i: (0, 0))
    return pl.pallas_call(
        body, name=name, grid=(nsteps,),
        in_specs=[row, gspec, row],
        out_specs=[pl.BlockSpec((8, LANES), lambda i: (0, 0)), row, row, gspec],
        out_shape=[jax.ShapeDtypeStruct((8, LANES), F32), jax.ShapeDtypeStruct((l, d), F32),
                   jax.ShapeDtypeStruct((l, d), BF16), jax.ShapeDtypeStruct((1, d), F32)],
        scratch_shapes=[pltpu.VMEM((8, d), F32), pltpu.VMEM((8, d), F32)],
        compiler_params=_cparams("arbitrary"),
    )(h, gain, tgt)


def _shift_down(v, k, rows):
    return jnp.where(rows >= k, pltpu.roll(v, k, 0), 0.0)


def _shift_up(v, k, rows):
    l = v.shape[0]
    return jnp.where(rows < l - k, pltpu.roll(v, l - k, 0), 0.0)


def _pool_diff(n, w, rows):
    s = n
    for k in (1, 2, 4, 8):
        s = s + jnp.where(k < w, _shift_down(s, k, rows), 0.0)
    cnt = jnp.minimum(rows + 1, w).astype(F32)
    return s / cnt - n, cnt


def _pool_diff_fwd(h, r, gain, *, name):
    l, d = h.shape
    per_group = d // N_GROUPS // LANES

    def body(h_ref, r_ref, g_ref, o_ref):
        w = jnp.left_shift(2, pl.program_id(0) // per_group)
        rows = lax.broadcasted_iota(jnp.int32, (l, 1), 0)
        n = h_ref[...] * r_ref[...] * g_ref[...]
        diff, _ = _pool_diff(n, w, rows)
        o_ref[...] = diff.astype(BF16)

    col = pl.BlockSpec((l, LANES), lambda j: (0, j))
    return pl.pallas_call(
        body, name=name, grid=(d // LANES,),
        in_specs=[col, pl.BlockSpec((l, 1), lambda j: (0, 0)), pl.BlockSpec((1, LANES), lambda j: (0, j))],
        out_specs=col, out_shape=jax.ShapeDtypeStruct((l, d), BF16),
        compiler_params=_cparams("parallel"),
    )(h, r, gain)


def _pool_diff_bwd(ddiff, *, name):
    l, d = ddiff.shape
    per_group = d // N_GROUPS // LANES

    def body(dd_ref, o_ref):
        w = jnp.left_shift(2, pl.program_id(0) // per_group)
        rows = lax.broadcasted_iota(jnp.int32, (l, 1), 0)
        dd = dd_ref[...]
        s = dd / jnp.minimum(rows + 1, w).astype(F32)
        for k in (1, 2, 4, 8):
            s = s + jnp.where(k < w, _shift_up(s, k, rows), 0.0)
        o_ref[...] = s - dd

    col = pl.BlockSpec((l, LANES), lambda j: (0, j))
    return pl.pallas_call(
        body, name=name, grid=(d // LANES,), in_specs=[col], out_specs=col,
        out_shape=jax.ShapeDtypeStruct((l, d), F32), compiler_params=_cparams("parallel"),
    )(ddiff)


def _pool_mix_fwd(h, diff_b, w_pool, scale, *, name):
    l, d = h.shape
    gd = d // N_GROUPS
    tr = _tile(l, 1408)

    def body(h_ref, d_ref, w_ref, s_ref, o_ref):
        y = jnp.dot(d_ref[...], w_ref[0], preferred_element_type=F32)
        o_ref[...] = h_ref[...] + y * s_ref[...]

    blk = pl.BlockSpec((tr, gd), lambda i, g: (i, g))
    vec = pl.BlockSpec((1, gd), lambda i, g: (0, g))
    return pl.pallas_call(
        body, name=name, grid=(l // tr, N_GROUPS),
        in_specs=[blk, blk, pl.BlockSpec((1, gd, gd), lambda i, g: (g, 0, 0)), vec],
        out_specs=blk, out_shape=jax.ShapeDtypeStruct((l, d), F32),
        compiler_params=_cparams("parallel", "parallel"),
    )(h, diff_b, w_pool, scale)


def _pool_mix_bwd(diff_b, w_pool, scale, dh1, *, name):
    l, d = dh1.shape
    gd = d // N_GROUPS
    tr = _tile(l, 1408)
    nsteps = l // tr

    def body(d_ref, w_ref, s_ref, dy_ref, dd_ref, dw_ref, ds_ref, sacc):
        i = pl.program_id(1)

        @pl.when(i == 0)
        def _():
            dw_ref[...] = jnp.zeros_like(dw_ref)
            sacc[...] = jnp.zeros_like(sacc)

        diff_b_, wg, dy = d_ref[...], w_ref[0], dy_ref[...]
        yy = jnp.dot(diff_b_, wg, preferred_element_type=F32)
        sacc[...] += _rows8(dy * yy)
        dyy_b = (dy * s_ref[...]).astype(BF16)
        dw_ref[0] += lax.dot_general(diff_b_, dyy_b, _DIMS["tn"], preferred_element_type=F32)
        dd_ref[...] = lax.dot_general(dyy_b, wg, _DIMS["nt"], preferred_element_type=F32)

        @pl.when(i == nsteps - 1)
        def _():
            ds_ref[...] = jnp.sum(sacc[...], axis=0, keepdims=True)

    blk = pl.BlockSpec((tr, gd), lambda g, i: (i, g))
    vec = pl.BlockSpec((1, gd), lambda g, i: (0, g))
    wspec = pl.BlockSpec((1, gd, gd), lambda g, i: (g, 0, 0))
    return pl.pallas_call(
        body, name=name, grid=(N_GROUPS, nsteps),
        in_specs=[blk, wspec, vec, blk],
        out_specs=[blk, wspec, vec],
        out_shape=[jax.ShapeDtypeStruct((l, d), F32), jax.ShapeDtypeStruct((N_GROUPS, gd, gd), F32),
                   jax.ShapeDtypeStruct((1, d), F32)],
        scratch_shapes=[pltpu.VMEM((8, gd), F32)],
        compiler_params=_cparams("parallel", "arbitrary"),
    )(diff_b, w_pool, scale, dh1)


def _conv(u, w, b, rows):
    return b + w[0:1] * _shift_down(u, 2, rows) + w[1:2] * _shift_down(u, 1, rows) + w[2:3] * u


def _convgate_fwd(u, cw, cb, *, name):
    l, f2 = u.shape
    f = f2 // 2
    tc = _tile(f, LANES)
    nc = f // tc

    def body(ug_ref, uv_ref, wg_ref, wv_ref, bg_ref, bv_ref, a_ref):
        rows = lax.broadcasted_iota(jnp.int32, (l, 1), 0)
        gate = _conv(ug_ref[...], wg_ref[...], bg_ref[...], rows)
        val = _conv(uv_ref[...], wv_ref[...], bv_ref[...], rows)
        a_ref[...] = (gate * jax.nn.sigmoid(gate) * val).astype(BF16)

    def spec(rows_, off):
        return pl.BlockSpec((rows_, tc), lambda j: (0, j + off))

    return pl.pallas_call(
        body, name=name, grid=(nc,),
        in_specs=[spec(l, 0), spec(l, nc), spec(3, 0), spec(3, nc), spec(1, 0), spec(1, nc)],
        out_specs=spec(l, 0), out_shape=jax.ShapeDtypeStruct((l, f), BF16),
        compiler_params=_cparams("parallel"),
    )(u, u, cw, cw, cb, cb)


def _convgate_bwd(u, cw, cb, da, *, name):
    l, f2 = u.shape
    f = f2 // 2
    tc = _tile(f, LANES)
    nc = f // tc

    def body(ug_ref, uv_ref, wg_ref, wv_ref, bg_ref, bv_ref, da_ref, du_ref, dcw_ref, dcb_ref):
        part = pl.program_id(0)
        rows = lax.broadcasted_iota(jnp.int32, (l, 1), 0)
        ug, uv = ug_ref[...], uv_ref[...]
        wg, wv = wg_ref[...], wv_ref[...]
        gate = _conv(ug, wg, bg_ref[...], rows)
        val = _conv(uv, wv, bv_ref[...], rows)
        sg = jax.nn.sigmoid(gate)
        dav = da_ref[...]
        dgate = dav * val * (sg * (1.0 + gate * (1.0 - sg)))
        dval = dav * (gate * sg)
        is_gate = part == 0
        dc = jnp.where(is_gate, dgate, dval)
        uo = jnp.where(is_gate, ug, uv)
        wo = jnp.where(is_gate, wg, wv)
        dcb_ref[...] = jnp.sum(dc, axis=0, keepdims=True)
        dcw_ref[0:1, :] = jnp.sum(dc * _shift_down(uo, 2, rows), axis=0, keepdims=True)
        dcw_ref[1:2, :] = jnp.sum(dc * _shift_down(uo, 1, rows), axis=0, keepdims=True)
        dcw_ref[2:3, :] = jnp.sum(dc * uo, axis=0, keepdims=True)
        du = wo[2:3] * dc + wo[1:2] * _shift_up(dc, 1, rows) + wo[0:1] * _shift_up(dc, 2, rows)
        du_ref[...] = du.astype(BF16)

    def spec(rows_, off):
        return pl.BlockSpec((rows_, tc), lambda p, j: (0, j + off))

    def own(rows_):
        return pl.BlockSpec((rows_, tc), lambda p, j: (0, p * nc + j))

    return pl.pallas_call(
        body, name=name, grid=(2, nc),
        in_specs=[spec(l, 0), spec(l, nc), spec(3, 0), spec(3, nc), spec(1, 0), spec(1, nc), spec(l, 0)],
        out_specs=[own(l), own(3), own(1)],
        out_shape=[jax.ShapeDtypeStruct((l, f2), BF16), jax.ShapeDtypeStruct((3, f2), F32),
                   jax.ShapeDtypeStruct((1, f2), F32)],
        compiler_params=_cparams("parallel", "parallel"),
    )(u, u, cw, cw, cb, cb, da)


def _split_bf16(v):
    hi = v.astype(BF16)
    return hi, (v - hi.astype(F32)).astype(BF16)


def _cumsum_mm(hi, lo, t2):
    return (jnp.dot(hi, t2, preferred_element_type=F32)
            + jnp.dot(lo, t2, preferred_element_type=F32))


def _tri_and_ones(tri_fn):
    row = lax.broadcasted_iota(jnp.int32, (BLK, 2 * BLK), 0)
    col = lax.broadcasted_iota(jnp.int32, (BLK, 2 * BLK), 1)
    return jnp.where((col >= BLK) | tri_fn(row, col), 1.0, 0.0).astype(BF16)


def _logits(z, mask):
    sp = jnp.log(1.0 + jnp.exp(-jnp.abs(z)))
    lb = jnp.minimum(z, 0.0) - sp
    lm = lb - z
    if mask is not None:
        lm = jnp.where(mask, lm, 0.0)
    return lb, lm


def _software_pipeline(stages, n, block_of, state):
    ns = len(stages)
    inflight = [None] * (ns - 1)
    for t in range(ns - 1):
        new = list(inflight)
        for s in range(t, -1, -1):
            y, state = stages[s](block_of(t - s), None if s == 0 else inflight[s - 1], state)
            new[s] = y
        inflight = new

    def steady(i, carry):
        inflight, state = carry
        new = [None] * (ns - 1)
        for s in range(ns - 1, -1, -1):
            y, state = stages[s](block_of(i + ns - 1 - s), None if s == 0 else inflight[s - 1], state)
            if s < ns - 1:
                new[s] = y
        return tuple(new), state

    inflight, state = lax.fori_loop(0, n - (ns - 1), steady, (tuple(inflight), state))
    inflight = list(inflight)
    for e in range(1, ns):
        new = list(inflight)
        for s in range(ns - 1, e - 1, -1):
            y, state = stages[s](block_of(n - 1 + e - s), inflight[s - 1], state)
            if s < ns - 1:
                new[s] = y
        inflight = new
    return state


def _attn_fwd(q, k, v, *, name):
    nh, l, dh = q.shape
    assert l % ATT_BQ == 0
    nq = l // ATT_BQ
    qscale = HEAD_DIM ** -0.5
    chunks = range(ATT_CHUNKS)

    def body(q_ref, k_ref, v_ref, o_ref, lt_ref):
        t_later = _tri_and_ones(lambda r, c: r > c)
        dmask = (lax.broadcasted_iota(jnp.int32, (BLK, BLK), 1)
                 < lax.broadcasted_iota(jnp.int32, (BLK, BLK), 0))

        def logits(qc, c0, mask):
            z = lax.dot_general(qc, k_ref[0, pl.ds(c0, BLK), :], _DIMS["nt"], preferred_element_type=F32)
            lb, lm = _logits(z, mask)
            return (lb,) + _split_bf16(lm)

        def weights(lb, hi, lo, mask, run):
            cs = _cumsum_mm(hi, lo, t_later)
            a = jnp.exp(lb + cs[:, :BLK] + run)
            if mask is not None:
                a = jnp.where(mask, a, 0.0)
            return a.astype(BF16), run + cs[:, BLK:]

        def accumulate(a_b, c0, acc):
            return acc + jnp.dot(a_b, v_ref[0, pl.ds(c0, BLK), :], preferred_element_type=F32)

        def qblock(qb, _):
            r0 = pl.multiple_of(qb * ATT_BQ, ATT_BQ)
            qs = [q_ref[0, pl.ds(r0 + rc * BLK, BLK), :] * jnp.asarray(qscale, BF16) for rc in chunks]
            accs = [jnp.zeros((BLK, dh), F32)] * ATT_CHUNKS
            runs = [jnp.zeros((BLK, BLK), F32)] * ATT_CHUNKS
            for dj in range(ATT_CHUNKS - 1, -1, -1):
                c0 = r0 + dj * BLK
                for rc in range(dj, ATT_CHUNKS):
                    mask = dmask if rc == dj else None
                    a_b, runs[rc] = weights(*logits(qs[rc], c0, mask), mask, runs[rc])
                    accs[rc] = accumulate(a_b, c0, accs[rc])

            def col0(b):
                return pl.multiple_of(r0 - (b + 1) * BLK, BLK)

            def stage_logits(b, _, state):
                return tuple(logits(qs[rc], col0(b), None) for rc in chunks), state

            def stage_weights(b, xs, state):
                accs, runs = state
                out = [weights(*xs[rc], None, runs[rc]) for rc in chunks]
                return tuple(o[0] for o in out), (accs, tuple(o[1] for o in out))

            def stage_acc(b, a_bs, state):
                accs, runs = state
                return None, (tuple(accumulate(a_bs[rc], col0(b), accs[rc]) for rc in chunks), runs)

            def left_region(state):
                return _software_pipeline([stage_logits, stage_weights, stage_acc], qb * ATT_CHUNKS, lambda b: b, state)

            accs, runs = lax.cond(qb > 0, left_region, lambda s: s, (tuple(accs), tuple(runs)))
            for rc in chunks:
                o_ref[0, pl.ds(r0 + rc * BLK, BLK), :] = accs[rc].astype(BF16)
                lt_ref[0, pl.ds(r0 + rc * BLK, BLK), :] = runs[rc]
            return 0

        lax.fori_loop(0, nq, qblock, 0)

    head = pl.BlockSpec((1, l, dh), lambda h: (h, 0, 0))
    return pl.pallas_call(
        body, name=name, grid=(nh,),
        in_specs=[head, head, head],
        out_specs=[head, pl.BlockSpec((1, l, BLK), lambda h: (h, 0, 0))],
        out_shape=[jax.ShapeDtypeStruct((nh, l, dh), BF16), jax.ShapeDtypeStruct((nh, l, BLK), F32)],
        compiler_params=_cparams("parallel"),
    )(q, k, v)


def _attn_bwd(q, k, v, do, ltot, *, name):
    nh, l, dh = q.shape
    assert l % ATT_BQ == 0
    nq = l // ATT_BQ
    qscale = HEAD_DIM ** -0.5
    chunks = range(ATT_CHUNKS)

    def body(q_ref, k_ref, v_ref, do_ref, lt_ref, dq_ref, dk_ref, dv_ref):
        t_incl = _tri_and_ones(lambda r, c: r <= c)
        t_excl = _tri_and_ones(lambda r, c: r < c)
        dmask = (lax.broadcasted_iota(jnp.int32, (BLK, BLK), 1)
                 < lax.broadcasted_iota(jnp.int32, (BLK, BLK), 0))
        dk_ref[...] = jnp.zeros_like(dk_ref)
        dv_ref[...] = jnp.zeros_like(dv_ref)

        def logits(qc, c0, mask):
            z = lax.dot_general(qc, k_ref[0, pl.ds(c0, BLK), :], _DIMS["nt"], preferred_element_type=F32)
            lb, lm = _logits(z, mask)
            return (lb,) + _split_bf16(lm)

        def weights(lb, hi, lo, doc, ltc, c0, mask, pre_lm):
            cs = _cumsum_mm(hi, lo, t_incl)
            da = lax.dot_general(doc, v_ref[0, pl.ds(c0, BLK), :], _DIMS["nt"], preferred_element_type=F32)
            a = jnp.exp(lb + (ltc - pre_lm - cs[:, :BLK]))
            if mask is not None:
                a = jnp.where(mask, a, 0.0)
            dl = a * da
            return (jnp.exp(lb), dl) + _split_bf16(dl) + (a.astype(BF16),), pre_lm + cs[:, BLK:]

        def logit_grad(beta, dl, dl_hi, dl_lo, mask, pre_dl):
            cd = _cumsum_mm(dl_hi, dl_lo, t_excl)
            dz = dl - beta * (dl + pre_dl + cd[:, :BLK])
            if mask is not None:
                dz = jnp.where(mask, dz, 0.0)
            return (dz * qscale).astype(BF16), pre_dl + cd[:, BLK:]

        def key_grads(c0, dz_parts, a_parts, q_rows, do_rows):
            dz_all = dz_parts[0] if len(dz_parts) == 1 else jnp.concatenate(dz_parts, axis=0)
            a_all = a_parts[0] if len(a_parts) == 1 else jnp.concatenate(a_parts, axis=0)
            dk_ref[0, pl.ds(c0, BLK), :] += lax.dot_general(dz_all, q_rows, _DIMS["tn"], preferred_element_type=F32)
            dv_ref[0, pl.ds(c0, BLK), :] += lax.dot_general(a_all, do_rows, _DIMS["tn"], preferred_element_type=F32)

        def qblock(qb, _):
            r0 = pl.multiple_of(qb * ATT_BQ, ATT_BQ)
            q_raw = q_ref[0, pl.ds(r0, ATT_BQ), :]
            do_all = do_ref[0, pl.ds(r0, ATT_BQ), :]
            qs = [q_raw[rc * BLK:(rc + 1) * BLK] * jnp.asarray(qscale, BF16) for rc in chunks]
            dos = [do_all[rc * BLK:(rc + 1) * BLK] for rc in chunks]
            lts = [lt_ref[0, pl.ds(r0 + rc * BLK, BLK), :] for rc in chunks]

            def col0(b):
                return pl.multiple_of(b * BLK, BLK)

            def stage_logits(b, _, state):
                return tuple(logits(qs[rc], col0(b), None) for rc in chunks), state

            def stage_weights(b, xs, state):
                dqs, pls, pds = state
                out = [weights(*xs[rc], dos[rc], lts[rc], col0(b), None, pls[rc]) for rc in chunks]
                return tuple(o[0] for o in out), (dqs, tuple(o[1] for o in out), pds)

            def stage_logit_grad(b, ys, state):
                dqs, pls, pds = state
                out = [logit_grad(*ys[rc][:4], None, pds[rc]) for rc in chunks]
                return tuple((out[rc][0], ys[rc][4]) for rc in chunks), (dqs, pls, tuple(o[1] for o in out))

            def stage_grads(b, ws, state):
                dqs, pls, pds = state
                c0 = col0(b)
                kj = k_ref[0, pl.ds(c0, BLK), :]
                dqs = tuple(dqs[rc] + jnp.dot(ws[rc][0], kj, preferred_element_type=F32) for rc in chunks)
                key_grads(c0, [w[0] for w in ws], [w[1] for w in ws], q_raw, do_all)
                return None, (dqs, pls, pds)

            zero = jnp.zeros((BLK, BLK), F32)
            state = ((jnp.zeros((BLK, dh), F32),) * ATT_CHUNKS, (zero,) * ATT_CHUNKS, (zero,) * ATT_CHUNKS)

            def left_region(state):
                return _software_pipeline([stage_logits, stage_weights, stage_logit_grad, stage_grads],
                                          qb * ATT_CHUNKS, lambda b: b, state)

            dqs, pls, pds = lax.cond(qb > 0, left_region, lambda s: s, state)
            dqs, pls, pds = list(dqs), list(pls), list(pds)
            for dj in chunks:
                c0 = r0 + dj * BLK
                kj = k_ref[0, pl.ds(c0, BLK), :]
                dz_parts, a_parts = [], []
                for rc in range(dj, ATT_CHUNKS):
                    mask = dmask if rc == dj else None
                    ys, pls[rc] = weights(*logits(qs[rc], c0, mask), dos[rc], lts[rc], c0, mask, pls[rc])
                    dz_b, pds[rc] = logit_grad(*ys[:4], mask, pds[rc])
                    a_b = ys[4]
                    dqs[rc] = dqs[rc] + jnp.dot(dz_b, kj, preferred_element_type=F32)
                    dz_parts.append(dz_b)
                    a_parts.append(a_b)
                key_grads(c0, dz_parts, a_parts, q_raw[dj * BLK:], do_all[dj * BLK:])
            for rc in chunks:
                dq_ref[0, pl.ds(r0 + rc * BLK, BLK), :] = dqs[rc]
            return 0

        lax.fori_loop(0, nq, qblock, 0)

    head = pl.BlockSpec((1, l, dh), lambda h: (h, 0, 0))
    return pl.pallas_call(
        body, name=name, grid=(nh,),
        in_specs=[head, head, head, head, pl.BlockSpec((1, l, BLK), lambda h: (h, 0, 0))],
        out_specs=[head, head, head],
        out_shape=[jax.ShapeDtypeStruct((nh, l, dh), F32)] * 3,
        compiler_params=_cparams("parallel"),
    )(q, k, v, do, ltot)


def _add_to_bf16(a, b, *, name):
    n, r, c = a.shape
    tr = _row_tile(r, 1536, 16)
    blk = pl.BlockSpec((1, tr, c), lambda i, j: (i, j, 0))

    def body(a_ref, b_ref, o_ref):
        o_ref[...] = (a_ref[...] + b_ref[...]).astype(BF16)

    return pl.pallas_call(body, name=name, grid=(n, r // tr), in_specs=[blk, blk], out_specs=blk,
                          out_shape=jax.ShapeDtypeStruct(a.shape, BF16),
                          compiler_params=_cparams("parallel", "parallel"))(a, b)


def _sum_slots(p, *, name):
    n, r, c = p.shape
    tr = _row_tile(r, 1536, 16)

    def body(p_ref, o_ref):
        acc = p_ref[0].astype(F32)
        for s in range(1, n):
            acc = acc + p_ref[s].astype(F32)
        o_ref[...] = acc

    return pl.pallas_call(body, name=name, grid=(r // tr,),
                          in_specs=[pl.BlockSpec((n, tr, c), lambda j: (0, j, 0))],
                          out_specs=pl.BlockSpec((tr, c), lambda j: (j, 0)),
                          out_shape=jax.ShapeDtypeStruct((r, c), F32),
                          compiler_params=_cparams("parallel"))(p)


def _adamw(w, g, m, v, *, name):
    r, c = w.shape
    tr = _row_tile(r, 512, 8)
    blk = pl.BlockSpec((tr, c), lambda i: (i, 0))

    def body(w_ref, g_ref, m_ref, v_ref, d_ref, mo_ref, vo_ref):
        gv = g_ref[...]
        mn = ADAM_B1 * m_ref[...] + (1.0 - ADAM_B1) * gv
        vn = ADAM_B2 * v_ref[...] + (1.0 - ADAM_B2) * (gv * gv)
        m_hat = mn / (1.0 - ADAM_B1 ** ADAM_STEP)
        v_hat = vn / (1.0 - ADAM_B2 ** ADAM_STEP)
        d_ref[...] = -ADAM_LR * (m_hat / (jnp.sqrt(v_hat) + ADAM_EPS) + ADAM_WD * w_ref[...])
        mo_ref[...] = mn
        vo_ref[...] = vn

    return pl.pallas_call(body, name=name, grid=(r // tr,), in_specs=[blk] * 4, out_specs=[blk] * 3,
                          out_shape=[jax.ShapeDtypeStruct((r, c), F32)] * 3,
                          compiler_params=_cparams("parallel"))(w, g, m, v)


_ANY = pl.BlockSpec(memory_space=pl.ANY)


def _place():
    x, y, c = lax.axis_index("x"), lax.axis_index("y"), lax.axis_index("c")
    chips = [(1 - x, y), (x, 1 - y), (1 - x, 1 - y)]
    return x, y, c, chips


def _gather_shards(wsh, *, name):
    _, r, cdim = wsh.shape

    def body(w_ref, out_ref, send_sems, recv_sems, local_sem):
        x, y, c, chips = _place()
        s = 2 * x + y
        sibling = (x, y, 1 - c)

        def copy(k, src, dst, to):
            return pltpu.make_async_remote_copy(src_ref=src, dst_ref=dst, send_sem=send_sems.at[k],
                                                recv_sem=recv_sems.at[k], device_id=to, device_id_type=MESH)

        mine = pltpu.make_async_copy(w_ref, out_ref.at[s], local_sem)
        mine.start()
        first = [copy(j, w_ref.at[c], out_ref.at[s, c], (*chip, c)) for j, chip in enumerate(chips)]
        for cp in first:
            cp.start()
        passed = []
        for j, (px, py) in enumerate(chips):
            sp = 2 * px + py
            copy(j, w_ref.at[c], out_ref.at[sp, c], sibling).wait_recv()
            fwd = copy(3 + j, out_ref.at[sp, c], out_ref.at[sp, c], sibling)
            fwd.start()
            passed.append(fwd)
        for j, (px, py) in enumerate(chips):
            sp = 2 * px + py
            copy(3 + j, w_ref.at[c], out_ref.at[sp, 1 - c], sibling).wait_recv()
        for cp in first + passed:
            cp.wait_send()
        mine.wait()

    return pl.pallas_call(
        body, name=name, in_specs=[_ANY], out_specs=_ANY,
        out_shape=jax.ShapeDtypeStruct((N_CHIPS, 2, r, cdim), wsh.dtype),
        scratch_shapes=[pltpu.SemaphoreType.DMA((6,)), pltpu.SemaphoreType.DMA((6,)), pltpu.SemaphoreType.DMA],
    )(wsh)


def _to_sibling(a, *, name):
    def body(a_ref, out_ref, send_sem, recv_sem):
        x, y, c, _ = _place()
        cp = pltpu.make_async_remote_copy(src_ref=a_ref, dst_ref=out_ref, send_sem=send_sem, recv_sem=recv_sem,
                                          device_id=(x, y, 1 - c), device_id_type=MESH)
        cp.start()
        cp.wait()

    return pl.pallas_call(
        body, name=name, in_specs=[_ANY], out_specs=_ANY,
        out_shape=jax.ShapeDtypeStruct(a.shape, a.dtype),
        scratch_shapes=[pltpu.SemaphoreType.DMA, pltpu.SemaphoreType.DMA],
    )(a)


def _scatter_rows(p, *, name):
    def body(p_ref, out_ref, send_sems, recv_sems, local_sem):
        x, y, c, chips = _place()
        s = 2 * x + y
        mine = pltpu.make_async_copy(p_ref.at[s], out_ref.at[s], local_sem)
        mine.start()
        sends = []
        for j, (px, py) in enumerate(chips):
            sp = 2 * px + py
            cp = pltpu.make_async_remote_copy(src_ref=p_ref.at[sp], dst_ref=out_ref.at[s], send_sem=send_sems.at[j],
                                              recv_sem=recv_sems.at[j], device_id=(px, py, c), device_id_type=MESH)
            cp.start()
            sends.append(cp)
        for j, (px, py) in enumerate(chips):
            sp = 2 * px + py
            pltpu.make_async_remote_copy(src_ref=p_ref.at[sp], dst_ref=out_ref.at[sp], send_sem=send_sems.at[j],
                                         recv_sem=recv_sems.at[j], device_id=(px, py, c),
                                         device_id_type=MESH).wait_recv()
        for cp in sends:
            cp.wait_send()
        mine.wait()

    return pl.pallas_call(
        body, name=name, in_specs=[_ANY], out_specs=_ANY,
        out_shape=jax.ShapeDtypeStruct(p.shape, p.dtype),
        scratch_shapes=[pltpu.SemaphoreType.DMA((3,)), pltpu.SemaphoreType.DMA((3,)), pltpu.SemaphoreType.DMA],
    )(p)


def _join_halves(half, *, name):
    def body(h_ref, out_ref, send_sem, recv_sem, local_sem):
        x, y, c, _ = _place()
        mine = pltpu.make_async_copy(h_ref, out_ref.at[c], local_sem)
        mine.start()
        cp = pltpu.make_async_remote_copy(src_ref=h_ref, dst_ref=out_ref.at[c], send_sem=send_sem, recv_sem=recv_sem,
                                          device_id=(x, y, 1 - c), device_id_type=MESH)
        cp.start()
        pltpu.make_async_remote_copy(src_ref=h_ref, dst_ref=out_ref.at[1 - c], send_sem=send_sem, recv_sem=recv_sem,
                                     device_id=(x, y, 1 - c), device_id_type=MESH).wait_recv()
        cp.wait_send()
        mine.wait()

    return pl.pallas_call(
        body, name=name, in_specs=[_ANY], out_specs=_ANY,
        out_shape=jax.ShapeDtypeStruct((2,) + half.shape, half.dtype),
        scratch_shapes=[pltpu.SemaphoreType.DMA, pltpu.SemaphoreType.DMA, pltpu.SemaphoreType.DMA],
    )(half)


def _all_devices(a, reduce, *, name):
    r, cdim = a.shape

    def body(a_ref, out_ref, *scratch):
        if reduce:
            buf, send_sems, recv_sems = scratch
        else:
            buf = out_ref
            send_sems, recv_sems = scratch
        x, y, c, _ = _place()
        me = 4 * x + 2 * y + c
        buf[me] = a_ref[...]
        peers = []
        for k in range(1, N_DEV):
            dx, dy, dc = (k >> 2) & 1, (k >> 1) & 1, k & 1
            peers.append((x ^ dx, y ^ dy, c ^ dc))
        sends = []
        for k, peer in enumerate(peers):
            cp = pltpu.make_async_remote_copy(src_ref=a_ref, dst_ref=buf.at[me], send_sem=send_sems.at[k],
                                              recv_sem=recv_sems.at[k], device_id=peer, device_id_type=MESH)
            cp.start()
            sends.append(cp)
        for k, (px, py, pc) in enumerate(peers):
            pltpu.make_async_remote_copy(src_ref=a_ref, dst_ref=buf.at[4 * px + 2 * py + pc],
                                         send_sem=send_sems.at[k], recv_sem=recv_sems.at[k],
                                         device_id=(px, py, pc), device_id_type=MESH).wait_recv()
        for cp in sends:
            cp.wait_send()
        if reduce:
            acc = buf[0]
            for k in range(1, N_DEV):
                acc = acc + buf[k]
            out_ref[...] = acc

    vm = pl.BlockSpec(memory_space=pltpu.VMEM)
    sems = [pltpu.SemaphoreType.DMA((N_DEV - 1,)), pltpu.SemaphoreType.DMA((N_DEV - 1,))]
    if reduce:
        out_shape = jax.ShapeDtypeStruct((r, cdim), F32)
        scratch = [pltpu.VMEM((N_DEV, r, cdim), F32)] + sems
    else:
        out_shape = jax.ShapeDtypeStruct((N_DEV, r, cdim), F32)
        scratch = sems
    return pl.pallas_call(
        body, name=name, in_specs=[vm], out_specs=vm, out_shape=out_shape, scratch_shapes=scratch,
        compiler_params=pltpu.CompilerParams(vmem_limit_bytes=VMEM_LIMIT),
    )(a)


def _flat_rows(parts, cols):
    flat = jnp.concatenate([p.reshape(-1) for p in parts])
    padded = -(-flat.size // (8 * cols)) * (8 * cols)
    return jnp.pad(flat, (0, padded - flat.size)).reshape(-1, cols)


def kernel(x, meta_tokens, mix_norm, ffn_norm, pool_w, pool_scale, kv_norm, w_kv, w_q, w_o, ffn_w_up, ffn_conv_w, ffn_conv_b, ffn_w_down, final_norm, loss_target, m_meta_tokens, m_mix_norm, m_ffn_norm, m_pool_w, m_pool_scale, m_kv_norm, m_w_kv, m_w_q, m_w_o, m_ffn_w_up, m_ffn_conv_w, m_ffn_conv_b, m_ffn_w_down, m_final_norm, v_meta_tokens, v_mix_norm, v_ffn_norm, v_pool_w, v_pool_scale, v_kv_norm, v_w_kv, v_w_q, v_w_o, v_ffn_w_up, v_ffn_conv_w, v_ffn_conv_b, v_ffn_w_down, v_final_norm):
    seq, d = x.shape[1], x.shape[2]
    l_real = N_META + seq
    lp = -(-l_real // ATT_BQ) * ATT_BQ
    f2 = ffn_w_up.shape[2] * N_CHIPS
    f = f2 // 2
    gd = d // N_GROUPS
    chip = 2 * lax.axis_index("x") + lax.axis_index("y")
    core = lax.axis_index("c")

    big_parts = [pool_w[0], w_kv, w_q[0], w_o[0], ffn_w_up, ffn_w_down]
    sizes = [p.size for p in big_parts]
    n_big = sum(sizes)
    cw = 1024
    assert n_big % (2 * 16 * cw) == 0
    rows_half = n_big // (2 * cw)
    wsh = _flat_rows([p.astype(BF16) for p in big_parts], cw).reshape(2, rows_half, cw)
    wall = _gather_shards(wsh, name="gather_weights").reshape(N_CHIPS, n_big)
    offs = [0]
    for sz in sizes:
        offs.append(offs[-1] + sz)

    def big(i, shape):
        return wall[:, offs[i]:offs[i + 1]].reshape((N_CHIPS,) + shape)

    wp_b = big(0, (N_GROUPS, gd // N_CHIPS, gd)).transpose(1, 0, 2, 3).reshape(N_GROUPS, gd, gd)
    wkv_b = big(1, (d, 2 * d // N_CHIPS)).transpose(1, 0, 2).reshape(d, 2 * d)
    wq_b = big(2, (d // N_CHIPS, d)).reshape(d, d)
    wo_b = big(3, (d // N_CHIPS, d)).reshape(d, d)
    wup_b = big(4, (2, d, f2 // N_CHIPS)).transpose(1, 2, 0, 3).reshape(2, d, f2)
    wdn_b = big(5, (2, f // N_CHIPS, d)).transpose(1, 0, 2, 3).reshape(2, f, d)

    small_parts = [meta_tokens, pool_scale, ffn_conv_w]
    ssizes = [p.size for p in small_parts]
    small = _flat_rows(small_parts, LANES)
    sall = _all_devices(small, False, name="gather_small")[::2].reshape(N_CHIPS, -1)
    meta_f = sall[:, :ssizes[0]].reshape(N_CHIPS, N_META, d // N_CHIPS).transpose(1, 0, 2).reshape(N_META, d)
    scale_f = sall[:, ssizes[0]:ssizes[0] + ssizes[1]].reshape(1, d)
    cw_f = sall[:, ssizes[0] + ssizes[1]:sum(ssizes)].reshape(N_CHIPS, 2, 3, f2 // N_CHIPS).transpose(1, 2, 0, 3).reshape(2, 3, f2)

    mix0, mix1 = mix_norm[0:1], mix_norm[1:2]
    fn0, fn1 = ffn_norm[0:1], ffn_norm[1:2]
    kvn_g = kv_norm.reshape(1, d)
    fin_g = final_norm.reshape(1, d)

    pad = lp - l_real
    h0 = jnp.concatenate([meta_f, x[0], jnp.zeros((pad, d), F32)], axis=0)
    tgt = jnp.pad(loss_target[0], ((N_META, pad), (0, 0)))

    _, r0 = _norm_fwd(h0, [mix0], name="norm_h0")
    diff_b = _pool_diff_fwd(h0, r0, mix0, name="pool_diff")
    h1 = _pool_mix_fwd(h0, diff_b, wp_b, scale_f, name="pool_mix")

    def ffn_fwd(h_in, layer, tag):
        (fb,), rf = _norm_fwd(h_in, [ffn_norm[layer:layer + 1]], name=f"norm_ffn{tag}")
        u = _mm(fb, wup_b[layer], "nn", tm=1408, tn=512, name=f"ffn_up{tag}")
        a = _convgate_fwd(u, cw_f[layer], ffn_conv_b[layer:layer + 1], name=f"convgate{tag}")
        h_out = _mm(a, wdn_b[layer], "nn", tm=384, tn=1024, res=h_in, name=f"ffn_down{tag}")
        return h_out, (fb, rf, u, a)

    h2, ffn0_saved = ffn_fwd(h1, 0, "0")
    (kvn_b, n1_b), r2 = _norm_fwd(h2, [kvn_g, mix1], name="norm_h2")
    kv_b = _mm(kvn_b, wkv_b, "nn", tm=1408, tn=512, out_dtype=BF16, name="kv_proj")
    q_b = _mm(n1_b, wq_b, "nn", tm=1408, tn=512, out_dtype=BF16, name="q_proj")

    def heads(t):
        return t.reshape(lp, N_HEADS, HEAD_DIM).transpose(1, 0, 2)

    def unheads(t):
        return t.transpose(1, 0, 2).reshape(lp, N_HEADS * HEAD_DIM)

    qh, kh, vh = heads(q_b), heads(kv_b[:, :d]), heads(kv_b[:, d:])
    oh, ltot = _attn_fwd(qh, kh, vh, name="attn_fwd")
    o_b = unheads(oh)
    h3 = _mm(o_b, wo_b, "nn", tm=384, tn=1024, res=h2, name="o_proj")
    h4, ffn1_saved = ffn_fwd(h3, 1, "1")

    loss8, dh4, dh4_b, dfin = _loss_bwd(h4, fin_g, tgt, seq, name="loss")

    def ffn_bwd(h_in, layer, saved, dh_out, dh_out_b, tag):
        fb, rf, u, a = saved
        dwdn = _mm(a, dh_out_b, "tn", tm=256, tn=1024, name=f"d_wdown{tag}")
        da = _mm(dh_out_b, wdn_b[layer], "nt", tm=384, tn=f, name=f"d_act{tag}")
        du_b, dcw, dcb = _convgate_bwd(u, cw_f[layer], ffn_conv_b[layer:layer + 1], da, name=f"convgate_bwd{tag}")
        dwup = _mm(fb, du_b, "tn", tm=1024, tn=512, name=f"d_wup{tag}")
        df = _mm(du_b, wup_b[layer], "nt", tm=384, tn=1024, name=f"d_ffn_in{tag}")
        dh_in, dh_in_b, (dfn,) = _norm_bwd(h_in, rf, [ffn_norm[layer:layer + 1]], [df], dh_out,
                                          name=f"norm_ffn_bwd{tag}")
        return dh_in, dh_in_b, dwup, dwdn, dcw, dcb, dfn

    dh3, dh3_b, dwup1, dwdn1, dcw1, dcb1, dfn1 = ffn_bwd(h3, 1, ffn1_saved, dh4, dh4_b, "1")

    dwo = _mm(o_b, dh3_b, "tn", tm=1024, tn=512, name="d_wo")
    do_b = _mm(dh3_b, wo_b, "nt", tm=384, tn=1024, out_dtype=BF16, name="d_attn_out")
    dqh, dkh, dvh = _attn_bwd(qh, kh, vh, heads(do_b), ltot, name="attn_bwd")
    dq_b = unheads(dqh).astype(BF16)
    dkv_b = jnp.concatenate([unheads(dkh), unheads(dvh)], axis=1).astype(BF16)
    dwq = _mm(n1_b, dq_b, "tn", tm=1024, tn=512, name="d_wq")
    dwkv = _mm(kvn_b, dkv_b, "tn", tm=1024, tn=512, name="d_wkv")
    dn1 = _mm(dq_b, wq_b, "nt", tm=384, tn=1024, name="d_n1")
    dkvn = _mm(dkv_b, wkv_b, "nt", tm=384, tn=1024, name="d_kvn")
    dh2, dh2_b, (dmix1, dkvg) = _norm_bwd(h2, r2, [mix1, kvn_g], [dn1, dkvn], dh3, name="norm_h2_bwd")

    dh1, dh1_b, dwup0, dwdn0, dcw0, dcb0, dfn0 = ffn_bwd(h1, 0, ffn0_saved, dh2, dh2_b, "0")

    ddiff, dwp, dscale = _pool_mix_bwd(diff_b, wp_b, scale_f, dh1, name="pool_mix_bwd")
    dn0 = _pool_diff_bwd(ddiff, name="pool_diff_bwd")
    dh0, _, (dmix0,) = _norm_bwd(h0, r0, [mix0], [dn0], dh1, name="norm_h0_bwd")

    grad_x = dh0[N_META:l_real][None]
    dmeta = dh0[:N_META]

    dwup = jnp.stack([dwup0, dwup1])
    dwdn = jnp.stack([dwdn0, dwdn1])
    g_parts = [
        dwp.reshape(N_GROUPS, N_CHIPS, gd // N_CHIPS, gd).transpose(1, 0, 2, 3),
        dwkv.reshape(d, N_CHIPS, 2 * d // N_CHIPS).transpose(1, 0, 2),
        dwq.reshape(N_CHIPS, d // N_CHIPS, d),
        dwo.reshape(N_CHIPS, d // N_CHIPS, d),
        dwup.reshape(2, d, N_CHIPS, f2 // N_CHIPS).transpose(2, 0, 1, 3),
        dwdn.reshape(2, N_CHIPS, f // N_CHIPS, d).transpose(1, 0, 2, 3),
    ]
    g_all = jnp.concatenate([p.reshape(N_CHIPS, -1) for p in g_parts], axis=1).reshape(N_CHIPS, 2, rows_half, cw)
    g_mine = lax.dynamic_index_in_dim(g_all, core, axis=1, keepdims=False)
    g_other = lax.dynamic_index_in_dim(g_all, 1 - core, axis=1, keepdims=False)
    from_sibling = _to_sibling(g_other, name="grads_to_sibling")
    chip_part = _add_to_bf16(g_mine, from_sibling, name="grads_chip_sum")
    slots = _scatter_rows(chip_part, name="grads_scatter")
    half = _sum_slots(slots, name="grads_sum")
    g_shard = _join_halves(half, name="grads_join").reshape(n_big)

    def gbig(i, shape):
        return g_shard[offs[i]:offs[i + 1]].reshape(shape)

    g_pool_w = gbig(0, pool_w.shape)
    g_w_kv = gbig(1, w_kv.shape)
    g_w_q = gbig(2, w_q.shape)
    g_w_o = gbig(3, w_o.shape)
    g_w_up = gbig(4, ffn_w_up.shape)
    g_w_dn = gbig(5, ffn_w_down.shape)

    dcw = jnp.stack([dcw0, dcw1])
    dcb = jnp.concatenate([dcb0, dcb1], axis=0)
    sg_parts = [jnp.concatenate([dmix0, dmix1], axis=0), jnp.concatenate([dfn0, dfn1], axis=0), dkvg, dfin, dcb,
                dmeta, dscale, dcw, loss8]
    sg_sizes = [p.size for p in sg_parts]
    sg = _all_devices(_flat_rows(sg_parts, LANES), True, name="reduce_small").reshape(-1)
    sg_offs = [0]
    for sz in sg_sizes:
        sg_offs.append(sg_offs[-1] + sz)

    def gsmall(i, shape):
        return sg[sg_offs[i]:sg_offs[i + 1]].reshape(shape)

    g_mix = gsmall(0, mix_norm.shape)
    g_ffn_norm = gsmall(1, ffn_norm.shape)
    g_kv_norm = gsmall(2, kv_norm.shape)
    g_final = gsmall(3, final_norm.shape)
    g_conv_b = gsmall(4, ffn_conv_b.shape)
    csh = d // N_CHIPS
    g_meta = lax.dynamic_slice_in_dim(gsmall(5, (N_META, d)), chip * csh, csh, axis=1)
    g_scale = lax.dynamic_slice_in_dim(gsmall(6, (1, d)), chip * csh, csh, axis=1)
    fsh = f2 // N_CHIPS
    g_conv_w = lax.dynamic_slice_in_dim(gsmall(7, (2, 3, f2)), chip * fsh, fsh, axis=2)
    loss = gsmall(8, (8 * LANES,))[0]

    weights = [meta_tokens, mix_norm, ffn_norm, pool_w, pool_scale, kv_norm, w_kv, w_q, w_o, ffn_w_up, ffn_conv_w,
               ffn_conv_b, ffn_w_down, final_norm]
    grads = [g_meta, g_mix, g_ffn_norm, g_pool_w, g_scale, g_kv_norm, g_w_kv, g_w_q, g_w_o, g_w_up, g_conv_w,
             g_conv_b, g_w_dn, g_final]
    ms = [m_meta_tokens, m_mix_norm, m_ffn_norm, m_pool_w, m_pool_scale, m_kv_norm, m_w_kv, m_w_q, m_w_o,
          m_ffn_w_up, m_ffn_conv_w, m_ffn_conv_b, m_ffn_w_down, m_final_norm]
    vs = [v_meta_tokens, v_mix_norm, v_ffn_norm, v_pool_w, v_pool_scale, v_kv_norm, v_w_kv, v_w_q, v_w_o,
          v_ffn_w_up, v_ffn_conv_w, v_ffn_conv_b, v_ffn_w_down, v_final_norm]
    names = ["meta", "mix", "ffnnorm", "poolw", "poolscale", "kvnorm", "wkv", "wq", "wo", "wup", "convw", "convb",
             "wdown", "final"]
    deltas, new_ms, new_vs = [], [], []
    for w, g, m, v, nm in zip(weights, grads, ms, vs, names):
        cols = w.shape[-1]
        if w.size % (8 * LANES) == 0 and w.ndim == 1:
            cols = LANES
        view = (w.size // cols, cols)
        dl, mn, vn = _adamw(w.reshape(view), g.reshape(view), m.reshape(view), v.reshape(view), name=f"adamw_{nm}")
        deltas.append(dl.reshape(w.shape))
        new_ms.append(mn.reshape(w.shape))
        new_vs.append(vn.reshape(w.shape))

    return (loss, grad_x, *grads, *deltas, *new_ms, *new_vs)
```

```python
import jax
import jax.numpy as jnp
from jax import lax
from jax.experimental import pallas as pl
from jax.experimental.pallas import tpu as pltpu

F32 = jnp.float32
BF16 = jnp.bfloat16

N_META = 16
N_HEADS = 16
HEAD_DIM = 64
POOL_WINDOWS = (2, 4, 8, 16)
N_GROUPS = 4
RMS_EPS = 1e-6
ADAM_LR = 0.001
ADAM_B1 = 0.9
ADAM_B2 = 0.999
ADAM_EPS = 1e-08
ADAM_WD = 0.01
ADAM_STEP = 10

LANES = 128
BLK = 128
ATT_CHUNKS = 3
ATT_BQ = ATT_CHUNKS * BLK
CONV_ROWS = 32
N_CHIPS = 4
N_DEV = 8
VMEM_LIMIT = 56 * 1024 * 1024
MESH = pl.DeviceIdType.MESH


def _cparams(*sem):
    return pltpu.CompilerParams(dimension_semantics=tuple(sem) if sem else None,
                                vmem_limit_bytes=VMEM_LIMIT)


def _tile(n, pref):
    best = None
    t = LANES
    while t <= min(n, pref):
        if n % t == 0:
            best = t
        t += LANES
    assert best is not None, (n, pref)
    return best


def _row_tile(r, pref, mult):
    best = r
    for t in range(mult, min(r, pref) + 1, mult):
        if r % t == 0:
            best = t
    return best


_DIMS = {
    "nn": (((1,), (0,)), ((), ())),
    "nt": (((1,), (1,)), ((), ())),
    "tn": (((0,), (0,)), ((), ())),
}


def _mm(a, b, dims, *, tm, tn, name, out_dtype=F32, res=None):
    if dims == "tn":
        k, m = a.shape
    else:
        m, k = a.shape
    n = b.shape[0] if dims == "nt" else b.shape[1]
    tm = _tile(m, tm)
    tn = _tile(n, tn)
    a_spec = pl.BlockSpec((k, tm), lambda i, j: (0, i)) if dims == "tn" else pl.BlockSpec((tm, k), lambda i, j: (i, 0))
    b_spec = pl.BlockSpec((tn, k), lambda i, j: (j, 0)) if dims == "nt" else pl.BlockSpec((k, tn), lambda i, j: (0, j))
    o_spec = pl.BlockSpec((tm, tn), lambda i, j: (i, j))
    dn = _DIMS[dims]

    def body(*refs):
        if res is None:
            a_ref, b_ref, o_ref = refs
        else:
            a_ref, b_ref, r_ref, o_ref = refs
        acc = lax.dot_general(a_ref[...], b_ref[...], dn, preferred_element_type=F32)
        if res is not None:
            acc = acc + r_ref[...]
        o_ref[...] = acc.astype(out_dtype)

    ins = [a, b] + ([] if res is None else [res])
    specs = [a_spec, b_spec] + ([] if res is None else [o_spec])
    return pl.pallas_call(
        body, name=name, grid=(m // tm, n // tn), in_specs=specs, out_specs=o_spec,
        out_shape=jax.ShapeDtypeStruct((m, n), out_dtype),
        compiler_params=_cparams("parallel", "arbitrary"),
    )(*ins)


def _norm_fwd(x, gains, *, name):
    l, d = x.shape
    tr = _tile(l, 384)
    ng = len(gains)

    def body(*refs):
        x_ref = refs[0]
        g_refs = refs[1:1 + ng]
        o_refs = refs[1 + ng:1 + 2 * ng]
        r_ref = refs[1 + 2 * ng]
        xv = x_ref[...]
        r = lax.rsqrt(jnp.mean(xv * xv, axis=-1, keepdims=True) + RMS_EPS)
        xn = xv * r
        for g_ref, o_ref in zip(g_refs, o_refs):
            o_ref[...] = (xn * g_ref[...]).astype(BF16)
        r_ref[...] = r

    row = pl.BlockSpec((tr, d), lambda i: (i, 0))
    gspec = pl.BlockSpec((1, d), lambda i: (0, 0))
    outs = pl.pallas_call(
        body, name=name, grid=(l // tr,),
        in_specs=[row] + [gspec] * ng,
        out_specs=[row] * ng + [pl.BlockSpec((tr, 1), lambda i: (i, 0))],
        out_shape=[jax.ShapeDtypeStruct((l, d), BF16)] * ng + [jax.ShapeDtypeStruct((l, 1), F32)],
        compiler_params=_cparams("parallel"),
    )(x, *gains)
    return list(outs[:ng]), outs[ng]


def _rows8(v):
    r, d = v.shape
    return jnp.sum(v.reshape(r // 8, 8, d), axis=0)


def _norm_bwd(x, r, gains, dns, dres, *, name):
    l, d = x.shape
    tr = _tile(l, 384)
    ng = len(gains)
    nsteps = l // tr

    def body(*refs):
        x_ref, r_ref, dres_ref = refs[0], refs[1], refs[2]
        g_refs = refs[3:3 + ng]
        dn_refs = refs[3 + ng:3 + 2 * ng]
        dx_ref, dxb_ref = refs[3 + 2 * ng], refs[4 + 2 * ng]
        dg_refs = refs[5 + 2 * ng:5 + 3 * ng]
        acc_refs = refs[5 + 3 * ng:5 + 4 * ng]
        i = pl.program_id(0)

        @pl.when(i == 0)
        def _():
            for acc in acc_refs:
                acc[...] = jnp.zeros_like(acc)

        rv = r_ref[...]
        xn = x_ref[...] * rv
        total = dres_ref[...]
        for g_ref, dn_ref, acc in zip(g_refs, dn_refs, acc_refs):
            dn = dn_ref[...]
            acc[...] += _rows8(dn * xn)
            dxn = dn * g_ref[...]
            total = total + rv * (dxn - xn * jnp.mean(dxn * xn, axis=-1, keepdims=True))
        dx_ref[...] = total
        dxb_ref[...] = total.astype(BF16)

        @pl.when(i == nsteps - 1)
        def _():
            for dg_ref, acc in zip(dg_refs, acc_refs):
                dg_ref[...] = jnp.sum(acc[...], axis=0, keepdims=True)

    row = pl.BlockSpec((tr, d), lambda i: (i, 0))
    gspec = pl.BlockSpec((1, d), lambda i: (0, 0))
    outs = pl.pallas_call(
        body, name=name, grid=(nsteps,),
        in_specs=[row, pl.BlockSpec((tr, 1), lambda i: (i, 0)), row] + [gspec] * ng + [row] * ng,
        out_specs=[row, row] + [gspec] * ng,
        out_shape=[jax.ShapeDtypeStruct((l, d), F32), jax.ShapeDtypeStruct((l, d), BF16)]
        + [jax.ShapeDtypeStruct((1, d), F32)] * ng,
        scratch_shapes=[pltpu.VMEM((8, d), F32)] * ng,
        compiler_params=_cparams("arbitrary"),
    )(x, r, dres, *gains, *dns)
    return outs[0], outs[1], list(outs[2:])


def _loss_bwd(h, gain, tgt, seq, *, name):
    l, d = h.shape
    tr = _tile(l, 384)
    nsteps = l // tr

    def body(h_ref, g_ref, t_ref, loss_ref, dh_ref, dhb_ref, dg_ref, lacc, gacc):
        i = pl.program_id(0)

        @pl.when(i == 0)
        def _():
            lacc[...] = jnp.zeros_like(lacc)
            gacc[...] = jnp.zeros_like(gacc)

        xv = h_ref[...]
        g = g_ref[...]
        r = lax.rsqrt(jnp.mean(xv * xv, axis=-1, keepdims=True) + RMS_EPS)
        xn = xv * r
        rows = i * tr + lax.broadcasted_iota(jnp.int32, (tr, 1), 0)
        valid = (rows >= N_META) & (rows < N_META + seq)
        e = jnp.where(valid, xn * g - t_ref[...], 0.0)
        lacc[...] += _rows8(e * e)
        dy = e * (1.0 / d)
        gacc[...] += _rows8(dy * xn)
        dxn = dy * g
        dx = r * (dxn - xn * jnp.mean(dxn * xn, axis=-1, keepdims=True))
        dh_ref[...] = dx
        dhb_ref[...] = dx.astype(BF16)

        @pl.when(i == nsteps - 1)
        def _():
            loss_ref[...] = jnp.full((8, LANES), 0.5 / d * jnp.sum(lacc[...]), F32)
            dg_ref[...] = jnp.sum(gacc[...], axis=0, keepdims=True)

    row = pl.BlockSpec((tr, d), lambda i: (i, 0))
    gspec = pl.BlockSpec((1, d), lambda i: (0, 0))
    return pl.pallas_call(
        body, name=name, grid=(nsteps,),
        in_specs=[row, gspec, row],
        out_specs=[pl.BlockSpec((8, LANES), lambda i: (0, 0)), row, row, gspec],
        out_shape=[jax.ShapeDtypeStruct((8, LANES), F32), jax.ShapeDtypeStruct((l, d), F32),
                   jax.ShapeDtypeStruct((l, d), BF16), jax.ShapeDtypeStruct((1, d), F32)],
        scratch_shapes=[pltpu.VMEM((8, d), F32), pltpu.VMEM((8, d), F32)],
        compiler_params=_cparams("arbitrary"),
    )(h, gain, tgt)


def _shift_down(v, k, rows):
    return jnp.where(rows >= k, pltpu.roll(v, k, 0), 0.0)


def _shift_up(v, k, rows):
    l = v.shape[0]
    return jnp.where(rows < l - k, pltpu.roll(v, l - k, 0), 0.0)


def _pool_diff(n, w, rows):
    s = n
    for k in (1, 2, 4, 8):
        s = s + jnp.where(k < w, _shift_down(s, k, rows), 0.0)
    cnt = jnp.minimum(rows + 1, w).astype(F32)
    return s / cnt - n, cnt


def _pool_diff_fwd(h, r, gain, *, name):
    l, d = h.shape
    per_group = d // N_GROUPS // LANES

    def body(h_ref, r_ref, g_ref, o_ref):
        w = jnp.left_shift(2, pl.program_id(0) // per_group)
        rows = lax.broadcasted_iota(jnp.int32, (l, 1), 0)
        n = h_ref[...] * r_ref[...] * g_ref[...]
        diff, _ = _pool_diff(n, w, rows)
        o_ref[...] = diff.astype(BF16)

    col = pl.BlockSpec((l, LANES), lambda j: (0, j))
    return pl.pallas_call(
        body, name=name, grid=(d // LANES,),
        in_specs=[col, pl.BlockSpec((l, 1), lambda j: (0, 0)), pl.BlockSpec((1, LANES), lambda j: (0, j))],
        out_specs=col, out_shape=jax.ShapeDtypeStruct((l, d), BF16),
        compiler_params=_cparams("parallel"),
    )(h, r, gain)


def _pool_diff_bwd(ddiff, *, name):
    l, d = ddiff.shape
    per_group = d // N_GROUPS // LANES

    def body(dd_ref, o_ref):
        w = jnp.left_shift(2, pl.program_id(0) // per_group)
        rows = lax.broadcasted_iota(jnp.int32, (l, 1), 0)
        dd = dd_ref[...]
        s = dd / jnp.minimum(rows + 1, w).astype(F32)
        for k in (1, 2, 4, 8):
            s = s + jnp.where(k < w, _shift_up(s, k, rows), 0.0)
        o_ref[...] = s - dd

    col = pl.BlockSpec((l, LANES), lambda j: (0, j))
    return pl.pallas_call(
        body, name=name, grid=(d // LANES,), in_specs=[col], out_specs=col,
        out_shape=jax.ShapeDtypeStruct((l, d), F32), compiler_params=_cparams("parallel"),
    )(ddiff)


def _pool_mix_fwd(h, diff_b, w_pool, scale, *, name):
    l, d = h.shape
    gd = d // N_GROUPS
    tr = _tile(l, 1408)

    def body(h_ref, d_ref, w_ref, s_ref, o_ref):
        y = jnp.dot(d_ref[...], w_ref[0], preferred_element_type=F32)
        o_ref[...] = h_ref[...] + y * s_ref[...]

    blk = pl.BlockSpec((tr, gd), lambda i, g: (i, g))
    vec = pl.BlockSpec((1, gd), lambda i, g: (0, g))
    return pl.pallas_call(
        body, name=name, grid=(l // tr, N_GROUPS),
        in_specs=[blk, blk, pl.BlockSpec((1, gd, gd), lambda i, g: (g, 0, 0)), vec],
        out_specs=blk, out_shape=jax.ShapeDtypeStruct((l, d), F32),
        compiler_params=_cparams("parallel", "parallel"),
    )(h, diff_b, w_pool, scale)


def _pool_mix_bwd(diff_b, w_pool, scale, dh1, *, name):
    l, d = dh1.shape
    gd = d // N_GROUPS
    tr = _tile(l, 1408)
    nsteps = l // tr

    def body(d_ref, w_ref, s_ref, dy_ref, dd_ref, dw_ref, ds_ref, sacc):
        i = pl.program_id(1)

        @pl.when(i == 0)
        def _():
            dw_ref[...] = jnp.zeros_like(dw_ref)
            sacc[...] = jnp.zeros_like(sacc)

        diff_b_, wg, dy = d_ref[...], w_ref[0], dy_ref[...]
        yy = jnp.dot(diff_b_, wg, preferred_element_type=F32)
        sacc[...] += _rows8(dy * yy)
        dyy_b = (dy * s_ref[...]).astype(BF16)
        dw_ref[0] += lax.dot_general(diff_b_, dyy_b, _DIMS["tn"], preferred_element_type=F32)
        dd_ref[...] = lax.dot_general(dyy_b, wg, _DIMS["nt"], preferred_element_type=F32)

        @pl.when(i == nsteps - 1)
        def _():
            ds_ref[...] = jnp.sum(sacc[...], axis=0, keepdims=True)

    blk = pl.BlockSpec((tr, gd), lambda g, i: (i, g))
    vec = pl.BlockSpec((1, gd), lambda g, i: (0, g))
    wspec = pl.BlockSpec((1, gd, gd), lambda g, i: (g, 0, 0))
    return pl.pallas_call(
        body, name=name, grid=(N_GROUPS, nsteps),
        in_specs=[blk, wspec, vec, blk],
        out_specs=[blk, wspec, vec],
        out_shape=[jax.ShapeDtypeStruct((l, d), F32), jax.ShapeDtypeStruct((N_GROUPS, gd, gd), F32),
                   jax.ShapeDtypeStruct((1, d), F32)],
        scratch_shapes=[pltpu.VMEM((8, gd), F32)],
        compiler_params=_cparams("parallel", "arbitrary"),
    )(diff_b, w_pool, scale, dh1)


def _conv_chunk(cur, prev, w, b, rowi):
    s1 = jnp.where(rowi < 1, pltpu.roll(prev, 1, 0), pltpu.roll(cur, 1, 0))
    s2 = jnp.where(rowi < 2, pltpu.roll(prev, 2, 0), pltpu.roll(cur, 2, 0))
    return b + w[0] * s2 + w[1] * s1 + w[2] * cur, s1, s2


def _bcast_rows(ref, rows):
    v = ref[...]
    return [jnp.broadcast_to(v[k:k + 1], (rows, v.shape[1])) for k in range(v.shape[0])]


def _convgate_fwd(u, cw, cb, *, name):
    l, f2 = u.shape
    f = f2 // 2
    tc = _tile(f, LANES)
    nc = f // tc
    rows = CONV_ROWS
    assert l % rows == 0

    def body(ug_ref, uv_ref, wg_ref, wv_ref, bg_ref, bv_ref, a_ref):
        rowi = lax.broadcasted_iota(jnp.int32, (rows, tc), 0)
        wg, wv = _bcast_rows(wg_ref, rows), _bcast_rows(wv_ref, rows)
        bg, bv = _bcast_rows(bg_ref, rows)[0], _bcast_rows(bv_ref, rows)[0]

        def chunk(i, carry):
            pg, pv = carry
            r = pl.multiple_of(i * rows, rows)
            cg, cv = ug_ref[pl.ds(r, rows), :], uv_ref[pl.ds(r, rows), :]
            gate, _, _ = _conv_chunk(cg, pg, wg, bg, rowi)
            val, _, _ = _conv_chunk(cv, pv, wv, bv, rowi)
            a_ref[pl.ds(r, rows), :] = (gate * jax.nn.sigmoid(gate) * val).astype(BF16)
            return cg, cv

        zero = jnp.zeros((rows, tc), F32)
        lax.fori_loop(0, l // rows, chunk, (zero, zero))

    def spec(rows_, off):
        return pl.BlockSpec((rows_, tc), lambda j: (0, j + off))

    return pl.pallas_call(
        body, name=name, grid=(nc,),
        in_specs=[spec(l, 0), spec(l, nc), spec(3, 0), spec(3, nc), spec(1, 0), spec(1, nc)],
        out_specs=spec(l, 0), out_shape=jax.ShapeDtypeStruct((l, f), BF16),
        compiler_params=_cparams("parallel"),
    )(u, u, cw, cw, cb, cb)


def _convgate_bwd(u, cw, cb, da, *, name):
    l, f2 = u.shape
    f = f2 // 2
    tc = _tile(f, LANES)
    nc = f // tc
    rows = CONV_ROWS
    assert l % rows == 0
    nchunks = l // rows

    def body(ug_ref, uv_ref, wg_ref, wv_ref, bg_ref, bv_ref, da_ref, du_ref, dcw_ref, dcb_ref):
        is_gate = pl.program_id(0) == 0
        rowi = lax.broadcasted_iota(jnp.int32, (rows, tc), 0)
        wg, wv = _bcast_rows(wg_ref, rows), _bcast_rows(wv_ref, rows)
        bg, bv = _bcast_rows(bg_ref, rows)[0], _bcast_rows(bv_ref, rows)[0]
        wo = [jnp.where(is_gate, a, b) for a, b in zip(wg, wv)]

        def chunk(ii, carry):
            dc_next, acc_b, acc_w0, acc_w1, acc_w2 = carry
            i = nchunks - 1 - ii
            r = pl.multiple_of(i * rows, rows)
            rp = pl.multiple_of(jnp.maximum(i - 1, 0) * rows, rows)
            cg, cv = ug_ref[pl.ds(r, rows), :], uv_ref[pl.ds(r, rows), :]
            pg = jnp.where(i > 0, ug_ref[pl.ds(rp, rows), :], 0.0)
            pv = jnp.where(i > 0, uv_ref[pl.ds(rp, rows), :], 0.0)
            gate, g1, g2 = _conv_chunk(cg, pg, wg, bg, rowi)
            val, v1, v2 = _conv_chunk(cv, pv, wv, bv, rowi)
            sg = jax.nn.sigmoid(gate)
            dav = da_ref[pl.ds(r, rows), :]
            dc = jnp.where(is_gate, dav * val * (sg * (1.0 + gate * (1.0 - sg))), dav * (gate * sg))
            acc_b = acc_b + _rows8(dc)
            acc_w2 = acc_w2 + _rows8(dc * jnp.where(is_gate, cg, cv))
            acc_w1 = acc_w1 + _rows8(dc * jnp.where(is_gate, g1, v1))
            acc_w0 = acc_w0 + _rows8(dc * jnp.where(is_gate, g2, v2))
            up1 = jnp.where(rowi >= rows - 1, pltpu.roll(dc_next, rows - 1, 0), pltpu.roll(dc, rows - 1, 0))
            up2 = jnp.where(rowi >= rows - 2, pltpu.roll(dc_next, rows - 2, 0), pltpu.roll(dc, rows - 2, 0))
            du_ref[pl.ds(r, rows), :] = (wo[2] * dc + wo[1] * up1 + wo[0] * up2).astype(BF16)
            return dc, acc_b, acc_w0, acc_w1, acc_w2

        zero8 = jnp.zeros((8, tc), F32)
        _, acc_b, acc_w0, acc_w1, acc_w2 = lax.fori_loop(
            0, nchunks, chunk, (jnp.zeros((rows, tc), F32), zero8, zero8, zero8, zero8))
        dcb_ref[...] = jnp.sum(acc_b, axis=0, keepdims=True)
        dcw_ref[0:1, :] = jnp.sum(acc_w0, axis=0, keepdims=True)
        dcw_ref[1:2, :] = jnp.sum(acc_w1, axis=0, keepdims=True)
        dcw_ref[2:3, :] = jnp.sum(acc_w2, axis=0, keepdims=True)

    def spec(rows_, off):
        return pl.BlockSpec((rows_, tc), lambda p, j: (0, j + off))

    def own(rows_):
        return pl.BlockSpec((rows_, tc), lambda p, j: (0, p * nc + j))

    return pl.pallas_call(
        body, name=name, grid=(2, nc),
        in_specs=[spec(l, 0), spec(l, nc), spec(3, 0), spec(3, nc), spec(1, 0), spec(1, nc), spec(l, 0)],
        out_specs=[own(l), own(3), own(1)],
        out_shape=[jax.ShapeDtypeStruct((l, f2), BF16), jax.ShapeDtypeStruct((3, f2), F32),
                   jax.ShapeDtypeStruct((1, f2), F32)],
        compiler_params=_cparams("parallel", "parallel"),
    )(u, u, cw, cw, cb, cb, da)


def _split_bf16(v):
    hi = v.astype(BF16)
    return hi, (v - hi.astype(F32)).astype(BF16)


def _cumsum_mm(hi, lo, t2):
    return (jnp.dot(hi, t2, preferred_element_type=F32)
            + jnp.dot(lo, t2, preferred_element_type=F32))


def _tri_and_ones(tri_fn):
    row = lax.broadcasted_iota(jnp.int32, (BLK, 2 * BLK), 0)
    col = lax.broadcasted_iota(jnp.int32, (BLK, 2 * BLK), 1)
    return jnp.where((col >= BLK) | tri_fn(row, col), 1.0, 0.0).astype(BF16)


def _logits(z, mask):
    sp = jnp.log(1.0 + jnp.exp(-jnp.abs(z)))
    lb = jnp.minimum(z, 0.0) - sp
    lm = lb - z
    if mask is not None:
        lm = jnp.where(mask, lm, 0.0)
    return lb, lm


def _software_pipeline(stages, n, block_of, state):
    ns = len(stages)
    inflight = [None] * (ns - 1)
    for t in range(ns - 1):
        new = list(inflight)
        for s in range(t, -1, -1):
            y, state = stages[s](block_of(t - s), None if s == 0 else inflight[s - 1], state)
            new[s] = y
        inflight = new

    def steady(i, carry):
        inflight, state = carry
        new = [None] * (ns - 1)
        for s in range(ns - 1, -1, -1):
            y, state = stages[s](block_of(i + ns - 1 - s), None if s == 0 else inflight[s - 1], state)
            if s < ns - 1:
                new[s] = y
        return tuple(new), state

    inflight, state = lax.fori_loop(0, n - (ns - 1), steady, (tuple(inflight), state))
    inflight = list(inflight)
    for e in range(1, ns):
        new = list(inflight)
        for s in range(ns - 1, e - 1, -1):
            y, state = stages[s](block_of(n - 1 + e - s), inflight[s - 1], state)
            if s < ns - 1:
                new[s] = y
        inflight = new
    return state


def _attn_fwd(q, k, v, *, name):
    nh, l, dh = q.shape
    assert l % ATT_BQ == 0
    nq = l // ATT_BQ
    qscale = HEAD_DIM ** -0.5
    chunks = range(ATT_CHUNKS)

    def body(q_ref, k_ref, v_ref, o_ref, lt_ref):
        t_later = _tri_and_ones(lambda r, c: r > c)
        dmask = (lax.broadcasted_iota(jnp.int32, (BLK, BLK), 1)
                 < lax.broadcasted_iota(jnp.int32, (BLK, BLK), 0))

        def logits(qc, c0, mask):
            z = lax.dot_general(qc, k_ref[0, pl.ds(c0, BLK), :], _DIMS["nt"], preferred_element_type=F32)
            lb, lm = _logits(z, mask)
            return (lb,) + _split_bf16(lm)

        def weights(lb, hi, lo, mask, run):
            cs = _cumsum_mm(hi, lo, t_later)
            a = jnp.exp(lb + cs[:, :BLK] + run)
            if mask is not None:
                a = jnp.where(mask, a, 0.0)
            return a.astype(BF16), run + cs[:, BLK:]

        def accumulate(a_b, c0, acc):
            return acc + jnp.dot(a_b, v_ref[0, pl.ds(c0, BLK), :], preferred_element_type=F32)

        def qblock(qb, _):
            r0 = pl.multiple_of(qb * ATT_BQ, ATT_BQ)
            qs = [q_ref[0, pl.ds(r0 + rc * BLK, BLK), :] * jnp.asarray(qscale, BF16) for rc in chunks]
            accs = [jnp.zeros((BLK, dh), F32)] * ATT_CHUNKS
            runs = [jnp.zeros((BLK, BLK), F32)] * ATT_CHUNKS
            for dj in range(ATT_CHUNKS - 1, -1, -1):
                c0 = r0 + dj * BLK
                for rc in range(dj, ATT_CHUNKS):
                    mask = dmask if rc == dj else None
                    a_b, runs[rc] = weights(*logits(qs[rc], c0, mask), mask, runs[rc])
                    accs[rc] = accumulate(a_b, c0, accs[rc])

            def col0(b):
                return pl.multiple_of(r0 - (b + 1) * BLK, BLK)

            def stage_logits(b, _, state):
                return tuple(logits(qs[rc], col0(b), None) for rc in chunks), state

            def stage_weights(b, xs, state):
                accs, runs = state
                out = [weights(*xs[rc], None, runs[rc]) for rc in chunks]
                return tuple(o[0] for o in out), (accs, tuple(o[1] for o in out))

            def stage_acc(b, a_bs, state):
                accs, runs = state
                return None, (tuple(accumulate(a_bs[rc], col0(b), accs[rc]) for rc in chunks), runs)

            def left_region(state):
                return _software_pipeline([stage_logits, stage_weights, stage_acc], qb * ATT_CHUNKS, lambda b: b, state)

            accs, runs = lax.cond(qb > 0, left_region, lambda s: s, (tuple(accs), tuple(runs)))
            for rc in chunks:
                o_ref[0, pl.ds(r0 + rc * BLK, BLK), :] = accs[rc].astype(BF16)
                lt_ref[0, pl.ds(r0 + rc * BLK, BLK), :] = runs[rc]
            return 0

        lax.fori_loop(0, nq, qblock, 0)

    head = pl.BlockSpec((1, l, dh), lambda h: (h, 0, 0))
    return pl.pallas_call(
        body, name=name, grid=(nh,),
        in_specs=[head, head, head],
        out_specs=[head, pl.BlockSpec((1, l, BLK), lambda h: (h, 0, 0))],
        out_shape=[jax.ShapeDtypeStruct((nh, l, dh), BF16), jax.ShapeDtypeStruct((nh, l, BLK), F32)],
        compiler_params=_cparams("parallel"),
    )(q, k, v)


def _attn_bwd(q, k, v, do, ltot, *, name):
    nh, l, dh = q.shape
    assert l % ATT_BQ == 0
    nq = l // ATT_BQ
    qscale = HEAD_DIM ** -0.5
    chunks = range(ATT_CHUNKS)

    def body(q_ref, k_ref, v_ref, do_ref, lt_ref, dq_ref, dk_ref, dv_ref):
        t_incl = _tri_and_ones(lambda r, c: r <= c)
        t_excl = _tri_and_ones(lambda r, c: r < c)
        dmask = (lax.broadcasted_iota(jnp.int32, (BLK, BLK), 1)
                 < lax.broadcasted_iota(jnp.int32, (BLK, BLK), 0))
        dk_ref[...] = jnp.zeros_like(dk_ref)
        dv_ref[...] = jnp.zeros_like(dv_ref)

        def logits(qc, c0, mask):
            z = lax.dot_general(qc, k_ref[0, pl.ds(c0, BLK), :], _DIMS["nt"], preferred_element_type=F32)
            lb, lm = _logits(z, mask)
            return (lb,) + _split_bf16(lm)

        def weights(lb, hi, lo, doc, ltc, c0, mask, pre_lm):
            cs = _cumsum_mm(hi, lo, t_incl)
            da = lax.dot_general(doc, v_ref[0, pl.ds(c0, BLK), :], _DIMS["nt"], preferred_element_type=F32)
            a = jnp.exp(lb + (ltc - pre_lm - cs[:, :BLK]))
            if mask is not None:
                a = jnp.where(mask, a, 0.0)
            dl = a * da
            return (jnp.exp(lb), dl) + _split_bf16(dl) + (a.astype(BF16),), pre_lm + cs[:, BLK:]

        def logit_grad(beta, dl, dl_hi, dl_lo, mask, pre_dl):
            cd = _cumsum_mm(dl_hi, dl_lo, t_excl)
            dz = dl - beta * (dl + pre_dl + cd[:, :BLK])
            if mask is not None:
                dz = jnp.where(mask, dz, 0.0)
            return (dz * qscale).astype(BF16), pre_dl + cd[:, BLK:]

        def key_grads(c0, dz_parts, a_parts, q_rows, do_rows):
            dz_all = dz_parts[0] if len(dz_parts) == 1 else jnp.concatenate(dz_parts, axis=0)
            a_all = a_parts[0] if len(a_parts) == 1 else jnp.concatenate(a_parts, axis=0)
            dk_ref[0, pl.ds(c0, BLK), :] += lax.dot_general(dz_all, q_rows, _DIMS["tn"], preferred_element_type=F32)
            dv_ref[0, pl.ds(c0, BLK), :] += lax.dot_general(a_all, do_rows, _DIMS["tn"], preferred_element_type=F32)

        def qblock(qb, _):
            r0 = pl.multiple_of(qb * ATT_BQ, ATT_BQ)
            q_raw = q_ref[0, pl.ds(r0, ATT_BQ), :]
            do_all = do_ref[0, pl.ds(r0, ATT_BQ), :]
            qs = [q_raw[rc * BLK:(rc + 1) * BLK] * jnp.asarray(qscale, BF16) for rc in chunks]
            dos = [do_all[rc * BLK:(rc + 1) * BLK] for rc in chunks]
            lts = [lt_ref[0, pl.ds(r0 + rc * BLK, BLK), :] for rc in chunks]

            def col0(b):
                return pl.multiple_of(b * BLK, BLK)

            def stage_logits(b, _, state):
                return tuple(logits(qs[rc], col0(b), None) for rc in chunks), state

            def stage_weights(b, xs, state):
                dqs, pls, pds = state
                out = [weights(*xs[rc], dos[rc], lts[rc], col0(b), None, pls[rc]) for rc in chunks]
                return tuple(o[0] for o in out), (dqs, tuple(o[1] for o in out), pds)

            def stage_logit_grad(b, ys, state):
                dqs, pls, pds = state
                out = [logit_grad(*ys[rc][:4], None, pds[rc]) for rc in chunks]
                return tuple((out[rc][0], ys[rc][4]) for rc in chunks), (dqs, pls, tuple(o[1] for o in out))

            def stage_grads(b, ws, state):
                dqs, pls, pds = state
                c0 = col0(b)
                kj = k_ref[0, pl.ds(c0, BLK), :]
                dqs = tuple(dqs[rc] + jnp.dot(ws[rc][0], kj, preferred_element_type=F32) for rc in chunks)
                key_grads(c0, [w[0] for w in ws], [w[1] for w in ws], q_raw, do_all)
                return None, (dqs, pls, pds)

            zero = jnp.zeros((BLK, BLK), F32)
            state = ((jnp.zeros((BLK, dh), F32),) * ATT_CHUNKS, (zero,) * ATT_CHUNKS, (zero,) * ATT_CHUNKS)

            def left_region(state):
                return _software_pipeline([stage_logits, stage_weights, stage_logit_grad, stage_grads],
                                          qb * ATT_CHUNKS, lambda b: b, state)

            dqs, pls, pds = lax.cond(qb > 0, left_region, lambda s: s, state)
            dqs, pls, pds = list(dqs), list(pls), list(pds)
            for dj in chunks:
                c0 = r0 + dj * BLK
                kj = k_ref[0, pl.ds(c0, BLK), :]
                dz_parts, a_parts = [], []
                for rc in range(dj, ATT_CHUNKS):
                    mask = dmask if rc == dj else None
                    ys, pls[rc] = weights(*logits(qs[rc], c0, mask), dos[rc], lts[rc], c0, mask, pls[rc])
                    dz_b, pds[rc] = logit_grad(*ys[:4], mask, pds[rc])
                    a_b = ys[4]
                    dqs[rc] = dqs[rc] + jnp.dot(dz_b, kj, preferred_element_type=F32)
                    dz_parts.append(dz_b)
                    a_parts.append(a_b)
                key_grads(c0, dz_parts, a_parts, q_raw[dj * BLK:], do_all[dj * BLK:])
            for rc in chunks:
                dq_ref[0, pl.ds(r0 + rc * BLK, BLK), :] = dqs[rc]
            return 0

        lax.fori_loop(0, nq, qblock, 0)

    head = pl.BlockSpec((1, l, dh), lambda h: (h, 0, 0))
    return pl.pallas_call(
        body, name=name, grid=(nh,),
        in_specs=[head, head, head, head, pl.BlockSpec((1, l, BLK), lambda h: (h, 0, 0))],
        out_specs=[head, head, head],
        out_shape=[jax.ShapeDtypeStruct((nh, l, dh), F32)] * 3,
        compiler_params=_cparams("parallel"),
    )(q, k, v, do, ltot)


def _add_to_bf16(a, b, *, name):
    n, r, c = a.shape
    tr = _row_tile(r, 1536, 16)
    blk = pl.BlockSpec((1, tr, c), lambda i, j: (i, j, 0))

    def body(a_ref, b_ref, o_ref):
        o_ref[...] = (a_ref[...] + b_ref[...]).astype(BF16)

    return pl.pallas_call(body, name=name, grid=(n, r // tr), in_specs=[blk, blk], out_specs=blk,
                          out_shape=jax.ShapeDtypeStruct(a.shape, BF16),
                          compiler_params=_cparams("parallel", "parallel"))(a, b)


def _sum_slots(p, *, name):
    n, r, c = p.shape
    tr = _row_tile(r, 1536, 16)

    def body(p_ref, o_ref):
        acc = p_ref[0].astype(F32)
        for s in range(1, n):
            acc = acc + p_ref[s].astype(F32)
        o_ref[...] = acc

    return pl.pallas_call(body, name=name, grid=(r // tr,),
                          in_specs=[pl.BlockSpec((n, tr, c), lambda j: (0, j, 0))],
                          out_specs=pl.BlockSpec((tr, c), lambda j: (j, 0)),
                          out_shape=jax.ShapeDtypeStruct((r, c), F32),
                          compiler_params=_cparams("parallel"))(p)


def _adamw(w, g, m, v, *, name):
    r, c = w.shape
    tr = _row_tile(r, 512, 8)
    blk = pl.BlockSpec((tr, c), lambda i: (i, 0))

    def body(w_ref, g_ref, m_ref, v_ref, d_ref, mo_ref, vo_ref):
        gv = g_ref[...]
        mn = ADAM_B1 * m_ref[...] + (1.0 - ADAM_B1) * gv
        vn = ADAM_B2 * v_ref[...] + (1.0 - ADAM_B2) * (gv * gv)
        m_hat = mn / (1.0 - ADAM_B1 ** ADAM_STEP)
        v_hat = vn / (1.0 - ADAM_B2 ** ADAM_STEP)
        d_ref[...] = -ADAM_LR * (m_hat / (jnp.sqrt(v_hat) + ADAM_EPS) + ADAM_WD * w_ref[...])
        mo_ref[...] = mn
        vo_ref[...] = vn

    return pl.pallas_call(body, name=name, grid=(r // tr,), in_specs=[blk] * 4, out_specs=[blk] * 3,
                          out_shape=[jax.ShapeDtypeStruct((r, c), F32)] * 3,
                          compiler_params=_cparams("parallel"))(w, g, m, v)


_ANY = pl.BlockSpec(memory_space=pl.ANY)


def _place():
    x, y, c = lax.axis_index("x"), lax.axis_index("y"), lax.axis_index("c")
    chips = [(1 - x, y), (x, 1 - y), (1 - x, 1 - y)]
    return x, y, c, chips


def _gather_shards(wsh, *, name):
    _, r, cdim = wsh.shape

    def body(w_ref, out_ref, send_sems, recv_sems):
        x, y, c, chips = _place()
        s = 2 * x + y
        sibling = (x, y, 1 - c)

        def copy(k, src, dst, to):
            return pltpu.make_async_remote_copy(src_ref=src, dst_ref=dst, send_sem=send_sems.at[k],
                                                recv_sem=recv_sems.at[k], device_id=to, device_id_type=MESH)

        first = [copy(j, w_ref.at[c], out_ref.at[s, c], (*chip, c)) for j, chip in enumerate(chips)]
        for cp in first:
            cp.start()
        passed = []
        for j, (px, py) in enumerate(chips):
            sp = 2 * px + py
            copy(j, w_ref.at[c], out_ref.at[sp, c], sibling).wait_recv()
            fwd = copy(3 + j, out_ref.at[sp, c], out_ref.at[sp, c], sibling)
            fwd.start()
            passed.append(fwd)
        for j, (px, py) in enumerate(chips):
            sp = 2 * px + py
            copy(3 + j, w_ref.at[c], out_ref.at[sp, 1 - c], sibling).wait_recv()
        for cp in first + passed:
            cp.wait_send()

    return pl.pallas_call(
        body, name=name, in_specs=[_ANY], out_specs=_ANY,
        out_shape=jax.ShapeDtypeStruct((N_CHIPS, 2, r, cdim), wsh.dtype),
        scratch_shapes=[pltpu.SemaphoreType.DMA((6,)), pltpu.SemaphoreType.DMA((6,))],
    )(wsh)


def _to_sibling(a, *, name):
    def body(a_ref, out_ref, send_sem, recv_sem):
        x, y, c, _ = _place()
        cp = pltpu.make_async_remote_copy(src_ref=a_ref, dst_ref=out_ref, send_sem=send_sem, recv_sem=recv_sem,
                                          device_id=(x, y, 1 - c), device_id_type=MESH)
        cp.start()
        cp.wait()

    return pl.pallas_call(
        body, name=name, in_specs=[_ANY], out_specs=_ANY,
        out_shape=jax.ShapeDtypeStruct(a.shape, a.dtype),
        scratch_shapes=[pltpu.SemaphoreType.DMA, pltpu.SemaphoreType.DMA],
    )(a)


def _scatter_rows(p, *, name):
    def body(p_ref, out_ref, send_sems, recv_sems):
        x, y, c, chips = _place()
        s = 2 * x + y
        sends = []
        for j, (px, py) in enumerate(chips):
            sp = 2 * px + py
            cp = pltpu.make_async_remote_copy(src_ref=p_ref.at[sp], dst_ref=out_ref.at[s], send_sem=send_sems.at[j],
                                              recv_sem=recv_sems.at[j], device_id=(px, py, c), device_id_type=MESH)
            cp.start()
            sends.append(cp)
        for j, (px, py) in enumerate(chips):
            sp = 2 * px + py
            pltpu.make_async_remote_copy(src_ref=p_ref.at[sp], dst_ref=out_ref.at[sp], send_sem=send_sems.at[j],
                                         recv_sem=recv_sems.at[j], device_id=(px, py, c),
                                         device_id_type=MESH).wait_recv()
        for cp in sends:
            cp.wait_send()

    return pl.pallas_call(
        body, name=name, in_specs=[_ANY], out_specs=_ANY,
        out_shape=jax.ShapeDtypeStruct(p.shape, p.dtype),
        scratch_shapes=[pltpu.SemaphoreType.DMA((3,)), pltpu.SemaphoreType.DMA((3,))],
    )(p)


def _all_devices(a, reduce, *, name):
    r, cdim = a.shape

    def body(a_ref, out_ref, *scratch):
        if reduce:
            buf, send_sems, recv_sems = scratch
        else:
            buf = out_ref
            send_sems, recv_sems = scratch
        x, y, c, _ = _place()
        me = 4 * x + 2 * y + c
        buf[me] = a_ref[...]
        peers = []
        for k in range(1, N_DEV):
            dx, dy, dc = (k >> 2) & 1, (k >> 1) & 1, k & 1
            peers.append((x ^ dx, y ^ dy, c ^ dc))
        sends = []
        for k, peer in enumerate(peers):
            cp = pltpu.make_async_remote_copy(src_ref=a_ref, dst_ref=buf.at[me], send_sem=send_sems.at[k],
                                              recv_sem=recv_sems.at[k], device_id=peer, device_id_type=MESH)
            cp.start()
            sends.append(cp)
        for k, (px, py, pc) in enumerate(peers):
            pltpu.make_async_remote_copy(src_ref=a_ref, dst_ref=buf.at[4 * px + 2 * py + pc],
                                         send_sem=send_sems.at[k], recv_sem=recv_sems.at[k],
                                         device_id=(px, py, pc), device_id_type=MESH).wait_recv()
        for cp in sends:
            cp.wait_send()
        if reduce:
            acc = buf[0]
            for k in range(1, N_DEV):
                acc = acc + buf[k]
            out_ref[...] = acc

    vm = pl.BlockSpec(memory_space=pltpu.VMEM)
    sems = [pltpu.SemaphoreType.DMA((N_DEV - 1,)), pltpu.SemaphoreType.DMA((N_DEV - 1,))]
    if reduce:
        out_shape = jax.ShapeDtypeStruct((r, cdim), F32)
        scratch = [pltpu.VMEM((N_DEV, r, cdim), F32)] + sems
    else:
        out_shape = jax.ShapeDtypeStruct((N_DEV, r, cdim), F32)
        scratch = sems
    return pl.pallas_call(
        body, name=name, in_specs=[vm], out_specs=vm, out_shape=out_shape, scratch_shapes=scratch,
        compiler_params=pltpu.CompilerParams(vmem_limit_bytes=VMEM_LIMIT),
    )(a)


def _flat_rows(parts, cols):
    flat = jnp.concatenate([p.reshape(-1) for p in parts])
    padded = -(-flat.size // (8 * cols)) * (8 * cols)
    return jnp.pad(flat, (0, padded - flat.size)).reshape(-1, cols)


def kernel(x, meta_tokens, mix_norm, ffn_norm, pool_w, pool_scale, kv_norm, w_kv, w_q, w_o, ffn_w_up, ffn_conv_w, ffn_conv_b, ffn_w_down, final_norm, loss_target, m_meta_tokens, m_mix_norm, m_ffn_norm, m_pool_w, m_pool_scale, m_kv_norm, m_w_kv, m_w_q, m_w_o, m_ffn_w_up, m_ffn_conv_w, m_ffn_conv_b, m_ffn_w_down, m_final_norm, v_meta_tokens, v_mix_norm, v_ffn_norm, v_pool_w, v_pool_scale, v_kv_norm, v_w_kv, v_w_q, v_w_o, v_ffn_w_up, v_ffn_conv_w, v_ffn_conv_b, v_ffn_w_down, v_final_norm):
    seq, d = x.shape[1], x.shape[2]
    l_real = N_META + seq
    lp = -(-l_real // ATT_BQ) * ATT_BQ
    f2 = ffn_w_up.shape[2] * N_CHIPS
    f = f2 // 2
    gd = d // N_GROUPS
    chip = 2 * lax.axis_index("x") + lax.axis_index("y")
    core = lax.axis_index("c")

    big_parts = [pool_w[0], w_kv, w_q[0], w_o[0], ffn_w_up, ffn_w_down]
    sizes = [p.size for p in big_parts]
    n_big = sum(sizes)
    cw = 1024
    assert n_big % (2 * 16 * cw) == 0
    rows_half = n_big // (2 * cw)
    wsh = _flat_rows([p.astype(BF16) for p in big_parts], cw).reshape(2, rows_half, cw)
    wall = lax.dynamic_update_index_in_dim(_gather_shards(wsh, name="gather_weights"), wsh, chip, 0)
    wall = wall.reshape(N_CHIPS, n_big)
    offs = [0]
    for sz in sizes:
        offs.append(offs[-1] + sz)

    def big(i, shape):
        return wall[:, offs[i]:offs[i + 1]].reshape((N_CHIPS,) + shape)

    wp_b = big(0, (N_GROUPS, gd // N_CHIPS, gd)).transpose(1, 0, 2, 3).reshape(N_GROUPS, gd, gd)
    wkv_b = big(1, (d, 2 * d // N_CHIPS)).transpose(1, 0, 2).reshape(d, 2 * d)
    wq_b = big(2, (d // N_CHIPS, d)).reshape(d, d)
    wo_b = big(3, (d // N_CHIPS, d)).reshape(d, d)
    wup_b = big(4, (2, d, f2 // N_CHIPS)).transpose(1, 2, 0, 3).reshape(2, d, f2)
    wdn_b = big(5, (2, f // N_CHIPS, d)).transpose(1, 0, 2, 3).reshape(2, f, d)

    small_parts = [meta_tokens, pool_scale, ffn_conv_w]
    ssizes = [p.size for p in small_parts]
    small = _flat_rows(small_parts, LANES)
    sall = _all_devices(small, False, name="gather_small")[::2].reshape(N_CHIPS, -1)
    meta_f = sall[:, :ssizes[0]].reshape(N_CHIPS, N_META, d // N_CHIPS).transpose(1, 0, 2).reshape(N_META, d)
    scale_f = sall[:, ssizes[0]:ssizes[0] + ssizes[1]].reshape(1, d)
    cw_f = sall[:, ssizes[0] + ssizes[1]:sum(ssizes)].reshape(N_CHIPS, 2, 3, f2 // N_CHIPS).transpose(1, 2, 0, 3).reshape(2, 3, f2)

    mix0, mix1 = mix_norm[0:1], mix_norm[1:2]
    fn0, fn1 = ffn_norm[0:1], ffn_norm[1:2]
    kvn_g = kv_norm.reshape(1, d)
    fin_g = final_norm.reshape(1, d)

    pad = lp - l_real
    h0 = jnp.concatenate([meta_f, x[0], jnp.zeros((pad, d), F32)], axis=0)
    tgt = jnp.pad(loss_target[0], ((N_META, pad), (0, 0)))

    _, r0 = _norm_fwd(h0, [mix0], name="norm_h0")
    diff_b = _pool_diff_fwd(h0, r0, mix0, name="pool_diff")
    h1 = _pool_mix_fwd(h0, diff_b, wp_b, scale_f, name="pool_mix")

    def ffn_fwd(h_in, layer, tag):
        (fb,), rf = _norm_fwd(h_in, [ffn_norm[layer:layer + 1]], name=f"norm_ffn{tag}")
        u = _mm(fb, wup_b[layer], "nn", tm=1408, tn=512, name=f"ffn_up{tag}")
        a = _convgate_fwd(u, cw_f[layer], ffn_conv_b[layer:layer + 1], name=f"convgate{tag}")
        h_out = _mm(a, wdn_b[layer], "nn", tm=384, tn=1024, res=h_in, name=f"ffn_down{tag}")
        return h_out, (fb, rf, u, a)

    h2, ffn0_saved = ffn_fwd(h1, 0, "0")
    (kvn_b, n1_b), r2 = _norm_fwd(h2, [kvn_g, mix1], name="norm_h2")
    kv_b = _mm(kvn_b, wkv_b, "nn", tm=1408, tn=512, out_dtype=BF16, name="kv_proj")
    q_b = _mm(n1_b, wq_b, "nn", tm=1408, tn=512, out_dtype=BF16, name="q_proj")

    def heads(t):
        return t.reshape(lp, N_HEADS, HEAD_DIM).transpose(1, 0, 2)

    def unheads(t):
        return t.transpose(1, 0, 2).reshape(lp, N_HEADS * HEAD_DIM)

    qh, kh, vh = heads(q_b), heads(kv_b[:, :d]), heads(kv_b[:, d:])
    oh, ltot = _attn_fwd(qh, kh, vh, name="attn_fwd")
    o_b = unheads(oh)
    h3 = _mm(o_b, wo_b, "nn", tm=384, tn=1024, res=h2, name="o_proj")
    h4, ffn1_saved = ffn_fwd(h3, 1, "1")

    loss8, dh4, dh4_b, dfin = _loss_bwd(h4, fin_g, tgt, seq, name="loss")

    def ffn_bwd(h_in, layer, saved, dh_out, dh_out_b, tag):
        fb, rf, u, a = saved
        dwdn = _mm(a, dh_out_b, "tn", tm=256, tn=1024, name=f"d_wdown{tag}")
        da = _mm(dh_out_b, wdn_b[layer], "nt", tm=384, tn=f, name=f"d_act{tag}")
        du_b, dcw, dcb = _convgate_bwd(u, cw_f[layer], ffn_conv_b[layer:layer + 1], da, name=f"convgate_bwd{tag}")
        dwup = _mm(fb, du_b, "tn", tm=1024, tn=512, name=f"d_wup{tag}")
        df = _mm(du_b, wup_b[layer], "nt", tm=384, tn=1024, name=f"d_ffn_in{tag}")
        dh_in, dh_in_b, (dfn,) = _norm_bwd(h_in, rf, [ffn_norm[layer:layer + 1]], [df], dh_out,
                                          name=f"norm_ffn_bwd{tag}")
        return dh_in, dh_in_b, dwup, dwdn, dcw, dcb, dfn

    dh3, dh3_b, dwup1, dwdn1, dcw1, dcb1, dfn1 = ffn_bwd(h3, 1, ffn1_saved, dh4, dh4_b, "1")

    dwo = _mm(o_b, dh3_b, "tn", tm=1024, tn=512, name="d_wo")
    do_b = _mm(dh3_b, wo_b, "nt", tm=384, tn=1024, out_dtype=BF16, name="d_attn_out")
    dqh, dkh, dvh = _attn_bwd(qh, kh, vh, heads(do_b), ltot, name="attn_bwd")
    dq_b = unheads(dqh).astype(BF16)
    dkv_b = jnp.concatenate([unheads(dkh), unheads(dvh)], axis=1).astype(BF16)
    dwq = _mm(n1_b, dq_b, "tn", tm=1024, tn=512, name="d_wq")
    dwkv = _mm(kvn_b, dkv_b, "tn", tm=1024, tn=512, name="d_wkv")
    dn1 = _mm(dq_b, wq_b, "nt", tm=384, tn=1024, name="d_n1")
    dkvn = _mm(dkv_b, wkv_b, "nt", tm=384, tn=1024, name="d_kvn")
    dh2, dh2_b, (dmix1, dkvg) = _norm_bwd(h2, r2, [mix1, kvn_g], [dn1, dkvn], dh3, name="norm_h2_bwd")

    dh1, dh1_b, dwup0, dwdn0, dcw0, dcb0, dfn0 = ffn_bwd(h1, 0, ffn0_saved, dh2, dh2_b, "0")

    ddiff, dwp, dscale = _pool_mix_bwd(diff_b, wp_b, scale_f, dh1, name="pool_mix_bwd")
    dn0 = _pool_diff_bwd(ddiff, name="pool_diff_bwd")
    dh0, _, (dmix0,) = _norm_bwd(h0, r0, [mix0], [dn0], dh1, name="norm_h0_bwd")

    grad_x = dh0[N_META:l_real][None]
    dmeta = dh0[:N_META]

    dwup = jnp.stack([dwup0, dwup1])
    dwdn = jnp.stack([dwdn0, dwdn1])
    g_parts = [
        dwp.reshape(N_GROUPS, N_CHIPS, gd // N_CHIPS, gd).transpose(1, 0, 2, 3),
        dwkv.reshape(d, N_CHIPS, 2 * d // N_CHIPS).transpose(1, 0, 2),
        dwq.reshape(N_CHIPS, d // N_CHIPS, d),
        dwo.reshape(N_CHIPS, d // N_CHIPS, d),
        dwup.reshape(2, d, N_CHIPS, f2 // N_CHIPS).transpose(2, 0, 1, 3),
        dwdn.reshape(2, N_CHIPS, f // N_CHIPS, d).transpose(1, 0, 2, 3),
    ]
    g_all = jnp.concatenate([p.reshape(N_CHIPS, -1) for p in g_parts], axis=1).reshape(N_CHIPS, 2, rows_half, cw)
    g_mine = lax.dynamic_index_in_dim(g_all, core, axis=1, keepdims=False)
    g_other = lax.dynamic_index_in_dim(g_all, 1 - core, axis=1, keepdims=False)
    from_sibling = _to_sibling(g_other, name="grads_to_sibling")
    chip_part = _add_to_bf16(g_mine, from_sibling, name="grads_chip_sum")
    slots = lax.dynamic_update_index_in_dim(_scatter_rows(chip_part, name="grads_scatter"),
                                            lax.dynamic_index_in_dim(chip_part, chip, 0, keepdims=False), chip, 0)
    half = _sum_slots(slots, name="grads_sum")
    other_half = _to_sibling(half, name="grads_join")
    g_shard = jnp.stack([jnp.where(core == 0, half, other_half),
                         jnp.where(core == 0, other_half, half)]).reshape(n_big)

    def gbig(i, shape):
        return g_shard[offs[i]:offs[i + 1]].reshape(shape)

    g_pool_w = gbig(0, pool_w.shape)
    g_w_kv = gbig(1, w_kv.shape)
    g_w_q = gbig(2, w_q.shape)
    g_w_o = gbig(3, w_o.shape)
    g_w_up = gbig(4, ffn_w_up.shape)
    g_w_dn = gbig(5, ffn_w_down.shape)

    dcw = jnp.stack([dcw0, dcw1])
    dcb = jnp.concatenate([dcb0, dcb1], axis=0)
    sg_parts = [jnp.concatenate([dmix0, dmix1], axis=0), jnp.concatenate([dfn0, dfn1], axis=0), dkvg, dfin, dcb,
                dmeta, dscale, dcw, loss8]
    sg_sizes = [p.size for p in sg_parts]
    sg = _all_devices(_flat_rows(sg_parts, LANES), True, name="reduce_small").reshape(-1)
    sg_offs = [0]
    for sz in sg_sizes:
        sg_offs.append(sg_offs[-1] + sz)

    def gsmall(i, shape):
        return sg[sg_offs[i]:sg_offs[i + 1]].reshape(shape)

    g_mix = gsmall(0, mix_norm.shape)
    g_ffn_norm = gsmall(1, ffn_norm.shape)
    g_kv_norm = gsmall(2, kv_norm.shape)
    g_final = gsmall(3, final_norm.shape)
    g_conv_b = gsmall(4, ffn_conv_b.shape)
    csh = d // N_CHIPS
    g_meta = lax.dynamic_slice_in_dim(gsmall(5, (N_META, d)), chip * csh, csh, axis=1)
    g_scale = lax.dynamic_slice_in_dim(gsmall(6, (1, d)), chip * csh, csh, axis=1)
    fsh = f2 // N_CHIPS
    g_conv_w = lax.dynamic_slice_in_dim(gsmall(7, (2, 3, f2)), chip * fsh, fsh, axis=2)
    loss = gsmall(8, (8 * LANES,))[0]

    weights = [meta_tokens, mix_norm, ffn_norm, pool_w, pool_scale, kv_norm, w_kv, w_q, w_o, ffn_w_up, ffn_conv_w,
               ffn_conv_b, ffn_w_down, final_norm]
    grads = [g_meta, g_mix, g_ffn_norm, g_pool_w, g_scale, g_kv_norm, g_w_kv, g_w_q, g_w_o, g_w_up, g_conv_w,
             g_conv_b, g_w_dn, g_final]
    ms = [m_meta_tokens, m_mix_norm, m_ffn_norm, m_pool_w, m_pool_scale, m_kv_norm, m_w_kv, m_w_q, m_w_o,
          m_ffn_w_up, m_ffn_conv_w, m_ffn_conv_b, m_ffn_w_down, m_final_norm]
    vs = [v_meta_tokens, v_mix_norm, v_ffn_norm, v_pool_w, v_pool_scale, v_kv_norm, v_w_kv, v_w_q, v_w_o,
          v_ffn_w_up, v_ffn_conv_w, v_ffn_conv_b, v_ffn_w_down, v_final_norm]
    names = ["meta", "mix", "ffnnorm", "poolw", "poolscale", "kvnorm", "wkv", "wq", "wo", "wup", "convw", "convb",
             "wdown", "final"]
    deltas, new_ms, new_vs = [], [], []
    for w, g, m, v, nm in zip(weights, grads, ms, vs, names):
        cols = w.shape[-1]
        if w.size % (8 * LANES) == 0 and w.ndim == 1:
            cols = LANES
        view = (w.size // cols, cols)
        dl, mn, vn = _adamw(w.reshape(view), g.reshape(view), m.reshape(view), v.reshape(view), name=f"adamw_{nm}")
        deltas.append(dl.reshape(w.shape))
        new_ms.append(mn.reshape(w.shape))
        new_vs.append(vn.reshape(w.shape))

    return (loss, grad_x, *grads, *deltas, *new_ms, *new_vs)
```

```python
import jax
import jax.numpy as jnp
from jax import lax
from jax.experimental import pallas as pl
from jax.experimental.pallas import tpu as pltpu

F32 = jnp.float32
BF16 = jnp.bfloat16

N_META = 16
N_HEADS = 16
HEAD_DIM = 64
POOL_WINDOWS = (2, 4, 8, 16)
N_GROUPS = 4
RMS_EPS = 1e-6
ADAM_LR = 0.001
ADAM_B1 = 0.9
ADAM_B2 = 0.999
ADAM_EPS = 1e-08
ADAM_WD = 0.01
ADAM_STEP = 10

LANES = 128
BLK = 128
ATT_CHUNKS = 3
ATT_BQ = ATT_CHUNKS * BLK
CONV_ROWS = 128
N_CHIPS = 4
N_DEV = 8
VMEM_LIMIT = 56 * 1024 * 1024
MESH = pl.DeviceIdType.MESH


def _cparams(*sem):
    return pltpu.CompilerParams(dimension_semantics=tuple(sem) if sem else None,
                                vmem_limit_bytes=VMEM_LIMIT)


def _tile(n, pref):
    best = None
    t = LANES
    while t <= min(n, pref):
        if n % t == 0:
            best = t
        t += LANES
    assert best is not None, (n, pref)
    return best


def _row_tile(r, pref, mult):
    best = r
    for t in range(mult, min(r, pref) + 1, mult):
        if r % t == 0:
            best = t
    return best


_DIMS = {
    "nn": (((1,), (0,)), ((), ())),
    "nt": (((1,), (1,)), ((), ())),
    "tn": (((0,), (0,)), ((), ())),
}


def _mm(a, b, dims, *, tm, tn, name, out_dtype=F32, res=None):
    if dims == "tn":
        k, m = a.shape
    else:
        m, k = a.shape
    n = b.shape[0] if dims == "nt" else b.shape[1]
    tm = _tile(m, tm)
    tn = _tile(n, tn)
    a_spec = pl.BlockSpec((k, tm), lambda i, j: (0, i)) if dims == "tn" else pl.BlockSpec((tm, k), lambda i, j: (i, 0))
    b_spec = pl.BlockSpec((tn, k), lambda i, j: (j, 0)) if dims == "nt" else pl.BlockSpec((k, tn), lambda i, j: (0, j))
    o_spec = pl.BlockSpec((tm, tn), lambda i, j: (i, j))
    dn = _DIMS[dims]

    def body(*refs):
        if res is None:
            a_ref, b_ref, o_ref = refs
        else:
            a_ref, b_ref, r_ref, o_ref = refs
        acc = lax.dot_general(a_ref[...], b_ref[...], dn, preferred_element_type=F32)
        if res is not None:
            acc = acc + r_ref[...]
        o_ref[...] = acc.astype(out_dtype)

    ins = [a, b] + ([] if res is None else [res])
    specs = [a_spec, b_spec] + ([] if res is None else [o_spec])
    return pl.pallas_call(
        body, name=name, grid=(m // tm, n // tn), in_specs=specs, out_specs=o_spec,
        out_shape=jax.ShapeDtypeStruct((m, n), out_dtype),
        compiler_params=_cparams("parallel", "arbitrary"),
    )(*ins)


def _norm_fwd(x, gains, *, name):
    l, d = x.shape
    tr = _tile(l, 384)
    ng = len(gains)

    def body(*refs):
        x_ref = refs[0]
        g_refs = refs[1:1 + ng]
        o_refs = refs[1 + ng:1 + 2 * ng]
        r_ref = refs[1 + 2 * ng]
        xv = x_ref[...]
        r = lax.rsqrt(jnp.mean(xv * xv, axis=-1, keepdims=True) + RMS_EPS)
        xn = xv * r
        for g_ref, o_ref in zip(g_refs, o_refs):
            o_ref[...] = (xn * g_ref[...]).astype(BF16)
        r_ref[...] = r

    row = pl.BlockSpec((tr, d), lambda i: (i, 0))
    gspec = pl.BlockSpec((1, d), lambda i: (0, 0))
    outs = pl.pallas_call(
        body, name=name, grid=(l // tr,),
        in_specs=[row] + [gspec] * ng,
        out_specs=[row] * ng + [pl.BlockSpec((tr, 1), lambda i: (i, 0))],
        out_shape=[jax.ShapeDtypeStruct((l, d), BF16)] * ng + [jax.ShapeDtypeStruct((l, 1), F32)],
        compiler_params=_cparams("parallel"),
    )(x, *gains)
    return list(outs[:ng]), outs[ng]


def _rows8(v):
    r, d = v.shape
    return jnp.sum(v.reshape(r // 8, 8, d), axis=0)


def _norm_bwd(x, r, gains, dns, dres, *, name):
    l, d = x.shape
    tr = _tile(l, 384)
    ng = len(gains)
    nsteps = l // tr

    def body(*refs):
        x_ref, r_ref, dres_ref = refs[0], refs[1], refs[2]
        g_refs = refs[3:3 + ng]
        dn_refs = refs[3 + ng:3 + 2 * ng]
        dx_ref, dxb_ref = refs[3 + 2 * ng], refs[4 + 2 * ng]
        dg_refs = refs[5 + 2 * ng:5 + 3 * ng]
        acc_refs = refs[5 + 3 * ng:5 + 4 * ng]
        i = pl.program_id(0)

        @pl.when(i == 0)
        def _():
            for acc in acc_refs:
                acc[...] = jnp.zeros_like(acc)

        rv = r_ref[...]
        xn = x_ref[...] * rv
        total = dres_ref[...]
        for g_ref, dn_ref, acc in zip(g_refs, dn_refs, acc_refs):
            dn = dn_ref[...]
            acc[...] += _rows8(dn * xn)
            dxn = dn * g_ref[...]
            total = total + rv * (dxn - xn * jnp.mean(dxn * xn, axis=-1, keepdims=True))
        dx_ref[...] = total
        dxb_ref[...] = total.astype(BF16)

        @pl.when(i == nsteps - 1)
        def _():
            for dg_ref, acc in zip(dg_refs, acc_refs):
                dg_ref[...] = jnp.sum(acc[...], axis=0, keepdims=True)

    row = pl.BlockSpec((tr, d), lambda i: (i, 0))
    gspec = pl.BlockSpec((1, d), lambda i: (0, 0))
    outs = pl.pallas_call(
        body, name=name, grid=(nsteps,),
        in_specs=[row, pl.BlockSpec((tr, 1), lambda i: (i, 0)), row] + [gspec] * ng + [row] * ng,
        out_specs=[row, row] + [gspec] * ng,
        out_shape=[jax.ShapeDtypeStruct((l, d), F32), jax.ShapeDtypeStruct((l, d), BF16)]
        + [jax.ShapeDtypeStruct((1, d), F32)] * ng,
        scratch_shapes=[pltpu.VMEM((8, d), F32)] * ng,
        compiler_params=_cparams("arbitrary"),
    )(x, r, dres, *gains, *dns)
    return outs[0], outs[1], list(outs[2:])


def _loss_bwd(h, gain, tgt, seq, *, name):
    l, d = h.shape
    tr = _tile(l, 384)
    nsteps = l // tr

    def body(h_ref, g_ref, t_ref, loss_ref, dh_ref, dhb_ref, dg_ref, lacc, gacc):
        i = pl.program_id(0)

        @pl.when(i == 0)
        def _():
            lacc[...] = jnp.zeros_like(lacc)
            gacc[...] = jnp.zeros_like(gacc)

        xv = h_ref[...]
        g = g_ref[...]
        r = lax.rsqrt(jnp.mean(xv * xv, axis=-1, keepdims=True) + RMS_EPS)
        xn = xv * r
        rows = i * tr + lax.broadcasted_iota(jnp.int32, (tr, 1), 0)
        valid = (rows >= N_META) & (rows < N_META + seq)
        e = jnp.where(valid, xn * g - t_ref[...], 0.0)
        lacc[...] += _rows8(e * e)
        dy = e * (1.0 / d)
        gacc[...] += _rows8(dy * xn)
        dxn = dy * g
        dx = r * (dxn - xn * jnp.mean(dxn * xn, axis=-1, keepdims=True))
        dh_ref[...] = dx
        dhb_ref[...] = dx.astype(BF16)

        @pl.when(i == nsteps - 1)
        def _():
            loss_ref[...] = jnp.full((8, LANES), 0.5 / d * jnp.sum(lacc[...]), F32)
            dg_ref[...] = jnp.sum(gacc[...], axis=0, keepdims=True)

    row = pl.BlockSpec((tr, d), lambda i: (i, 0))
    gspec = pl.BlockSpec((1, d), lambda i: (0, 0))
    return pl.pallas_call(
        body, name=name, grid=(nsteps,),
        in_specs=[row, gspec, row],
        out_specs=[pl.BlockSpec((8, LANES), lambda i: (0, 0)), row, row, gspec],
        out_shape=[jax.ShapeDtypeStruct((8, LANES), F32), jax.ShapeDtypeStruct((l, d), F32),
                   jax.ShapeDtypeStruct((l, d), BF16), jax.ShapeDtypeStruct((1, d), F32)],
        scratch_shapes=[pltpu.VMEM((8, d), F32), pltpu.VMEM((8, d), F32)],
        compiler_params=_cparams("arbitrary"),
    )(h, gain, tgt)


def _shift_down(v, k, rows):
    return jnp.where(rows >= k, pltpu.roll(v, k, 0), 0.0)


def _shift_up(v, k, rows):
    l = v.shape[0]
    return jnp.where(rows < l - k, pltpu.roll(v, l - k, 0), 0.0)


def _pool_diff(n, w, rows):
    s = n
    for k in (1, 2, 4, 8):
        s = s + jnp.where(k < w, _shift_down(s, k, rows), 0.0)
    cnt = jnp.minimum(rows + 1, w).astype(F32)
    return s / cnt - n, cnt


def _pool_diff_fwd(h, r, gain, *, name):
    l, d = h.shape
    per_group = d // N_GROUPS // LANES

    def body(h_ref, r_ref, g_ref, o_ref):
        w = jnp.left_shift(2, pl.program_id(0) // per_group)
        rows = lax.broadcasted_iota(jnp.int32, (l, 1), 0)
        n = h_ref[...] * r_ref[...] * g_ref[...]
        diff, _ = _pool_diff(n, w, rows)
        o_ref[...] = diff.astype(BF16)

    col = pl.BlockSpec((l, LANES), lambda j: (0, j))
    return pl.pallas_call(
        body, name=name, grid=(d // LANES,),
        in_specs=[col, pl.BlockSpec((l, 1), lambda j: (0, 0)), pl.BlockSpec((1, LANES), lambda j: (0, j))],
        out_specs=col, out_shape=jax.ShapeDtypeStruct((l, d), BF16),
        compiler_params=_cparams("parallel"),
    )(h, r, gain)


def _pool_diff_bwd(ddiff, *, name):
    l, d = ddiff.shape
    per_group = d // N_GROUPS // LANES

    def body(dd_ref, o_ref):
        w = jnp.left_shift(2, pl.program_id(0) // per_group)
        rows = lax.broadcasted_iota(jnp.int32, (l, 1), 0)
        dd = dd_ref[...]
        s = dd / jnp.minimum(rows + 1, w).astype(F32)
        for k in (1, 2, 4, 8):
            s = s + jnp.where(k < w, _shift_up(s, k, rows), 0.0)
        o_ref[...] = s - dd

    col = pl.BlockSpec((l, LANES), lambda j: (0, j))
    return pl.pallas_call(
        body, name=name, grid=(d // LANES,), in_specs=[col], out_specs=col,
        out_shape=jax.ShapeDtypeStruct((l, d), F32), compiler_params=_cparams("parallel"),
    )(ddiff)


def _pool_mix_fwd(h, diff_b, w_pool, scale, *, name):
    l, d = h.shape
    gd = d // N_GROUPS
    tr = _tile(l, 1408)

    def body(h_ref, d_ref, w_ref, s_ref, o_ref):
        y = jnp.dot(d_ref[...], w_ref[0], preferred_element_type=F32)
        o_ref[...] = h_ref[...] + y * s_ref[...]

    blk = pl.BlockSpec((tr, gd), lambda i, g: (i, g))
    vec = pl.BlockSpec((1, gd), lambda i, g: (0, g))
    return pl.pallas_call(
        body, name=name, grid=(l // tr, N_GROUPS),
        in_specs=[blk, blk, pl.BlockSpec((1, gd, gd), lambda i, g: (g, 0, 0)), vec],
        out_specs=blk, out_shape=jax.ShapeDtypeStruct((l, d), F32),
        compiler_params=_cparams("parallel", "parallel"),
    )(h, diff_b, w_pool, scale)


def _pool_mix_bwd(diff_b, w_pool, scale, dh1, *, name):
    l, d = dh1.shape
    gd = d // N_GROUPS
    tr = _tile(l, 1408)
    nsteps = l // tr

    def body(d_ref, w_ref, s_ref, dy_ref, dd_ref, dw_ref, ds_ref, sacc):
        i = pl.program_id(1)

        @pl.when(i == 0)
        def _():
            dw_ref[...] = jnp.zeros_like(dw_ref)
            sacc[...] = jnp.zeros_like(sacc)

        diff_b_, wg, dy = d_ref[...], w_ref[0], dy_ref[...]
        yy = jnp.dot(diff_b_, wg, preferred_element_type=F32)
        sacc[...] += _rows8(dy * yy)
        dyy_b = (dy * s_ref[...]).astype(BF16)
        dw_ref[0] += lax.dot_general(diff_b_, dyy_b, _DIMS["tn"], preferred_element_type=F32)
        dd_ref[...] = lax.dot_general(dyy_b, wg, _DIMS["nt"], preferred_element_type=F32)

        @pl.when(i == nsteps - 1)
        def _():
            ds_ref[...] = jnp.sum(sacc[...], axis=0, keepdims=True)

    blk = pl.BlockSpec((tr, gd), lambda g, i: (i, g))
    vec = pl.BlockSpec((1, gd), lambda g, i: (0, g))
    wspec = pl.BlockSpec((1, gd, gd), lambda g, i: (g, 0, 0))
    return pl.pallas_call(
        body, name=name, grid=(N_GROUPS, nsteps),
        in_specs=[blk, wspec, vec, blk],
        out_specs=[blk, wspec, vec],
        out_shape=[jax.ShapeDtypeStruct((l, d), F32), jax.ShapeDtypeStruct((N_GROUPS, gd, gd), F32),
                   jax.ShapeDtypeStruct((1, d), F32)],
        scratch_shapes=[pltpu.VMEM((8, gd), F32)],
        compiler_params=_cparams("parallel", "arbitrary"),
    )(diff_b, w_pool, scale, dh1)


def _conv_chunk(cur, prev, w, b, rowi):
    s1 = jnp.where(rowi < 1, pltpu.roll(prev, 1, 0), pltpu.roll(cur, 1, 0))
    s2 = jnp.where(rowi < 2, pltpu.roll(prev, 2, 0), pltpu.roll(cur, 2, 0))
    return b + w[0] * s2 + w[1] * s1 + w[2] * cur, s1, s2


def _bcast_rows(ref, rows):
    v = ref[...]
    return [jnp.broadcast_to(v[k:k + 1], (rows, v.shape[1])) for k in range(v.shape[0])]


def _convgate_fwd(u, cw, cb, *, name):
    l, f2 = u.shape
    f = f2 // 2
    tc = _tile(f, LANES)
    nc = f // tc
    rows = CONV_ROWS
    assert l % rows == 0

    def body(ug_ref, uv_ref, wg_ref, wv_ref, bg_ref, bv_ref, a_ref):
        rowi = lax.broadcasted_iota(jnp.int32, (rows, tc), 0)
        wg, wv = _bcast_rows(wg_ref, rows), _bcast_rows(wv_ref, rows)
        bg, bv = _bcast_rows(bg_ref, rows)[0], _bcast_rows(bv_ref, rows)[0]

        def chunk(i, carry):
            pg, pv = carry
            r = pl.multiple_of(i * rows, rows)
            cg, cv = ug_ref[pl.ds(r, rows), :], uv_ref[pl.ds(r, rows), :]
            gate, _, _ = _conv_chunk(cg, pg, wg, bg, rowi)
            val, _, _ = _conv_chunk(cv, pv, wv, bv, rowi)
            a_ref[pl.ds(r, rows), :] = (gate * jax.nn.sigmoid(gate) * val).astype(BF16)
            return cg, cv

        zero = jnp.zeros((rows, tc), F32)
        lax.fori_loop(0, l // rows, chunk, (zero, zero))

    def spec(rows_, off):
        return pl.BlockSpec((rows_, tc), lambda j: (0, j + off))

    return pl.pallas_call(
        body, name=name, grid=(nc,),
        in_specs=[spec(l, 0), spec(l, nc), spec(3, 0), spec(3, nc), spec(1, 0), spec(1, nc)],
        out_specs=spec(l, 0), out_shape=jax.ShapeDtypeStruct((l, f), BF16),
        compiler_params=_cparams("parallel"),
    )(u, u, cw, cw, cb, cb)


def _convgate_bwd(u, cw, cb, da, *, name):
    l, f2 = u.shape
    f = f2 // 2
    tc = _tile(f, LANES)
    nc = f // tc
    rows = CONV_ROWS
    assert l % rows == 0
    nchunks = l // rows

    def body(ug_ref, uv_ref, wg_ref, wv_ref, bg_ref, bv_ref, da_ref,
             dug_ref, duv_ref, dcwg_ref, dcwv_ref, dcbg_ref, dcbv_ref):
        rowi = lax.broadcasted_iota(jnp.int32, (rows, tc), 0)
        wg, wv = _bcast_rows(wg_ref, rows), _bcast_rows(wv_ref, rows)
        bg, bv = _bcast_rows(bg_ref, rows)[0], _bcast_rows(bv_ref, rows)[0]

        def tap_sums(acc, dc, taps):
            return (acc[0] + _rows8(dc),) + tuple(a + _rows8(dc * t) for a, t in zip(acc[1:], taps))

        def input_grad(dc, dc_next, w):
            up1 = jnp.where(rowi >= rows - 1, pltpu.roll(dc_next, rows - 1, 0), pltpu.roll(dc, rows - 1, 0))
            up2 = jnp.where(rowi >= rows - 2, pltpu.roll(dc_next, rows - 2, 0), pltpu.roll(dc, rows - 2, 0))
            return (w[2] * dc + w[1] * up1 + w[0] * up2).astype(BF16)

        def chunk(ii, carry):
            dcg_next, dcv_next, acc_g, acc_v = carry
            i = nchunks - 1 - ii
            r = pl.multiple_of(i * rows, rows)
            rp = pl.multiple_of(jnp.maximum(i - 1, 0) * rows, rows)
            cg, cv = ug_ref[pl.ds(r, rows), :], uv_ref[pl.ds(r, rows), :]
            pg = jnp.where(i > 0, ug_ref[pl.ds(rp, rows), :], 0.0)
            pv = jnp.where(i > 0, uv_ref[pl.ds(rp, rows), :], 0.0)
            gate, g1, g2 = _conv_chunk(cg, pg, wg, bg, rowi)
            val, v1, v2 = _conv_chunk(cv, pv, wv, bv, rowi)
            sg = jax.nn.sigmoid(gate)
            dav = da_ref[pl.ds(r, rows), :]
            dcg = dav * val * (sg * (1.0 + gate * (1.0 - sg)))
            dcv = dav * (gate * sg)
            acc_g = tap_sums(acc_g, dcg, (g2, g1, cg))
            acc_v = tap_sums(acc_v, dcv, (v2, v1, cv))
            dug_ref[pl.ds(r, rows), :] = input_grad(dcg, dcg_next, wg)
            duv_ref[pl.ds(r, rows), :] = input_grad(dcv, dcv_next, wv)
            return dcg, dcv, acc_g, acc_v

        zero = jnp.zeros((rows, tc), F32)
        zero8 = (jnp.zeros((8, tc), F32),) * 4
        _, _, acc_g, acc_v = lax.fori_loop(0, nchunks, chunk, (zero, zero, zero8, zero8))
        for acc, dcw_ref, dcb_ref in ((acc_g, dcwg_ref, dcbg_ref), (acc_v, dcwv_ref, dcbv_ref)):
            dcb_ref[...] = jnp.sum(acc[0], axis=0, keepdims=True)
            for k in range(3):
                dcw_ref[k:k + 1, :] = jnp.sum(acc[1 + k], axis=0, keepdims=True)

    def spec(rows_, off):
        return pl.BlockSpec((rows_, tc), lambda j: (0, j + off))

    return pl.pallas_call(
        body, name=name, grid=(nc,),
        in_specs=[spec(l, 0), spec(l, nc), spec(3, 0), spec(3, nc), spec(1, 0), spec(1, nc), spec(l, 0)],
        out_specs=[spec(l, 0), spec(l, 0), spec(3, 0), spec(3, 0), spec(1, 0), spec(1, 0)],
        out_shape=[jax.ShapeDtypeStruct((l, f), BF16)] * 2 + [jax.ShapeDtypeStruct((3, f), F32)] * 2
        + [jax.ShapeDtypeStruct((1, f), F32)] * 2,
        compiler_params=_cparams("parallel"),
    )(u, u, cw, cw, cb, cb, da)


def _cumsum_mm(v_b, t2):
    return jnp.dot(v_b, t2, preferred_element_type=F32)


def _tri_and_ones(tri_fn):
    row = lax.broadcasted_iota(jnp.int32, (BLK, 2 * BLK), 0)
    col = lax.broadcasted_iota(jnp.int32, (BLK, 2 * BLK), 1)
    return jnp.where((col >= BLK) | tri_fn(row, col), 1.0, 0.0).astype(BF16)


def _logits(z, mask):
    sp = jnp.log(1.0 + jnp.exp(-jnp.abs(z)))
    lb = jnp.minimum(z, 0.0) - sp
    lm = lb - z
    if mask is not None:
        lm = jnp.where(mask, lm, 0.0)
    return lb, lm


def _software_pipeline(stages, n, block_of, state):
    ns = len(stages)
    inflight = [None] * (ns - 1)
    for t in range(ns - 1):
        new = list(inflight)
        for s in range(t, -1, -1):
            y, state = stages[s](block_of(t - s), None if s == 0 else inflight[s - 1], state)
            new[s] = y
        inflight = new

    def steady(i, carry):
        inflight, state = carry
        new = [None] * (ns - 1)
        for s in range(ns - 1, -1, -1):
            y, state = stages[s](block_of(i + ns - 1 - s), None if s == 0 else inflight[s - 1], state)
            if s < ns - 1:
                new[s] = y
        return tuple(new), state

    inflight, state = lax.fori_loop(0, n - (ns - 1), steady, (tuple(inflight), state))
    inflight = list(inflight)
    for e in range(1, ns):
        new = list(inflight)
        for s in range(ns - 1, e - 1, -1):
            y, state = stages[s](block_of(n - 1 + e - s), inflight[s - 1], state)
            if s < ns - 1:
                new[s] = y
        inflight = new
    return state


def _attn_fwd(q, k, v, *, name):
    nh, l, dh = q.shape
    assert l % ATT_BQ == 0
    nq = l // ATT_BQ
    qscale = HEAD_DIM ** -0.5
    chunks = range(ATT_CHUNKS)

    def body(q_ref, k_ref, v_ref, o_ref, lt_ref, z_scr, cs_scr, pv_scr):
        t_later = _tri_and_ones(lambda r, c: r > c)
        dmask = (lax.broadcasted_iota(jnp.int32, (BLK, BLK), 1)
                 < lax.broadcasted_iota(jnp.int32, (BLK, BLK), 0))

        def logits(qc, c0, mask):
            z = lax.dot_general(qc, k_ref[0, pl.ds(c0, BLK), :], _DIMS["nt"], preferred_element_type=F32)
            lb, lm = _logits(z, mask)
            return lb, lm.astype(BF16)

        def weights(lb, lm_b, mask, run):
            cs = _cumsum_mm(lm_b, t_later)
            a = jnp.exp(lb + cs[:, :BLK] + run)
            if mask is not None:
                a = jnp.where(mask, a, 0.0)
            return a.astype(BF16), run + cs[:, BLK:]

        def accumulate(a_b, c0, acc):
            return acc + jnp.dot(a_b, v_ref[0, pl.ds(c0, BLK), :], preferred_element_type=F32)

        def qblock(qb, _):
            r0 = pl.multiple_of(qb * ATT_BQ, ATT_BQ)
            qs = [q_ref[0, pl.ds(r0 + rc * BLK, BLK), :] * jnp.asarray(qscale, BF16) for rc in chunks]
            accs = [jnp.zeros((BLK, dh), F32)] * ATT_CHUNKS
            runs = [jnp.zeros((BLK, BLK), F32)] * ATT_CHUNKS
            for dj in range(ATT_CHUNKS - 1, -1, -1):
                c0 = r0 + dj * BLK
                for rc in range(dj, ATT_CHUNKS):
                    mask = dmask if rc == dj else None
                    a_b, runs[rc] = weights(*logits(qs[rc], c0, mask), mask, runs[rc])
                    accs[rc] = accumulate(a_b, c0, accs[rc])

            def col0(b):
                return pl.multiple_of(r0 - (b + 1) * BLK, BLK)

            def stage_scores(b, _, state):
                for rc in chunks:
                    z_scr[rc] = lax.dot_general(qs[rc], k_ref[0, pl.ds(col0(b), BLK), :], _DIMS["nt"],
                                                preferred_element_type=F32)
                return (), state

            def stage_cumsum(b, _, state):
                lbs = []
                for rc in chunks:
                    lb, lm = _logits(z_scr[rc], None)
                    cs_scr[rc] = _cumsum_mm(lm.astype(BF16), t_later)
                    lbs.append(lb)
                return tuple(lbs), state

            def stage_weights(b, lbs, state):
                accs, runs = state
                new_runs = []
                for rc in chunks:
                    cs = cs_scr[rc]
                    a = jnp.exp(lbs[rc] + cs[:, :BLK] + runs[rc])
                    pv_scr[rc] = jnp.dot(a.astype(BF16), v_ref[0, pl.ds(col0(b), BLK), :],
                                         preferred_element_type=F32)
                    new_runs.append(runs[rc] + cs[:, BLK:])
                return (), (accs, tuple(new_runs))

            def stage_acc(b, _, state):
                accs, runs = state
                return None, (tuple(accs[rc] + pv_scr[rc] for rc in chunks), runs)

            def left_region(state):
                return _software_pipeline([stage_scores, stage_cumsum, stage_weights, stage_acc],
                                          qb * ATT_CHUNKS, lambda b: b, state)

            accs, runs = lax.cond(qb > 0, left_region, lambda s: s, (tuple(accs), tuple(runs)))
            for rc in chunks:
                o_ref[0, pl.ds(r0 + rc * BLK, BLK), :] = accs[rc].astype(BF16)
                lt_ref[0, pl.ds(r0 + rc * BLK, BLK), :] = runs[rc]
            return 0

        lax.fori_loop(0, nq, qblock, 0)

    head = pl.BlockSpec((1, l, dh), lambda h: (h, 0, 0))
    return pl.pallas_call(
        body, name=name, grid=(nh,),
        in_specs=[head, head, head],
        out_specs=[head, pl.BlockSpec((1, l, BLK), lambda h: (h, 0, 0))],
        out_shape=[jax.ShapeDtypeStruct((nh, l, dh), BF16), jax.ShapeDtypeStruct((nh, l, BLK), F32)],
        scratch_shapes=[pltpu.VMEM((ATT_CHUNKS, BLK, BLK), F32), pltpu.VMEM((ATT_CHUNKS, BLK, 2 * BLK), F32),
                        pltpu.VMEM((ATT_CHUNKS, BLK, dh), F32)],
        compiler_params=_cparams("parallel"),
    )(q, k, v)


def _attn_bwd(q, k, v, do, ltot, *, name):
    nh, l, dh = q.shape
    assert l % ATT_BQ == 0
    nq = l // ATT_BQ
    qscale = HEAD_DIM ** -0.5
    chunks = range(ATT_CHUNKS)

    def body(q_ref, k_ref, v_ref, do_ref, lt_ref, dq_ref, dk_ref, dv_ref):
        t_incl = _tri_and_ones(lambda r, c: r <= c)
        t_excl = _tri_and_ones(lambda r, c: r < c)
        dmask = (lax.broadcasted_iota(jnp.int32, (BLK, BLK), 1)
                 < lax.broadcasted_iota(jnp.int32, (BLK, BLK), 0))
        dk_ref[...] = jnp.zeros_like(dk_ref)
        dv_ref[...] = jnp.zeros_like(dv_ref)

        def logits(qc, c0, mask):
            z = lax.dot_general(qc, k_ref[0, pl.ds(c0, BLK), :], _DIMS["nt"], preferred_element_type=F32)
            lb, lm = _logits(z, mask)
            return lb, lm.astype(BF16)

        def weights(lb, lm_b, doc, ltc, c0, mask, pre_lm):
            cs = _cumsum_mm(lm_b, t_incl)
            da = lax.dot_general(doc, v_ref[0, pl.ds(c0, BLK), :], _DIMS["nt"], preferred_element_type=F32)
            a = jnp.exp(lb + (ltc - pre_lm - cs[:, :BLK]))
            if mask is not None:
                a = jnp.where(mask, a, 0.0)
            dl = a * da
            return (jnp.exp(lb), dl, dl.astype(BF16), a.astype(BF16)), pre_lm + cs[:, BLK:]

        def logit_grad(beta, dl, dl_b, mask, pre_dl):
            cd = _cumsum_mm(dl_b, t_excl)
            dz = dl - beta * (dl + pre_dl + cd[:, :BLK])
            if mask is not None:
                dz = jnp.where(mask, dz, 0.0)
            return (dz * qscale).astype(BF16), pre_dl + cd[:, BLK:]

        def key_grads(c0, dz_parts, a_parts, q_rows, do_rows):
            dz_all = dz_parts[0] if len(dz_parts) == 1 else jnp.concatenate(dz_parts, axis=0)
            a_all = a_parts[0] if len(a_parts) == 1 else jnp.concatenate(a_parts, axis=0)
            dk_ref[0, pl.ds(c0, BLK), :] += lax.dot_general(dz_all, q_rows, _DIMS["tn"], preferred_element_type=F32)
            dv_ref[0, pl.ds(c0, BLK), :] += lax.dot_general(a_all, do_rows, _DIMS["tn"], preferred_element_type=F32)

        def qblock(qb, _):
            r0 = pl.multiple_of(qb * ATT_BQ, ATT_BQ)
            q_raw = q_ref[0, pl.ds(r0, ATT_BQ), :]
            do_all = do_ref[0, pl.ds(r0, ATT_BQ), :]
            qs = [q_raw[rc * BLK:(rc + 1) * BLK] * jnp.asarray(qscale, BF16) for rc in chunks]
            dos = [do_all[rc * BLK:(rc + 1) * BLK] for rc in chunks]
            lts = [lt_ref[0, pl.ds(r0 + rc * BLK, BLK), :] for rc in chunks]

            def col0(b):
                return pl.multiple_of(b * BLK, BLK)

            def stage_logits(b, _, state):
                return tuple(logits(qs[rc], col0(b), None) for rc in chunks), state

            def stage_weights(b, xs, state):
                dqs, pls, pds = state
                out = [weights(*xs[rc], dos[rc], lts[rc], col0(b), None, pls[rc]) for rc in chunks]
                return tuple(o[0] for o in out), (dqs, tuple(o[1] for o in out), pds)

            def stage_logit_grad(b, ys, state):
                dqs, pls, pds = state
                out = [logit_grad(*ys[rc][:3], None, pds[rc]) for rc in chunks]
                return tuple((out[rc][0], ys[rc][3]) for rc in chunks), (dqs, pls, tuple(o[1] for o in out))

            def stage_grads(b, ws, state):
                dqs, pls, pds = state
                c0 = col0(b)
                kj = k_ref[0, pl.ds(c0, BLK), :]
                dqs = tuple(dqs[rc] + jnp.dot(ws[rc][0], kj, preferred_element_type=F32) for rc in chunks)
                key_grads(c0, [w[0] for w in ws], [w[1] for w in ws], q_raw, do_all)
                return None, (dqs, pls, pds)

            zero = jnp.zeros((BLK, BLK), F32)
            state = ((jnp.zeros((BLK, dh), F32),) * ATT_CHUNKS, (zero,) * ATT_CHUNKS, (zero,) * ATT_CHUNKS)

            def left_region(state):
                return _software_pipeline([stage_logits, stage_weights, stage_logit_grad, stage_grads],
                                          qb * ATT_CHUNKS, lambda b: b, state)

            dqs, pls, pds = lax.cond(qb > 0, left_region, lambda s: s, state)
            dqs, pls, pds = list(dqs), list(pls), list(pds)
            for dj in chunks:
                c0 = r0 + dj * BLK
                kj = k_ref[0, pl.ds(c0, BLK), :]
                dz_parts, a_parts = [], []
                for rc in range(dj, ATT_CHUNKS):
                    mask = dmask if rc == dj else None
                    ys, pls[rc] = weights(*logits(qs[rc], c0, mask), dos[rc], lts[rc], c0, mask, pls[rc])
                    dz_b, pds[rc] = logit_grad(*ys[:3], mask, pds[rc])
                    a_b = ys[3]
                    dqs[rc] = dqs[rc] + jnp.dot(dz_b, kj, preferred_element_type=F32)
                    dz_parts.append(dz_b)
                    a_parts.append(a_b)
                key_grads(c0, dz_parts, a_parts, q_raw[dj * BLK:], do_all[dj * BLK:])
            for rc in chunks:
                dq_ref[0, pl.ds(r0 + rc * BLK, BLK), :] = dqs[rc]
            return 0

        lax.fori_loop(0, nq, qblock, 0)

    head = pl.BlockSpec((1, l, dh), lambda h: (h, 0, 0))
    return pl.pallas_call(
        body, name=name, grid=(nh,),
        in_specs=[head, head, head, head, pl.BlockSpec((1, l, BLK), lambda h: (h, 0, 0))],
        out_specs=[head, head, head],
        out_shape=[jax.ShapeDtypeStruct((nh, l, dh), F32)] * 3,
        compiler_params=_cparams("parallel"),
    )(q, k, v, do, ltot)


def _add_to_bf16(a, b, *, name):
    n, r, c = a.shape
    tr = _row_tile(r, 1536, 16)
    blk = pl.BlockSpec((1, tr, c), lambda i, j: (i, j, 0))

    def body(a_ref, b_ref, o_ref):
        o_ref[...] = (a_ref[...] + b_ref[...]).astype(BF16)

    return pl.pallas_call(body, name=name, grid=(n, r // tr), in_specs=[blk, blk], out_specs=blk,
                          out_shape=jax.ShapeDtypeStruct(a.shape, BF16),
                          compiler_params=_cparams("parallel", "parallel"))(a, b)


def _sum_slots(p, *, name):
    n, r, c = p.shape
    tr = _row_tile(r, 1536, 16)

    def body(p_ref, o_ref):
        acc = p_ref[0].astype(F32)
        for s in range(1, n):
            acc = acc + p_ref[s].astype(F32)
        o_ref[...] = acc

    return pl.pallas_call(body, name=name, grid=(r // tr,),
                          in_specs=[pl.BlockSpec((n, tr, c), lambda j: (0, j, 0))],
                          out_specs=pl.BlockSpec((tr, c), lambda j: (j, 0)),
                          out_shape=jax.ShapeDtypeStruct((r, c), F32),
                          compiler_params=_cparams("parallel"))(p)


def _adamw(w, g, m, v, *, name):
    r, c = w.shape
    tr = _row_tile(r, 512, 8)
    blk = pl.BlockSpec((tr, c), lambda i: (i, 0))

    def body(w_ref, g_ref, m_ref, v_ref, d_ref, mo_ref, vo_ref):
        gv = g_ref[...]
        mn = ADAM_B1 * m_ref[...] + (1.0 - ADAM_B1) * gv
        vn = ADAM_B2 * v_ref[...] + (1.0 - ADAM_B2) * (gv * gv)
        m_hat = mn / (1.0 - ADAM_B1 ** ADAM_STEP)
        v_hat = vn / (1.0 - ADAM_B2 ** ADAM_STEP)
        d_ref[...] = -ADAM_LR * (m_hat / (jnp.sqrt(v_hat) + ADAM_EPS) + ADAM_WD * w_ref[...])
        mo_ref[...] = mn
        vo_ref[...] = vn

    return pl.pallas_call(body, name=name, grid=(r // tr,), in_specs=[blk] * 4, out_specs=[blk] * 3,
                          out_shape=[jax.ShapeDtypeStruct((r, c), F32)] * 3,
                          compiler_params=_cparams("parallel"))(w, g, m, v)


_ANY = pl.BlockSpec(memory_space=pl.ANY)


def _place():
    x, y, c = lax.axis_index("x"), lax.axis_index("y"), lax.axis_index("c")
    chips = [(1 - x, y), (x, 1 - y), (1 - x, 1 - y)]
    return x, y, c, chips


def _gather_shards(wsh, *, name):
    _, r, cdim = wsh.shape

    def body(w_ref, out_ref, send_sems, recv_sems):
        x, y, c, chips = _place()
        s = 2 * x + y
        sibling = (x, y, 1 - c)

        def copy(k, src, dst, to):
            return pltpu.make_async_remote_copy(src_ref=src, dst_ref=dst, send_sem=send_sems.at[k],
                                                recv_sem=recv_sems.at[k], device_id=to, device_id_type=MESH)

        first = [copy(j, w_ref.at[c], out_ref.at[s, c], (*chip, c)) for j, chip in enumerate(chips)]
        for cp in first:
            cp.start()
        passed = []
        for j, (px, py) in enumerate(chips):
            sp = 2 * px + py
            copy(j, w_ref.at[c], out_ref.at[sp, c], sibling).wait_recv()
            fwd = copy(3 + j, out_ref.at[sp, c], out_ref.at[sp, c], sibling)
            fwd.start()
            passed.append(fwd)
        for j, (px, py) in enumerate(chips):
            sp = 2 * px + py
            copy(3 + j, w_ref.at[c], out_ref.at[sp, 1 - c], sibling).wait_recv()
        for cp in first + passed:
            cp.wait_send()

    return pl.pallas_call(
        body, name=name, in_specs=[_ANY], out_specs=_ANY,
        out_shape=jax.ShapeDtypeStruct((N_CHIPS, 2, r, cdim), wsh.dtype),
        scratch_shapes=[pltpu.SemaphoreType.DMA((6,)), pltpu.SemaphoreType.DMA((6,))],
    )(wsh)


def _to_sibling(a, *, name):
    def body(a_ref, out_ref, send_sem, recv_sem):
        x, y, c, _ = _place()
        cp = pltpu.make_async_remote_copy(src_ref=a_ref, dst_ref=out_ref, send_sem=send_sem, recv_sem=recv_sem,
                                          device_id=(x, y, 1 - c), device_id_type=MESH)
        cp.start()
        cp.wait()

    return pl.pallas_call(
        body, name=name, in_specs=[_ANY], out_specs=_ANY,
        out_shape=jax.ShapeDtypeStruct(a.shape, a.dtype),
        scratch_shapes=[pltpu.SemaphoreType.DMA, pltpu.SemaphoreType.DMA],
    )(a)


def _scatter_rows(p, *, name):
    def body(p_ref, out_ref, send_sems, recv_sems):
        x, y, c, chips = _place()
        s = 2 * x + y
        sends = []
        for j, (px, py) in enumerate(chips):
            sp = 2 * px + py
            cp = pltpu.make_async_remote_copy(src_ref=p_ref.at[sp], dst_ref=out_ref.at[s], send_sem=send_sems.at[j],
                                              recv_sem=recv_sems.at[j], device_id=(px, py, c), device_id_type=MESH)
            cp.start()
            sends.append(cp)
        for j, (px, py) in enumerate(chips):
            sp = 2 * px + py
            pltpu.make_async_remote_copy(src_ref=p_ref.at[sp], dst_ref=out_ref.at[sp], send_sem=send_sems.at[j],
                                         recv_sem=recv_sems.at[j], device_id=(px, py, c),
                                         device_id_type=MESH).wait_recv()
        for cp in sends:
            cp.wait_send()

    return pl.pallas_call(
        body, name=name, in_specs=[_ANY], out_specs=_ANY,
        out_shape=jax.ShapeDtypeStruct(p.shape, p.dtype),
        scratch_shapes=[pltpu.SemaphoreType.DMA((3,)), pltpu.SemaphoreType.DMA((3,))],
    )(p)


def _all_devices(a, reduce, *, name):
    r, cdim = a.shape

    def body(a_ref, out_ref, *scratch):
        if reduce:
            buf, send_sems, recv_sems = scratch
        else:
            buf = out_ref
            send_sems, recv_sems = scratch
        x, y, c, _ = _place()
        me = 4 * x + 2 * y + c
        buf[me] = a_ref[...]
        peers = []
        for k in range(1, N_DEV):
            dx, dy, dc = (k >> 2) & 1, (k >> 1) & 1, k & 1
            peers.append((x ^ dx, y ^ dy, c ^ dc))
        sends = []
        for k, peer in enumerate(peers):
            cp = pltpu.make_async_remote_copy(src_ref=a_ref, dst_ref=buf.at[me], send_sem=send_sems.at[k],
                                              recv_sem=recv_sems.at[k], device_id=peer, device_id_type=MESH)
            cp.start()
            sends.append(cp)
        for k, (px, py, pc) in enumerate(peers):
            pltpu.make_async_remote_copy(src_ref=a_ref, dst_ref=buf.at[4 * px + 2 * py + pc],
                                         send_sem=send_sems.at[k], recv_sem=recv_sems.at[k],
                                         device_id=(px, py, pc), device_id_type=MESH).wait_recv()
        for cp in sends:
            cp.wait_send()
        if reduce:
            acc = buf[0]
            for k in range(1, N_DEV):
                acc = acc + buf[k]
            out_ref[...] = acc

    vm = pl.BlockSpec(memory_space=pltpu.VMEM)
    sems = [pltpu.SemaphoreType.DMA((N_DEV - 1,)), pltpu.SemaphoreType.DMA((N_DEV - 1,))]
    if reduce:
        out_shape = jax.ShapeDtypeStruct((r, cdim), F32)
        scratch = [pltpu.VMEM((N_DEV, r, cdim), F32)] + sems
    else:
        out_shape = jax.ShapeDtypeStruct((N_DEV, r, cdim), F32)
        scratch = sems
    return pl.pallas_call(
        body, name=name, in_specs=[vm], out_specs=vm, out_shape=out_shape, scratch_shapes=scratch,
        compiler_params=pltpu.CompilerParams(vmem_limit_bytes=VMEM_LIMIT),
    )(a)


def _flat_rows(parts, cols):
    flat = jnp.concatenate([p.reshape(-1) for p in parts])
    padded = -(-flat.size // (8 * cols)) * (8 * cols)
    return jnp.pad(flat, (0, padded - flat.size)).reshape(-1, cols)


def kernel(x, meta_tokens, mix_norm, ffn_norm, pool_w, pool_scale, kv_norm, w_kv, w_q, w_o, ffn_w_up, ffn_conv_w, ffn_conv_b, ffn_w_down, final_norm, loss_target, m_meta_tokens, m_mix_norm, m_ffn_norm, m_pool_w, m_pool_scale, m_kv_norm, m_w_kv, m_w_q, m_w_o, m_ffn_w_up, m_ffn_conv_w, m_ffn_conv_b, m_ffn_w_down, m_final_norm, v_meta_tokens, v_mix_norm, v_ffn_norm, v_pool_w, v_pool_scale, v_kv_norm, v_w_kv, v_w_q, v_w_o, v_ffn_w_up, v_ffn_conv_w, v_ffn_conv_b, v_ffn_w_down, v_final_norm):
    seq, d = x.shape[1], x.shape[2]
    l_real = N_META + seq
    lp = -(-l_real // ATT_BQ) * ATT_BQ
    f2 = ffn_w_up.shape[2] * N_CHIPS
    f = f2 // 2
    gd = d // N_GROUPS
    chip = 2 * lax.axis_index("x") + lax.axis_index("y")
    core = lax.axis_index("c")

    big_parts = [pool_w[0], w_kv, w_q[0], w_o[0], ffn_w_up, ffn_w_down]
    sizes = [p.size for p in big_parts]
    n_big = sum(sizes)
    cw = 1024
    assert n_big % (2 * 16 * cw) == 0
    rows_half = n_big // (2 * cw)
    wsh = _flat_rows([p.astype(BF16) for p in big_parts], cw).reshape(2, rows_half, cw)
    wall = lax.dynamic_update_index_in_dim(_gather_shards(wsh, name="gather_weights"), wsh, chip, 0)
    wall = wall.reshape(N_CHIPS, n_big)
    offs = [0]
    for sz in sizes:
        offs.append(offs[-1] + sz)

    def big(i, shape):
        return wall[:, offs[i]:offs[i + 1]].reshape((N_CHIPS,) + shape)

    wp_b = big(0, (N_GROUPS, gd // N_CHIPS, gd)).transpose(1, 0, 2, 3).reshape(N_GROUPS, gd, gd)
    wkv_b = big(1, (d, 2 * d // N_CHIPS)).transpose(1, 0, 2).reshape(d, 2 * d)
    wq_b = big(2, (d // N_CHIPS, d)).reshape(d, d)
    wo_b = big(3, (d // N_CHIPS, d)).reshape(d, d)
    wup_b = big(4, (2, d, f2 // N_CHIPS)).transpose(1, 2, 0, 3).reshape(2, d, f2)
    wdn_b = big(5, (2, f // N_CHIPS, d)).transpose(1, 0, 2, 3).reshape(2, f, d)

    small_parts = [meta_tokens, pool_scale, ffn_conv_w]
    ssizes = [p.size for p in small_parts]
    small = _flat_rows(small_parts, LANES)
    sall = _all_devices(small, False, name="gather_small")[::2].reshape(N_CHIPS, -1)
    meta_f = sall[:, :ssizes[0]].reshape(N_CHIPS, N_META, d // N_CHIPS).transpose(1, 0, 2).reshape(N_META, d)
    scale_f = sall[:, ssizes[0]:ssizes[0] + ssizes[1]].reshape(1, d)
    cw_f = sall[:, ssizes[0] + ssizes[1]:sum(ssizes)].reshape(N_CHIPS, 2, 3, f2 // N_CHIPS).transpose(1, 2, 0, 3).reshape(2, 3, f2)

    mix0, mix1 = mix_norm[0:1], mix_norm[1:2]
    fn0, fn1 = ffn_norm[0:1], ffn_norm[1:2]
    kvn_g = kv_norm.reshape(1, d)
    fin_g = final_norm.reshape(1, d)

    pad = lp - l_real
    h0 = jnp.concatenate([meta_f, x[0], jnp.zeros((pad, d), F32)], axis=0)
    tgt = jnp.pad(loss_target[0], ((N_META, pad), (0, 0)))

    _, r0 = _norm_fwd(h0, [mix0], name="norm_h0")
    diff_b = _pool_diff_fwd(h0, r0, mix0, name="pool_diff")
    h1 = _pool_mix_fwd(h0, diff_b, wp_b, scale_f, name="pool_mix")

    def ffn_fwd(h_in, layer, tag):
        (fb,), rf = _norm_fwd(h_in, [ffn_norm[layer:layer + 1]], name=f"norm_ffn{tag}")
        u = _mm(fb, wup_b[layer], "nn", tm=1408, tn=512, name=f"ffn_up{tag}")
        a = _convgate_fwd(u, cw_f[layer], ffn_conv_b[layer:layer + 1], name=f"convgate{tag}")
        h_out = _mm(a, wdn_b[layer], "nn", tm=384, tn=1024, res=h_in, name=f"ffn_down{tag}")
        return h_out, (fb, rf, u, a)

    h2, ffn0_saved = ffn_fwd(h1, 0, "0")
    (kvn_b, n1_b), r2 = _norm_fwd(h2, [kvn_g, mix1], name="norm_h2")
    kv_b = _mm(kvn_b, wkv_b, "nn", tm=1408, tn=512, out_dtype=BF16, name="kv_proj")
    q_b = _mm(n1_b, wq_b, "nn", tm=1408, tn=512, out_dtype=BF16, name="q_proj")

    def heads(t):
        return t.reshape(lp, N_HEADS, HEAD_DIM).transpose(1, 0, 2)

    def unheads(t):
        return t.transpose(1, 0, 2).reshape(lp, N_HEADS * HEAD_DIM)

    qh, kh, vh = heads(q_b), heads(kv_b[:, :d]), heads(kv_b[:, d:])
    oh, ltot = _attn_fwd(qh, kh, vh, name="attn_fwd")
    o_b = unheads(oh)
    h3 = _mm(o_b, wo_b, "nn", tm=384, tn=1024, res=h2, name="o_proj")
    h4, ffn1_saved = ffn_fwd(h3, 1, "1")

    loss8, dh4, dh4_b, dfin = _loss_bwd(h4, fin_g, tgt, seq, name="loss")

    def ffn_bwd(h_in, layer, saved, dh_out, dh_out_b, tag):
        fb, rf, u, a = saved
        dwdn = _mm(a, dh_out_b, "tn", tm=256, tn=1024, name=f"d_wdown{tag}")
        da = _mm(dh_out_b, wdn_b[layer], "nt", tm=384, tn=f, name=f"d_act{tag}")
        du_g, du_v, dcw_g, dcw_v, dcb_g, dcb_v = _convgate_bwd(u, cw_f[layer], ffn_conv_b[layer:layer + 1], da,
                                                               name=f"convgate_bwd{tag}")
        dcw = jnp.concatenate([dcw_g, dcw_v], axis=1)
        dcb = jnp.concatenate([dcb_g, dcb_v], axis=1)
        dwup = jnp.concatenate([_mm(fb, du_g, "tn", tm=1024, tn=512, name=f"d_wup_gate{tag}"),
                                _mm(fb, du_v, "tn", tm=1024, tn=512, name=f"d_wup_val{tag}")], axis=1)
        df = _mm(du_g, wup_b[layer][:, :f], "nt", tm=384, tn=1024, name=f"d_ffn_in_gate{tag}")
        df = _mm(du_v, wup_b[layer][:, f:], "nt", tm=384, tn=1024, res=df, name=f"d_ffn_in_val{tag}")
        dh_in, dh_in_b, (dfn,) = _norm_bwd(h_in, rf, [ffn_norm[layer:layer + 1]], [df], dh_out,
                                          name=f"norm_ffn_bwd{tag}")
        return dh_in, dh_in_b, dwup, dwdn, dcw, dcb, dfn

    dh3, dh3_b, dwup1, dwdn1, dcw1, dcb1, dfn1 = ffn_bwd(h3, 1, ffn1_saved, dh4, dh4_b, "1")

    dwo = _mm(o_b, dh3_b, "tn", tm=1024, tn=512, name="d_wo")
    do_b = _mm(dh3_b, wo_b, "nt", tm=384, tn=1024, out_dtype=BF16, name="d_attn_out")
    dqh, dkh, dvh = _attn_bwd(qh, kh, vh, heads(do_b), ltot, name="attn_bwd")
    dq_b = unheads(dqh).astype(BF16)
    dkv_b = jnp.concatenate([unheads(dkh), unheads(dvh)], axis=1).astype(BF16)
    dwq = _mm(n1_b, dq_b, "tn", tm=1024, tn=512, name="d_wq")
    dwkv = _mm(kvn_b, dkv_b, "tn", tm=1024, tn=512, name="d_wkv")
    dn1 = _mm(dq_b, wq_b, "nt", tm=384, tn=1024, name="d_n1")
    dkvn = _mm(dkv_b, wkv_b, "nt", tm=384, tn=1024, name="d_kvn")
    dh2, dh2_b, (dmix1, dkvg) = _norm_bwd(h2, r2, [mix1, kvn_g], [dn1, dkvn], dh3, name="norm_h2_bwd")

    dh1, dh1_b, dwup0, dwdn0, dcw0, dcb0, dfn0 = ffn_bwd(h1, 0, ffn0_saved, dh2, dh2_b, "0")

    ddiff, dwp, dscale = _pool_mix_bwd(diff_b, wp_b, scale_f, dh1, name="pool_mix_bwd")
    dn0 = _pool_diff_bwd(ddiff, name="pool_diff_bwd")
    dh0, _, (dmix0,) = _norm_bwd(h0, r0, [mix0], [dn0], dh1, name="norm_h0_bwd")

    grad_x = dh0[N_META:l_real][None]
    dmeta = dh0[:N_META]

    dwup = jnp.stack([dwup0, dwup1])
    dwdn = jnp.stack([dwdn0, dwdn1])
    g_parts = [
        dwp.reshape(N_GROUPS, N_CHIPS, gd // N_CHIPS, gd).transpose(1, 0, 2, 3),
        dwkv.reshape(d, N_CHIPS, 2 * d // N_CHIPS).transpose(1, 0, 2),
        dwq.reshape(N_CHIPS, d // N_CHIPS, d),
        dwo.reshape(N_CHIPS, d // N_CHIPS, d),
        dwup.reshape(2, d, N_CHIPS, f2 // N_CHIPS).transpose(2, 0, 1, 3),
        dwdn.reshape(2, N_CHIPS, f // N_CHIPS, d).transpose(1, 0, 2, 3),
    ]
    g_all = jnp.concatenate([p.reshape(N_CHIPS, -1) for p in g_parts], axis=1).reshape(N_CHIPS, 2, rows_half, cw)
    g_mine = lax.dynamic_index_in_dim(g_all, core, axis=1, keepdims=False)
    g_other = lax.dynamic_index_in_dim(g_all, 1 - core, axis=1, keepdims=False)
    from_sibling = _to_sibling(g_other, name="grads_to_sibling")
    chip_part = _add_to_bf16(g_mine, from_sibling, name="grads_chip_sum")
    slots = lax.dynamic_update_index_in_dim(_scatter_rows(chip_part, name="grads_scatter"),
                                            lax.dynamic_index_in_dim(chip_part, chip, 0, keepdims=False), chip, 0)
    half = _sum_slots(slots, name="grads_sum")
    other_half = _to_sibling(half, name="grads_join")
    g_shard = jnp.stack([jnp.where(core == 0, half, other_half),
                         jnp.where(core == 0, other_half, half)]).reshape(n_big)

    def gbig(i, shape):
        return g_shard[offs[i]:offs[i + 1]].reshape(shape)

    g_pool_w = gbig(0, pool_w.shape)
    g_w_kv = gbig(1, w_kv.shape)
    g_w_q = gbig(2, w_q.shape)
    g_w_o = gbig(3, w_o.shape)
    g_w_up = gbig(4, ffn_w_up.shape)
    g_w_dn = gbig(5, ffn_w_down.shape)

    dcw = jnp.stack([dcw0, dcw1])
    dcb = jnp.concatenate([dcb0, dcb1], axis=0)
    sg_parts = [jnp.concatenate([dmix0, dmix1], axis=0), jnp.concatenate([dfn0, dfn1], axis=0), dkvg, dfin, dcb,
                dmeta, dscale, dcw, loss8]
    sg_sizes = [p.size for p in sg_parts]
    sg = _all_devices(_flat_rows(sg_parts, LANES), True, name="reduce_small").reshape(-1)
    sg_offs = [0]
    for sz in sg_sizes:
        sg_offs.append(sg_offs[-1] + sz)

    def gsmall(i, shape):
        return sg[sg_offs[i]:sg_offs[i + 1]].reshape(shape)

    g_mix = gsmall(0, mix_norm.shape)
    g_ffn_norm = gsmall(1, ffn_norm.shape)
    g_kv_norm = gsmall(2, kv_norm.shape)
    g_final = gsmall(3, final_norm.shape)
    g_conv_b = gsmall(4, ffn_conv_b.shape)
    csh = d // N_CHIPS
    g_meta = lax.dynamic_slice_in_dim(gsmall(5, (N_META, d)), chip * csh, csh, axis=1)
    g_scale = lax.dynamic_slice_in_dim(gsmall(6, (1, d)), chip * csh, csh, axis=1)
    fsh = f2 // N_CHIPS
    g_conv_w = lax.dynamic_slice_in_dim(gsmall(7, (2, 3, f2)), chip * fsh, fsh, axis=2)
    loss = gsmall(8, (8 * LANES,))[0]

    weights = [meta_tokens, mix_norm, ffn_norm, pool_w, pool_scale, kv_norm, w_kv, w_q, w_o, ffn_w_up, ffn_conv_w,
               ffn_conv_b, ffn_w_down, final_norm]
    grads = [g_meta, g_mix, g_ffn_norm, g_pool_w, g_scale, g_kv_norm, g_w_kv, g_w_q, g_w_o, g_w_up, g_conv_w,
             g_conv_b, g_w_dn, g_final]
    ms = [m_meta_tokens, m_mix_norm, m_ffn_norm, m_pool_w, m_pool_scale, m_kv_norm, m_w_kv, m_w_q, m_w_o,
          m_ffn_w_up, m_ffn_conv_w, m_ffn_conv_b, m_ffn_w_down, m_final_norm]
    vs = [v_meta_tokens, v_mix_norm, v_ffn_norm, v_pool_w, v_pool_scale, v_kv_norm, v_w_kv, v_w_q, v_w_o,
          v_ffn_w_up, v_ffn_conv_w, v_ffn_conv_b, v_ffn_w_down, v_final_norm]
    names = ["meta", "mix", "ffnnorm", "poolw", "poolscale", "kvnorm", "wkv", "wq", "wo", "wup", "convw", "convb",
             "wdown", "final"]
    deltas, new_ms, new_vs = [], [], []
    for w, g, m, v, nm in zip(weights, grads, ms, vs, names):
        cols = w.shape[-1]
        if w.size % (8 * LANES) == 0 and w.ndim == 1:
            cols = LANES
        view = (w.size // cols, cols)
        dl, mn, vn = _adamw(w.reshape(view), g.reshape(view), m.reshape(view), v.reshape(view), name=f"adamw_{nm}")
        deltas.append(dl.reshape(w.shape))
        new_ms.append(mn.reshape(w.shape))
        new_vs.append(vn.reshape(w.shape))

    return (loss, grad_x, *grads, *deltas, *new_ms, *new_vs)
```

```python
import jax
import jax.numpy as jnp
from jax import lax
from jax.experimental import pallas as pl
from jax.experimental.pallas import tpu as pltpu

F32 = jnp.float32
BF16 = jnp.bfloat16

N_META = 16
N_HEADS = 16
HEAD_DIM = 64
POOL_WINDOWS = (2, 4, 8, 16)
N_GROUPS = 4
RMS_EPS = 1e-6
ADAM_LR = 0.001
ADAM_B1 = 0.9
ADAM_B2 = 0.999
ADAM_EPS = 1e-08
ADAM_WD = 0.01
ADAM_STEP = 10

LANES = 128
BLK = 128
ATT_CHUNKS = 3
ATT_BQ = ATT_CHUNKS * BLK
CONV_ROWS = 128
N_CHIPS = 4
N_DEV = 8
VMEM_LIMIT = 56 * 1024 * 1024
MESH = pl.DeviceIdType.MESH


def _cparams(*sem):
    return pltpu.CompilerParams(dimension_semantics=tuple(sem) if sem else None,
                                vmem_limit_bytes=VMEM_LIMIT)


def _tile(n, pref):
    best = None
    t = LANES
    while t <= min(n, pref):
        if n % t == 0:
            best = t
        t += LANES
    assert best is not None, (n, pref)
    return best


def _row_tile(r, pref, mult):
    best = r
    for t in range(mult, min(r, pref) + 1, mult):
        if r % t == 0:
            best = t
    return best


_DIMS = {
    "nn": (((1,), (0,)), ((), ())),
    "nt": (((1,), (1,)), ((), ())),
    "tn": (((0,), (0,)), ((), ())),
}


def _mm(a, b, dims, *, tm, tn, name, out_dtype=F32, res=None):
    if dims == "tn":
        k, m = a.shape
    else:
        m, k = a.shape
    n = b.shape[0] if dims == "nt" else b.shape[1]
    tm = _tile(m, tm)
    tn = _tile(n, tn)
    a_spec = pl.BlockSpec((k, tm), lambda i, j: (0, i)) if dims == "tn" else pl.BlockSpec((tm, k), lambda i, j: (i, 0))
    b_spec = pl.BlockSpec((tn, k), lambda i, j: (j, 0)) if dims == "nt" else pl.BlockSpec((k, tn), lambda i, j: (0, j))
    o_spec = pl.BlockSpec((tm, tn), lambda i, j: (i, j))
    dn = _DIMS[dims]

    def body(*refs):
        if res is None:
            a_ref, b_ref, o_ref = refs
        else:
            a_ref, b_ref, r_ref, o_ref = refs
        acc = lax.dot_general(a_ref[...], b_ref[...], dn, preferred_element_type=F32)
        if res is not None:
            acc = acc + r_ref[...]
        o_ref[...] = acc.astype(out_dtype)

    ins = [a, b] + ([] if res is None else [res])
    specs = [a_spec, b_spec] + ([] if res is None else [o_spec])
    return pl.pallas_call(
        body, name=name, grid=(m // tm, n // tn), in_specs=specs, out_specs=o_spec,
        out_shape=jax.ShapeDtypeStruct((m, n), out_dtype),
        compiler_params=_cparams("parallel", "arbitrary"),
    )(*ins)


def _norm_fwd(x, gains, *, name):
    l, d = x.shape
    tr = _tile(l, 384)
    ng = len(gains)

    def body(*refs):
        x_ref = refs[0]
        g_refs = refs[1:1 + ng]
        o_refs = refs[1 + ng:1 + 2 * ng]
        r_ref = refs[1 + 2 * ng]
        xv = x_ref[...]
        r = lax.rsqrt(jnp.mean(xv * xv, axis=-1, keepdims=True) + RMS_EPS)
        xn = xv * r
        for g_ref, o_ref in zip(g_refs, o_refs):
            o_ref[...] = (xn * g_ref[...]).astype(BF16)
        r_ref[...] = r

    row = pl.BlockSpec((tr, d), lambda i: (i, 0))
    gspec = pl.BlockSpec((1, d), lambda i: (0, 0))
    outs = pl.pallas_call(
        body, name=name, grid=(l // tr,),
        in_specs=[row] + [gspec] * ng,
        out_specs=[row] * ng + [pl.BlockSpec((tr, 1), lambda i: (i, 0))],
        out_shape=[jax.ShapeDtypeStruct((l, d), BF16)] * ng + [jax.ShapeDtypeStruct((l, 1), F32)],
        compiler_params=_cparams("parallel"),
    )(x, *gains)
    return list(outs[:ng]), outs[ng]


def _rows8(v):
    r, d = v.shape
    return jnp.sum(v.reshape(r // 8, 8, d), axis=0)


def _norm_bwd(x, r, gains, dns, dres, *, name):
    l, d = x.shape
    tr = _tile(l, 384)
    ng = len(gains)
    nsteps = l // tr

    def body(*refs):
        x_ref, r_ref, dres_ref = refs[0], refs[1], refs[2]
        g_refs = refs[3:3 + ng]
        dn_refs = refs[3 + ng:3 + 2 * ng]
        dx_ref, dxb_ref = refs[3 + 2 * ng], refs[4 + 2 * ng]
        dg_refs = refs[5 + 2 * ng:5 + 3 * ng]
        acc_refs = refs[5 + 3 * ng:5 + 4 * ng]
        i = pl.program_id(0)

        @pl.when(i == 0)
        def _():
            for acc in acc_refs:
                acc[...] = jnp.zeros_like(acc)

        rv = r_ref[...]
        xn = x_ref[...] * rv
        total = dres_ref[...]
        for g_ref, dn_ref, acc in zip(g_refs, dn_refs, acc_refs):
            dn = dn_ref[...]
            acc[...] += _rows8(dn * xn)
            dxn = dn * g_ref[...]
            total = total + rv * (dxn - xn * jnp.mean(dxn * xn, axis=-1, keepdims=True))
        dx_ref[...] = total
        dxb_ref[...] = total.astype(BF16)

        @pl.when(i == nsteps - 1)
        def _():
            for dg_ref, acc in zip(dg_refs, acc_refs):
                dg_ref[...] = jnp.sum(acc[...], axis=0, keepdims=True)

    row = pl.BlockSpec((tr, d), lambda i: (i, 0))
    gspec = pl.BlockSpec((1, d), lambda i: (0, 0))
    outs = pl.pallas_call(
        body, name=name, grid=(nsteps,),
        in_specs=[row, pl.BlockSpec((tr, 1), lambda i: (i, 0)), row] + [gspec] * ng + [row] * ng,
        out_specs=[row, row] + [gspec] * ng,
        out_shape=[jax.ShapeDtypeStruct((l, d), F32), jax.ShapeDtypeStruct((l, d), BF16)]
        + [jax.ShapeDtypeStruct((1, d), F32)] * ng,
        scratch_shapes=[pltpu.VMEM((8, d), F32)] * ng,
        compiler_params=_cparams("arbitrary"),
    )(x, r, dres, *gains, *dns)
    return outs[0], outs[1], list(outs[2:])


def _loss_bwd(h, gain, tgt, seq, *, name):
    l, d = h.shape
    tr = _tile(l, 384)
    nsteps = l // tr

    def body(h_ref, g_ref, t_ref, loss_ref, dh_ref, dhb_ref, dg_ref, lacc, gacc):
        i = pl.program_id(0)

        @pl.when(i == 0)
        def _():
            lacc[...] = jnp.zeros_like(lacc)
            gacc[...] = jnp.zeros_like(gacc)

        xv = h_ref[...]
        g = g_ref[...]
        r = lax.rsqrt(jnp.mean(xv * xv, axis=-1, keepdims=True) + RMS_EPS)
        xn = xv * r
        rows = i * tr + lax.broadcasted_iota(jnp.int32, (tr, 1), 0)
        valid = (rows >= N_META) & (rows < N_META + seq)
        e = jnp.where(valid, xn * g - t_ref[...], 0.0)
        lacc[...] += _rows8(e * e)
        dy = e * (1.0 / d)
        gacc[...] += _rows8(dy * xn)
        dxn = dy * g
        dx = r * (dxn - xn * jnp.mean(dxn * xn, axis=-1, keepdims=True))
        dh_ref[...] = dx
        dhb_ref[...] = dx.astype(BF16)

        @pl.when(i == nsteps - 1)
        def _():
            loss_ref[...] = jnp.full((8, LANES), 0.5 / d * jnp.sum(lacc[...]), F32)
            dg_ref[...] = jnp.sum(gacc[...], axis=0, keepdims=True)

    row = pl.BlockSpec((tr, d), lambda i: (i, 0))
    gspec = pl.BlockSpec((1, d), lambda i: (0, 0))
    return pl.pallas_call(
        body, name=name, grid=(nsteps,),
        in_specs=[row, gspec, row],
        out_specs=[pl.BlockSpec((8, LANES), lambda i: (0, 0)), row, row, gspec],
        out_shape=[jax.ShapeDtypeStruct((8, LANES), F32), jax.ShapeDtypeStruct((l, d), F32),
                   jax.ShapeDtypeStruct((l, d), BF16), jax.ShapeDtypeStruct((1, d), F32)],
        scratch_shapes=[pltpu.VMEM((8, d), F32), pltpu.VMEM((8, d), F32)],
        compiler_params=_cparams("arbitrary"),
    )(h, gain, tgt)


def _shift_down(v, k, rows):
    return jnp.where(rows >= k, pltpu.roll(v, k, 0), 0.0)


def _shift_up(v, k, rows):
    l = v.shape[0]
    return jnp.where(rows < l - k, pltpu.roll(v, l - k, 0), 0.0)


def _pool_diff(n, w, rows):
    s = n
    for k in (1, 2, 4, 8):
        s = s + jnp.where(k < w, _shift_down(s, k, rows), 0.0)
    cnt = jnp.minimum(rows + 1, w).astype(F32)
    return s / cnt - n, cnt


def _pool_diff_fwd(h, r, gain, *, name):
    l, d = h.shape
    per_group = d // N_GROUPS // LANES

    def body(h_ref, r_ref, g_ref, o_ref):
        w = jnp.left_shift(2, pl.program_id(0) // per_group)
        rows = lax.broadcasted_iota(jnp.int32, (l, 1), 0)
        n = h_ref[...] * r_ref[...] * g_ref[...]
        diff, _ = _pool_diff(n, w, rows)
        o_ref[...] = diff.astype(BF16)

    col = pl.BlockSpec((l, LANES), lambda j: (0, j))
    return pl.pallas_call(
        body, name=name, grid=(d // LANES,),
        in_specs=[col, pl.BlockSpec((l, 1), lambda j: (0, 0)), pl.BlockSpec((1, LANES), lambda j: (0, j))],
        out_specs=col, out_shape=jax.ShapeDtypeStruct((l, d), BF16),
        compiler_params=_cparams("parallel"),
    )(h, r, gain)


def _pool_diff_bwd(ddiff, *, name):
    l, d = ddiff.shape
    per_group = d // N_GROUPS // LANES

    def body(dd_ref, o_ref):
        w = jnp.left_shift(2, pl.program_id(0) // per_group)
        rows = lax.broadcasted_iota(jnp.int32, (l, 1), 0)
        dd = dd_ref[...]
        s = dd / jnp.minimum(rows + 1, w).astype(F32)
        for k in (1, 2, 4, 8):
            s = s + jnp.where(k < w, _shift_up(s, k, rows), 0.0)
        o_ref[...] = s - dd

    col = pl.BlockSpec((l, LANES), lambda j: (0, j))
    return pl.pallas_call(
        body, name=name, grid=(d // LANES,), in_specs=[col], out_specs=col,
        out_shape=jax.ShapeDtypeStruct((l, d), F32), compiler_params=_cparams("parallel"),
    )(ddiff)


def _pool_mix_fwd(h, diff_b, w_pool, scale, *, name):
    l, d = h.shape
    gd = d // N_GROUPS
    tr = _tile(l, 1408)

    def body(h_ref, d_ref, w_ref, s_ref, o_ref):
        y = jnp.dot(d_ref[...], w_ref[0], preferred_element_type=F32)
        o_ref[...] = h_ref[...] + y * s_ref[...]

    blk = pl.BlockSpec((tr, gd), lambda i, g: (i, g))
    vec = pl.BlockSpec((1, gd), lambda i, g: (0, g))
    return pl.pallas_call(
        body, name=name, grid=(l // tr, N_GROUPS),
        in_specs=[blk, blk, pl.BlockSpec((1, gd, gd), lambda i, g: (g, 0, 0)), vec],
        out_specs=blk, out_shape=jax.ShapeDtypeStruct((l, d), F32),
        compiler_params=_cparams("parallel", "parallel"),
    )(h, diff_b, w_pool, scale)


def _pool_mix_bwd(diff_b, w_pool, scale, dh1, *, name):
    l, d = dh1.shape
    gd = d // N_GROUPS
    tr = _tile(l, 1408)
    nsteps = l // tr

    def body(d_ref, w_ref, s_ref, dy_ref, dd_ref, dw_ref, ds_ref, sacc):
        i = pl.program_id(1)

        @pl.when(i == 0)
        def _():
            dw_ref[...] = jnp.zeros_like(dw_ref)
            sacc[...] = jnp.zeros_like(sacc)

        diff_b_, wg, dy = d_ref[...], w_ref[0], dy_ref[...]
        yy = jnp.dot(diff_b_, wg, preferred_element_type=F32)
        sacc[...] += _rows8(dy * yy)
        dyy_b = (dy * s_ref[...]).astype(BF16)
        dw_ref[0] += lax.dot_general(diff_b_, dyy_b, _DIMS["tn"], preferred_element_type=F32)
        dd_ref[...] = lax.dot_general(dyy_b, wg, _DIMS["nt"], preferred_element_type=F32)

        @pl.when(i == nsteps - 1)
        def _():
            ds_ref[...] = jnp.sum(sacc[...], axis=0, keepdims=True)

    blk = pl.BlockSpec((tr, gd), lambda g, i: (i, g))
    vec = pl.BlockSpec((1, gd), lambda g, i: (0, g))
    wspec = pl.BlockSpec((1, gd, gd), lambda g, i: (g, 0, 0))
    return pl.pallas_call(
        body, name=name, grid=(N_GROUPS, nsteps),
        in_specs=[blk, wspec, vec, blk],
        out_specs=[blk, wspec, vec],
        out_shape=[jax.ShapeDtypeStruct((l, d), F32), jax.ShapeDtypeStruct((N_GROUPS, gd, gd), F32),
                   jax.ShapeDtypeStruct((1, d), F32)],
        scratch_shapes=[pltpu.VMEM((8, gd), F32)],
        compiler_params=_cparams("parallel", "arbitrary"),
    )(diff_b, w_pool, scale, dh1)


def _conv_chunk(cur, prev, w, b, rowi):
    s1 = jnp.where(rowi < 1, pltpu.roll(prev, 1, 0), pltpu.roll(cur, 1, 0))
    s2 = jnp.where(rowi < 2, pltpu.roll(prev, 2, 0), pltpu.roll(cur, 2, 0))
    return b + w[0] * s2 + w[1] * s1 + w[2] * cur, s1, s2


def _bcast_rows(ref, rows):
    v = ref[...]
    return [jnp.broadcast_to(v[k:k + 1], (rows, v.shape[1])) for k in range(v.shape[0])]


def _convgate_fwd(u, cw, cb, *, name):
    l, f2 = u.shape
    f = f2 // 2
    tc = _tile(f, LANES)
    nc = f // tc
    rows = CONV_ROWS
    assert l % rows == 0

    def body(ug_ref, uv_ref, wg_ref, wv_ref, bg_ref, bv_ref, a_ref):
        rowi = lax.broadcasted_iota(jnp.int32, (rows, tc), 0)
        wg, wv = _bcast_rows(wg_ref, rows), _bcast_rows(wv_ref, rows)
        bg, bv = _bcast_rows(bg_ref, rows)[0], _bcast_rows(bv_ref, rows)[0]

        def chunk(i, carry):
            pg, pv = carry
            r = pl.multiple_of(i * rows, rows)
            cg, cv = ug_ref[pl.ds(r, rows), :], uv_ref[pl.ds(r, rows), :]
            gate, _, _ = _conv_chunk(cg, pg, wg, bg, rowi)
            val, _, _ = _conv_chunk(cv, pv, wv, bv, rowi)
            a_ref[pl.ds(r, rows), :] = (gate * jax.nn.sigmoid(gate) * val).astype(BF16)
            return cg, cv

        zero = jnp.zeros((rows, tc), F32)
        lax.fori_loop(0, l // rows, chunk, (zero, zero))

    def spec(rows_, off):
        return pl.BlockSpec((rows_, tc), lambda j: (0, j + off))

    return pl.pallas_call(
        body, name=name, grid=(nc,),
        in_specs=[spec(l, 0), spec(l, nc), spec(3, 0), spec(3, nc), spec(1, 0), spec(1, nc)],
        out_specs=spec(l, 0), out_shape=jax.ShapeDtypeStruct((l, f), BF16),
        compiler_params=_cparams("parallel"),
    )(u, u, cw, cw, cb, cb)


def _convgate_bwd(u, cw, cb, da, *, name):
    l, f2 = u.shape
    f = f2 // 2
    tc = _tile(f, LANES)
    nc = f // tc
    rows = CONV_ROWS
    assert l % rows == 0
    nchunks = l // rows

    def body(ug_ref, uv_ref, wg_ref, wv_ref, bg_ref, bv_ref, da_ref,
             dug_ref, duv_ref, dcwg_ref, dcwv_ref, dcbg_ref, dcbv_ref):
        rowi = lax.broadcasted_iota(jnp.int32, (rows, tc), 0)
        wg, wv = _bcast_rows(wg_ref, rows), _bcast_rows(wv_ref, rows)
        bg, bv = _bcast_rows(bg_ref, rows)[0], _bcast_rows(bv_ref, rows)[0]

        def tap_sums(acc, dc, taps):
            return (acc[0] + _rows8(dc),) + tuple(a + _rows8(dc * t) for a, t in zip(acc[1:], taps))

        def input_grad(dc, dc_next, w):
            up1 = jnp.where(rowi >= rows - 1, pltpu.roll(dc_next, rows - 1, 0), pltpu.roll(dc, rows - 1, 0))
            up2 = jnp.where(rowi >= rows - 2, pltpu.roll(dc_next, rows - 2, 0), pltpu.roll(dc, rows - 2, 0))
            return (w[2] * dc + w[1] * up1 + w[0] * up2).astype(BF16)

        def chunk(ii, carry):
            dcg_next, dcv_next, acc_g, acc_v = carry
            i = nchunks - 1 - ii
            r = pl.multiple_of(i * rows, rows)
            rp = pl.multiple_of(jnp.maximum(i - 1, 0) * rows, rows)
            cg, cv = ug_ref[pl.ds(r, rows), :], uv_ref[pl.ds(r, rows), :]
            pg = jnp.where(i > 0, ug_ref[pl.ds(rp, rows), :], 0.0)
            pv = jnp.where(i > 0, uv_ref[pl.ds(rp, rows), :], 0.0)
            gate, g1, g2 = _conv_chunk(cg, pg, wg, bg, rowi)
            val, v1, v2 = _conv_chunk(cv, pv, wv, bv, rowi)
            sg = jax.nn.sigmoid(gate)
            dav = da_ref[pl.ds(r, rows), :]
            dcg = dav * val * (sg * (1.0 + gate * (1.0 - sg)))
            dcv = dav * (gate * sg)
            acc_g = tap_sums(acc_g, dcg, (g2, g1, cg))
            acc_v = tap_sums(acc_v, dcv, (v2, v1, cv))
            dug_ref[pl.ds(r, rows), :] = input_grad(dcg, dcg_next, wg)
            duv_ref[pl.ds(r, rows), :] = input_grad(dcv, dcv_next, wv)
            return dcg, dcv, acc_g, acc_v

        zero = jnp.zeros((rows, tc), F32)
        zero8 = (jnp.zeros((8, tc), F32),) * 4
        _, _, acc_g, acc_v = lax.fori_loop(0, nchunks, chunk, (zero, zero, zero8, zero8))
        for acc, dcw_ref, dcb_ref in ((acc_g, dcwg_ref, dcbg_ref), (acc_v, dcwv_ref, dcbv_ref)):
            dcb_ref[...] = jnp.sum(acc[0], axis=0, keepdims=True)
            for k in range(3):
                dcw_ref[k:k + 1, :] = jnp.sum(acc[1 + k], axis=0, keepdims=True)

    def spec(rows_, off):
        return pl.BlockSpec((rows_, tc), lambda j: (0, j + off))

    return pl.pallas_call(
        body, name=name, grid=(nc,),
        in_specs=[spec(l, 0), spec(l, nc), spec(3, 0), spec(3, nc), spec(1, 0), spec(1, nc), spec(l, 0)],
        out_specs=[spec(l, 0), spec(l, 0), spec(3, 0), spec(3, 0), spec(1, 0), spec(1, 0)],
        out_shape=[jax.ShapeDtypeStruct((l, f), BF16)] * 2 + [jax.ShapeDtypeStruct((3, f), F32)] * 2
        + [jax.ShapeDtypeStruct((1, f), F32)] * 2,
        compiler_params=_cparams("parallel"),
    )(u, u, cw, cw, cb, cb, da)


def _cumsum_mm(v_b, t2):
    return jnp.dot(v_b, t2, preferred_element_type=F32)


def _tri_and_ones(tri_fn):
    row = lax.broadcasted_iota(jnp.int32, (BLK, 2 * BLK), 0)
    col = lax.broadcasted_iota(jnp.int32, (BLK, 2 * BLK), 1)
    return jnp.where((col >= BLK) | tri_fn(row, col), 1.0, 0.0).astype(BF16)


def _logits(z, mask):
    sp = jnp.log(1.0 + jnp.exp(-jnp.abs(z)))
    lb = jnp.minimum(z, 0.0) - sp
    lm = lb - z
    if mask is not None:
        lm = jnp.where(mask, lm, 0.0)
    return lb, lm


def _software_pipeline(stages, n, block_of, state):
    ns = len(stages)
    inflight = [None] * (ns - 1)
    for t in range(ns - 1):
        new = list(inflight)
        for s in range(t, -1, -1):
            y, state = stages[s](block_of(t - s), None if s == 0 else inflight[s - 1], state)
            new[s] = y
        inflight = new

    def steady(i, carry):
        inflight, state = carry
        new = [None] * (ns - 1)
        for s in range(ns - 1, -1, -1):
            y, state = stages[s](block_of(i + ns - 1 - s), None if s == 0 else inflight[s - 1], state)
            if s < ns - 1:
                new[s] = y
        return tuple(new), state

    inflight, state = lax.fori_loop(0, n - (ns - 1), steady, (tuple(inflight), state))
    inflight = list(inflight)
    for e in range(1, ns):
        new = list(inflight)
        for s in range(ns - 1, e - 1, -1):
            y, state = stages[s](block_of(n - 1 + e - s), inflight[s - 1], state)
            if s < ns - 1:
                new[s] = y
        inflight = new
    return state


def _attn_fwd(q, k, v, *, name):
    nh, l, dh = q.shape
    assert l % ATT_BQ == 0
    nq = l // ATT_BQ
    qscale = HEAD_DIM ** -0.5
    chunks = range(ATT_CHUNKS)

    def body(q_ref, k_ref, v_ref, o_ref, lt_ref, z_scr, cs_scr, pv_scr):
        t_later = _tri_and_ones(lambda r, c: r > c)
        dmask = (lax.broadcasted_iota(jnp.int32, (BLK, BLK), 1)
                 < lax.broadcasted_iota(jnp.int32, (BLK, BLK), 0))

        def logits(qc, c0, mask):
            z = lax.dot_general(qc, k_ref[0, pl.ds(c0, BLK), :], _DIMS["nt"], preferred_element_type=F32)
            lb, lm = _logits(z, mask)
            return lb, lm.astype(BF16)

        def weights(lb, lm_b, mask, run):
            cs = _cumsum_mm(lm_b, t_later)
            a = jnp.exp(lb + cs[:, :BLK] + run)
            if mask is not None:
                a = jnp.where(mask, a, 0.0)
            return a.astype(BF16), run + cs[:, BLK:]

        def accumulate(a_b, c0, acc):
            return acc + jnp.dot(a_b, v_ref[0, pl.ds(c0, BLK), :], preferred_element_type=F32)

        def qblock(qb, _):
            r0 = pl.multiple_of(qb * ATT_BQ, ATT_BQ)
            qs = [q_ref[0, pl.ds(r0 + rc * BLK, BLK), :] * jnp.asarray(qscale, BF16) for rc in chunks]
            accs = [jnp.zeros((BLK, dh), F32)] * ATT_CHUNKS
            runs = [jnp.zeros((BLK, BLK), F32)] * ATT_CHUNKS
            for dj in range(ATT_CHUNKS - 1, -1, -1):
                c0 = r0 + dj * BLK
                for rc in range(dj, ATT_CHUNKS):
                    mask = dmask if rc == dj else None
                    a_b, runs[rc] = weights(*logits(qs[rc], c0, mask), mask, runs[rc])
                    accs[rc] = accumulate(a_b, c0, accs[rc])

            def col0(b):
                return pl.multiple_of(r0 - (b + 1) * BLK, BLK)

            def stage_scores(b, _, state):
                for rc in chunks:
                    z_scr[rc] = lax.dot_general(qs[rc], k_ref[0, pl.ds(col0(b), BLK), :], _DIMS["nt"],
                                                preferred_element_type=F32)
                return (), state

            def stage_cumsum(b, _, state):
                lbs = []
                for rc in chunks:
                    lb, lm = _logits(z_scr[rc], None)
                    cs_scr[rc] = _cumsum_mm(lm.astype(BF16), t_later)
                    lbs.append(lb)
                return tuple(lbs), state

            def stage_weights(b, lbs, state):
                accs, runs = state
                new_runs = []
                for rc in chunks:
                    cs = cs_scr[rc]
                    a = jnp.exp(lbs[rc] + cs[:, :BLK] + runs[rc])
                    pv_scr[rc] = jnp.dot(a.astype(BF16), v_ref[0, pl.ds(col0(b), BLK), :],
                                         preferred_element_type=F32)
                    new_runs.append(runs[rc] + cs[:, BLK:])
                return (), (accs, tuple(new_runs))

            def stage_acc(b, _, state):
                accs, runs = state
                return None, (tuple(accs[rc] + pv_scr[rc] for rc in chunks), runs)

            def left_region(state):
                return _software_pipeline([stage_scores, stage_cumsum, stage_weights, stage_acc],
                                          qb * ATT_CHUNKS, lambda b: b, state)

            accs, runs = lax.cond(qb > 0, left_region, lambda s: s, (tuple(accs), tuple(runs)))
            for rc in chunks:
                o_ref[0, pl.ds(r0 + rc * BLK, BLK), :] = accs[rc].astype(BF16)
                lt_ref[0, pl.ds(r0 + rc * BLK, BLK), :] = runs[rc]
            return 0

        lax.fori_loop(0, nq, qblock, 0)

    head = pl.BlockSpec((1, l, dh), lambda h: (h, 0, 0))
    return pl.pallas_call(
        body, name=name, grid=(nh,),
        in_specs=[head, head, head],
        out_specs=[head, pl.BlockSpec((1, l, BLK), lambda h: (h, 0, 0))],
        out_shape=[jax.ShapeDtypeStruct((nh, l, dh), BF16), jax.ShapeDtypeStruct((nh, l, BLK), F32)],
        scratch_shapes=[pltpu.VMEM((ATT_CHUNKS, BLK, BLK), F32), pltpu.VMEM((ATT_CHUNKS, BLK, 2 * BLK), F32),
                        pltpu.VMEM((ATT_CHUNKS, BLK, dh), F32)],
        compiler_params=_cparams("parallel"),
    )(q, k, v)


def _attn_bwd(q, k, v, do, ltot, *, name):
    nh, l, dh = q.shape
    assert l % ATT_BQ == 0
    nq = l // ATT_BQ
    qscale = HEAD_DIM ** -0.5
    chunks = range(ATT_CHUNKS)

    def body(q_ref, k_ref, v_ref, do_ref, lt_ref, dq_ref, dk_ref, dv_ref,
             z_scr, cs_scr, da_scr, cd_scr, dqp_scr, dkp_scr, dvp_scr):
        t_incl = _tri_and_ones(lambda r, c: r <= c)
        t_excl = _tri_and_ones(lambda r, c: r < c)
        dmask = (lax.broadcasted_iota(jnp.int32, (BLK, BLK), 1)
                 < lax.broadcasted_iota(jnp.int32, (BLK, BLK), 0))
        dk_ref[...] = jnp.zeros_like(dk_ref)
        dv_ref[...] = jnp.zeros_like(dv_ref)

        def logits(qc, c0, mask):
            z = lax.dot_general(qc, k_ref[0, pl.ds(c0, BLK), :], _DIMS["nt"], preferred_element_type=F32)
            lb, lm = _logits(z, mask)
            return lb, lm.astype(BF16)

        def weights(lb, lm_b, doc, ltc, c0, mask, pre_lm):
            cs = _cumsum_mm(lm_b, t_incl)
            da = lax.dot_general(doc, v_ref[0, pl.ds(c0, BLK), :], _DIMS["nt"], preferred_element_type=F32)
            a = jnp.exp(lb + (ltc - pre_lm - cs[:, :BLK]))
            if mask is not None:
                a = jnp.where(mask, a, 0.0)
            dl = a * da
            return (jnp.exp(lb), dl, dl.astype(BF16), a.astype(BF16)), pre_lm + cs[:, BLK:]

        def logit_grad(beta, dl, dl_b, mask, pre_dl):
            cd = _cumsum_mm(dl_b, t_excl)
            dz = dl - beta * (dl + pre_dl + cd[:, :BLK])
            if mask is not None:
                dz = jnp.where(mask, dz, 0.0)
            return (dz * qscale).astype(BF16), pre_dl + cd[:, BLK:]

        def key_grads(c0, dz_parts, a_parts, q_rows, do_rows):
            dz_all = dz_parts[0] if len(dz_parts) == 1 else jnp.concatenate(dz_parts, axis=0)
            a_all = a_parts[0] if len(a_parts) == 1 else jnp.concatenate(a_parts, axis=0)
            dk_ref[0, pl.ds(c0, BLK), :] += lax.dot_general(dz_all, q_rows, _DIMS["tn"], preferred_element_type=F32)
            dv_ref[0, pl.ds(c0, BLK), :] += lax.dot_general(a_all, do_rows, _DIMS["tn"], preferred_element_type=F32)

        def qblock(qb, _):
            r0 = pl.multiple_of(qb * ATT_BQ, ATT_BQ)
            q_raw = q_ref[0, pl.ds(r0, ATT_BQ), :]
            do_all = do_ref[0, pl.ds(r0, ATT_BQ), :]
            qs = [q_raw[rc * BLK:(rc + 1) * BLK] * jnp.asarray(qscale, BF16) for rc in chunks]
            dos = [do_all[rc * BLK:(rc + 1) * BLK] for rc in chunks]
            lts = [lt_ref[0, pl.ds(r0 + rc * BLK, BLK), :] for rc in chunks]

            def col0(b):
                return pl.multiple_of(b * BLK, BLK)

            def stage_scores(b, _, state):
                kj = k_ref[0, pl.ds(col0(b), BLK), :]
                for rc in chunks:
                    z_scr[rc] = lax.dot_general(qs[rc], kj, _DIMS["nt"], preferred_element_type=F32)
                return (), state

            def stage_prefix(b, _, state):
                vj = v_ref[0, pl.ds(col0(b), BLK), :]
                lbs = []
                for rc in chunks:
                    lb, lm = _logits(z_scr[rc], None)
                    cs_scr[rc] = _cumsum_mm(lm.astype(BF16), t_incl)
                    da_scr[rc] = lax.dot_general(dos[rc], vj, _DIMS["nt"], preferred_element_type=F32)
                    lbs.append(lb)
                return tuple(lbs), state

            def stage_weights(b, lbs, state):
                dqs, pls, pds = state
                out, new_pls = [], []
                for rc in chunks:
                    cs = cs_scr[rc]
                    a = jnp.exp(lbs[rc] + (lts[rc] - pls[rc] - cs[:, :BLK]))
                    dl = a * da_scr[rc]
                    cd_scr[rc] = _cumsum_mm(dl.astype(BF16), t_excl)
                    out.append((jnp.exp(lbs[rc]), dl, a.astype(BF16)))
                    new_pls.append(pls[rc] + cs[:, BLK:])
                return tuple(out), (dqs, tuple(new_pls), pds)

            def stage_products(b, ys, state):
                dqs, pls, pds = state
                c0 = col0(b)
                kj = k_ref[0, pl.ds(c0, BLK), :]
                dz_bs, new_pds = [], []
                for rc in chunks:
                    beta, dl, _ = ys[rc]
                    cd = cd_scr[rc]
                    dz_b = ((dl - beta * (dl + pds[rc] + cd[:, :BLK])) * qscale).astype(BF16)
                    dqp_scr[rc] = jnp.dot(dz_b, kj, preferred_element_type=F32)
                    dz_bs.append(dz_b)
                    new_pds.append(pds[rc] + cd[:, BLK:])
                dkp_scr[...] = lax.dot_general(jnp.concatenate(dz_bs, axis=0), q_raw, _DIMS["tn"],
                                               preferred_element_type=F32)
                dvp_scr[...] = lax.dot_general(jnp.concatenate([y[2] for y in ys], axis=0), do_all, _DIMS["tn"],
                                               preferred_element_type=F32)
                return (), (dqs, pls, tuple(new_pds))

            def stage_acc(b, _, state):
                dqs, pls, pds = state
                c0 = col0(b)
                dk_ref[0, pl.ds(c0, BLK), :] += dkp_scr[...]
                dv_ref[0, pl.ds(c0, BLK), :] += dvp_scr[...]
                return None, (tuple(dqs[rc] + dqp_scr[rc] for rc in chunks), pls, pds)

            stages = [stage_scores, stage_prefix, stage_weights, stage_products, stage_acc]
            zero = jnp.zeros((BLK, BLK), F32)
            state = ((jnp.zeros((BLK, dh), F32),) * ATT_CHUNKS, (zero,) * ATT_CHUNKS, (zero,) * ATT_CHUNKS)

            def pipelined(state):
                return _software_pipeline(stages, qb * ATT_CHUNKS, lambda b: b, state)

            def one_by_one(state):
                def block(b, state):
                    x = None
                    for stage in stages:
                        x, state = stage(b, x, state)
                    return state
                return lax.fori_loop(0, qb * ATT_CHUNKS, block, state)

            dqs, pls, pds = lax.cond(qb * ATT_CHUNKS >= len(stages) - 1, pipelined, one_by_one, state)
            dqs, pls, pds = list(dqs), list(pls), list(pds)
            for dj in chunks:
                c0 = r0 + dj * BLK
                kj = k_ref[0, pl.ds(c0, BLK), :]
                dz_parts, a_parts = [], []
                for rc in range(dj, ATT_CHUNKS):
                    mask = dmask if rc == dj else None
                    ys, pls[rc] = weights(*logits(qs[rc], c0, mask), dos[rc], lts[rc], c0, mask, pls[rc])
                    dz_b, pds[rc] = logit_grad(*ys[:3], mask, pds[rc])
                    a_b = ys[3]
                    dqs[rc] = dqs[rc] + jnp.dot(dz_b, kj, preferred_element_type=F32)
                    dz_parts.append(dz_b)
                    a_parts.append(a_b)
                key_grads(c0, dz_parts, a_parts, q_raw[dj * BLK:], do_all[dj * BLK:])
            for rc in chunks:
                dq_ref[0, pl.ds(r0 + rc * BLK, BLK), :] = dqs[rc]
            return 0

        lax.fori_loop(0, nq, qblock, 0)

    head = pl.BlockSpec((1, l, dh), lambda h: (h, 0, 0))
    sq = pltpu.VMEM((ATT_CHUNKS, BLK, BLK), F32)
    sw = pltpu.VMEM((ATT_CHUNKS, BLK, 2 * BLK), F32)
    return pl.pallas_call(
        body, name=name, grid=(nh,),
        in_specs=[head, head, head, head, pl.BlockSpec((1, l, BLK), lambda h: (h, 0, 0))],
        out_specs=[head, head, head],
        out_shape=[jax.ShapeDtypeStruct((nh, l, dh), F32)] * 3,
        scratch_shapes=[sq, sw, sq, sw, pltpu.VMEM((ATT_CHUNKS, BLK, dh), F32),
                        pltpu.VMEM((BLK, dh), F32), pltpu.VMEM((BLK, dh), F32)],
        compiler_params=_cparams("parallel"),
    )(q, k, v, do, ltot)


def _add_to_bf16(a, b, *, name):
    n, r, c = a.shape
    tr = _row_tile(r, 1536, 16)
    blk = pl.BlockSpec((1, tr, c), lambda i, j: (i, j, 0))

    def body(a_ref, b_ref, o_ref):
        o_ref[...] = (a_ref[...] + b_ref[...].astype(F32)).astype(BF16)

    return pl.pallas_call(body, name=name, grid=(n, r // tr), in_specs=[blk, blk], out_specs=blk,
                          out_shape=jax.ShapeDtypeStruct(a.shape, BF16),
                          compiler_params=_cparams("parallel", "parallel"))(a, b)


def _sum_slots(p, *, name):
    n, r, c = p.shape
    tr = _row_tile(r, 1536, 16)

    def body(p_ref, o_ref):
        acc = p_ref[0].astype(F32)
        for s in range(1, n):
            acc = acc + p_ref[s].astype(F32)
        o_ref[...] = acc

    return pl.pallas_call(body, name=name, grid=(r // tr,),
                          in_specs=[pl.BlockSpec((n, tr, c), lambda j: (0, j, 0))],
                          out_specs=pl.BlockSpec((tr, c), lambda j: (j, 0)),
                          out_shape=jax.ShapeDtypeStruct((r, c), F32),
                          compiler_params=_cparams("parallel"))(p)


def _adamw(w, g, m, v, *, name):
    r, c = w.shape
    tr = _row_tile(r, 512, 8)
    blk = pl.BlockSpec((tr, c), lambda i: (i, 0))

    def body(w_ref, g_ref, m_ref, v_ref, d_ref, mo_ref, vo_ref):
        gv = g_ref[...]
        mn = ADAM_B1 * m_ref[...] + (1.0 - ADAM_B1) * gv
        vn = ADAM_B2 * v_ref[...] + (1.0 - ADAM_B2) * (gv * gv)
        m_hat = mn / (1.0 - ADAM_B1 ** ADAM_STEP)
        v_hat = vn / (1.0 - ADAM_B2 ** ADAM_STEP)
        d_ref[...] = -ADAM_LR * (m_hat / (jnp.sqrt(v_hat) + ADAM_EPS) + ADAM_WD * w_ref[...])
        mo_ref[...] = mn
        vo_ref[...] = vn

    return pl.pallas_call(body, name=name, grid=(r // tr,), in_specs=[blk] * 4, out_specs=[blk] * 3,
                          out_shape=[jax.ShapeDtypeStruct((r, c), F32)] * 3,
                          compiler_params=_cparams("parallel"))(w, g, m, v)


_ANY = pl.BlockSpec(memory_space=pl.ANY)


def _place():
    x, y, c = lax.axis_index("x"), lax.axis_index("y"), lax.axis_index("c")
    chips = [(1 - x, y), (x, 1 - y), (1 - x, 1 - y)]
    return x, y, c, chips


def _gather_shards(wsh, *, name):
    _, r, cdim = wsh.shape

    def body(w_ref, out_ref, send_sems, recv_sems):
        x, y, c, chips = _place()
        s = 2 * x + y
        sibling = (x, y, 1 - c)

        def copy(k, src, dst, to):
            return pltpu.make_async_remote_copy(src_ref=src, dst_ref=dst, send_sem=send_sems.at[k],
                                                recv_sem=recv_sems.at[k], device_id=to, device_id_type=MESH)

        first = [copy(j, w_ref.at[c], out_ref.at[s, c], (*chip, c)) for j, chip in enumerate(chips)]
        for cp in first:
            cp.start()
        passed = []
        for j, (px, py) in enumerate(chips):
            sp = 2 * px + py
            copy(j, w_ref.at[c], out_ref.at[sp, c], sibling).wait_recv()
            fwd = copy(3 + j, out_ref.at[sp, c], out_ref.at[sp, c], sibling)
            fwd.start()
            passed.append(fwd)
        for j, (px, py) in enumerate(chips):
            sp = 2 * px + py
            copy(3 + j, w_ref.at[c], out_ref.at[sp, 1 - c], sibling).wait_recv()
        for cp in first + passed:
            cp.wait_send()

    return pl.pallas_call(
        body, name=name, in_specs=[_ANY], out_specs=_ANY,
        out_shape=jax.ShapeDtypeStruct((N_CHIPS, 2, r, cdim), wsh.dtype),
        scratch_shapes=[pltpu.SemaphoreType.DMA((6,)), pltpu.SemaphoreType.DMA((6,))],
    )(wsh)


def _to_sibling(a, *, name):
    def body(a_ref, out_ref, send_sem, recv_sem):
        x, y, c, _ = _place()
        cp = pltpu.make_async_remote_copy(src_ref=a_ref, dst_ref=out_ref, send_sem=send_sem, recv_sem=recv_sem,
                                          device_id=(x, y, 1 - c), device_id_type=MESH)
        cp.start()
        cp.wait()

    return pl.pallas_call(
        body, name=name, in_specs=[_ANY], out_specs=_ANY,
        out_shape=jax.ShapeDtypeStruct(a.shape, a.dtype),
        scratch_shapes=[pltpu.SemaphoreType.DMA, pltpu.SemaphoreType.DMA],
    )(a)


def _scatter_rows(p, *, name):
    def body(p_ref, out_ref, send_sems, recv_sems):
        x, y, c, chips = _place()
        s = 2 * x + y
        sends = []
        for j, (px, py) in enumerate(chips):
            sp = 2 * px + py
            cp = pltpu.make_async_remote_copy(src_ref=p_ref.at[sp], dst_ref=out_ref.at[s], send_sem=send_sems.at[j],
                                              recv_sem=recv_sems.at[j], device_id=(px, py, c), device_id_type=MESH)
            cp.start()
            sends.append(cp)
        for j, (px, py) in enumerate(chips):
            sp = 2 * px + py
            pltpu.make_async_remote_copy(src_ref=p_ref.at[sp], dst_ref=out_ref.at[sp], send_sem=send_sems.at[j],
                                         recv_sem=recv_sems.at[j], device_id=(px, py, c),
                                         device_id_type=MESH).wait_recv()
        for cp in sends:
            cp.wait_send()

    return pl.pallas_call(
        body, name=name, in_specs=[_ANY], out_specs=_ANY,
        out_shape=jax.ShapeDtypeStruct(p.shape, p.dtype),
        scratch_shapes=[pltpu.SemaphoreType.DMA((3,)), pltpu.SemaphoreType.DMA((3,))],
    )(p)


def _all_devices(a, reduce, *, name):
    r, cdim = a.shape

    def body(a_ref, out_ref, *scratch):
        if reduce:
            buf, send_sems, recv_sems = scratch
        else:
            buf = out_ref
            send_sems, recv_sems = scratch
        x, y, c, _ = _place()
        me = 4 * x + 2 * y + c
        buf[me] = a_ref[...]
        peers = []
        for k in range(1, N_DEV):
            dx, dy, dc = (k >> 2) & 1, (k >> 1) & 1, k & 1
            peers.append((x ^ dx, y ^ dy, c ^ dc))
        sends = []
        for k, peer in enumerate(peers):
            cp = pltpu.make_async_remote_copy(src_ref=a_ref, dst_ref=buf.at[me], send_sem=send_sems.at[k],
                                              recv_sem=recv_sems.at[k], device_id=peer, device_id_type=MESH)
            cp.start()
            sends.append(cp)
        for k, (px, py, pc) in enumerate(peers):
            pltpu.make_async_remote_copy(src_ref=a_ref, dst_ref=buf.at[4 * px + 2 * py + pc],
                                         send_sem=send_sems.at[k], recv_sem=recv_sems.at[k],
                                         device_id=(px, py, pc), device_id_type=MESH).wait_recv()
        for cp in sends:
            cp.wait_send()
        if reduce:
            acc = buf[0]
            for k in range(1, N_DEV):
                acc = acc + buf[k]
            out_ref[...] = acc

    vm = pl.BlockSpec(memory_space=pltpu.VMEM)
    sems = [pltpu.SemaphoreType.DMA((N_DEV - 1,)), pltpu.SemaphoreType.DMA((N_DEV - 1,))]
    if reduce:
        out_shape = jax.ShapeDtypeStruct((r, cdim), F32)
        scratch = [pltpu.VMEM((N_DEV, r, cdim), F32)] + sems
    else:
        out_shape = jax.ShapeDtypeStruct((N_DEV, r, cdim), F32)
        scratch = sems
    return pl.pallas_call(
        body, name=name, in_specs=[vm], out_specs=vm, out_shape=out_shape, scratch_shapes=scratch,
        compiler_params=pltpu.CompilerParams(vmem_limit_bytes=VMEM_LIMIT),
    )(a)


def _flat_rows(parts, cols):
    flat = jnp.concatenate([p.reshape(-1) for p in parts])
    padded = -(-flat.size // (8 * cols)) * (8 * cols)
    return jnp.pad(flat, (0, padded - flat.size)).reshape(-1, cols)


def kernel(x, meta_tokens, mix_norm, ffn_norm, pool_w, pool_scale, kv_norm, w_kv, w_q, w_o, ffn_w_up, ffn_conv_w, ffn_conv_b, ffn_w_down, final_norm, loss_target, m_meta_tokens, m_mix_norm, m_ffn_norm, m_pool_w, m_pool_scale, m_kv_norm, m_w_kv, m_w_q, m_w_o, m_ffn_w_up, m_ffn_conv_w, m_ffn_conv_b, m_ffn_w_down, m_final_norm, v_meta_tokens, v_mix_norm, v_ffn_norm, v_pool_w, v_pool_scale, v_kv_norm, v_w_kv, v_w_q, v_w_o, v_ffn_w_up, v_ffn_conv_w, v_ffn_conv_b, v_ffn_w_down, v_final_norm):
    seq, d = x.shape[1], x.shape[2]
    l_real = N_META + seq
    lp = -(-l_real // ATT_BQ) * ATT_BQ
    f2 = ffn_w_up.shape[2] * N_CHIPS
    f = f2 // 2
    gd = d // N_GROUPS
    chip = 2 * lax.axis_index("x") + lax.axis_index("y")
    core = lax.axis_index("c")

    big_parts = [pool_w[0], w_kv, w_q[0], w_o[0], ffn_w_up, ffn_w_down]
    sizes = [p.size for p in big_parts]
    n_big = sum(sizes)
    cw = 1024
    assert n_big % (2 * 16 * cw) == 0
    rows_half = n_big // (2 * cw)
    wsh = _flat_rows([p.astype(BF16) for p in big_parts], cw).reshape(2, rows_half, cw)
    wall = lax.dynamic_update_index_in_dim(_gather_shards(wsh, name="gather_weights"), wsh, chip, 0)
    wall = wall.reshape(N_CHIPS, n_big)
    offs = [0]
    for sz in sizes:
        offs.append(offs[-1] + sz)

    def big(i, shape):
        return wall[:, offs[i]:offs[i + 1]].reshape((N_CHIPS,) + shape)

    wp_b = big(0, (N_GROUPS, gd // N_CHIPS, gd)).transpose(1, 0, 2, 3).reshape(N_GROUPS, gd, gd)
    wkv_b = big(1, (d, 2 * d // N_CHIPS)).transpose(1, 0, 2).reshape(d, 2 * d)
    wq_b = big(2, (d // N_CHIPS, d)).reshape(d, d)
    wo_b = big(3, (d // N_CHIPS, d)).reshape(d, d)
    wup_b = big(4, (2, d, f2 // N_CHIPS)).transpose(1, 2, 0, 3).reshape(2, d, f2)
    wdn_b = big(5, (2, f // N_CHIPS, d)).transpose(1, 0, 2, 3).reshape(2, f, d)

    small_parts = [meta_tokens, pool_scale, ffn_conv_w]
    ssizes = [p.size for p in small_parts]
    small = _flat_rows(small_parts, LANES)
    sall = _all_devices(small, False, name="gather_small")[::2].reshape(N_CHIPS, -1)
    meta_f = sall[:, :ssizes[0]].reshape(N_CHIPS, N_META, d // N_CHIPS).transpose(1, 0, 2).reshape(N_META, d)
    scale_f = sall[:, ssizes[0]:ssizes[0] + ssizes[1]].reshape(1, d)
    cw_f = sall[:, ssizes[0] + ssizes[1]:sum(ssizes)].reshape(N_CHIPS, 2, 3, f2 // N_CHIPS).transpose(1, 2, 0, 3).reshape(2, 3, f2)

    mix0, mix1 = mix_norm[0:1], mix_norm[1:2]
    fn0, fn1 = ffn_norm[0:1], ffn_norm[1:2]
    kvn_g = kv_norm.reshape(1, d)
    fin_g = final_norm.reshape(1, d)

    pad = lp - l_real
    h0 = jnp.concatenate([meta_f, x[0], jnp.zeros((pad, d), F32)], axis=0)
    tgt = jnp.pad(loss_target[0], ((N_META, pad), (0, 0)))

    _, r0 = _norm_fwd(h0, [mix0], name="norm_h0")
    diff_b = _pool_diff_fwd(h0, r0, mix0, name="pool_diff")
    h1 = _pool_mix_fwd(h0, diff_b, wp_b, scale_f, name="pool_mix")

    def ffn_fwd(h_in, layer, tag):
        (fb,), rf = _norm_fwd(h_in, [ffn_norm[layer:layer + 1]], name=f"norm_ffn{tag}")
        u = _mm(fb, wup_b[layer], "nn", tm=1408, tn=512, name=f"ffn_up{tag}")
        a = _convgate_fwd(u, cw_f[layer], ffn_conv_b[layer:layer + 1], name=f"convgate{tag}")
        h_out = _mm(a, wdn_b[layer], "nn", tm=384, tn=1024, res=h_in, name=f"ffn_down{tag}")
        return h_out, (fb, rf, u, a)

    h2, ffn0_saved = ffn_fwd(h1, 0, "0")
    (kvn_b, n1_b), r2 = _norm_fwd(h2, [kvn_g, mix1], name="norm_h2")
    kv_b = _mm(kvn_b, wkv_b, "nn", tm=1408, tn=512, out_dtype=BF16, name="kv_proj")
    q_b = _mm(n1_b, wq_b, "nn", tm=1408, tn=512, out_dtype=BF16, name="q_proj")

    def heads(t):
        return t.reshape(lp, N_HEADS, HEAD_DIM).transpose(1, 0, 2)

    def unheads(t):
        return t.transpose(1, 0, 2).reshape(lp, N_HEADS * HEAD_DIM)

    qh, kh, vh = heads(q_b), heads(kv_b[:, :d]), heads(kv_b[:, d:])
    oh, ltot = _attn_fwd(qh, kh, vh, name="attn_fwd")
    o_b = unheads(oh)
    h3 = _mm(o_b, wo_b, "nn", tm=384, tn=1024, res=h2, name="o_proj")
    h4, ffn1_saved = ffn_fwd(h3, 1, "1")

    loss8, dh4, dh4_b, dfin = _loss_bwd(h4, fin_g, tgt, seq, name="loss")

    def ffn_bwd(h_in, layer, saved, dh_out, dh_out_b, tag):
        fb, rf, u, a = saved
        dwdn = _mm(a, dh_out_b, "tn", tm=256, tn=1024, name=f"d_wdown{tag}")
        da = _mm(dh_out_b, wdn_b[layer], "nt", tm=384, tn=f, name=f"d_act{tag}")
        du_g, du_v, dcw_g, dcw_v, dcb_g, dcb_v = _convgate_bwd(u, cw_f[layer], ffn_conv_b[layer:layer + 1], da,
                                                               name=f"convgate_bwd{tag}")
        dcw = jnp.concatenate([dcw_g, dcw_v], axis=1)
        dcb = jnp.concatenate([dcb_g, dcb_v], axis=1)
        dwup = jnp.concatenate([_mm(fb, du_g, "tn", tm=1024, tn=512, name=f"d_wup_gate{tag}"),
                                _mm(fb, du_v, "tn", tm=1024, tn=512, name=f"d_wup_val{tag}")], axis=1)
        df = _mm(du_g, wup_b[layer][:, :f], "nt", tm=384, tn=1024, name=f"d_ffn_in_gate{tag}")
        df = _mm(du_v, wup_b[layer][:, f:], "nt", tm=384, tn=1024, res=df, name=f"d_ffn_in_val{tag}")
        dh_in, dh_in_b, (dfn,) = _norm_bwd(h_in, rf, [ffn_norm[layer:layer + 1]], [df], dh_out,
                                          name=f"norm_ffn_bwd{tag}")
        return dh_in, dh_in_b, dwup, dwdn, dcw, dcb, dfn

    dh3, dh3_b, dwup1, dwdn1, dcw1, dcb1, dfn1 = ffn_bwd(h3, 1, ffn1_saved, dh4, dh4_b, "1")

    dwo = _mm(o_b, dh3_b, "tn", tm=1024, tn=512, name="d_wo")
    do_b = _mm(dh3_b, wo_b, "nt", tm=384, tn=1024, out_dtype=BF16, name="d_attn_out")
    dqh, dkh, dvh = _attn_bwd(qh, kh, vh, heads(do_b), ltot, name="attn_bwd")
    dq_b = unheads(dqh).astype(BF16)
    dkv_b = jnp.concatenate([unheads(dkh), unheads(dvh)], axis=1).astype(BF16)
    dwq = _mm(n1_b, dq_b, "tn", tm=1024, tn=512, name="d_wq")
    dwkv = _mm(kvn_b, dkv_b, "tn", tm=1024, tn=512, name="d_wkv")
    dn1 = _mm(dq_b, wq_b, "nt", tm=384, tn=1024, name="d_n1")
    dkvn = _mm(dkv_b, wkv_b, "nt", tm=384, tn=1024, name="d_kvn")
    dh2, dh2_b, (dmix1, dkvg) = _norm_bwd(h2, r2, [mix1, kvn_g], [dn1, dkvn], dh3, name="norm_h2_bwd")

    dh1, dh1_b, dwup0, dwdn0, dcw0, dcb0, dfn0 = ffn_bwd(h1, 0, ffn0_saved, dh2, dh2_b, "0")

    ddiff, dwp, dscale = _pool_mix_bwd(diff_b, wp_b, scale_f, dh1, name="pool_mix_bwd")
    dn0 = _pool_diff_bwd(ddiff, name="pool_diff_bwd")
    dh0, _, (dmix0,) = _norm_bwd(h0, r0, [mix0], [dn0], dh1, name="norm_h0_bwd")

    grad_x = dh0[N_META:l_real][None]
    dmeta = dh0[:N_META]

    dwup = jnp.stack([dwup0, dwup1])
    dwdn = jnp.stack([dwdn0, dwdn1])
    g_parts = [
        dwp.reshape(N_GROUPS, N_CHIPS, gd // N_CHIPS, gd).transpose(1, 0, 2, 3),
        dwkv.reshape(d, N_CHIPS, 2 * d // N_CHIPS).transpose(1, 0, 2),
        dwq.reshape(N_CHIPS, d // N_CHIPS, d),
        dwo.reshape(N_CHIPS, d // N_CHIPS, d),
        dwup.reshape(2, d, N_CHIPS, f2 // N_CHIPS).transpose(2, 0, 1, 3),
        dwdn.reshape(2, N_CHIPS, f // N_CHIPS, d).transpose(1, 0, 2, 3),
    ]
    g_all = jnp.concatenate([p.reshape(N_CHIPS, -1) for p in g_parts], axis=1).reshape(N_CHIPS, 2, rows_half, cw)
    g_mine = lax.dynamic_index_in_dim(g_all, core, axis=1, keepdims=False)
    g_other = lax.dynamic_index_in_dim(g_all, 1 - core, axis=1, keepdims=False)
    from_sibling = _to_sibling(g_other.astype(BF16), name="grads_to_sibling")
    chip_part = _add_to_bf16(g_mine, from_sibling, name="grads_chip_sum")
    slots = lax.dynamic_update_index_in_dim(_scatter_rows(chip_part, name="grads_scatter"),
                                            lax.dynamic_index_in_dim(chip_part, chip, 0, keepdims=False), chip, 0)
    half = _sum_slots(slots, name="grads_sum")
    other_half = _to_sibling(half, name="grads_join")
    g_shard = jnp.stack([jnp.where(core == 0, half, other_half),
                         jnp.where(core == 0, other_half, half)]).reshape(n_big)

    def gbig(i, shape):
        return g_shard[offs[i]:offs[i + 1]].reshape(shape)

    g_pool_w = gbig(0, pool_w.shape)
    g_w_kv = gbig(1, w_kv.shape)
    g_w_q = gbig(2, w_q.shape)
    g_w_o = gbig(3, w_o.shape)
    g_w_up = gbig(4, ffn_w_up.shape)
    g_w_dn = gbig(5, ffn_w_down.shape)

    dcw = jnp.stack([dcw0, dcw1])
    dcb = jnp.concatenate([dcb0, dcb1], axis=0)
    sg_parts = [jnp.concatenate([dmix0, dmix1], axis=0), jnp.concatenate([dfn0, dfn1], axis=0), dkvg, dfin, dcb,
                dmeta, dscale, dcw, loss8]
    sg_sizes = [p.size for p in sg_parts]
    sg = _all_devices(_flat_rows(sg_parts, LANES), True, name="reduce_small").reshape(-1)
    sg_offs = [0]
    for sz in sg_sizes:
        sg_offs.append(sg_offs[-1] + sz)

    def gsmall(i, shape):
        return sg[sg_offs[i]:sg_offs[i + 1]].reshape(shape)

    g_mix = gsmall(0, mix_norm.shape)
    g_ffn_norm = gsmall(1, ffn_norm.shape)
    g_kv_norm = gsmall(2, kv_norm.shape)
    g_final = gsmall(3, final_norm.shape)
    g_conv_b = gsmall(4, ffn_conv_b.shape)
    csh = d // N_CHIPS
    g_meta = lax.dynamic_slice_in_dim(gsmall(5, (N_META, d)), chip * csh, csh, axis=1)
    g_scale = lax.dynamic_slice_in_dim(gsmall(6, (1, d)), chip * csh, csh, axis=1)
    fsh = f2 // N_CHIPS
    g_conv_w = lax.dynamic_slice_in_dim(gsmall(7, (2, 3, f2)), chip * fsh, fsh, axis=2)
    loss = gsmall(8, (8 * LANES,))[0]

    weights = [meta_tokens, mix_norm, ffn_norm, pool_w, pool_scale, kv_norm, w_kv, w_q, w_o, ffn_w_up, ffn_conv_w,
               ffn_conv_b, ffn_w_down, final_norm]
    grads = [g_meta, g_mix, g_ffn_norm, g_pool_w, g_scale, g_kv_norm, g_w_kv, g_w_q, g_w_o, g_w_up, g_conv_w,
             g_conv_b, g_w_dn, g_final]
    ms = [m_meta_tokens, m_mix_norm, m_ffn_norm, m_pool_w, m_pool_scale, m_kv_norm, m_w_kv, m_w_q, m_w_o,
          m_ffn_w_up, m_ffn_conv_w, m_ffn_conv_b, m_ffn_w_down, m_final_norm]
    vs = [v_meta_tokens, v_mix_norm, v_ffn_norm, v_pool_w, v_pool_scale, v_kv_norm, v_w_kv, v_w_q, v_w_o,
          v_ffn_w_up, v_ffn_conv_w, v_ffn_conv_b, v_ffn_w_down, v_final_norm]
    names = ["meta", "mix", "ffnnorm", "poolw", "poolscale", "kvnorm", "wkv", "wq", "wo", "wup", "convw", "convb",
             "wdown", "final"]
    deltas, new_ms, new_vs = [], [], []
    for w, g, m, v, nm in zip(weights, grads, ms, vs, names):
        cols = w.shape[-1]
        if w.size % (8 * LANES) == 0 and w.ndim == 1:
            cols = LANES
        view = (w.size // cols, cols)
        dl, mn, vn = _adamw(w.reshape(view), g.reshape(view), m.reshape(view), v.reshape(view), name=f"adamw_{nm}")
        deltas.append(dl.reshape(w.shape))
        new_ms.append(mn.reshape(w.shape))
        new_vs.append(vn.reshape(w.shape))

    return (loss, grad_x, *grads, *deltas, *new_ms, *new_vs)
```

```python
import jax
import jax.numpy as jnp
from jax import lax
from jax.experimental import pallas as pl
from jax.experimental.pallas import tpu as pltpu

F32 = jnp.float32
BF16 = jnp.bfloat16

N_META = 16
N_HEADS = 16
HEAD_DIM = 64
POOL_WINDOWS = (2, 4, 8, 16)
N_GROUPS = 4
RMS_EPS = 1e-6
ADAM_LR = 0.001
ADAM_B1 = 0.9
ADAM_B2 = 0.999
ADAM_EPS = 1e-08
ADAM_WD = 0.01
ADAM_STEP = 10

LANES = 128
BLK = 128
ATT_CHUNKS = 3
ATT_BQ = ATT_CHUNKS * BLK
CONV_ROWS = 128
N_CHIPS = 4
N_DEV = 8
VMEM_LIMIT = 56 * 1024 * 1024
MESH = pl.DeviceIdType.MESH


def _cparams(*sem):
    return pltpu.CompilerParams(dimension_semantics=tuple(sem) if sem else None,
                                vmem_limit_bytes=VMEM_LIMIT)


def _tile(n, pref):
    best = None
    t = LANES
    while t <= min(n, pref):
        if n % t == 0:
            best = t
        t += LANES
    assert best is not None, (n, pref)
    return best


def _row_tile(r, pref, mult):
    best = r
    for t in range(mult, min(r, pref) + 1, mult):
        if r % t == 0:
            best = t
    return best


_DIMS = {
    "nn": (((1,), (0,)), ((), ())),
    "nt": (((1,), (1,)), ((), ())),
    "tn": (((0,), (0,)), ((), ())),
}


def _mm(a, b, dims, *, tm, tn, name, out_dtype=F32, res=None):
    if dims == "tn":
        k, m = a.shape
    else:
        m, k = a.shape
    n = b.shape[0] if dims == "nt" else b.shape[1]
    tm = _tile(m, tm)
    tn = _tile(n, tn)
    a_spec = pl.BlockSpec((k, tm), lambda i, j: (0, i)) if dims == "tn" else pl.BlockSpec((tm, k), lambda i, j: (i, 0))
    b_spec = pl.BlockSpec((tn, k), lambda i, j: (j, 0)) if dims == "nt" else pl.BlockSpec((k, tn), lambda i, j: (0, j))
    o_spec = pl.BlockSpec((tm, tn), lambda i, j: (i, j))
    dn = _DIMS[dims]

    def body(*refs):
        if res is None:
            a_ref, b_ref, o_ref = refs
        else:
            a_ref, b_ref, r_ref, o_ref = refs
        acc = lax.dot_general(a_ref[...], b_ref[...], dn, preferred_element_type=F32)
        if res is not None:
            acc = acc + r_ref[...]
        o_ref[...] = acc.astype(out_dtype)

    ins = [a, b] + ([] if res is None else [res])
    specs = [a_spec, b_spec] + ([] if res is None else [o_spec])
    return pl.pallas_call(
        body, name=name, grid=(m // tm, n // tn), in_specs=specs, out_specs=o_spec,
        out_shape=jax.ShapeDtypeStruct((m, n), out_dtype),
        compiler_params=_cparams("parallel", "arbitrary"),
    )(*ins)


def _norm_fwd(x, gains, *, name):
    l, d = x.shape
    tr = _tile(l, 384)
    ng = len(gains)

    def body(*refs):
        x_ref = refs[0]
        g_refs = refs[1:1 + ng]
        o_refs = refs[1 + ng:1 + 2 * ng]
        r_ref = refs[1 + 2 * ng]
        xv = x_ref[...]
        r = lax.rsqrt(jnp.mean(xv * xv, axis=-1, keepdims=True) + RMS_EPS)
        xn = xv * r
        for g_ref, o_ref in zip(g_refs, o_refs):
            o_ref[...] = (xn * g_ref[...]).astype(BF16)
        r_ref[...] = r

    row = pl.BlockSpec((tr, d), lambda i: (i, 0))
    gspec = pl.BlockSpec((1, d), lambda i: (0, 0))
    outs = pl.pallas_call(
        body, name=name, grid=(l // tr,),
        in_specs=[row] + [gspec] * ng,
        out_specs=[row] * ng + [pl.BlockSpec((tr, 1), lambda i: (i, 0))],
        out_shape=[jax.ShapeDtypeStruct((l, d), BF16)] * ng + [jax.ShapeDtypeStruct((l, 1), F32)],
        compiler_params=_cparams("parallel"),
    )(x, *gains)
    return list(outs[:ng]), outs[ng]


def _rows8(v):
    r, d = v.shape
    return jnp.sum(v.reshape(r // 8, 8, d), axis=0)


def _norm_bwd(x, r, gains, dns, dres, *, name):
    l, d = x.shape
    tr = _tile(l, 384)
    ng = len(gains)
    nsteps = l // tr

    def body(*refs):
        x_ref, r_ref, dres_ref = refs[0], refs[1], refs[2]
        g_refs = refs[3:3 + ng]
        dn_refs = refs[3 + ng:3 + 2 * ng]
        dx_ref, dxb_ref = refs[3 + 2 * ng], refs[4 + 2 * ng]
        dg_refs = refs[5 + 2 * ng:5 + 3 * ng]
        acc_refs = refs[5 + 3 * ng:5 + 4 * ng]
        i = pl.program_id(0)

        @pl.when(i == 0)
        def _():
            for acc in acc_refs:
                acc[...] = jnp.zeros_like(acc)

        rv = r_ref[...]
        xn = x_ref[...] * rv
        total = dres_ref[...]
        for g_ref, dn_ref, acc in zip(g_refs, dn_refs, acc_refs):
            dn = dn_ref[...]
            acc[...] += _rows8(dn * xn)
            dxn = dn * g_ref[...]
            total = total + rv * (dxn - xn * jnp.mean(dxn * xn, axis=-1, keepdims=True))
        dx_ref[...] = total
        dxb_ref[...] = total.astype(BF16)

        @pl.when(i == nsteps - 1)
        def _():
            for dg_ref, acc in zip(dg_refs, acc_refs):
                dg_ref[...] = jnp.sum(acc[...], axis=0, keepdims=True)

    row = pl.BlockSpec((tr, d), lambda i: (i, 0))
    gspec = pl.BlockSpec((1, d), lambda i: (0, 0))
    outs = pl.pallas_call(
        body, name=name, grid=(nsteps,),
        in_specs=[row, pl.BlockSpec((tr, 1), lambda i: (i, 0)), row] + [gspec] * ng + [row] * ng,
        out_specs=[row, row] + [gspec] * ng,
        out_shape=[jax.ShapeDtypeStruct((l, d), F32), jax.ShapeDtypeStruct((l, d), BF16)]
        + [jax.ShapeDtypeStruct((1, d), F32)] * ng,
        scratch_shapes=[pltpu.VMEM((8, d), F32)] * ng,
        compiler_params=_cparams("arbitrary"),
    )(x, r, dres, *gains, *dns)
    return outs[0], outs[1], list(outs[2:])


def _loss_bwd(h, gain, tgt, seq, *, name):
    l, d = h.shape
    tr = _tile(l, 384)
    nsteps = l // tr

    def body(h_ref, g_ref, t_ref, loss_ref, dh_ref, dhb_ref, dg_ref, lacc, gacc):
        i = pl.program_id(0)

        @pl.when(i == 0)
        def _():
            lacc[...] = jnp.zeros_like(lacc)
            gacc[...] = jnp.zeros_like(gacc)

        xv = h_ref[...]
        g = g_ref[...]
        r = lax.rsqrt(jnp.mean(xv * xv, axis=-1, keepdims=True) + RMS_EPS)
        xn = xv * r
        rows = i * tr + lax.broadcasted_iota(jnp.int32, (tr, 1), 0)
        valid = (rows >= N_META) & (rows < N_META + seq)
        e = jnp.where(valid, xn * g - t_ref[...], 0.0)
        lacc[...] += _rows8(e * e)
        dy = e * (1.0 / d)
        gacc[...] += _rows8(dy * xn)
        dxn = dy * g
        dx = r * (dxn - xn * jnp.mean(dxn * xn, axis=-1, keepdims=True))
        dh_ref[...] = dx
        dhb_ref[...] = dx.astype(BF16)

        @pl.when(i == nsteps - 1)
        def _():
            loss_ref[...] = jnp.full((8, LANES), 0.5 / d * jnp.sum(lacc[...]), F32)
            dg_ref[...] = jnp.sum(gacc[...], axis=0, keepdims=True)

    row = pl.BlockSpec((tr, d), lambda i: (i, 0))
    gspec = pl.BlockSpec((1, d), lambda i: (0, 0))
    return pl.pallas_call(
        body, name=name, grid=(nsteps,),
        in_specs=[row, gspec, row],
        out_specs=[pl.BlockSpec((8, LANES), lambda i: (0, 0)), row, row, gspec],
        out_shape=[jax.ShapeDtypeStruct((8, LANES), F32), jax.ShapeDtypeStruct((l, d), F32),
                   jax.ShapeDtypeStruct((l, d), BF16), jax.ShapeDtypeStruct((1, d), F32)],
        scratch_shapes=[pltpu.VMEM((8, d), F32), pltpu.VMEM((8, d), F32)],
        compiler_params=_cparams("arbitrary"),
    )(h, gain, tgt)


def _shift_down(v, k, rows):
    return jnp.where(rows >= k, pltpu.roll(v, k, 0), 0.0)


def _shift_up(v, k, rows):
    l = v.shape[0]
    return jnp.where(rows < l - k, pltpu.roll(v, l - k, 0), 0.0)


def _pool_diff(n, w, rows):
    s = n
    for k in (1, 2, 4, 8):
        s = s + jnp.where(k < w, _shift_down(s, k, rows), 0.0)
    cnt = jnp.minimum(rows + 1, w).astype(F32)
    return s / cnt - n, cnt


def _pool_diff_fwd(h, r, gain, *, name):
    l, d = h.shape
    per_group = d // N_GROUPS // LANES

    def body(h_ref, r_ref, g_ref, o_ref):
        w = jnp.left_shift(2, pl.program_id(0) // per_group)
        rows = lax.broadcasted_iota(jnp.int32, (l, 1), 0)
        n = h_ref[...] * r_ref[...] * g_ref[...]
        diff, _ = _pool_diff(n, w, rows)
        o_ref[...] = diff.astype(BF16)

    col = pl.BlockSpec((l, LANES), lambda j: (0, j))
    return pl.pallas_call(
        body, name=name, grid=(d // LANES,),
        in_specs=[col, pl.BlockSpec((l, 1), lambda j: (0, 0)), pl.BlockSpec((1, LANES), lambda j: (0, j))],
        out_specs=col, out_shape=jax.ShapeDtypeStruct((l, d), BF16),
        compiler_params=_cparams("parallel"),
    )(h, r, gain)


def _pool_diff_bwd(ddiff, *, name):
    l, d = ddiff.shape
    per_group = d // N_GROUPS // LANES

    def body(dd_ref, o_ref):
        w = jnp.left_shift(2, pl.program_id(0) // per_group)
        rows = lax.broadcasted_iota(jnp.int32, (l, 1), 0)
        dd = dd_ref[...]
        s = dd / jnp.minimum(rows + 1, w).astype(F32)
        for k in (1, 2, 4, 8):
            s = s + jnp.where(k < w, _shift_up(s, k, rows), 0.0)
        o_ref[...] = s - dd

    col = pl.BlockSpec((l, LANES), lambda j: (0, j))
    return pl.pallas_call(
        body, name=name, grid=(d // LANES,), in_specs=[col], out_specs=col,
        out_shape=jax.ShapeDtypeStruct((l, d), F32), compiler_params=_cparams("parallel"),
    )(ddiff)


def _pool_mix_fwd(h, diff_b, w_pool, scale, *, name):
    l, d = h.shape
    gd = d // N_GROUPS
    tr = _tile(l, 1408)

    def body(h_ref, d_ref, w_ref, s_ref, o_ref):
        y = jnp.dot(d_ref[...], w_ref[0], preferred_element_type=F32)
        o_ref[...] = h_ref[...] + y * s_ref[...]

    blk = pl.BlockSpec((tr, gd), lambda i, g: (i, g))
    vec = pl.BlockSpec((1, gd), lambda i, g: (0, g))
    return pl.pallas_call(
        body, name=name, grid=(l // tr, N_GROUPS),
        in_specs=[blk, blk, pl.BlockSpec((1, gd, gd), lambda i, g: (g, 0, 0)), vec],
        out_specs=blk, out_shape=jax.ShapeDtypeStruct((l, d), F32),
        compiler_params=_cparams("parallel", "parallel"),
    )(h, diff_b, w_pool, scale)


def _pool_mix_bwd(diff_b, w_pool, scale, dh1, *, name):
    l, d = dh1.shape
    gd = d // N_GROUPS
    tr = _tile(l, 1408)
    nsteps = l // tr

    def body(d_ref, w_ref, s_ref, dy_ref, dd_ref, dw_ref, ds_ref, sacc):
        i = pl.program_id(1)

        @pl.when(i == 0)
        def _():
            dw_ref[...] = jnp.zeros_like(dw_ref)
            sacc[...] = jnp.zeros_like(sacc)

        diff_b_, wg, dy = d_ref[...], w_ref[0], dy_ref[...]
        yy = jnp.dot(diff_b_, wg, preferred_element_type=F32)
        sacc[...] += _rows8(dy * yy)
        dyy_b = (dy * s_ref[...]).astype(BF16)
        dw_ref[0] += lax.dot_general(diff_b_, dyy_b, _DIMS["tn"], preferred_element_type=F32)
        dd_ref[...] = lax.dot_general(dyy_b, wg, _DIMS["nt"], preferred_element_type=F32)

        @pl.when(i == nsteps - 1)
        def _():
            ds_ref[...] = jnp.sum(sacc[...], axis=0, keepdims=True)

    blk = pl.BlockSpec((tr, gd), lambda g, i: (i, g))
    vec = pl.BlockSpec((1, gd), lambda g, i: (0, g))
    wspec = pl.BlockSpec((1, gd, gd), lambda g, i: (g, 0, 0))
    return pl.pallas_call(
        body, name=name, grid=(N_GROUPS, nsteps),
        in_specs=[blk, wspec, vec, blk],
        out_specs=[blk, wspec, vec],
        out_shape=[jax.ShapeDtypeStruct((l, d), F32), jax.ShapeDtypeStruct((N_GROUPS, gd, gd), F32),
                   jax.ShapeDtypeStruct((1, d), F32)],
        scratch_shapes=[pltpu.VMEM((8, gd), F32)],
        compiler_params=_cparams("parallel", "arbitrary"),
    )(diff_b, w_pool, scale, dh1)


def _conv_chunk(cur, prev, w, b, rowi):
    s1 = jnp.where(rowi < 1, pltpu.roll(prev, 1, 0), pltpu.roll(cur, 1, 0))
    s2 = jnp.where(rowi < 2, pltpu.roll(prev, 2, 0), pltpu.roll(cur, 2, 0))
    return b + w[0] * s2 + w[1] * s1 + w[2] * cur, s1, s2


def _bcast_rows(ref, rows):
    v = ref[...]
    return [jnp.broadcast_to(v[k:k + 1], (rows, v.shape[1])) for k in range(v.shape[0])]


def _convgate_fwd(u, cw, cb, *, name):
    l, f2 = u.shape
    f = f2 // 2
    tc = _tile(f, LANES)
    nc = f // tc
    rows = CONV_ROWS
    assert l % rows == 0

    def body(ug_ref, uv_ref, wg_ref, wv_ref, bg_ref, bv_ref, a_ref):
        rowi = lax.broadcasted_iota(jnp.int32, (rows, tc), 0)
        wg, wv = _bcast_rows(wg_ref, rows), _bcast_rows(wv_ref, rows)
        bg, bv = _bcast_rows(bg_ref, rows)[0], _bcast_rows(bv_ref, rows)[0]

        def chunk(i, carry):
            pg, pv = carry
            r = pl.multiple_of(i * rows, rows)
            cg, cv = ug_ref[pl.ds(r, rows), :], uv_ref[pl.ds(r, rows), :]
            gate, _, _ = _conv_chunk(cg, pg, wg, bg, rowi)
            val, _, _ = _conv_chunk(cv, pv, wv, bv, rowi)
            a_ref[pl.ds(r, rows), :] = (gate * jax.nn.sigmoid(gate) * val).astype(BF16)
            return cg, cv

        zero = jnp.zeros((rows, tc), F32)
        lax.fori_loop(0, l // rows, chunk, (zero, zero))

    def spec(rows_, off):
        return pl.BlockSpec((rows_, tc), lambda j: (0, j + off))

    return pl.pallas_call(
        body, name=name, grid=(nc,),
        in_specs=[spec(l, 0), spec(l, nc), spec(3, 0), spec(3, nc), spec(1, 0), spec(1, nc)],
        out_specs=spec(l, 0), out_shape=jax.ShapeDtypeStruct((l, f), BF16),
        compiler_params=_cparams("parallel"),
    )(u, u, cw, cw, cb, cb)


def _convgate_bwd(u, cw, cb, da, *, name):
    l, f2 = u.shape
    f = f2 // 2
    tc = _tile(f, LANES)
    nc = f // tc
    rows = CONV_ROWS
    assert l % rows == 0
    nchunks = l // rows

    def body(ug_ref, uv_ref, wg_ref, wv_ref, bg_ref, bv_ref, da_ref,
             dug_ref, duv_ref, dcwg_ref, dcwv_ref, dcbg_ref, dcbv_ref):
        rowi = lax.broadcasted_iota(jnp.int32, (rows, tc), 0)
        wg, wv = _bcast_rows(wg_ref, rows), _bcast_rows(wv_ref, rows)
        bg, bv = _bcast_rows(bg_ref, rows)[0], _bcast_rows(bv_ref, rows)[0]

        def tap_sums(acc, dc, taps):
            return (acc[0] + _rows8(dc),) + tuple(a + _rows8(dc * t) for a, t in zip(acc[1:], taps))

        def input_grad(dc, dc_next, w):
            up1 = jnp.where(rowi >= rows - 1, pltpu.roll(dc_next, rows - 1, 0), pltpu.roll(dc, rows - 1, 0))
            up2 = jnp.where(rowi >= rows - 2, pltpu.roll(dc_next, rows - 2, 0), pltpu.roll(dc, rows - 2, 0))
            return (w[2] * dc + w[1] * up1 + w[0] * up2).astype(BF16)

        def chunk(ii, carry):
            dcg_next, dcv_next, acc_g, acc_v = carry
            i = nchunks - 1 - ii
            r = pl.multiple_of(i * rows, rows)
            rp = pl.multiple_of(jnp.maximum(i - 1, 0) * rows, rows)
            cg, cv = ug_ref[pl.ds(r, rows), :], uv_ref[pl.ds(r, rows), :]
            pg = jnp.where(i > 0, ug_ref[pl.ds(rp, rows), :], 0.0)
            pv = jnp.where(i > 0, uv_ref[pl.ds(rp, rows), :], 0.0)
            gate, g1, g2 = _conv_chunk(cg, pg, wg, bg, rowi)
            val, v1, v2 = _conv_chunk(cv, pv, wv, bv, rowi)
            sg = jax.nn.sigmoid(gate)
            dav = da_ref[pl.ds(r, rows), :]
            dcg = dav * val * (sg * (1.0 + gate * (1.0 - sg)))
            dcv = dav * (gate * sg)
            acc_g = tap_sums(acc_g, dcg, (g2, g1, cg))
            acc_v = tap_sums(acc_v, dcv, (v2, v1, cv))
            dug_ref[pl.ds(r, rows), :] = input_grad(dcg, dcg_next, wg)
            duv_ref[pl.ds(r, rows), :] = input_grad(dcv, dcv_next, wv)
            return dcg, dcv, acc_g, acc_v

        zero = jnp.zeros((rows, tc), F32)
        zero8 = (jnp.zeros((8, tc), F32),) * 4
        _, _, acc_g, acc_v = lax.fori_loop(0, nchunks, chunk, (zero, zero, zero8, zero8))
        for acc, dcw_ref, dcb_ref in ((acc_g, dcwg_ref, dcbg_ref), (acc_v, dcwv_ref, dcbv_ref)):
            dcb_ref[...] = jnp.sum(acc[0], axis=0, keepdims=True)
            for k in range(3):
                dcw_ref[k:k + 1, :] = jnp.sum(acc[1 + k], axis=0, keepdims=True)

    def spec(rows_, off):
        return pl.BlockSpec((rows_, tc), lambda j: (0, j + off))

    return pl.pallas_call(
        body, name=name, grid=(nc,),
        in_specs=[spec(l, 0), spec(l, nc), spec(3, 0), spec(3, nc), spec(1, 0), spec(1, nc), spec(l, 0)],
        out_specs=[spec(l, 0), spec(l, 0), spec(3, 0), spec(3, 0), spec(1, 0), spec(1, 0)],
        out_shape=[jax.ShapeDtypeStruct((l, f), BF16)] * 2 + [jax.ShapeDtypeStruct((3, f), F32)] * 2
        + [jax.ShapeDtypeStruct((1, f), F32)] * 2,
        compiler_params=_cparams("parallel"),
    )(u, u, cw, cw, cb, cb, da)


def _cumsum_mm(v_b, t2):
    return jnp.dot(v_b, t2, preferred_element_type=F32)


def _tri_and_ones(tri_fn):
    row = lax.broadcasted_iota(jnp.int32, (BLK, 2 * BLK), 0)
    col = lax.broadcasted_iota(jnp.int32, (BLK, 2 * BLK), 1)
    return jnp.where((col >= BLK) | tri_fn(row, col), 1.0, 0.0).astype(BF16)


def _logits(z, mask):
    sp = jnp.log(1.0 + jnp.exp(-jnp.abs(z)))
    lb = jnp.minimum(z, 0.0) - sp
    lm = lb - z
    if mask is not None:
        lm = jnp.where(mask, lm, 0.0)
    return lb, lm


def _software_pipeline(stages, n, block_of, state):
    ns = len(stages)
    inflight = [None] * (ns - 1)
    for t in range(ns - 1):
        new = list(inflight)
        for s in range(t, -1, -1):
            y, state = stages[s](block_of(t - s), None if s == 0 else inflight[s - 1], state)
            new[s] = y
        inflight = new

    def steady(i, carry):
        inflight, state = carry
        new = [None] * (ns - 1)
        for s in range(ns - 1, -1, -1):
            y, state = stages[s](block_of(i + ns - 1 - s), None if s == 0 else inflight[s - 1], state)
            if s < ns - 1:
                new[s] = y
        return tuple(new), state

    inflight, state = lax.fori_loop(0, n - (ns - 1), steady, (tuple(inflight), state))
    inflight = list(inflight)
    for e in range(1, ns):
        new = list(inflight)
        for s in range(ns - 1, e - 1, -1):
            y, state = stages[s](block_of(n - 1 + e - s), inflight[s - 1], state)
            if s < ns - 1:
                new[s] = y
        inflight = new
    return state


def _attn_fwd(q, k, v, *, name):
    nh, l, dh = q.shape
    assert l % ATT_BQ == 0
    nq = l // ATT_BQ
    qscale = HEAD_DIM ** -0.5
    chunks = range(ATT_CHUNKS)

    def body(q_ref, k_ref, v_ref, o_ref, lt_ref, z_scr, cs_scr, pv_scr):
        t_later = _tri_and_ones(lambda r, c: r > c)
        dmask = (lax.broadcasted_iota(jnp.int32, (BLK, BLK), 1)
                 < lax.broadcasted_iota(jnp.int32, (BLK, BLK), 0))

        def logits(qc, c0, mask):
            z = lax.dot_general(qc, k_ref[0, pl.ds(c0, BLK), :], _DIMS["nt"], preferred_element_type=F32)
            lb, lm = _logits(z, mask)
            return lb, lm.astype(BF16)

        def weights(lb, lm_b, mask, run):
            cs = _cumsum_mm(lm_b, t_later)
            a = jnp.exp(lb + cs[:, :BLK] + run)
            if mask is not None:
                a = jnp.where(mask, a, 0.0)
            return a.astype(BF16), run + cs[:, BLK:]

        def accumulate(a_b, c0, acc):
            return acc + jnp.dot(a_b, v_ref[0, pl.ds(c0, BLK), :], preferred_element_type=F32)

        def qblock(qb, _):
            r0 = pl.multiple_of(qb * ATT_BQ, ATT_BQ)
            qs = [q_ref[0, pl.ds(r0 + rc * BLK, BLK), :] * jnp.asarray(qscale, BF16) for rc in chunks]
            accs = [jnp.zeros((BLK, dh), F32)] * ATT_CHUNKS
            runs = [jnp.zeros((BLK, BLK), F32)] * ATT_CHUNKS
            for dj in range(ATT_CHUNKS - 1, -1, -1):
                c0 = r0 + dj * BLK
                for rc in range(dj, ATT_CHUNKS):
                    mask = dmask if rc == dj else None
                    a_b, runs[rc] = weights(*logits(qs[rc], c0, mask), mask, runs[rc])
                    accs[rc] = accumulate(a_b, c0, accs[rc])

            def col0(b):
                return pl.multiple_of(r0 - (b + 1) * BLK, BLK)

            def stage_scores(b, _, state):
                for rc in chunks:
                    z_scr[rc] = lax.dot_general(qs[rc], k_ref[0, pl.ds(col0(b), BLK), :], _DIMS["nt"],
                                                preferred_element_type=F32)
                return (), state

            def stage_cumsum(b, _, state):
                lbs = []
                for rc in chunks:
                    lb, lm = _logits(z_scr[rc], None)
                    cs_scr[rc] = _cumsum_mm(lm.astype(BF16), t_later)
                    lbs.append(lb)
                return tuple(lbs), state

            def stage_weights(b, lbs, state):
                accs, runs = state
                new_runs = []
                for rc in chunks:
                    cs = cs_scr[rc]
                    a = jnp.exp(lbs[rc] + cs[:, :BLK] + runs[rc])
                    pv_scr[rc] = jnp.dot(a.astype(BF16), v_ref[0, pl.ds(col0(b), BLK), :],
                                         preferred_element_type=F32)
                    new_runs.append(runs[rc] + cs[:, BLK:])
                return (), (accs, tuple(new_runs))

            def stage_acc(b, _, state):
                accs, runs = state
                return None, (tuple(accs[rc] + pv_scr[rc] for rc in chunks), runs)

            def left_region(state):
                return _software_pipeline([stage_scores, stage_cumsum, stage_weights, stage_acc],
                                          qb * ATT_CHUNKS, lambda b: b, state)

            accs, runs = lax.cond(qb > 0, left_region, lambda s: s, (tuple(accs), tuple(runs)))
            for rc in chunks:
                o_ref[0, pl.ds(r0 + rc * BLK, BLK), :] = accs[rc].astype(BF16)
                lt_ref[0, pl.ds(r0 + rc * BLK, BLK), :] = runs[rc]
            return 0

        lax.fori_loop(0, nq, qblock, 0)

    head = pl.BlockSpec((1, l, dh), lambda h: (h, 0, 0))
    return pl.pallas_call(
        body, name=name, grid=(nh,),
        in_specs=[head, head, head],
        out_specs=[head, pl.BlockSpec((1, l, BLK), lambda h: (h, 0, 0))],
        out_shape=[jax.ShapeDtypeStruct((nh, l, dh), BF16), jax.ShapeDtypeStruct((nh, l, BLK), F32)],
        scratch_shapes=[pltpu.VMEM((ATT_CHUNKS, BLK, BLK), F32), pltpu.VMEM((ATT_CHUNKS, BLK, 2 * BLK), F32),
                        pltpu.VMEM((ATT_CHUNKS, BLK, dh), F32)],
        compiler_params=_cparams("parallel"),
    )(q, k, v)


def _attn_bwd(q, k, v, do, ltot, *, name):
    nh, l, dh = q.shape
    assert l % ATT_BQ == 0
    nq = l // ATT_BQ
    qscale = HEAD_DIM ** -0.5
    chunks = range(ATT_CHUNKS)

    def body(q_ref, k_ref, v_ref, do_ref, lt_ref, dq_ref, dk_ref, dv_ref,
             z_scr, cs_scr, da_scr, cd_scr, dqp_scr, dkp_scr, dvp_scr):
        t_incl = _tri_and_ones(lambda r, c: r <= c)
        t_excl = _tri_and_ones(lambda r, c: r < c)
        dmask = (lax.broadcasted_iota(jnp.int32, (BLK, BLK), 1)
                 < lax.broadcasted_iota(jnp.int32, (BLK, BLK), 0))
        dk_ref[...] = jnp.zeros_like(dk_ref)
        dv_ref[...] = jnp.zeros_like(dv_ref)

        def logits(qc, c0, mask):
            z = lax.dot_general(qc, k_ref[0, pl.ds(c0, BLK), :], _DIMS["nt"], preferred_element_type=F32)
            lb, lm = _logits(z, mask)
            return lb, lm.astype(BF16)

        def weights(lb, lm_b, doc, ltc, c0, mask, pre_lm):
            cs = _cumsum_mm(lm_b, t_incl)
            da = lax.dot_general(doc, v_ref[0, pl.ds(c0, BLK), :], _DIMS["nt"], preferred_element_type=F32)
            a = jnp.exp(lb + (ltc - pre_lm - cs[:, :BLK]))
            if mask is not None:
                a = jnp.where(mask, a, 0.0)
            dl = a * da
            return (jnp.exp(lb), dl, dl.astype(BF16), a.astype(BF16)), pre_lm + cs[:, BLK:]

        def logit_grad(beta, dl, dl_b, mask, pre_dl):
            cd = _cumsum_mm(dl_b, t_excl)
            dz = dl - beta * (dl + pre_dl + cd[:, :BLK])
            if mask is not None:
                dz = jnp.where(mask, dz, 0.0)
            return (dz * qscale).astype(BF16), pre_dl + cd[:, BLK:]

        def key_grads(c0, dz_parts, a_parts, q_rows, do_rows):
            dz_all = dz_parts[0] if len(dz_parts) == 1 else jnp.concatenate(dz_parts, axis=0)
            a_all = a_parts[0] if len(a_parts) == 1 else jnp.concatenate(a_parts, axis=0)
            dk_ref[0, pl.ds(c0, BLK), :] += lax.dot_general(dz_all, q_rows, _DIMS["tn"], preferred_element_type=F32)
            dv_ref[0, pl.ds(c0, BLK), :] += lax.dot_general(a_all, do_rows, _DIMS["tn"], preferred_element_type=F32)

        def qblock(qb, _):
            r0 = pl.multiple_of(qb * ATT_BQ, ATT_BQ)
            q_raw = q_ref[0, pl.ds(r0, ATT_BQ), :]
            do_all = do_ref[0, pl.ds(r0, ATT_BQ), :]
            qs = [q_raw[rc * BLK:(rc + 1) * BLK] * jnp.asarray(qscale, BF16) for rc in chunks]
            dos = [do_all[rc * BLK:(rc + 1) * BLK] for rc in chunks]
            lts = [lt_ref[0, pl.ds(r0 + rc * BLK, BLK), :] for rc in chunks]

            def col0(b):
                return pl.multiple_of(b * BLK, BLK)

            def stage_scores(b, _, state):
                kj = k_ref[0, pl.ds(col0(b), BLK), :]
                for rc in chunks:
                    z_scr[rc] = lax.dot_general(qs[rc], kj, _DIMS["nt"], preferred_element_type=F32)
                return (), state

            def stage_prefix(b, _, state):
                vj = v_ref[0, pl.ds(col0(b), BLK), :]
                lbs = []
                for rc in chunks:
                    lb, lm = _logits(z_scr[rc], None)
                    cs_scr[rc] = _cumsum_mm(lm.astype(BF16), t_incl)
                    da_scr[rc] = lax.dot_general(dos[rc], vj, _DIMS["nt"], preferred_element_type=F32)
                    lbs.append(lb)
                return tuple(lbs), state

            def stage_weights(b, lbs, state):
                dqs, pls, pds = state
                out, new_pls = [], []
                for rc in chunks:
                    cs = cs_scr[rc]
                    a = jnp.exp(lbs[rc] + (lts[rc] - pls[rc] - cs[:, :BLK]))
                    dl = a * da_scr[rc]
                    cd_scr[rc] = _cumsum_mm(dl.astype(BF16), t_excl)
                    out.append((jnp.exp(lbs[rc]), dl, a.astype(BF16)))
                    new_pls.append(pls[rc] + cs[:, BLK:])
                return tuple(out), (dqs, tuple(new_pls), pds)

            def stage_products(b, ys, state):
                dqs, pls, pds = state
                c0 = col0(b)
                kj = k_ref[0, pl.ds(c0, BLK), :]
                dz_bs, new_pds = [], []
                for rc in chunks:
                    beta, dl, _ = ys[rc]
                    cd = cd_scr[rc]
                    dz_b = ((dl - beta * (dl + pds[rc] + cd[:, :BLK])) * qscale).astype(BF16)
                    dqp_scr[rc] = jnp.dot(dz_b, kj, preferred_element_type=F32)
                    dz_bs.append(dz_b)
                    new_pds.append(pds[rc] + cd[:, BLK:])
                dkp_scr[...] = lax.dot_general(jnp.concatenate(dz_bs, axis=0), q_raw, _DIMS["tn"],
                                               preferred_element_type=F32)
                dvp_scr[...] = lax.dot_general(jnp.concatenate([y[2] for y in ys], axis=0), do_all, _DIMS["tn"],
                                               preferred_element_type=F32)
                return (), (dqs, pls, tuple(new_pds))

            def stage_acc(b, _, state):
                dqs, pls, pds = state
                c0 = col0(b)
                dk_ref[0, pl.ds(c0, BLK), :] += dkp_scr[...]
                dv_ref[0, pl.ds(c0, BLK), :] += dvp_scr[...]
                return None, (tuple(dqs[rc] + dqp_scr[rc] for rc in chunks), pls, pds)

            stages = [stage_scores, stage_prefix, stage_weights, stage_products, stage_acc]
            zero = jnp.zeros((BLK, BLK), F32)
            state = ((jnp.zeros((BLK, dh), F32),) * ATT_CHUNKS, (zero,) * ATT_CHUNKS, (zero,) * ATT_CHUNKS)

            def pipelined(state):
                return _software_pipeline(stages, qb * ATT_CHUNKS, lambda b: b, state)

            def one_by_one(state):
                def block(b, state):
                    x = None
                    for stage in stages:
                        x, state = stage(b, x, state)
                    return state
                return lax.fori_loop(0, qb * ATT_CHUNKS, block, state)

            dqs, pls, pds = lax.cond(qb * ATT_CHUNKS >= len(stages) - 1, pipelined, one_by_one, state)
            dqs, pls, pds = list(dqs), list(pls), list(pds)
            for dj in chunks:
                c0 = r0 + dj * BLK
                kj = k_ref[0, pl.ds(c0, BLK), :]
                dz_parts, a_parts = [], []
                for rc in range(dj, ATT_CHUNKS):
                    mask = dmask if rc == dj else None
                    ys, pls[rc] = weights(*logits(qs[rc], c0, mask), dos[rc], lts[rc], c0, mask, pls[rc])
                    dz_b, pds[rc] = logit_grad(*ys[:3], mask, pds[rc])
                    a_b = ys[3]
                    dqs[rc] = dqs[rc] + jnp.dot(dz_b, kj, preferred_element_type=F32)
                    dz_parts.append(dz_b)
                    a_parts.append(a_b)
                key_grads(c0, dz_parts, a_parts, q_raw[dj * BLK:], do_all[dj * BLK:])
            for rc in chunks:
                dq_ref[0, pl.ds(r0 + rc * BLK, BLK), :] = dqs[rc]
            return 0

        lax.fori_loop(0, nq, qblock, 0)

    head = pl.BlockSpec((1, l, dh), lambda h: (h, 0, 0))
    sq = pltpu.VMEM((ATT_CHUNKS, BLK, BLK), F32)
    sw = pltpu.VMEM((ATT_CHUNKS, BLK, 2 * BLK), F32)
    return pl.pallas_call(
        body, name=name, grid=(nh,),
        in_specs=[head, head, head, head, pl.BlockSpec((1, l, BLK), lambda h: (h, 0, 0))],
        out_specs=[head, head, head],
        out_shape=[jax.ShapeDtypeStruct((nh, l, dh), F32)] * 3,
        scratch_shapes=[sq, sw, sq, sw, pltpu.VMEM((ATT_CHUNKS, BLK, dh), F32),
                        pltpu.VMEM((BLK, dh), F32), pltpu.VMEM((BLK, dh), F32)],
        compiler_params=_cparams("parallel"),
    )(q, k, v, do, ltot)


def _add_to_bf16(a, b, *, name):
    n, r, c = a.shape
    tr = _row_tile(r, 1536, 16)
    blk = pl.BlockSpec((1, tr, c), lambda i, j: (i, j, 0))

    def body(a_ref, b_ref, o_ref):
        o_ref[...] = (a_ref[...] + b_ref[...].astype(F32)).astype(BF16)

    return pl.pallas_call(body, name=name, grid=(n, r // tr), in_specs=[blk, blk], out_specs=blk,
                          out_shape=jax.ShapeDtypeStruct(a.shape, BF16),
                          compiler_params=_cparams("parallel", "parallel"))(a, b)


def _sum_slots(p, *, name):
    n, r, c = p.shape
    tr = _row_tile(r, 1536, 16)

    def body(p_ref, o_ref):
        acc = p_ref[0].astype(F32)
        for s in range(1, n):
            acc = acc + p_ref[s].astype(F32)
        o_ref[...] = acc

    return pl.pallas_call(body, name=name, grid=(r // tr,),
                          in_specs=[pl.BlockSpec((n, tr, c), lambda j: (0, j, 0))],
                          out_specs=pl.BlockSpec((tr, c), lambda j: (j, 0)),
                          out_shape=jax.ShapeDtypeStruct((r, c), F32),
                          compiler_params=_cparams("parallel"))(p)


def _adamw(w, g, m, v, *, name):
    r, c = w.shape
    tr = _row_tile(r, 512, 8)
    blk = pl.BlockSpec((tr, c), lambda i: (i, 0))

    def body(w_ref, g_ref, m_ref, v_ref, d_ref, mo_ref, vo_ref):
        gv = g_ref[...]
        mn = ADAM_B1 * m_ref[...] + (1.0 - ADAM_B1) * gv
        vn = ADAM_B2 * v_ref[...] + (1.0 - ADAM_B2) * (gv * gv)
        m_hat = mn / (1.0 - ADAM_B1 ** ADAM_STEP)
        v_hat = vn / (1.0 - ADAM_B2 ** ADAM_STEP)
        d_ref[...] = -ADAM_LR * (m_hat / (jnp.sqrt(v_hat) + ADAM_EPS) + ADAM_WD * w_ref[...])
        mo_ref[...] = mn
        vo_ref[...] = vn

    return pl.pallas_call(body, name=name, grid=(r // tr,), in_specs=[blk] * 4, out_specs=[blk] * 3,
                          out_shape=[jax.ShapeDtypeStruct((r, c), F32)] * 3,
                          compiler_params=_cparams("parallel"))(w, g, m, v)


_ANY = pl.BlockSpec(memory_space=pl.ANY)


def _place():
    x, y, c = lax.axis_index("x"), lax.axis_index("y"), lax.axis_index("c")
    chips = [(1 - x, y), (x, 1 - y), (1 - x, 1 - y)]
    return x, y, c, chips


def _gather_shards(wsh, *, name):
    _, r, cdim = wsh.shape

    def body(w_ref, out_ref, stage_ref, send_sems, recv_sems):
        x, y, c, chips = _place()
        s = 2 * x + y
        sibling = (x, y, 1 - c)

        def copy(k, src, dst, to):
            return pltpu.make_async_remote_copy(src_ref=src, dst_ref=dst, send_sem=send_sems.at[k],
                                                recv_sem=recv_sems.at[k], device_id=to, device_id_type=MESH)

        first = [copy(j, w_ref.at[c], out_ref.at[s, c], (*chip, c)) for j, chip in enumerate(chips)]
        for cp in first:
            cp.start()
        passed = []
        for j, (px, py) in enumerate(chips):
            sp = 2 * px + py
            copy(j, w_ref.at[c], out_ref.at[sp, c], sibling).wait_recv()
            fwd = copy(3 + j, out_ref.at[sp, c], out_ref.at[sp, c], sibling)
            fwd.start()
            passed.append(fwd)
        for h in range(2):
            pltpu.sync_copy(w_ref.at[h], stage_ref)
            pltpu.sync_copy(stage_ref, out_ref.at[s, h])
        for j, (px, py) in enumerate(chips):
            sp = 2 * px + py
            copy(3 + j, w_ref.at[c], out_ref.at[sp, 1 - c], sibling).wait_recv()
        for cp in first + passed:
            cp.wait_send()

    return pl.pallas_call(
        body, name=name, in_specs=[_ANY], out_specs=_ANY,
        out_shape=jax.ShapeDtypeStruct((N_CHIPS, 2, r, cdim), wsh.dtype),
        scratch_shapes=[pltpu.VMEM((r, cdim), wsh.dtype), pltpu.SemaphoreType.DMA((6,)),
                        pltpu.SemaphoreType.DMA((6,))],
        compiler_params=pltpu.CompilerParams(vmem_limit_bytes=VMEM_LIMIT),
    )(wsh)


def _to_sibling(a, *, name):
    def body(a_ref, out_ref, send_sem, recv_sem):
        x, y, c, _ = _place()
        cp = pltpu.make_async_remote_copy(src_ref=a_ref, dst_ref=out_ref, send_sem=send_sem, recv_sem=recv_sem,
                                          device_id=(x, y, 1 - c), device_id_type=MESH)
        cp.start()
        cp.wait()

    return pl.pallas_call(
        body, name=name, in_specs=[_ANY], out_specs=_ANY,
        out_shape=jax.ShapeDtypeStruct(a.shape, a.dtype),
        scratch_shapes=[pltpu.SemaphoreType.DMA, pltpu.SemaphoreType.DMA],
    )(a)


def _scatter_rows(p, *, name):
    def body(p_ref, out_ref, stage_ref, send_sems, recv_sems):
        x, y, c, chips = _place()
        s = 2 * x + y
        sends = []
        for j, (px, py) in enumerate(chips):
            sp = 2 * px + py
            cp = pltpu.make_async_remote_copy(src_ref=p_ref.at[sp], dst_ref=out_ref.at[s], send_sem=send_sems.at[j],
                                              recv_sem=recv_sems.at[j], device_id=(px, py, c), device_id_type=MESH)
            cp.start()
            sends.append(cp)
        pltpu.sync_copy(p_ref.at[s], stage_ref)
        pltpu.sync_copy(stage_ref, out_ref.at[s])
        for j, (px, py) in enumerate(chips):
            sp = 2 * px + py
            pltpu.make_async_remote_copy(src_ref=p_ref.at[sp], dst_ref=out_ref.at[sp], send_sem=send_sems.at[j],
                                         recv_sem=recv_sems.at[j], device_id=(px, py, c),
                                         device_id_type=MESH).wait_recv()
        for cp in sends:
            cp.wait_send()

    return pl.pallas_call(
        body, name=name, in_specs=[_ANY], out_specs=_ANY,
        out_shape=jax.ShapeDtypeStruct(p.shape, p.dtype),
        scratch_shapes=[pltpu.VMEM(p.shape[1:], p.dtype), pltpu.SemaphoreType.DMA((3,)),
                        pltpu.SemaphoreType.DMA((3,))],
        compiler_params=pltpu.CompilerParams(vmem_limit_bytes=VMEM_LIMIT),
    )(p)


def _all_devices(a, reduce, *, name):
    r, cdim = a.shape

    def body(a_ref, out_ref, *scratch):
        if reduce:
            buf, send_sems, recv_sems = scratch
        else:
            buf = out_ref
            send_sems, recv_sems = scratch
        x, y, c, _ = _place()
        me = 4 * x + 2 * y + c
        buf[me] = a_ref[...]
        peers = []
        for k in range(1, N_DEV):
            dx, dy, dc = (k >> 2) & 1, (k >> 1) & 1, k & 1
            peers.append((x ^ dx, y ^ dy, c ^ dc))
        sends = []
        for k, peer in enumerate(peers):
            cp = pltpu.make_async_remote_copy(src_ref=a_ref, dst_ref=buf.at[me], send_sem=send_sems.at[k],
                                              recv_sem=recv_sems.at[k], device_id=peer, device_id_type=MESH)
            cp.start()
            sends.append(cp)
        for k, (px, py, pc) in enumerate(peers):
            pltpu.make_async_remote_copy(src_ref=a_ref, dst_ref=buf.at[4 * px + 2 * py + pc],
                                         send_sem=send_sems.at[k], recv_sem=recv_sems.at[k],
                                         device_id=(px, py, pc), device_id_type=MESH).wait_recv()
        for cp in sends:
            cp.wait_send()
        if reduce:
            acc = buf[0]
            for k in range(1, N_DEV):
                acc = acc + buf[k]
            out_ref[...] = acc

    vm = pl.BlockSpec(memory_space=pltpu.VMEM)
    sems = [pltpu.SemaphoreType.DMA((N_DEV - 1,)), pltpu.SemaphoreType.DMA((N_DEV - 1,))]
    if reduce:
        out_shape = jax.ShapeDtypeStruct((r, cdim), F32)
        scratch = [pltpu.VMEM((N_DEV, r, cdim), F32)] + sems
    else:
        out_shape = jax.ShapeDtypeStruct((N_DEV, r, cdim), F32)
        scratch = sems
    return pl.pallas_call(
        body, name=name, in_specs=[vm], out_specs=vm, out_shape=out_shape, scratch_shapes=scratch,
        compiler_params=pltpu.CompilerParams(vmem_limit_bytes=VMEM_LIMIT),
    )(a)


def _flat_rows(parts, cols):
    flat = jnp.concatenate([p.reshape(-1) for p in parts])
    padded = -(-flat.size // (8 * cols)) * (8 * cols)
    return jnp.pad(flat, (0, padded - flat.size)).reshape(-1, cols)


def kernel(x, meta_tokens, mix_norm, ffn_norm, pool_w, pool_scale, kv_norm, w_kv, w_q, w_o, ffn_w_up, ffn_conv_w, ffn_conv_b, ffn_w_down, final_norm, loss_target, m_meta_tokens, m_mix_norm, m_ffn_norm, m_pool_w, m_pool_scale, m_kv_norm, m_w_kv, m_w_q, m_w_o, m_ffn_w_up, m_ffn_conv_w, m_ffn_conv_b, m_ffn_w_down, m_final_norm, v_meta_tokens, v_mix_norm, v_ffn_norm, v_pool_w, v_pool_scale, v_kv_norm, v_w_kv, v_w_q, v_w_o, v_ffn_w_up, v_ffn_conv_w, v_ffn_conv_b, v_ffn_w_down, v_final_norm):
    seq, d = x.shape[1], x.shape[2]
    l_real = N_META + seq
    lp = -(-l_real // ATT_BQ) * ATT_BQ
    f2 = ffn_w_up.shape[2] * N_CHIPS
    f = f2 // 2
    gd = d // N_GROUPS
    chip = 2 * lax.axis_index("x") + lax.axis_index("y")
    core = lax.axis_index("c")

    big_parts = [pool_w[0], w_kv, w_q[0], w_o[0], ffn_w_up, ffn_w_down]
    sizes = [p.size for p in big_parts]
    n_big = sum(sizes)
    cw = 1024
    assert n_big % (2 * 16 * cw) == 0
    rows_half = n_big // (2 * cw)
    wsh = _flat_rows([p.astype(BF16) for p in big_parts], cw).reshape(2, rows_half, cw)
    wall = _gather_shards(wsh, name="gather_weights").reshape(N_CHIPS, n_big)
    offs = [0]
    for sz in sizes:
        offs.append(offs[-1] + sz)

    def big(i, shape):
        return wall[:, offs[i]:offs[i + 1]].reshape((N_CHIPS,) + shape)

    wp_b = big(0, (N_GROUPS, gd // N_CHIPS, gd)).transpose(1, 0, 2, 3).reshape(N_GROUPS, gd, gd)
    wkv_b = big(1, (d, 2 * d // N_CHIPS)).transpose(1, 0, 2).reshape(d, 2 * d)
    wq_b = big(2, (d // N_CHIPS, d)).reshape(d, d)
    wo_b = big(3, (d // N_CHIPS, d)).reshape(d, d)
    wup_b = big(4, (2, d, f2 // N_CHIPS)).transpose(1, 2, 0, 3).reshape(2, d, f2)
    wdn_b = big(5, (2, f // N_CHIPS, d)).transpose(1, 0, 2, 3).reshape(2, f, d)

    small_parts = [meta_tokens, pool_scale, ffn_conv_w]
    ssizes = [p.size for p in small_parts]
    small = _flat_rows(small_parts, LANES)
    sall = _all_devices(small, False, name="gather_small")[::2].reshape(N_CHIPS, -1)
    meta_f = sall[:, :ssizes[0]].reshape(N_CHIPS, N_META, d // N_CHIPS).transpose(1, 0, 2).reshape(N_META, d)
    scale_f = sall[:, ssizes[0]:ssizes[0] + ssizes[1]].reshape(1, d)
    cw_f = sall[:, ssizes[0] + ssizes[1]:sum(ssizes)].reshape(N_CHIPS, 2, 3, f2 // N_CHIPS).transpose(1, 2, 0, 3).reshape(2, 3, f2)

    mix0, mix1 = mix_norm[0:1], mix_norm[1:2]
    fn0, fn1 = ffn_norm[0:1], ffn_norm[1:2]
    kvn_g = kv_norm.reshape(1, d)
    fin_g = final_norm.reshape(1, d)

    pad = lp - l_real
    h0 = jnp.concatenate([meta_f, x[0], jnp.zeros((pad, d), F32)], axis=0)
    tgt = jnp.pad(loss_target[0], ((N_META, pad), (0, 0)))

    _, r0 = _norm_fwd(h0, [mix0], name="norm_h0")
    diff_b = _pool_diff_fwd(h0, r0, mix0, name="pool_diff")
    h1 = _pool_mix_fwd(h0, diff_b, wp_b, scale_f, name="pool_mix")

    def ffn_fwd(h_in, layer, tag):
        (fb,), rf = _norm_fwd(h_in, [ffn_norm[layer:layer + 1]], name=f"norm_ffn{tag}")
        u = _mm(fb, wup_b[layer], "nn", tm=1408, tn=512, name=f"ffn_up{tag}")
        a = _convgate_fwd(u, cw_f[layer], ffn_conv_b[layer:layer + 1], name=f"convgate{tag}")
        h_out = _mm(a, wdn_b[layer], "nn", tm=384, tn=1024, res=h_in, name=f"ffn_down{tag}")
        return h_out, (fb, rf, u, a)

    h2, ffn0_saved = ffn_fwd(h1, 0, "0")
    (kvn_b, n1_b), r2 = _norm_fwd(h2, [kvn_g, mix1], name="norm_h2")
    kv_b = _mm(kvn_b, wkv_b, "nn", tm=1408, tn=512, out_dtype=BF16, name="kv_proj")
    q_b = _mm(n1_b, wq_b, "nn", tm=1408, tn=512, out_dtype=BF16, name="q_proj")

    def heads(t):
        return t.reshape(lp, N_HEADS, HEAD_DIM).transpose(1, 0, 2)

    def unheads(t):
        return t.transpose(1, 0, 2).reshape(lp, N_HEADS * HEAD_DIM)

    qh, kh, vh = heads(q_b), heads(kv_b[:, :d]), heads(kv_b[:, d:])
    oh, ltot = _attn_fwd(qh, kh, vh, name="attn_fwd")
    o_b = unheads(oh)
    h3 = _mm(o_b, wo_b, "nn", tm=384, tn=1024, res=h2, name="o_proj")
    h4, ffn1_saved = ffn_fwd(h3, 1, "1")

    loss8, dh4, dh4_b, dfin = _loss_bwd(h4, fin_g, tgt, seq, name="loss")

    def ffn_bwd(h_in, layer, saved, dh_out, dh_out_b, tag):
        fb, rf, u, a = saved
        dwdn = _mm(a, dh_out_b, "tn", tm=256, tn=1024, name=f"d_wdown{tag}")
        da = _mm(dh_out_b, wdn_b[layer], "nt", tm=384, tn=f, name=f"d_act{tag}")
        du_g, du_v, dcw_g, dcw_v, dcb_g, dcb_v = _convgate_bwd(u, cw_f[layer], ffn_conv_b[layer:layer + 1], da,
                                                               name=f"convgate_bwd{tag}")
        dcw = jnp.concatenate([dcw_g, dcw_v], axis=1)
        dcb = jnp.concatenate([dcb_g, dcb_v], axis=1)
        dwup = (_mm(fb, du_g, "tn", tm=1024, tn=512, name=f"d_wup_gate{tag}"),
                _mm(fb, du_v, "tn", tm=1024, tn=512, name=f"d_wup_val{tag}"))
        df = _mm(du_g, wup_b[layer][:, :f], "nt", tm=384, tn=1024, name=f"d_ffn_in_gate{tag}")
        df = _mm(du_v, wup_b[layer][:, f:], "nt", tm=384, tn=1024, res=df, name=f"d_ffn_in_val{tag}")
        dh_in, dh_in_b, (dfn,) = _norm_bwd(h_in, rf, [ffn_norm[layer:layer + 1]], [df], dh_out,
                                          name=f"norm_ffn_bwd{tag}")
        return dh_in, dh_in_b, dwup, dwdn, dcw, dcb, dfn

    dh3, dh3_b, dwup1, dwdn1, dcw1, dcb1, dfn1 = ffn_bwd(h3, 1, ffn1_saved, dh4, dh4_b, "1")

    dwo = _mm(o_b, dh3_b, "tn", tm=1024, tn=512, name="d_wo")
    do_b = _mm(dh3_b, wo_b, "nt", tm=384, tn=1024, out_dtype=BF16, name="d_attn_out")
    dqh, dkh, dvh = _attn_bwd(qh, kh, vh, heads(do_b), ltot, name="attn_bwd")
    dq_b = unheads(dqh).astype(BF16)
    dkv_b = jnp.concatenate([unheads(dkh), unheads(dvh)], axis=1).astype(BF16)
    dwq = _mm(n1_b, dq_b, "tn", tm=1024, tn=512, name="d_wq")
    dwkv = _mm(kvn_b, dkv_b, "tn", tm=1024, tn=512, name="d_wkv")
    dn1 = _mm(dq_b, wq_b, "nt", tm=384, tn=1024, name="d_n1")
    dkvn = _mm(dkv_b, wkv_b, "nt", tm=384, tn=1024, name="d_kvn")
    dh2, dh2_b, (dmix1, dkvg) = _norm_bwd(h2, r2, [mix1, kvn_g], [dn1, dkvn], dh3, name="norm_h2_bwd")

    dh1, dh1_b, dwup0, dwdn0, dcw0, dcb0, dfn0 = ffn_bwd(h1, 0, ffn0_saved, dh2, dh2_b, "0")

    ddiff, dwp, dscale = _pool_mix_bwd(diff_b, wp_b, scale_f, dh1, name="pool_mix_bwd")
    dn0 = _pool_diff_bwd(ddiff, name="pool_diff_bwd")
    dh0, _, (dmix0,) = _norm_bwd(h0, r0, [mix0], [dn0], dh1, name="norm_h0_bwd")

    grad_x = dh0[N_META:l_real][None]
    dmeta = dh0[:N_META]

    def shard_of(s):
        def cols(t, n):
            return t[..., s * n:(s + 1) * n]

        def rows(t, n):
            return t[s * n:(s + 1) * n]

        def up_cols(halves):
            n = f2 // N_CHIPS
            gate_or_val, first = (halves[0], s * n) if s * n < f else (halves[1], s * n - f)
            return gate_or_val[:, first:first + n]

        pieces = [dwp[:, s * (gd // N_CHIPS):(s + 1) * (gd // N_CHIPS), :], cols(dwkv, 2 * d // N_CHIPS),
                  rows(dwq, d // N_CHIPS), rows(dwo, d // N_CHIPS), up_cols(dwup0), up_cols(dwup1),
                  rows(dwdn0, f // N_CHIPS), rows(dwdn1, f // N_CHIPS)]
        return jnp.concatenate([p.reshape(-1) for p in pieces])

    g_all = jnp.stack([shard_of(s) for s in range(N_CHIPS)]).reshape(N_CHIPS, 2, rows_half, cw)
    g_mine = lax.dynamic_index_in_dim(g_all, core, axis=1, keepdims=False)
    g_other = lax.dynamic_index_in_dim(g_all, 1 - core, axis=1, keepdims=False)
    from_sibling = _to_sibling(g_other.astype(BF16), name="grads_to_sibling")
    chip_part = _add_to_bf16(g_mine, from_sibling, name="grads_chip_sum")
    half = _sum_slots(_scatter_rows(chip_part, name="grads_scatter"), name="grads_sum")
    other_half = _to_sibling(half, name="grads_join")
    g_shard = jnp.stack([jnp.where(core == 0, half, other_half),
                         jnp.where(core == 0, other_half, half)]).reshape(n_big)

    def gbig(i, shape):
        return g_shard[offs[i]:offs[i + 1]].reshape(shape)

    g_pool_w = gbig(0, pool_w.shape)
    g_w_kv = gbig(1, w_kv.shape)
    g_w_q = gbig(2, w_q.shape)
    g_w_o = gbig(3, w_o.shape)
    g_w_up = gbig(4, ffn_w_up.shape)
    g_w_dn = gbig(5, ffn_w_down.shape)

    dcw = jnp.stack([dcw0, dcw1])
    dcb = jnp.concatenate([dcb0, dcb1], axis=0)
    sg_parts = [jnp.concatenate([dmix0, dmix1], axis=0), jnp.concatenate([dfn0, dfn1], axis=0), dkvg, dfin, dcb,
                dmeta, dscale, dcw, loss8]
    sg_sizes = [p.size for p in sg_parts]
    sg = _all_devices(_flat_rows(sg_parts, LANES), True, name="reduce_small").reshape(-1)
    sg_offs = [0]
    for sz in sg_sizes:
        sg_offs.append(sg_offs[-1] + sz)

    def gsmall(i, shape):
        return sg[sg_offs[i]:sg_offs[i + 1]].reshape(shape)

    g_mix = gsmall(0, mix_norm.shape)
    g_ffn_norm = gsmall(1, ffn_norm.shape)
    g_kv_norm = gsmall(2, kv_norm.shape)
    g_final = gsmall(3, final_norm.shape)
    g_conv_b = gsmall(4, ffn_conv_b.shape)
    csh = d // N_CHIPS
    g_meta = lax.dynamic_slice_in_dim(gsmall(5, (N_META, d)), chip * csh, csh, axis=1)
    g_scale = lax.dynamic_slice_in_dim(gsmall(6, (1, d)), chip * csh, csh, axis=1)
    fsh = f2 // N_CHIPS
    g_conv_w = lax.dynamic_slice_in_dim(gsmall(7, (2, 3, f2)), chip * fsh, fsh, axis=2)
    loss = gsmall(8, (8 * LANES,))[0]

    weights = [meta_tokens, mix_norm, ffn_norm, pool_w, pool_scale, kv_norm, w_kv, w_q, w_o, ffn_w_up, ffn_conv_w,
               ffn_conv_b, ffn_w_down, final_norm]
    grads = [g_meta, g_mix, g_ffn_norm, g_pool_w, g_scale, g_kv_norm, g_w_kv, g_w_q, g_w_o, g_w_up, g_conv_w,
             g_conv_b, g_w_dn, g_final]
    ms = [m_meta_tokens, m_mix_norm, m_ffn_norm, m_pool_w, m_pool_scale, m_kv_norm, m_w_kv, m_w_q, m_w_o,
          m_ffn_w_up, m_ffn_conv_w, m_ffn_conv_b, m_ffn_w_down, m_final_norm]
    vs = [v_meta_tokens, v_mix_norm, v_ffn_norm, v_pool_w, v_pool_scale, v_kv_norm, v_w_kv, v_w_q, v_w_o,
          v_ffn_w_up, v_ffn_conv_w, v_ffn_conv_b, v_ffn_w_down, v_final_norm]
    names = ["meta", "mix", "ffnnorm", "poolw", "poolscale", "kvnorm", "wkv", "wq", "wo", "wup", "convw", "convb",
             "wdown", "final"]
    deltas, new_ms, new_vs = [], [], []
    for w, g, m, v, nm in zip(weights, grads, ms, vs, names):
        cols = w.shape[-1]
        if w.size % (8 * LANES) == 0 and w.ndim == 1:
            cols = LANES
        view = (w.size // cols, cols)
        dl, mn, vn = _adamw(w.reshape(view), g.reshape(view), m.reshape(view), v.reshape(view), name=f"adamw_{nm}")
        deltas.append(dl.reshape(w.shape))
        new_ms.append(mn.reshape(w.shape))
        new_vs.append(vn.reshape(w.shape))

    return (loss, grad_x, *grads, *deltas, *new_ms, *new_vs)
```

```python
import jax
import jax.numpy as jnp
from jax import lax
from jax.experimental import pallas as pl
from jax.experimental.pallas import tpu as pltpu

F32 = jnp.float32
BF16 = jnp.bfloat16

N_META = 16
N_HEADS = 16
HEAD_DIM = 64
POOL_WINDOWS = (2, 4, 8, 16)
N_GROUPS = 4
RMS_EPS = 1e-6
ADAM_LR = 0.001
ADAM_B1 = 0.9
ADAM_B2 = 0.999
ADAM_EPS = 1e-08
ADAM_WD = 0.01
ADAM_STEP = 10

LANES = 128
BLK = 128
ATT_CHUNKS = 3
ATT_BQ = ATT_CHUNKS * BLK
CONV_ROWS = 128
N_CHIPS = 4
N_DEV = 8
VMEM_LIMIT = 56 * 1024 * 1024
MESH = pl.DeviceIdType.MESH


def _cparams(*sem):
    return pltpu.CompilerParams(dimension_semantics=tuple(sem) if sem else None,
                                vmem_limit_bytes=VMEM_LIMIT)


def _tile(n, pref):
    best = None
    t = LANES
    while t <= min(n, pref):
        if n % t == 0:
            best = t
        t += LANES
    assert best is not None, (n, pref)
    return best


def _row_tile(r, pref, mult):
    best = r
    for t in range(mult, min(r, pref) + 1, mult):
        if r % t == 0:
            best = t
    return best


_DIMS = {
    "nn": (((1,), (0,)), ((), ())),
    "nt": (((1,), (1,)), ((), ())),
    "tn": (((0,), (0,)), ((), ())),
}


def _mm(a, b, dims, *, tm, tn, name, out_dtype=F32, res=None, out_shards=1):
    if dims == "tn":
        k, m = a.shape
    else:
        m, k = a.shape
    n = b.shape[0] if dims == "nt" else b.shape[1]
    tm = _tile(m, tm)
    tn = _tile(n // out_shards, tn)
    a_spec = pl.BlockSpec((k, tm), lambda i, j: (0, i)) if dims == "tn" else pl.BlockSpec((tm, k), lambda i, j: (i, 0))
    b_spec = pl.BlockSpec((tn, k), lambda i, j: (j, 0)) if dims == "nt" else pl.BlockSpec((k, tn), lambda i, j: (0, j))
    o_spec = pl.BlockSpec((tm, tn), lambda i, j: (i, j))
    out_shape = jax.ShapeDtypeStruct((m, n), out_dtype)
    if out_shards > 1:
        assert res is None
        per = n // out_shards // tn
        o_spec = pl.BlockSpec((None, tm, tn), lambda i, j: (j // per, i, j % per))
        out_shape = jax.ShapeDtypeStruct((out_shards, m, n // out_shards), out_dtype)
    dn = _DIMS[dims]

    def body(*refs):
        if res is None:
            a_ref, b_ref, o_ref = refs
        else:
            a_ref, b_ref, r_ref, o_ref = refs
        acc = lax.dot_general(a_ref[...], b_ref[...], dn, preferred_element_type=F32)
        if res is not None:
            acc = acc + r_ref[...]
        o_ref[...] = acc.astype(out_dtype)

    ins = [a, b] + ([] if res is None else [res])
    specs = [a_spec, b_spec] + ([] if res is None else [o_spec])
    return pl.pallas_call(
        body, name=name, grid=(m // tm, n // tn), in_specs=specs, out_specs=o_spec, out_shape=out_shape,
        compiler_params=_cparams("parallel", "arbitrary"),
    )(*ins)


def _norm_fwd(x, gains, *, name):
    l, d = x.shape
    tr = _tile(l, 384)
    ng = len(gains)

    def body(*refs):
        x_ref = refs[0]
        g_refs = refs[1:1 + ng]
        o_refs = refs[1 + ng:1 + 2 * ng]
        r_ref = refs[1 + 2 * ng]
        xv = x_ref[...]
        r = lax.rsqrt(jnp.mean(xv * xv, axis=-1, keepdims=True) + RMS_EPS)
        xn = xv * r
        for g_ref, o_ref in zip(g_refs, o_refs):
            o_ref[...] = (xn * g_ref[...]).astype(BF16)
        r_ref[...] = r

    row = pl.BlockSpec((tr, d), lambda i: (i, 0))
    gspec = pl.BlockSpec((1, d), lambda i: (0, 0))
    outs = pl.pallas_call(
        body, name=name, grid=(l // tr,),
        in_specs=[row] + [gspec] * ng,
        out_specs=[row] * ng + [pl.BlockSpec((tr, 1), lambda i: (i, 0))],
        out_shape=[jax.ShapeDtypeStruct((l, d), BF16)] * ng + [jax.ShapeDtypeStruct((l, 1), F32)],
        compiler_params=_cparams("parallel"),
    )(x, *gains)
    return list(outs[:ng]), outs[ng]


def _rows8(v):
    r, d = v.shape
    return jnp.sum(v.reshape(r // 8, 8, d), axis=0)


def _norm_bwd(x, r, gains, dns, dres, *, name):
    l, d = x.shape
    tr = _tile(l, 384)
    ng = len(gains)
    nsteps = l // tr

    def body(*refs):
        x_ref, r_ref, dres_ref = refs[0], refs[1], refs[2]
        g_refs = refs[3:3 + ng]
        dn_refs = refs[3 + ng:3 + 2 * ng]
        dx_ref, dxb_ref = refs[3 + 2 * ng], refs[4 + 2 * ng]
        dg_refs = refs[5 + 2 * ng:5 + 3 * ng]
        acc_refs = refs[5 + 3 * ng:5 + 4 * ng]
        i = pl.program_id(0)

        @pl.when(i == 0)
        def _():
            for acc in acc_refs:
                acc[...] = jnp.zeros_like(acc)

        rv = r_ref[...]
        xn = x_ref[...] * rv
        total = dres_ref[...]
        for g_ref, dn_ref, acc in zip(g_refs, dn_refs, acc_refs):
            dn = dn_ref[...]
            acc[...] += _rows8(dn * xn)
            dxn = dn * g_ref[...]
            total = total + rv * (dxn - xn * jnp.mean(dxn * xn, axis=-1, keepdims=True))
        dx_ref[...] = total
        dxb_ref[...] = total.astype(BF16)

        @pl.when(i == nsteps - 1)
        def _():
            for dg_ref, acc in zip(dg_refs, acc_refs):
                dg_ref[...] = jnp.sum(acc[...], axis=0, keepdims=True)

    row = pl.BlockSpec((tr, d), lambda i: (i, 0))
    gspec = pl.BlockSpec((1, d), lambda i: (0, 0))
    outs = pl.pallas_call(
        body, name=name, grid=(nsteps,),
        in_specs=[row, pl.BlockSpec((tr, 1), lambda i: (i, 0)), row] + [gspec] * ng + [row] * ng,
        out_specs=[row, row] + [gspec] * ng,
        out_shape=[jax.ShapeDtypeStruct((l, d), F32), jax.ShapeDtypeStruct((l, d), BF16)]
        + [jax.ShapeDtypeStruct((1, d), F32)] * ng,
        scratch_shapes=[pltpu.VMEM((8, d), F32)] * ng,
        compiler_params=_cparams("arbitrary"),
    )(x, r, dres, *gains, *dns)
    return outs[0], outs[1], list(outs[2:])


def _loss_bwd(h, gain, tgt, seq, *, name):
    l, d = h.shape
    tr = _tile(l, 384)
    nsteps = l // tr

    def body(h_ref, g_ref, t_ref, loss_ref, dh_ref, dhb_ref, dg_ref, lacc, gacc):
        i = pl.program_id(0)

        @pl.when(i == 0)
        def _():
            lacc[...] = jnp.zeros_like(lacc)
            gacc[...] = jnp.zeros_like(gacc)

        xv = h_ref[...]
        g = g_ref[...]
        r = lax.rsqrt(jnp.mean(xv * xv, axis=-1, keepdims=True) + RMS_EPS)
        xn = xv * r
        rows = i * tr + lax.broadcasted_iota(jnp.int32, (tr, 1), 0)
        valid = (rows >= N_META) & (rows < N_META + seq)
        e = jnp.where(valid, xn * g - t_ref[...], 0.0)
        lacc[...] += _rows8(e * e)
        dy = e * (1.0 / d)
        gacc[...] += _rows8(dy * xn)
        dxn = dy * g
        dx = r * (dxn - xn * jnp.mean(dxn * xn, axis=-1, keepdims=True))
        dh_ref[...] = dx
        dhb_ref[...] = dx.astype(BF16)

        @pl.when(i == nsteps - 1)
        def _():
            loss_ref[...] = jnp.full((8, LANES), 0.5 / d * jnp.sum(lacc[...]), F32)
            dg_ref[...] = jnp.sum(gacc[...], axis=0, keepdims=True)

    row = pl.BlockSpec((tr, d), lambda i: (i, 0))
    gspec = pl.BlockSpec((1, d), lambda i: (0, 0))
    return pl.pallas_call(
        body, name=name, grid=(nsteps,),
        in_specs=[row, gspec, row],
        out_specs=[pl.BlockSpec((8, LANES), lambda i: (0, 0)), row, row, gspec],
        out_shape=[jax.ShapeDtypeStruct((8, LANES), F32), jax.ShapeDtypeStruct((l, d), F32),
                   jax.ShapeDtypeStruct((l, d), BF16), jax.ShapeDtypeStruct((1, d), F32)],
        scratch_shapes=[pltpu.VMEM((8, d), F32), pltpu.VMEM((8, d), F32)],
        compiler_params=_cparams("arbitrary"),
    )(h, gain, tgt)


def _shift_down(v, k, rows):
    return jnp.where(rows >= k, pltpu.roll(v, k, 0), 0.0)


def _shift_up(v, k, rows):
    l = v.shape[0]
    return jnp.where(rows < l - k, pltpu.roll(v, l - k, 0), 0.0)


def _pool_diff(n, w, rows):
    s = n
    for k in (1, 2, 4, 8):
        s = s + jnp.where(k < w, _shift_down(s, k, rows), 0.0)
    cnt = jnp.minimum(rows + 1, w).astype(F32)
    return s / cnt - n, cnt


def _pool_diff_fwd(h, r, gain, *, name):
    l, d = h.shape
    per_group = d // N_GROUPS // LANES

    def body(h_ref, r_ref, g_ref, o_ref):
        w = jnp.left_shift(2, pl.program_id(0) // per_group)
        rows = lax.broadcasted_iota(jnp.int32, (l, 1), 0)
        n = h_ref[...] * r_ref[...] * g_ref[...]
        diff, _ = _pool_diff(n, w, rows)
        o_ref[...] = diff.astype(BF16)

    col = pl.BlockSpec((l, LANES), lambda j: (0, j))
    return pl.pallas_call(
        body, name=name, grid=(d // LANES,),
        in_specs=[col, pl.BlockSpec((l, 1), lambda j: (0, 0)), pl.BlockSpec((1, LANES), lambda j: (0, j))],
        out_specs=col, out_shape=jax.ShapeDtypeStruct((l, d), BF16),
        compiler_params=_cparams("parallel"),
    )(h, r, gain)


def _pool_diff_bwd(ddiff, *, name):
    l, d = ddiff.shape
    per_group = d // N_GROUPS // LANES

    def body(dd_ref, o_ref):
        w = jnp.left_shift(2, pl.program_id(0) // per_group)
        rows = lax.broadcasted_iota(jnp.int32, (l, 1), 0)
        dd = dd_ref[...]
        s = dd / jnp.minimum(rows + 1, w).astype(F32)
        for k in (1, 2, 4, 8):
            s = s + jnp.where(k < w, _shift_up(s, k, rows), 0.0)
        o_ref[...] = s - dd

    col = pl.BlockSpec((l, LANES), lambda j: (0, j))
    return pl.pallas_call(
        body, name=name, grid=(d // LANES,), in_specs=[col], out_specs=col,
        out_shape=jax.ShapeDtypeStruct((l, d), F32), compiler_params=_cparams("parallel"),
    )(ddiff)


def _pool_mix_fwd(h, diff_b, w_pool, scale, *, name):
    l, d = h.shape
    gd = d // N_GROUPS
    tr = _tile(l, 1408)

    def body(h_ref, d_ref, w_ref, s_ref, o_ref):
        y = jnp.dot(d_ref[...], w_ref[0], preferred_element_type=F32)
        o_ref[...] = h_ref[...] + y * s_ref[...]

    blk = pl.BlockSpec((tr, gd), lambda i, g: (i, g))
    vec = pl.BlockSpec((1, gd), lambda i, g: (0, g))
    return pl.pallas_call(
        body, name=name, grid=(l // tr, N_GROUPS),
        in_specs=[blk, blk, pl.BlockSpec((1, gd, gd), lambda i, g: (g, 0, 0)), vec],
        out_specs=blk, out_shape=jax.ShapeDtypeStruct((l, d), F32),
        compiler_params=_cparams("parallel", "parallel"),
    )(h, diff_b, w_pool, scale)


def _pool_mix_bwd(diff_b, w_pool, scale, dh1, *, name):
    l, d = dh1.shape
    gd = d // N_GROUPS
    tr = _tile(l, 1408)
    nsteps = l // tr

    def body(d_ref, w_ref, s_ref, dy_ref, dd_ref, dw_ref, ds_ref, sacc):
        i = pl.program_id(1)

        @pl.when(i == 0)
        def _():
            dw_ref[...] = jnp.zeros_like(dw_ref)
            sacc[...] = jnp.zeros_like(sacc)

        diff_b_, wg, dy = d_ref[...], w_ref[0], dy_ref[...]
        yy = jnp.dot(diff_b_, wg, preferred_element_type=F32)
        sacc[...] += _rows8(dy * yy)
        dyy_b = (dy * s_ref[...]).astype(BF16)
        dw_ref[0] += lax.dot_general(diff_b_, dyy_b, _DIMS["tn"], preferred_element_type=F32)
        dd_ref[...] = lax.dot_general(dyy_b, wg, _DIMS["nt"], preferred_element_type=F32)

        @pl.when(i == nsteps - 1)
        def _():
            ds_ref[...] = jnp.sum(sacc[...], axis=0, keepdims=True)

    blk = pl.BlockSpec((tr, gd), lambda g, i: (i, g))
    vec = pl.BlockSpec((1, gd), lambda g, i: (0, g))
    wspec = pl.BlockSpec((1, gd, gd), lambda g, i: (g, 0, 0))
    return pl.pallas_call(
        body, name=name, grid=(N_GROUPS, nsteps),
        in_specs=[blk, wspec, vec, blk],
        out_specs=[blk, wspec, vec],
        out_shape=[jax.ShapeDtypeStruct((l, d), F32), jax.ShapeDtypeStruct((N_GROUPS, gd, gd), F32),
                   jax.ShapeDtypeStruct((1, d), F32)],
        scratch_shapes=[pltpu.VMEM((8, gd), F32)],
        compiler_params=_cparams("parallel", "arbitrary"),
    )(diff_b, w_pool, scale, dh1)


def _conv_chunk(cur, prev, w, b, rowi):
    s1 = jnp.where(rowi < 1, pltpu.roll(prev, 1, 0), pltpu.roll(cur, 1, 0))
    s2 = jnp.where(rowi < 2, pltpu.roll(prev, 2, 0), pltpu.roll(cur, 2, 0))
    return b + w[0] * s2 + w[1] * s1 + w[2] * cur, s1, s2


def _bcast_rows(ref, rows):
    v = ref[...]
    return [jnp.broadcast_to(v[k:k + 1], (rows, v.shape[1])) for k in range(v.shape[0])]


def _convgate_fwd(u, cw, cb, *, name):
    l, f2 = u.shape
    f = f2 // 2
    tc = _tile(f, LANES)
    nc = f // tc
    rows = CONV_ROWS
    assert l % rows == 0

    def body(ug_ref, uv_ref, wg_ref, wv_ref, bg_ref, bv_ref, a_ref):
        rowi = lax.broadcasted_iota(jnp.int32, (rows, tc), 0)
        wg, wv = _bcast_rows(wg_ref, rows), _bcast_rows(wv_ref, rows)
        bg, bv = _bcast_rows(bg_ref, rows)[0], _bcast_rows(bv_ref, rows)[0]

        def chunk(i, carry):
            pg, pv = carry
            r = pl.multiple_of(i * rows, rows)
            cg, cv = ug_ref[pl.ds(r, rows), :], uv_ref[pl.ds(r, rows), :]
            gate, _, _ = _conv_chunk(cg, pg, wg, bg, rowi)
            val, _, _ = _conv_chunk(cv, pv, wv, bv, rowi)
            a_ref[pl.ds(r, rows), :] = (gate * jax.nn.sigmoid(gate) * val).astype(BF16)
            return cg, cv

        zero = jnp.zeros((rows, tc), F32)
        lax.fori_loop(0, l // rows, chunk, (zero, zero))

    def spec(rows_, off):
        return pl.BlockSpec((rows_, tc), lambda j: (0, j + off))

    return pl.pallas_call(
        body, name=name, grid=(nc,),
        in_specs=[spec(l, 0), spec(l, nc), spec(3, 0), spec(3, nc), spec(1, 0), spec(1, nc)],
        out_specs=spec(l, 0), out_shape=jax.ShapeDtypeStruct((l, f), BF16),
        compiler_params=_cparams("parallel"),
    )(u, u, cw, cw, cb, cb)


def _convgate_bwd(u, cw, cb, da, *, name):
    l, f2 = u.shape
    f = f2 // 2
    tc = _tile(f, LANES)
    nc = f // tc
    rows = CONV_ROWS
    assert l % rows == 0
    nchunks = l // rows

    def body(ug_ref, uv_ref, wg_ref, wv_ref, bg_ref, bv_ref, da_ref,
             dug_ref, duv_ref, dcwg_ref, dcwv_ref, dcbg_ref, dcbv_ref):
        rowi = lax.broadcasted_iota(jnp.int32, (rows, tc), 0)
        wg, wv = _bcast_rows(wg_ref, rows), _bcast_rows(wv_ref, rows)
        bg, bv = _bcast_rows(bg_ref, rows)[0], _bcast_rows(bv_ref, rows)[0]

        def tap_sums(acc, dc, taps):
            return (acc[0] + _rows8(dc),) + tuple(a + _rows8(dc * t) for a, t in zip(acc[1:], taps))

        def input_grad(dc, dc_next, w):
            up1 = jnp.where(rowi >= rows - 1, pltpu.roll(dc_next, rows - 1, 0), pltpu.roll(dc, rows - 1, 0))
            up2 = jnp.where(rowi >= rows - 2, pltpu.roll(dc_next, rows - 2, 0), pltpu.roll(dc, rows - 2, 0))
            return (w[2] * dc + w[1] * up1 + w[0] * up2).astype(BF16)

        def chunk(ii, carry):
            dcg_next, dcv_next, acc_g, acc_v = carry
            i = nchunks - 1 - ii
            r = pl.multiple_of(i * rows, rows)
            rp = pl.multiple_of(jnp.maximum(i - 1, 0) * rows, rows)
            cg, cv = ug_ref[pl.ds(r, rows), :], uv_ref[pl.ds(r, rows), :]
            pg = jnp.where(i > 0, ug_ref[pl.ds(rp, rows), :], 0.0)
            pv = jnp.where(i > 0, uv_ref[pl.ds(rp, rows), :], 0.0)
            gate, g1, g2 = _conv_chunk(cg, pg, wg, bg, rowi)
            val, v1, v2 = _conv_chunk(cv, pv, wv, bv, rowi)
            sg = jax.nn.sigmoid(gate)
            dav = da_ref[pl.ds(r, rows), :]
            dcg = dav * val * (sg * (1.0 + gate * (1.0 - sg)))
            dcv = dav * (gate * sg)
            acc_g = tap_sums(acc_g, dcg, (g2, g1, cg))
            acc_v = tap_sums(acc_v, dcv, (v2, v1, cv))
            dug_ref[pl.ds(r, rows), :] = input_grad(dcg, dcg_next, wg)
            duv_ref[pl.ds(r, rows), :] = input_grad(dcv, dcv_next, wv)
            return dcg, dcv, acc_g, acc_v

        zero = jnp.zeros((rows, tc), F32)
        zero8 = (jnp.zeros((8, tc), F32),) * 4
        _, _, acc_g, acc_v = lax.fori_loop(0, nchunks, chunk, (zero, zero, zero8, zero8))
        for acc, dcw_ref, dcb_ref in ((acc_g, dcwg_ref, dcbg_ref), (acc_v, dcwv_ref, dcbv_ref)):
            dcb_ref[...] = jnp.sum(acc[0], axis=0, keepdims=True)
            for k in range(3):
                dcw_ref[k:k + 1, :] = jnp.sum(acc[1 + k], axis=0, keepdims=True)

    def spec(rows_, off):
        return pl.BlockSpec((rows_, tc), lambda j: (0, j + off))

    return pl.pallas_call(
        body, name=name, grid=(nc,),
        in_specs=[spec(l, 0), spec(l, nc), spec(3, 0), spec(3, nc), spec(1, 0), spec(1, nc), spec(l, 0)],
        out_specs=[spec(l, 0), spec(l, 0), spec(3, 0), spec(3, 0), spec(1, 0), spec(1, 0)],
        out_shape=[jax.ShapeDtypeStruct((l, f), BF16)] * 2 + [jax.ShapeDtypeStruct((3, f), F32)] * 2
        + [jax.ShapeDtypeStruct((1, f), F32)] * 2,
        compiler_params=_cparams("parallel"),
    )(u, u, cw, cw, cb, cb, da)


def _cumsum_mm(v_b, t2):
    return jnp.dot(v_b, t2, preferred_element_type=F32)


def _tri_and_ones(tri_fn):
    row = lax.broadcasted_iota(jnp.int32, (BLK, 2 * BLK), 0)
    col = lax.broadcasted_iota(jnp.int32, (BLK, 2 * BLK), 1)
    return jnp.where((col >= BLK) | tri_fn(row, col), 1.0, 0.0).astype(BF16)


def _logits(z, mask):
    sp = jnp.log(1.0 + jnp.exp(-jnp.abs(z)))
    lb = jnp.minimum(z, 0.0) - sp
    lm = lb - z
    if mask is not None:
        lm = jnp.where(mask, lm, 0.0)
    return lb, lm


def _software_pipeline(stages, n, block_of, state):
    ns = len(stages)
    inflight = [None] * (ns - 1)
    for t in range(ns - 1):
        new = list(inflight)
        for s in range(t, -1, -1):
            y, state = stages[s](block_of(t - s), None if s == 0 else inflight[s - 1], state)
            new[s] = y
        inflight = new

    def steady(i, carry):
        inflight, state = carry
        new = [None] * (ns - 1)
        for s in range(ns - 1, -1, -1):
            y, state = stages[s](block_of(i + ns - 1 - s), None if s == 0 else inflight[s - 1], state)
            if s < ns - 1:
                new[s] = y
        return tuple(new), state

    inflight, state = lax.fori_loop(0, n - (ns - 1), steady, (tuple(inflight), state))
    inflight = list(inflight)
    for e in range(1, ns):
        new = list(inflight)
        for s in range(ns - 1, e - 1, -1):
            y, state = stages[s](block_of(n - 1 + e - s), inflight[s - 1], state)
            if s < ns - 1:
                new[s] = y
        inflight = new
    return state


def _attn_fwd(q, k, v, *, name):
    nh, l, dh = q.shape
    assert l % ATT_BQ == 0
    nq = l // ATT_BQ
    qscale = HEAD_DIM ** -0.5
    chunks = range(ATT_CHUNKS)

    def body(q_ref, k_ref, v_ref, o_ref, lt_ref, z_scr, cs_scr, pv_scr):
        t_later = _tri_and_ones(lambda r, c: r > c)
        dmask = (lax.broadcasted_iota(jnp.int32, (BLK, BLK), 1)
                 < lax.broadcasted_iota(jnp.int32, (BLK, BLK), 0))

        def logits(qc, c0, mask):
            z = lax.dot_general(qc, k_ref[0, pl.ds(c0, BLK), :], _DIMS["nt"], preferred_element_type=F32)
            lb, lm = _logits(z, mask)
            return lb, lm.astype(BF16)

        def weights(lb, lm_b, mask, run):
            cs = _cumsum_mm(lm_b, t_later)
            a = jnp.exp(lb + cs[:, :BLK] + run)
            if mask is not None:
                a = jnp.where(mask, a, 0.0)
            return a.astype(BF16), run + cs[:, BLK:]

        def accumulate(a_b, c0, acc):
            return acc + jnp.dot(a_b, v_ref[0, pl.ds(c0, BLK), :], preferred_element_type=F32)

        def qblock(qb, _):
            r0 = pl.multiple_of(qb * ATT_BQ, ATT_BQ)
            qs = [q_ref[0, pl.ds(r0 + rc * BLK, BLK), :] * jnp.asarray(qscale, BF16) for rc in chunks]
            accs = [jnp.zeros((BLK, dh), F32)] * ATT_CHUNKS
            runs = [jnp.zeros((BLK, BLK), F32)] * ATT_CHUNKS
            for dj in range(ATT_CHUNKS - 1, -1, -1):
                c0 = r0 + dj * BLK
                for rc in range(dj, ATT_CHUNKS):
                    mask = dmask if rc == dj else None
                    a_b, runs[rc] = weights(*logits(qs[rc], c0, mask), mask, runs[rc])
                    accs[rc] = accumulate(a_b, c0, accs[rc])

            def col0(b):
                return pl.multiple_of(r0 - (b + 1) * BLK, BLK)

            def stage_scores(b, _, state):
                for rc in chunks:
                    z_scr[rc] = lax.dot_general(qs[rc], k_ref[0, pl.ds(col0(b), BLK), :], _DIMS["nt"],
                                                preferred_element_type=F32)
                return (), state

            def stage_cumsum(b, _, state):
                lbs = []
                for rc in chunks:
                    lb, lm = _logits(z_scr[rc], None)
                    cs_scr[rc] = _cumsum_mm(lm.astype(BF16), t_later)
                    lbs.append(lb)
                return tuple(lbs), state

            def stage_weights(b, lbs, state):
                accs, runs = state
                new_runs = []
                for rc in chunks:
                    cs = cs_scr[rc]
                    a = jnp.exp(lbs[rc] + cs[:, :BLK] + runs[rc])
                    pv_scr[rc] = jnp.dot(a.astype(BF16), v_ref[0, pl.ds(col0(b), BLK), :],
                                         preferred_element_type=F32)
                    new_runs.append(runs[rc] + cs[:, BLK:])
                return (), (accs, tuple(new_runs))

            def stage_acc(b, _, state):
                accs, runs = state
                return None, (tuple(accs[rc] + pv_scr[rc] for rc in chunks), runs)

            def left_region(state):
                return _software_pipeline([stage_scores, stage_cumsum, stage_weights, stage_acc],
                                          qb * ATT_CHUNKS, lambda b: b, state)

            accs, runs = lax.cond(qb > 0, left_region, lambda s: s, (tuple(accs), tuple(runs)))
            for rc in chunks:
                o_ref[0, pl.ds(r0 + rc * BLK, BLK), :] = accs[rc].astype(BF16)
                lt_ref[0, pl.ds(r0 + rc * BLK, BLK), :] = runs[rc]
            return 0

        lax.fori_loop(0, nq, qblock, 0)

    head = pl.BlockSpec((1, l, dh), lambda h: (h, 0, 0))
    return pl.pallas_call(
        body, name=name, grid=(nh,),
        in_specs=[head, head, head],
        out_specs=[head, pl.BlockSpec((1, l, BLK), lambda h: (h, 0, 0))],
        out_shape=[jax.ShapeDtypeStruct((nh, l, dh), BF16), jax.ShapeDtypeStruct((nh, l, BLK), F32)],
        scratch_shapes=[pltpu.VMEM((ATT_CHUNKS, BLK, BLK), F32), pltpu.VMEM((ATT_CHUNKS, BLK, 2 * BLK), F32),
                        pltpu.VMEM((ATT_CHUNKS, BLK, dh), F32)],
        compiler_params=_cparams("parallel"),
    )(q, k, v)


def _attn_bwd(q, k, v, do, ltot, *, name):
    nh, l, dh = q.shape
    assert l % ATT_BQ == 0
    nq = l // ATT_BQ
    qscale = HEAD_DIM ** -0.5
    chunks = range(ATT_CHUNKS)

    def body(q_ref, k_ref, v_ref, do_ref, lt_ref, dq_ref, dk_ref, dv_ref,
             z_scr, cs_scr, da_scr, cd_scr, dqp_scr, dkp_scr, dvp_scr):
        t_incl = _tri_and_ones(lambda r, c: r <= c)
        t_excl = _tri_and_ones(lambda r, c: r < c)
        dmask = (lax.broadcasted_iota(jnp.int32, (BLK, BLK), 1)
                 < lax.broadcasted_iota(jnp.int32, (BLK, BLK), 0))
        dk_ref[...] = jnp.zeros_like(dk_ref)
        dv_ref[...] = jnp.zeros_like(dv_ref)

        def logits(qc, c0, mask):
            z = lax.dot_general(qc, k_ref[0, pl.ds(c0, BLK), :], _DIMS["nt"], preferred_element_type=F32)
            lb, lm = _logits(z, mask)
            return lb, lm.astype(BF16)

        def weights(lb, lm_b, doc, ltc, c0, mask, pre_lm):
            cs = _cumsum_mm(lm_b, t_incl)
            da = lax.dot_general(doc, v_ref[0, pl.ds(c0, BLK), :], _DIMS["nt"], preferred_element_type=F32)
            a = jnp.exp(lb + (ltc - pre_lm - cs[:, :BLK]))
            if mask is not None:
                a = jnp.where(mask, a, 0.0)
            dl = a * da
            return (jnp.exp(lb), dl, dl.astype(BF16), a.astype(BF16)), pre_lm + cs[:, BLK:]

        def logit_grad(beta, dl, dl_b, mask, pre_dl):
            cd = _cumsum_mm(dl_b, t_excl)
            dz = dl - beta * (dl + pre_dl + cd[:, :BLK])
            if mask is not None:
                dz = jnp.where(mask, dz, 0.0)
            return (dz * qscale).astype(BF16), pre_dl + cd[:, BLK:]

        def key_grads(c0, dz_parts, a_parts, q_rows, do_rows):
            dz_all = dz_parts[0] if len(dz_parts) == 1 else jnp.concatenate(dz_parts, axis=0)
            a_all = a_parts[0] if len(a_parts) == 1 else jnp.concatenate(a_parts, axis=0)
            dk_ref[0, pl.ds(c0, BLK), :] += lax.dot_general(dz_all, q_rows, _DIMS["tn"], preferred_element_type=F32)
            dv_ref[0, pl.ds(c0, BLK), :] += lax.dot_general(a_all, do_rows, _DIMS["tn"], preferred_element_type=F32)

        def qblock(qb, _):
            r0 = pl.multiple_of(qb * ATT_BQ, ATT_BQ)
            q_raw = q_ref[0, pl.ds(r0, ATT_BQ), :]
            do_all = do_ref[0, pl.ds(r0, ATT_BQ), :]
            qs = [q_raw[rc * BLK:(rc + 1) * BLK] * jnp.asarray(qscale, BF16) for rc in chunks]
            dos = [do_all[rc * BLK:(rc + 1) * BLK] for rc in chunks]
            lts = [lt_ref[0, pl.ds(r0 + rc * BLK, BLK), :] for rc in chunks]

            def col0(b):
                return pl.multiple_of(b * BLK, BLK)

            def stage_scores(b, _, state):
                kj = k_ref[0, pl.ds(col0(b), BLK), :]
                for rc in chunks:
                    z_scr[rc] = lax.dot_general(qs[rc], kj, _DIMS["nt"], preferred_element_type=F32)
                return (), state

            def stage_prefix(b, _, state):
                vj = v_ref[0, pl.ds(col0(b), BLK), :]
                lbs = []
                for rc in chunks:
                    lb, lm = _logits(z_scr[rc], None)
                    cs_scr[rc] = _cumsum_mm(lm.astype(BF16), t_incl)
                    da_scr[rc] = lax.dot_general(dos[rc], vj, _DIMS["nt"], preferred_element_type=F32)
                    lbs.append(lb)
                return tuple(lbs), state

            def stage_weights(b, lbs, state):
                dqs, pls, pds = state
                out, new_pls = [], []
                for rc in chunks:
                    cs = cs_scr[rc]
                    a = jnp.exp(lbs[rc] + (lts[rc] - pls[rc] - cs[:, :BLK]))
                    dl = a * da_scr[rc]
                    cd_scr[rc] = _cumsum_mm(dl.astype(BF16), t_excl)
                    out.append((jnp.exp(lbs[rc]), dl, a.astype(BF16)))
                    new_pls.append(pls[rc] + cs[:, BLK:])
                return tuple(out), (dqs, tuple(new_pls), pds)

            def stage_products(b, ys, state):
                dqs, pls, pds = state
                c0 = col0(b)
                kj = k_ref[0, pl.ds(c0, BLK), :]
                dz_bs, new_pds = [], []
                for rc in chunks:
                    beta, dl, _ = ys[rc]
                    cd = cd_scr[rc]
                    dz_b = ((dl - beta * (dl + pds[rc] + cd[:, :BLK])) * qscale).astype(BF16)
                    dqp_scr[rc] = jnp.dot(dz_b, kj, preferred_element_type=F32)
                    dz_bs.append(dz_b)
                    new_pds.append(pds[rc] + cd[:, BLK:])
                dkp_scr[...] = lax.dot_general(jnp.concatenate(dz_bs, axis=0), q_raw, _DIMS["tn"],
                                               preferred_element_type=F32)
                dvp_scr[...] = lax.dot_general(jnp.concatenate([y[2] for y in ys], axis=0), do_all, _DIMS["tn"],
                                               preferred_element_type=F32)
                return (), (dqs, pls, tuple(new_pds))

            def stage_acc(b, _, state):
                dqs, pls, pds = state
                c0 = col0(b)
                dk_ref[0, pl.ds(c0, BLK), :] += dkp_scr[...]
                dv_ref[0, pl.ds(c0, BLK), :] += dvp_scr[...]
                return None, (tuple(dqs[rc] + dqp_scr[rc] for rc in chunks), pls, pds)

            stages = [stage_scores, stage_prefix, stage_weights, stage_products, stage_acc]
            zero = jnp.zeros((BLK, BLK), F32)
            state = ((jnp.zeros((BLK, dh), F32),) * ATT_CHUNKS, (zero,) * ATT_CHUNKS, (zero,) * ATT_CHUNKS)

            def pipelined(state):
                return _software_pipeline(stages, qb * ATT_CHUNKS, lambda b: b, state)

            def one_by_one(state):
                def block(b, state):
                    x = None
                    for stage in stages:
                        x, state = stage(b, x, state)
                    return state
                return lax.fori_loop(0, qb * ATT_CHUNKS, block, state)

            dqs, pls, pds = lax.cond(qb * ATT_CHUNKS >= len(stages) - 1, pipelined, one_by_one, state)
            dqs, pls, pds = list(dqs), list(pls), list(pds)
            for dj in chunks:
                c0 = r0 + dj * BLK
                kj = k_ref[0, pl.ds(c0, BLK), :]
                dz_parts, a_parts = [], []
                for rc in range(dj, ATT_CHUNKS):
                    mask = dmask if rc == dj else None
                    ys, pls[rc] = weights(*logits(qs[rc], c0, mask), dos[rc], lts[rc], c0, mask, pls[rc])
                    dz_b, pds[rc] = logit_grad(*ys[:3], mask, pds[rc])
                    a_b = ys[3]
                    dqs[rc] = dqs[rc] + jnp.dot(dz_b, kj, preferred_element_type=F32)
                    dz_parts.append(dz_b)
                    a_parts.append(a_b)
                key_grads(c0, dz_parts, a_parts, q_raw[dj * BLK:], do_all[dj * BLK:])
            for rc in chunks:
                dq_ref[0, pl.ds(r0 + rc * BLK, BLK), :] = dqs[rc]
            return 0

        lax.fori_loop(0, nq, qblock, 0)

    head = pl.BlockSpec((1, l, dh), lambda h: (h, 0, 0))
    sq = pltpu.VMEM((ATT_CHUNKS, BLK, BLK), F32)
    sw = pltpu.VMEM((ATT_CHUNKS, BLK, 2 * BLK), F32)
    return pl.pallas_call(
        body, name=name, grid=(nh,),
        in_specs=[head, head, head, head, pl.BlockSpec((1, l, BLK), lambda h: (h, 0, 0))],
        out_specs=[head, head, head],
        out_shape=[jax.ShapeDtypeStruct((nh, l, dh), F32)] * 3,
        scratch_shapes=[sq, sw, sq, sw, pltpu.VMEM((ATT_CHUNKS, BLK, dh), F32),
                        pltpu.VMEM((BLK, dh), F32), pltpu.VMEM((BLK, dh), F32)],
        compiler_params=_cparams("parallel"),
    )(q, k, v, do, ltot)


def _add_to_bf16(a, b, *, name):
    n, r, c = a.shape
    tr = _row_tile(r, 1536, 16)
    blk = pl.BlockSpec((1, tr, c), lambda i, j: (i, j, 0))

    def body(a_ref, b_ref, o_ref):
        o_ref[...] = (a_ref[...] + b_ref[...].astype(F32)).astype(BF16)

    return pl.pallas_call(body, name=name, grid=(n, r // tr), in_specs=[blk, blk], out_specs=blk,
                          out_shape=jax.ShapeDtypeStruct(a.shape, BF16),
                          compiler_params=_cparams("parallel", "parallel"))(a, b)


def _sum_slots(p, *, name):
    n, r, c = p.shape
    tr = _row_tile(r, 1536, 16)

    def body(p_ref, o_ref):
        acc = p_ref[0].astype(F32)
        for s in range(1, n):
            acc = acc + p_ref[s].astype(F32)
        o_ref[...] = acc

    return pl.pallas_call(body, name=name, grid=(r // tr,),
                          in_specs=[pl.BlockSpec((n, tr, c), lambda j: (0, j, 0))],
                          out_specs=pl.BlockSpec((tr, c), lambda j: (j, 0)),
                          out_shape=jax.ShapeDtypeStruct((r, c), F32),
                          compiler_params=_cparams("parallel"))(p)


def _adamw(w, g, m, v, *, name):
    r, c = w.shape
    tr = _row_tile(r, 512, 8)
    blk = pl.BlockSpec((tr, c), lambda i: (i, 0))

    def body(w_ref, g_ref, m_ref, v_ref, d_ref, mo_ref, vo_ref):
        gv = g_ref[...]
        mn = ADAM_B1 * m_ref[...] + (1.0 - ADAM_B1) * gv
        vn = ADAM_B2 * v_ref[...] + (1.0 - ADAM_B2) * (gv * gv)
        m_hat = mn / (1.0 - ADAM_B1 ** ADAM_STEP)
        v_hat = vn / (1.0 - ADAM_B2 ** ADAM_STEP)
        d_ref[...] = -ADAM_LR * (m_hat / (jnp.sqrt(v_hat) + ADAM_EPS) + ADAM_WD * w_ref[...])
        mo_ref[...] = mn
        vo_ref[...] = vn

    return pl.pallas_call(body, name=name, grid=(r // tr,), in_specs=[blk] * 4, out_specs=[blk] * 3,
                          out_shape=[jax.ShapeDtypeStruct((r, c), F32)] * 3,
                          compiler_params=_cparams("parallel"))(w, g, m, v)


_ANY = pl.BlockSpec(memory_space=pl.ANY)


def _place():
    x, y, c = lax.axis_index("x"), lax.axis_index("y"), lax.axis_index("c")
    chips = [(1 - x, y), (x, 1 - y), (1 - x, 1 - y)]
    return x, y, c, chips


def _gather_shards(wsh, *, name):
    _, r, cdim = wsh.shape

    def body(w_ref, out_ref, stage_ref, send_sems, recv_sems):
        x, y, c, chips = _place()
        s = 2 * x + y
        sibling = (x, y, 1 - c)

        def copy(k, src, dst, to):
            return pltpu.make_async_remote_copy(src_ref=src, dst_ref=dst, send_sem=send_sems.at[k],
                                                recv_sem=recv_sems.at[k], device_id=to, device_id_type=MESH)

        first = [copy(j, w_ref.at[c], out_ref.at[s, c], (*chip, c)) for j, chip in enumerate(chips)]
        for cp in first:
            cp.start()
        passed = []
        for j, (px, py) in enumerate(chips):
            sp = 2 * px + py
            copy(j, w_ref.at[c], out_ref.at[sp, c], sibling).wait_recv()
            fwd = copy(3 + j, out_ref.at[sp, c], out_ref.at[sp, c], sibling)
            fwd.start()
            passed.append(fwd)
        for h in range(2):
            pltpu.sync_copy(w_ref.at[h], stage_ref)
            pltpu.sync_copy(stage_ref, out_ref.at[s, h])
        for j, (px, py) in enumerate(chips):
            sp = 2 * px + py
            copy(3 + j, w_ref.at[c], out_ref.at[sp, 1 - c], sibling).wait_recv()
        for cp in first + passed:
            cp.wait_send()

    return pl.pallas_call(
        body, name=name, in_specs=[_ANY], out_specs=_ANY,
        out_shape=jax.ShapeDtypeStruct((N_CHIPS, 2, r, cdim), wsh.dtype),
        scratch_shapes=[pltpu.VMEM((r, cdim), wsh.dtype), pltpu.SemaphoreType.DMA((6,)),
                        pltpu.SemaphoreType.DMA((6,))],
        compiler_params=pltpu.CompilerParams(vmem_limit_bytes=VMEM_LIMIT),
    )(wsh)


def _to_sibling(a, *, name):
    def body(a_ref, out_ref, send_sem, recv_sem):
        x, y, c, _ = _place()
        cp = pltpu.make_async_remote_copy(src_ref=a_ref, dst_ref=out_ref, send_sem=send_sem, recv_sem=recv_sem,
                                          device_id=(x, y, 1 - c), device_id_type=MESH)
        cp.start()
        cp.wait()

    return pl.pallas_call(
        body, name=name, in_specs=[_ANY], out_specs=_ANY,
        out_shape=jax.ShapeDtypeStruct(a.shape, a.dtype),
        scratch_shapes=[pltpu.SemaphoreType.DMA, pltpu.SemaphoreType.DMA],
    )(a)


def _scatter_rows(p, *, name):
    def body(p_ref, out_ref, stage_ref, send_sems, recv_sems):
        x, y, c, chips = _place()
        s = 2 * x + y
        sends = []
        for j, (px, py) in enumerate(chips):
            sp = 2 * px + py
            cp = pltpu.make_async_remote_copy(src_ref=p_ref.at[sp], dst_ref=out_ref.at[s], send_sem=send_sems.at[j],
                                              recv_sem=recv_sems.at[j], device_id=(px, py, c), device_id_type=MESH)
            cp.start()
            sends.append(cp)
        pltpu.sync_copy(p_ref.at[s], stage_ref)
        pltpu.sync_copy(stage_ref, out_ref.at[s])
        for j, (px, py) in enumerate(chips):
            sp = 2 * px + py
            pltpu.make_async_remote_copy(src_ref=p_ref.at[sp], dst_ref=out_ref.at[sp], send_sem=send_sems.at[j],
                                         recv_sem=recv_sems.at[j], device_id=(px, py, c),
                                         device_id_type=MESH).wait_recv()
        for cp in sends:
            cp.wait_send()

    return pl.pallas_call(
        body, name=name, in_specs=[_ANY], out_specs=_ANY,
        out_shape=jax.ShapeDtypeStruct(p.shape, p.dtype),
        scratch_shapes=[pltpu.VMEM(p.shape[1:], p.dtype), pltpu.SemaphoreType.DMA((3,)),
                        pltpu.SemaphoreType.DMA((3,))],
        compiler_params=pltpu.CompilerParams(vmem_limit_bytes=VMEM_LIMIT),
    )(p)


def _all_devices(a, reduce, *, name):
    r, cdim = a.shape

    def body(a_ref, out_ref, *scratch):
        if reduce:
            buf, send_sems, recv_sems = scratch
        else:
            buf = out_ref
            send_sems, recv_sems = scratch
        x, y, c, _ = _place()
        me = 4 * x + 2 * y + c
        buf[me] = a_ref[...]
        peers = []
        for k in range(1, N_DEV):
            dx, dy, dc = (k >> 2) & 1, (k >> 1) & 1, k & 1
            peers.append((x ^ dx, y ^ dy, c ^ dc))
        sends = []
        for k, peer in enumerate(peers):
            cp = pltpu.make_async_remote_copy(src_ref=a_ref, dst_ref=buf.at[me], send_sem=send_sems.at[k],
                                              recv_sem=recv_sems.at[k], device_id=peer, device_id_type=MESH)
            cp.start()
            sends.append(cp)
        for k, (px, py, pc) in enumerate(peers):
            pltpu.make_async_remote_copy(src_ref=a_ref, dst_ref=buf.at[4 * px + 2 * py + pc],
                                         send_sem=send_sems.at[k], recv_sem=recv_sems.at[k],
                                         device_id=(px, py, pc), device_id_type=MESH).wait_recv()
        for cp in sends:
            cp.wait_send()
        if reduce:
            acc = buf[0]
            for k in range(1, N_DEV):
                acc = acc + buf[k]
            out_ref[...] = acc

    vm = pl.BlockSpec(memory_space=pltpu.VMEM)
    sems = [pltpu.SemaphoreType.DMA((N_DEV - 1,)), pltpu.SemaphoreType.DMA((N_DEV - 1,))]
    if reduce:
        out_shape = jax.ShapeDtypeStruct((r, cdim), F32)
        scratch = [pltpu.VMEM((N_DEV, r, cdim), F32)] + sems
    else:
        out_shape = jax.ShapeDtypeStruct((N_DEV, r, cdim), F32)
        scratch = sems
    return pl.pallas_call(
        body, name=name, in_specs=[vm], out_specs=vm, out_shape=out_shape, scratch_shapes=scratch,
        compiler_params=pltpu.CompilerParams(vmem_limit_bytes=VMEM_LIMIT),
    )(a)


def _flat_rows(parts, cols):
    flat = jnp.concatenate([p.reshape(-1) for p in parts])
    padded = -(-flat.size // (8 * cols)) * (8 * cols)
    return jnp.pad(flat, (0, padded - flat.size)).reshape(-1, cols)


def kernel(x, meta_tokens, mix_norm, ffn_norm, pool_w, pool_scale, kv_norm, w_kv, w_q, w_o, ffn_w_up, ffn_conv_w, ffn_conv_b, ffn_w_down, final_norm, loss_target, m_meta_tokens, m_mix_norm, m_ffn_norm, m_pool_w, m_pool_scale, m_kv_norm, m_w_kv, m_w_q, m_w_o, m_ffn_w_up, m_ffn_conv_w, m_ffn_conv_b, m_ffn_w_down, m_final_norm, v_meta_tokens, v_mix_norm, v_ffn_norm, v_pool_w, v_pool_scale, v_kv_norm, v_w_kv, v_w_q, v_w_o, v_ffn_w_up, v_ffn_conv_w, v_ffn_conv_b, v_ffn_w_down, v_final_norm):
    seq, d = x.shape[1], x.shape[2]
    l_real = N_META + seq
    lp = -(-l_real // ATT_BQ) * ATT_BQ
    f2 = ffn_w_up.shape[2] * N_CHIPS
    f = f2 // 2
    gd = d // N_GROUPS
    chip = 2 * lax.axis_index("x") + lax.axis_index("y")
    core = lax.axis_index("c")

    big_parts = [pool_w[0], w_kv, w_q[0], w_o[0], ffn_w_up, ffn_w_down]
    sizes = [p.size for p in big_parts]
    n_big = sum(sizes)
    cw = 1024
    assert n_big % (2 * 16 * cw) == 0
    rows_half = n_big // (2 * cw)
    wsh = _flat_rows([p.astype(BF16) for p in big_parts], cw).reshape(2, rows_half, cw)
    wall = _gather_shards(wsh, name="gather_weights").reshape(N_CHIPS, n_big)
    offs = [0]
    for sz in sizes:
        offs.append(offs[-1] + sz)

    def big(i, shape):
        return wall[:, offs[i]:offs[i + 1]].reshape((N_CHIPS,) + shape)

    wp_b = big(0, (N_GROUPS, gd // N_CHIPS, gd)).transpose(1, 0, 2, 3).reshape(N_GROUPS, gd, gd)
    wkv_b = big(1, (d, 2 * d // N_CHIPS)).transpose(1, 0, 2).reshape(d, 2 * d)
    wq_b = big(2, (d // N_CHIPS, d)).reshape(d, d)
    wo_b = big(3, (d // N_CHIPS, d)).reshape(d, d)
    wup_b = big(4, (2, d, f2 // N_CHIPS)).transpose(1, 2, 0, 3).reshape(2, d, f2)
    wdn_b = big(5, (2, f // N_CHIPS, d)).transpose(1, 0, 2, 3).reshape(2, f, d)

    small_parts = [meta_tokens, pool_scale, ffn_conv_w]
    ssizes = [p.size for p in small_parts]
    small = _flat_rows(small_parts, LANES)
    sall = _all_devices(small, False, name="gather_small")[::2].reshape(N_CHIPS, -1)
    meta_f = sall[:, :ssizes[0]].reshape(N_CHIPS, N_META, d // N_CHIPS).transpose(1, 0, 2).reshape(N_META, d)
    scale_f = sall[:, ssizes[0]:ssizes[0] + ssizes[1]].reshape(1, d)
    cw_f = sall[:, ssizes[0] + ssizes[1]:sum(ssizes)].reshape(N_CHIPS, 2, 3, f2 // N_CHIPS).transpose(1, 2, 0, 3).reshape(2, 3, f2)

    mix0, mix1 = mix_norm[0:1], mix_norm[1:2]
    fn0, fn1 = ffn_norm[0:1], ffn_norm[1:2]
    kvn_g = kv_norm.reshape(1, d)
    fin_g = final_norm.reshape(1, d)

    pad = lp - l_real
    h0 = jnp.concatenate([meta_f, x[0], jnp.zeros((pad, d), F32)], axis=0)
    tgt = jnp.pad(loss_target[0], ((N_META, pad), (0, 0)))

    _, r0 = _norm_fwd(h0, [mix0], name="norm_h0")
    diff_b = _pool_diff_fwd(h0, r0, mix0, name="pool_diff")
    h1 = _pool_mix_fwd(h0, diff_b, wp_b, scale_f, name="pool_mix")

    def ffn_fwd(h_in, layer, tag):
        (fb,), rf = _norm_fwd(h_in, [ffn_norm[layer:layer + 1]], name=f"norm_ffn{tag}")
        u = _mm(fb, wup_b[layer], "nn", tm=1408, tn=512, name=f"ffn_up{tag}")
        a = _convgate_fwd(u, cw_f[layer], ffn_conv_b[layer:layer + 1], name=f"convgate{tag}")
        h_out = _mm(a, wdn_b[layer], "nn", tm=384, tn=1024, res=h_in, name=f"ffn_down{tag}")
        return h_out, (fb, rf, u, a)

    h2, ffn0_saved = ffn_fwd(h1, 0, "0")
    (kvn_b, n1_b), r2 = _norm_fwd(h2, [kvn_g, mix1], name="norm_h2")
    kv_b = _mm(kvn_b, wkv_b, "nn", tm=1408, tn=512, out_dtype=BF16, name="kv_proj")
    q_b = _mm(n1_b, wq_b, "nn", tm=1408, tn=512, out_dtype=BF16, name="q_proj")

    def heads(t):
        return t.reshape(lp, N_HEADS, HEAD_DIM).transpose(1, 0, 2)

    def unheads(t):
        return t.transpose(1, 0, 2).reshape(lp, N_HEADS * HEAD_DIM)

    qh, kh, vh = heads(q_b), heads(kv_b[:, :d]), heads(kv_b[:, d:])
    oh, ltot = _attn_fwd(qh, kh, vh, name="attn_fwd")
    o_b = unheads(oh)
    h3 = _mm(o_b, wo_b, "nn", tm=384, tn=1024, res=h2, name="o_proj")
    h4, ffn1_saved = ffn_fwd(h3, 1, "1")

    loss8, dh4, dh4_b, dfin = _loss_bwd(h4, fin_g, tgt, seq, name="loss")

    def ffn_bwd(h_in, layer, saved, dh_out, dh_out_b, tag):
        fb, rf, u, a = saved
        dwdn = _mm(a, dh_out_b, "tn", tm=256, tn=1024, name=f"d_wdown{tag}")
        da = _mm(dh_out_b, wdn_b[layer], "nt", tm=384, tn=f, name=f"d_act{tag}")
        du_g, du_v, dcw_g, dcw_v, dcb_g, dcb_v = _convgate_bwd(u, cw_f[layer], ffn_conv_b[layer:layer + 1], da,
                                                               name=f"convgate_bwd{tag}")
        dcw = jnp.concatenate([dcw_g, dcw_v], axis=1)
        dcb = jnp.concatenate([dcb_g, dcb_v], axis=1)
        half_chips = N_CHIPS // 2
        dwup = jnp.concatenate(
            [_mm(fb, du_g, "tn", tm=512, tn=f2 // N_CHIPS, out_shards=half_chips, name=f"d_wup_gate{tag}"),
             _mm(fb, du_v, "tn", tm=512, tn=f2 // N_CHIPS, out_shards=half_chips, name=f"d_wup_val{tag}")], axis=0)
        df = _mm(du_g, wup_b[layer][:, :f], "nt", tm=384, tn=1024, name=f"d_ffn_in_gate{tag}")
        df = _mm(du_v, wup_b[layer][:, f:], "nt", tm=384, tn=1024, res=df, name=f"d_ffn_in_val{tag}")
        dh_in, dh_in_b, (dfn,) = _norm_bwd(h_in, rf, [ffn_norm[layer:layer + 1]], [df], dh_out,
                                          name=f"norm_ffn_bwd{tag}")
        return dh_in, dh_in_b, dwup, dwdn, dcw, dcb, dfn

    dh3, dh3_b, dwup1, dwdn1, dcw1, dcb1, dfn1 = ffn_bwd(h3, 1, ffn1_saved, dh4, dh4_b, "1")

    dwo = _mm(o_b, dh3_b, "tn", tm=1024, tn=512, name="d_wo")
    do_b = _mm(dh3_b, wo_b, "nt", tm=384, tn=1024, out_dtype=BF16, name="d_attn_out")
    dqh, dkh, dvh = _attn_bwd(qh, kh, vh, heads(do_b), ltot, name="attn_bwd")
    dq_b = unheads(dqh).astype(BF16)
    dkv_b = jnp.concatenate([unheads(dkh), unheads(dvh)], axis=1).astype(BF16)
    dwq = _mm(n1_b, dq_b, "tn", tm=1024, tn=512, name="d_wq")
    dwkv = _mm(kvn_b, dkv_b, "tn", tm=1024, tn=512, out_shards=N_CHIPS, name="d_wkv")
    dn1 = _mm(dq_b, wq_b, "nt", tm=384, tn=1024, name="d_n1")
    dkvn = _mm(dkv_b, wkv_b, "nt", tm=384, tn=1024, name="d_kvn")
    dh2, dh2_b, (dmix1, dkvg) = _norm_bwd(h2, r2, [mix1, kvn_g], [dn1, dkvn], dh3, name="norm_h2_bwd")

    dh1, dh1_b, dwup0, dwdn0, dcw0, dcb0, dfn0 = ffn_bwd(h1, 0, ffn0_saved, dh2, dh2_b, "0")

    ddiff, dwp, dscale = _pool_mix_bwd(diff_b, wp_b, scale_f, dh1, name="pool_mix_bwd")
    dn0 = _pool_diff_bwd(ddiff, name="pool_diff_bwd")
    dh0, _, (dmix0,) = _norm_bwd(h0, r0, [mix0], [dn0], dh1, name="norm_h0_bwd")

    grad_x = dh0[N_META:l_real][None]
    dmeta = dh0[:N_META]

    g_parts = [
        dwp.reshape(N_GROUPS, N_CHIPS, gd // N_CHIPS, gd).transpose(1, 0, 2, 3),
        dwkv,
        dwq.reshape(N_CHIPS, d // N_CHIPS, d),
        dwo.reshape(N_CHIPS, d // N_CHIPS, d),
        dwup0, dwup1,
        dwdn0.reshape(N_CHIPS, f // N_CHIPS, d), dwdn1.reshape(N_CHIPS, f // N_CHIPS, d),
    ]
    g_all = jnp.concatenate([p.reshape(N_CHIPS, -1) for p in g_parts], axis=1).reshape(N_CHIPS, 2, rows_half, cw)
    g_mine = lax.dynamic_index_in_dim(g_all, core, axis=1, keepdims=False)
    g_other = lax.dynamic_index_in_dim(g_all, 1 - core, axis=1, keepdims=False)
    from_sibling = _to_sibling(g_other.astype(BF16), name="grads_to_sibling")
    chip_part = _add_to_bf16(g_mine, from_sibling, name="grads_chip_sum")
    half = _sum_slots(_scatter_rows(chip_part, name="grads_scatter"), name="grads_sum")
    other_half = _to_sibling(half, name="grads_join")
    g_shard = jnp.stack([jnp.where(core == 0, half, other_half),
                         jnp.where(core == 0, other_half, half)]).reshape(n_big)

    def gbig(i, shape):
        return g_shard[offs[i]:offs[i + 1]].reshape(shape)

    g_pool_w = gbig(0, pool_w.shape)
    g_w_kv = gbig(1, w_kv.shape)
    g_w_q = gbig(2, w_q.shape)
    g_w_o = gbig(3, w_o.shape)
    g_w_up = gbig(4, ffn_w_up.shape)
    g_w_dn = gbig(5, ffn_w_down.shape)

    dcw = jnp.stack([dcw0, dcw1])
    dcb = jnp.concatenate([dcb0, dcb1], axis=0)
    sg_parts = [jnp.concatenate([dmix0, dmix1], axis=0), jnp.concatenate([dfn0, dfn1], axis=0), dkvg, dfin, dcb,
                dmeta, dscale, dcw, loss8]
    sg_sizes = [p.size for p in sg_parts]
    sg = _all_devices(_flat_rows(sg_parts, LANES), True, name="reduce_small").reshape(-1)
    sg_offs = [0]
    for sz in sg_sizes:
        sg_offs.append(sg_offs[-1] + sz)

    def gsmall(i, shape):
        return sg[sg_offs[i]:sg_offs[i + 1]].reshape(shape)

    g_mix = gsmall(0, mix_norm.shape)
    g_ffn_norm = gsmall(1, ffn_norm.shape)
    g_kv_norm = gsmall(2, kv_norm.shape)
    g_final = gsmall(3, final_norm.shape)
    g_conv_b = gsmall(4, ffn_conv_b.shape)
    csh = d // N_CHIPS
    g_meta = lax.dynamic_slice_in_dim(gsmall(5, (N_META, d)), chip * csh, csh, axis=1)
    g_scale = lax.dynamic_slice_in_dim(gsmall(6, (1, d)), chip * csh, csh, axis=1)
    fsh = f2 // N_CHIPS
    g_conv_w = lax.dynamic_slice_in_dim(gsmall(7, (2, 3, f2)), chip * fsh, fsh, axis=2)
    loss = gsmall(8, (8 * LANES,))[0]

    weights = [meta_tokens, mix_norm, ffn_norm, pool_w, pool_scale, kv_norm, w_kv, w_q, w_o, ffn_w_up, ffn_conv_w,
               ffn_conv_b, ffn_w_down, final_norm]
    grads = [g_meta, g_mix, g_ffn_norm, g_pool_w, g_scale, g_kv_norm, g_w_kv, g_w_q, g_w_o, g_w_up, g_conv_w,
             g_conv_b, g_w_dn, g_final]
    ms = [m_meta_tokens, m_mix_norm, m_ffn_norm, m_pool_w, m_pool_scale, m_kv_norm, m_w_kv, m_w_q, m_w_o,
          m_ffn_w_up, m_ffn_conv_w, m_ffn_conv_b, m_ffn_w_down, m_final_norm]
    vs = [v_meta_tokens, v_mix_norm, v_ffn_norm, v_pool_w, v_pool_scale, v_kv_norm, v_w_kv, v_w_q, v_w_o,
          v_ffn_w_up, v_ffn_conv_w, v_ffn_conv_b, v_ffn_w_down, v_final_norm]
    names = ["meta", "mix", "ffnnorm", "poolw", "poolscale", "kvnorm", "wkv", "wq", "wo", "wup", "convw", "convb",
             "wdown", "final"]
    deltas, new_ms, new_vs = [], [], []
    for w, g, m, v, nm in zip(weights, grads, ms, vs, names):
        cols = w.shape[-1]
        if w.size % (8 * LANES) == 0 and w.ndim == 1:
            cols = LANES
        view = (w.size // cols, cols)
        dl, mn, vn = _adamw(w.reshape(view), g.reshape(view), m.reshape(view), v.reshape(view), name=f"adamw_{nm}")
        deltas.append(dl.reshape(w.shape))
        new_ms.append(mn.reshape(w.shape))
        new_vs.append(vn.reshape(w.shape))

    return (loss, grad_x, *grads, *deltas, *new_ms, *new_vs)
```

```python
import jax
import jax.numpy as jnp
from jax import lax
from jax.experimental import pallas as pl
from jax.experimental.pallas import tpu as pltpu

F32 = jnp.float32
BF16 = jnp.bfloat16

N_META = 16
N_HEADS = 16
HEAD_DIM = 64
POOL_WINDOWS = (2, 4, 8, 16)
N_GROUPS = 4
RMS_EPS = 1e-6
ADAM_LR = 0.001
ADAM_B1 = 0.9
ADAM_B2 = 0.999
ADAM_EPS = 1e-08
ADAM_WD = 0.01
ADAM_STEP = 10

LANES = 128
BLK = 128
ATT_CHUNKS = 3
ATT_BQ = ATT_CHUNKS * BLK
CONV_ROWS = 128
N_CHIPS = 4
N_DEV = 8
VMEM_LIMIT = 56 * 1024 * 1024
MESH = pl.DeviceIdType.MESH


def _cparams(*sem):
    return pltpu.CompilerParams(dimension_semantics=tuple(sem) if sem else None,
                                vmem_limit_bytes=VMEM_LIMIT)


def _tile(n, pref):
    best = None
    t = LANES
    while t <= min(n, pref):
        if n % t == 0:
            best = t
        t += LANES
    assert best is not None, (n, pref)
    return best


def _row_tile(r, pref, mult):
    best = r
    for t in range(mult, min(r, pref) + 1, mult):
        if r % t == 0:
            best = t
    return best


_DIMS = {
    "nn": (((1,), (0,)), ((), ())),
    "nt": (((1,), (1,)), ((), ())),
    "tn": (((0,), (0,)), ((), ())),
}


def _mm(a, b, dims, *, tm, tn, name, out_dtype=F32, res=None, out_shards=1, gather=None):
    if dims == "tn":
        k, m = a.shape
    else:
        m, k = a.shape
    n = b.shape[0] if dims == "nt" else b.shape[1]
    tm = _tile(m, tm)
    tn = _tile(n // out_shards, tn)
    a_spec = pl.BlockSpec((k, tm), lambda i, j: (0, i)) if dims == "tn" else pl.BlockSpec((tm, k), lambda i, j: (i, 0))
    b_spec = pl.BlockSpec((tn, k), lambda i, j: (j, 0)) if dims == "nt" else pl.BlockSpec((k, tn), lambda i, j: (0, j))
    o_spec = pl.BlockSpec((tm, tn), lambda i, j: (i, j))
    out_shape = jax.ShapeDtypeStruct((m, n), out_dtype)
    if out_shards > 1:
        assert res is None
        per = n // out_shards // tn
        o_spec = pl.BlockSpec((None, tm, tn), lambda i, j: (j // per, i, j % per))
        out_shape = jax.ShapeDtypeStruct((out_shards, m, n // out_shards), out_dtype)
    dn = _DIMS[dims]

    grid = (m // tm, n // tn)

    def body(*refs):
        refs = list(refs)
        a_ref, b_ref = refs[:2]
        r_ref = refs[2] if res is not None else None
        o_ref = refs[2 + (res is not None) + (gather is not None)]
        acc = lax.dot_general(a_ref[...], b_ref[...], dn, preferred_element_type=F32)
        if res is not None:
            acc = acc + r_ref[...]
        o_ref[...] = acc.astype(out_dtype)
        if gather is not None:
            w_ref, g_ref, stage_ref, send_sems, recv_sems = refs[2 + (res is not None)], *refs[-4:]
            step = pl.program_id(0) * grid[1] + pl.program_id(1)

            @pl.when(step == 0)
            def _():
                _gather_begin(w_ref, g_ref, send_sems, recv_sems)

            @pl.when(step == grid[0] * grid[1] - 1)
            def _():
                _gather_end(w_ref, g_ref, stage_ref, send_sems, recv_sems)

    ins = [a, b] + ([] if res is None else [res])
    specs = [a_spec, b_spec] + ([] if res is None else [o_spec])
    if gather is None:
        return pl.pallas_call(
            body, name=name, grid=grid, in_specs=specs, out_specs=o_spec, out_shape=out_shape,
            compiler_params=_cparams("parallel", "arbitrary"),
        )(*ins)
    g_shape, scratch = _gather_out_and_scratch(gather)
    return pl.pallas_call(
        body, name=name, grid=grid, in_specs=specs + [_ANY], out_specs=[o_spec, _ANY],
        out_shape=[out_shape, g_shape], scratch_shapes=scratch,
        compiler_params=_cparams("arbitrary", "arbitrary"),
    )(*ins, gather)


def _norm_fwd(x, gains, *, name):
    l, d = x.shape
    tr = _tile(l, 384)
    ng = len(gains)

    def body(*refs):
        x_ref = refs[0]
        g_refs = refs[1:1 + ng]
        o_refs = refs[1 + ng:1 + 2 * ng]
        r_ref = refs[1 + 2 * ng]
        xv = x_ref[...]
        r = lax.rsqrt(jnp.mean(xv * xv, axis=-1, keepdims=True) + RMS_EPS)
        xn = xv * r
        for g_ref, o_ref in zip(g_refs, o_refs):
            o_ref[...] = (xn * g_ref[...]).astype(BF16)
        r_ref[...] = r

    row = pl.BlockSpec((tr, d), lambda i: (i, 0))
    gspec = pl.BlockSpec((1, d), lambda i: (0, 0))
    outs = pl.pallas_call(
        body, name=name, grid=(l // tr,),
        in_specs=[row] + [gspec] * ng,
        out_specs=[row] * ng + [pl.BlockSpec((tr, 1), lambda i: (i, 0))],
        out_shape=[jax.ShapeDtypeStruct((l, d), BF16)] * ng + [jax.ShapeDtypeStruct((l, 1), F32)],
        compiler_params=_cparams("parallel"),
    )(x, *gains)
    return list(outs[:ng]), outs[ng]


def _rows8(v):
    r, d = v.shape
    return jnp.sum(v.reshape(r // 8, 8, d), axis=0)


def _norm_bwd(x, r, gains, dns, dres, *, name):
    l, d = x.shape
    tr = _tile(l, 384)
    ng = len(gains)
    nsteps = l // tr

    def body(*refs):
        x_ref, r_ref, dres_ref = refs[0], refs[1], refs[2]
        g_refs = refs[3:3 + ng]
        dn_refs = refs[3 + ng:3 + 2 * ng]
        dx_ref, dxb_ref = refs[3 + 2 * ng], refs[4 + 2 * ng]
        dg_refs = refs[5 + 2 * ng:5 + 3 * ng]
        acc_refs = refs[5 + 3 * ng:5 + 4 * ng]
        i = pl.program_id(0)

        @pl.when(i == 0)
        def _():
            for acc in acc_refs:
                acc[...] = jnp.zeros_like(acc)

        rv = r_ref[...]
        xn = x_ref[...] * rv
        total = dres_ref[...]
        for g_ref, dn_ref, acc in zip(g_refs, dn_refs, acc_refs):
            dn = dn_ref[...]
            acc[...] += _rows8(dn * xn)
            dxn = dn * g_ref[...]
            total = total + rv * (dxn - xn * jnp.mean(dxn * xn, axis=-1, keepdims=True))
        dx_ref[...] = total
        dxb_ref[...] = total.astype(BF16)

        @pl.when(i == nsteps - 1)
        def _():
            for dg_ref, acc in zip(dg_refs, acc_refs):
                dg_ref[...] = jnp.sum(acc[...], axis=0, keepdims=True)

    row = pl.BlockSpec((tr, d), lambda i: (i, 0))
    gspec = pl.BlockSpec((1, d), lambda i: (0, 0))
    outs = pl.pallas_call(
        body, name=name, grid=(nsteps,),
        in_specs=[row, pl.BlockSpec((tr, 1), lambda i: (i, 0)), row] + [gspec] * ng + [row] * ng,
        out_specs=[row, row] + [gspec] * ng,
        out_shape=[jax.ShapeDtypeStruct((l, d), F32), jax.ShapeDtypeStruct((l, d), BF16)]
        + [jax.ShapeDtypeStruct((1, d), F32)] * ng,
        scratch_shapes=[pltpu.VMEM((8, d), F32)] * ng,
        compiler_params=_cparams("arbitrary"),
    )(x, r, dres, *gains, *dns)
    return outs[0], outs[1], list(outs[2:])


def _loss_bwd(h, gain, tgt, seq, *, name):
    l, d = h.shape
    tr = _tile(l, 384)
    nsteps = l // tr

    def body(h_ref, g_ref, t_ref, loss_ref, dh_ref, dhb_ref, dg_ref, lacc, gacc):
        i = pl.program_id(0)

        @pl.when(i == 0)
        def _():
            lacc[...] = jnp.zeros_like(lacc)
            gacc[...] = jnp.zeros_like(gacc)

        xv = h_ref[...]
        g = g_ref[...]
        r = lax.rsqrt(jnp.mean(xv * xv, axis=-1, keepdims=True) + RMS_EPS)
        xn = xv * r
        rows = i * tr + lax.broadcasted_iota(jnp.int32, (tr, 1), 0)
        valid = (rows >= N_META) & (rows < N_META + seq)
        e = jnp.where(valid, xn * g - t_ref[...], 0.0)
        lacc[...] += _rows8(e * e)
        dy = e * (1.0 / d)
        gacc[...] += _rows8(dy * xn)
        dxn = dy * g
        dx = r * (dxn - xn * jnp.mean(dxn * xn, axis=-1, keepdims=True))
        dh_ref[...] = dx
        dhb_ref[...] = dx.astype(BF16)

        @pl.when(i == nsteps - 1)
        def _():
            loss_ref[...] = jnp.full((8, LANES), 0.5 / d * jnp.sum(lacc[...]), F32)
            dg_ref[...] = jnp.sum(gacc[...], axis=0, keepdims=True)

    row = pl.BlockSpec((tr, d), lambda i: (i, 0))
    gspec = pl.BlockSpec((1, d), lambda i: (0, 0))
    return pl.pallas_call(
        body, name=name, grid=(nsteps,),
        in_specs=[row, gspec, row],
        out_specs=[pl.BlockSpec((8, LANES), lambda i: (0, 0)), row, row, gspec],
        out_shape=[jax.ShapeDtypeStruct((8, LANES), F32), jax.ShapeDtypeStruct((l, d), F32),
                   jax.ShapeDtypeStruct((l, d), BF16), jax.ShapeDtypeStruct((1, d), F32)],
        scratch_shapes=[pltpu.VMEM((8, d), F32), pltpu.VMEM((8, d), F32)],
        compiler_params=_cparams("arbitrary"),
    )(h, gain, tgt)


def _shift_down(v, k, rows):
    return jnp.where(rows >= k, pltpu.roll(v, k, 0), 0.0)


def _shift_up(v, k, rows):
    l = v.shape[0]
    return jnp.where(rows < l - k, pltpu.roll(v, l - k, 0), 0.0)


def _pool_diff(n, w, rows):
    s = n
    for k in (1, 2, 4, 8):
        s = s + jnp.where(k < w, _shift_down(s, k, rows), 0.0)
    cnt = jnp.minimum(rows + 1, w).astype(F32)
    return s / cnt - n, cnt


def _pool_diff_fwd(h, r, gain, *, name):
    l, d = h.shape
    per_group = d // N_GROUPS // LANES

    def body(h_ref, r_ref, g_ref, o_ref):
        w = jnp.left_shift(2, pl.program_id(0) // per_group)
        rows = lax.broadcasted_iota(jnp.int32, (l, 1), 0)
        n = h_ref[...] * r_ref[...] * g_ref[...]
        diff, _ = _pool_diff(n, w, rows)
        o_ref[...] = diff.astype(BF16)

    col = pl.BlockSpec((l, LANES), lambda j: (0, j))
    return pl.pallas_call(
        body, name=name, grid=(d // LANES,),
        in_specs=[col, pl.BlockSpec((l, 1), lambda j: (0, 0)), pl.BlockSpec((1, LANES), lambda j: (0, j))],
        out_specs=col, out_shape=jax.ShapeDtypeStruct((l, d), BF16),
        compiler_params=_cparams("parallel"),
    )(h, r, gain)


def _pool_diff_bwd(ddiff, *, name):
    l, d = ddiff.shape
    per_group = d // N_GROUPS // LANES

    def body(dd_ref, o_ref):
        w = jnp.left_shift(2, pl.program_id(0) // per_group)
        rows = lax.broadcasted_iota(jnp.int32, (l, 1), 0)
        dd = dd_ref[...]
        s = dd / jnp.minimum(rows + 1, w).astype(F32)
        for k in (1, 2, 4, 8):
            s = s + jnp.where(k < w, _shift_up(s, k, rows), 0.0)
        o_ref[...] = s - dd

    col = pl.BlockSpec((l, LANES), lambda j: (0, j))
    return pl.pallas_call(
        body, name=name, grid=(d // LANES,), in_specs=[col], out_specs=col,
        out_shape=jax.ShapeDtypeStruct((l, d), F32), compiler_params=_cparams("parallel"),
    )(ddiff)


def _pool_mix_fwd(h, diff_b, w_pool, scale, *, name):
    l, d = h.shape
    gd = d // N_GROUPS
    tr = _tile(l, 1408)

    def body(h_ref, d_ref, w_ref, s_ref, o_ref):
        y = jnp.dot(d_ref[...], w_ref[0], preferred_element_type=F32)
        o_ref[...] = h_ref[...] + y * s_ref[...]

    blk = pl.BlockSpec((tr, gd), lambda i, g: (i, g))
    vec = pl.BlockSpec((1, gd), lambda i, g: (0, g))
    return pl.pallas_call(
        body, name=name, grid=(l // tr, N_GROUPS),
        in_specs=[blk, blk, pl.BlockSpec((1, gd, gd), lambda i, g: (g, 0, 0)), vec],
        out_specs=blk, out_shape=jax.ShapeDtypeStruct((l, d), F32),
        compiler_params=_cparams("parallel", "parallel"),
    )(h, diff_b, w_pool, scale)


def _pool_mix_bwd(diff_b, w_pool, scale, dh1, *, name):
    l, d = dh1.shape
    gd = d // N_GROUPS
    tr = _tile(l, 1408)
    nsteps = l // tr

    def body(d_ref, w_ref, s_ref, dy_ref, dd_ref, dw_ref, ds_ref, sacc):
        i = pl.program_id(1)

        @pl.when(i == 0)
        def _():
            dw_ref[...] = jnp.zeros_like(dw_ref)
            sacc[...] = jnp.zeros_like(sacc)

        diff_b_, wg, dy = d_ref[...], w_ref[0], dy_ref[...]
        yy = jnp.dot(diff_b_, wg, preferred_element_type=F32)
        sacc[...] += _rows8(dy * yy)
        dyy_b = (dy * s_ref[...]).astype(BF16)
        dw_ref[0] += lax.dot_general(diff_b_, dyy_b, _DIMS["tn"], preferred_element_type=F32)
        dd_ref[...] = lax.dot_general(dyy_b, wg, _DIMS["nt"], preferred_element_type=F32)

        @pl.when(i == nsteps - 1)
        def _():
            ds_ref[...] = jnp.sum(sacc[...], axis=0, keepdims=True)

    blk = pl.BlockSpec((tr, gd), lambda g, i: (i, g))
    vec = pl.BlockSpec((1, gd), lambda g, i: (0, g))
    wspec = pl.BlockSpec((1, gd, gd), lambda g, i: (g, 0, 0))
    return pl.pallas_call(
        body, name=name, grid=(N_GROUPS, nsteps),
        in_specs=[blk, wspec, vec, blk],
        out_specs=[blk, wspec, vec],
        out_shape=[jax.ShapeDtypeStruct((l, d), F32), jax.ShapeDtypeStruct((N_GROUPS, gd, gd), F32),
                   jax.ShapeDtypeStruct((1, d), F32)],
        scratch_shapes=[pltpu.VMEM((8, gd), F32)],
        compiler_params=_cparams("parallel", "arbitrary"),
    )(diff_b, w_pool, scale, dh1)


def _conv_chunk(cur, prev, w, b, rowi):
    s1 = jnp.where(rowi < 1, pltpu.roll(prev, 1, 0), pltpu.roll(cur, 1, 0))
    s2 = jnp.where(rowi < 2, pltpu.roll(prev, 2, 0), pltpu.roll(cur, 2, 0))
    return b + w[0] * s2 + w[1] * s1 + w[2] * cur, s1, s2


def _bcast_rows(ref, rows):
    v = ref[...]
    return [jnp.broadcast_to(v[k:k + 1], (rows, v.shape[1])) for k in range(v.shape[0])]


def _convgate_fwd(u, cw, cb, *, name, gather=None):
    l, f2 = u.shape
    f = f2 // 2
    tc = _tile(f, LANES)
    nc = f // tc
    rows = CONV_ROWS
    assert l % rows == 0

    def body(ug_ref, uv_ref, wg_ref, wv_ref, bg_ref, bv_ref, *rest):
        a_ref = rest[0] if gather is None else rest[1]
        if gather is not None:
            w_ref, _, g_ref, stage_ref, send_sems, recv_sems = rest

            @pl.when(pl.program_id(0) == 0)
            def _():
                _gather_begin(w_ref, g_ref, send_sems, recv_sems)

        rowi = lax.broadcasted_iota(jnp.int32, (rows, tc), 0)
        wg, wv = _bcast_rows(wg_ref, rows), _bcast_rows(wv_ref, rows)
        bg, bv = _bcast_rows(bg_ref, rows)[0], _bcast_rows(bv_ref, rows)[0]

        def chunk(i, carry):
            pg, pv = carry
            r = pl.multiple_of(i * rows, rows)
            cg, cv = ug_ref[pl.ds(r, rows), :], uv_ref[pl.ds(r, rows), :]
            gate, _, _ = _conv_chunk(cg, pg, wg, bg, rowi)
            val, _, _ = _conv_chunk(cv, pv, wv, bv, rowi)
            a_ref[pl.ds(r, rows), :] = (gate * jax.nn.sigmoid(gate) * val).astype(BF16)
            return cg, cv

        zero = jnp.zeros((rows, tc), F32)
        lax.fori_loop(0, l // rows, chunk, (zero, zero))
        if gather is not None:
            @pl.when(pl.program_id(0) == nc - 1)
            def _():
                _gather_end(w_ref, g_ref, stage_ref, send_sems, recv_sems)

    def spec(rows_, off):
        return pl.BlockSpec((rows_, tc), lambda j: (0, j + off))

    in_specs = [spec(l, 0), spec(l, nc), spec(3, 0), spec(3, nc), spec(1, 0), spec(1, nc)]
    out_shape = jax.ShapeDtypeStruct((l, f), BF16)
    if gather is None:
        return pl.pallas_call(
            body, name=name, grid=(nc,), in_specs=in_specs, out_specs=spec(l, 0), out_shape=out_shape,
            compiler_params=_cparams("parallel"),
        )(u, u, cw, cw, cb, cb)
    g_shape, scratch = _gather_out_and_scratch(gather)
    return pl.pallas_call(
        body, name=name, grid=(nc,), in_specs=in_specs + [_ANY], out_specs=[spec(l, 0), _ANY],
        out_shape=[out_shape, g_shape], scratch_shapes=scratch,
        compiler_params=_cparams("arbitrary"),
    )(u, u, cw, cw, cb, cb, gather)


def _convgate_bwd(u, cw, cb, da, *, name):
    l, f2 = u.shape
    f = f2 // 2
    tc = _tile(f, LANES)
    nc = f // tc
    rows = CONV_ROWS
    assert l % rows == 0
    nchunks = l // rows

    def body(ug_ref, uv_ref, wg_ref, wv_ref, bg_ref, bv_ref, da_ref,
             dug_ref, duv_ref, dcwg_ref, dcwv_ref, dcbg_ref, dcbv_ref):
        rowi = lax.broadcasted_iota(jnp.int32, (rows, tc), 0)
        wg, wv = _bcast_rows(wg_ref, rows), _bcast_rows(wv_ref, rows)
        bg, bv = _bcast_rows(bg_ref, rows)[0], _bcast_rows(bv_ref, rows)[0]

        def tap_sums(acc, dc, taps):
            return (acc[0] + _rows8(dc),) + tuple(a + _rows8(dc * t) for a, t in zip(acc[1:], taps))

        def input_grad(dc, dc_next, w):
            up1 = jnp.where(rowi >= rows - 1, pltpu.roll(dc_next, rows - 1, 0), pltpu.roll(dc, rows - 1, 0))
            up2 = jnp.where(rowi >= rows - 2, pltpu.roll(dc_next, rows - 2, 0), pltpu.roll(dc, rows - 2, 0))
            return (w[2] * dc + w[1] * up1 + w[0] * up2).astype(BF16)

        def chunk(ii, carry):
            dcg_next, dcv_next, acc_g, acc_v = carry
            i = nchunks - 1 - ii
            r = pl.multiple_of(i * rows, rows)
            rp = pl.multiple_of(jnp.maximum(i - 1, 0) * rows, rows)
            cg, cv = ug_ref[pl.ds(r, rows), :], uv_ref[pl.ds(r, rows), :]
            pg = jnp.where(i > 0, ug_ref[pl.ds(rp, rows), :], 0.0)
            pv = jnp.where(i > 0, uv_ref[pl.ds(rp, rows), :], 0.0)
            gate, g1, g2 = _conv_chunk(cg, pg, wg, bg, rowi)
            val, v1, v2 = _conv_chunk(cv, pv, wv, bv, rowi)
            sg = jax.nn.sigmoid(gate)
            dav = da_ref[pl.ds(r, rows), :]
            dcg = dav * val * (sg * (1.0 + gate * (1.0 - sg)))
            dcv = dav * (gate * sg)
            acc_g = tap_sums(acc_g, dcg, (g2, g1, cg))
            acc_v = tap_sums(acc_v, dcv, (v2, v1, cv))
            dug_ref[pl.ds(r, rows), :] = input_grad(dcg, dcg_next, wg)
            duv_ref[pl.ds(r, rows), :] = input_grad(dcv, dcv_next, wv)
            return dcg, dcv, acc_g, acc_v

        zero = jnp.zeros((rows, tc), F32)
        zero8 = (jnp.zeros((8, tc), F32),) * 4
        _, _, acc_g, acc_v = lax.fori_loop(0, nchunks, chunk, (zero, zero, zero8, zero8))
        for acc, dcw_ref, dcb_ref in ((acc_g, dcwg_ref, dcbg_ref), (acc_v, dcwv_ref, dcbv_ref)):
            dcb_ref[...] = jnp.sum(acc[0], axis=0, keepdims=True)
            for k in range(3):
                dcw_ref[k:k + 1, :] = jnp.sum(acc[1 + k], axis=0, keepdims=True)

    def spec(rows_, off):
        return pl.BlockSpec((rows_, tc), lambda j: (0, j + off))

    return pl.pallas_call(
        body, name=name, grid=(nc,),
        in_specs=[spec(l, 0), spec(l, nc), spec(3, 0), spec(3, nc), spec(1, 0), spec(1, nc), spec(l, 0)],
        out_specs=[spec(l, 0), spec(l, 0), spec(3, 0), spec(3, 0), spec(1, 0), spec(1, 0)],
        out_shape=[jax.ShapeDtypeStruct((l, f), BF16)] * 2 + [jax.ShapeDtypeStruct((3, f), F32)] * 2
        + [jax.ShapeDtypeStruct((1, f), F32)] * 2,
        compiler_params=_cparams("parallel"),
    )(u, u, cw, cw, cb, cb, da)


def _cumsum_mm(v_b, t2):
    return jnp.dot(v_b, t2, preferred_element_type=F32)


def _tri_and_ones(tri_fn):
    row = lax.broadcasted_iota(jnp.int32, (BLK, 2 * BLK), 0)
    col = lax.broadcasted_iota(jnp.int32, (BLK, 2 * BLK), 1)
    return jnp.where((col >= BLK) | tri_fn(row, col), 1.0, 0.0).astype(BF16)


def _logits(z, mask):
    sp = jnp.log(1.0 + jnp.exp(-jnp.abs(z)))
    lb = jnp.minimum(z, 0.0) - sp
    lm = lb - z
    if mask is not None:
        lm = jnp.where(mask, lm, 0.0)
    return lb, lm


def _software_pipeline(stages, n, block_of, state):
    ns = len(stages)
    inflight = [None] * (ns - 1)
    for t in range(ns - 1):
        new = list(inflight)
        for s in range(t, -1, -1):
            y, state = stages[s](block_of(t - s), None if s == 0 else inflight[s - 1], state)
            new[s] = y
        inflight = new

    def steady(i, carry):
        inflight, state = carry
        new = [None] * (ns - 1)
        for s in range(ns - 1, -1, -1):
            y, state = stages[s](block_of(i + ns - 1 - s), None if s == 0 else inflight[s - 1], state)
            if s < ns - 1:
                new[s] = y
        return tuple(new), state

    inflight, state = lax.fori_loop(0, n - (ns - 1), steady, (tuple(inflight), state))
    inflight = list(inflight)
    for e in range(1, ns):
        new = list(inflight)
        for s in range(ns - 1, e - 1, -1):
            y, state = stages[s](block_of(n - 1 + e - s), inflight[s - 1], state)
            if s < ns - 1:
                new[s] = y
        inflight = new
    return state


def _attn_fwd(q, k, v, *, name):
    nh, l, dh = q.shape
    assert l % ATT_BQ == 0
    nq = l // ATT_BQ
    qscale = HEAD_DIM ** -0.5
    chunks = range(ATT_CHUNKS)

    def body(q_ref, k_ref, v_ref, o_ref, lt_ref, z_scr, cs_scr, pv_scr):
        t_later = _tri_and_ones(lambda r, c: r > c)
        dmask = (lax.broadcasted_iota(jnp.int32, (BLK, BLK), 1)
                 < lax.broadcasted_iota(jnp.int32, (BLK, BLK), 0))

        def logits(qc, c0, mask):
            z = lax.dot_general(qc, k_ref[0, pl.ds(c0, BLK), :], _DIMS["nt"], preferred_element_type=F32)
            lb, lm = _logits(z, mask)
            return lb, lm.astype(BF16)

        def weights(lb, lm_b, mask, run):
            cs = _cumsum_mm(lm_b, t_later)
            a = jnp.exp(lb + cs[:, :BLK] + run)
            if mask is not None:
                a = jnp.where(mask, a, 0.0)
            return a.astype(BF16), run + cs[:, BLK:]

        def accumulate(a_b, c0, acc):
            return acc + jnp.dot(a_b, v_ref[0, pl.ds(c0, BLK), :], preferred_element_type=F32)

        def qblock(qb, _):
            r0 = pl.multiple_of(qb * ATT_BQ, ATT_BQ)
            qs = [q_ref[0, pl.ds(r0 + rc * BLK, BLK), :] * jnp.asarray(qscale, BF16) for rc in chunks]
            accs = [jnp.zeros((BLK, dh), F32)] * ATT_CHUNKS
            runs = [jnp.zeros((BLK, BLK), F32)] * ATT_CHUNKS
            for dj in range(ATT_CHUNKS - 1, -1, -1):
                c0 = r0 + dj * BLK
                for rc in range(dj, ATT_CHUNKS):
                    mask = dmask if rc == dj else None
                    a_b, runs[rc] = weights(*logits(qs[rc], c0, mask), mask, runs[rc])
                    accs[rc] = accumulate(a_b, c0, accs[rc])

            def col0(b):
                return pl.multiple_of(r0 - (b + 1) * BLK, BLK)

            def stage_scores(b, _, state):
                for rc in chunks:
                    z_scr[rc] = lax.dot_general(qs[rc], k_ref[0, pl.ds(col0(b), BLK), :], _DIMS["nt"],
                                                preferred_element_type=F32)
                return (), state

            def stage_cumsum(b, _, state):
                lbs = []
                for rc in chunks:
                    lb, lm = _logits(z_scr[rc], None)
                    cs_scr[rc] = _cumsum_mm(lm.astype(BF16), t_later)
                    lbs.append(lb)
                return tuple(lbs), state

            def stage_weights(b, lbs, state):
                accs, runs = state
                new_runs = []
                for rc in chunks:
                    cs = cs_scr[rc]
                    a = jnp.exp(lbs[rc] + cs[:, :BLK] + runs[rc])
                    pv_scr[rc] = jnp.dot(a.astype(BF16), v_ref[0, pl.ds(col0(b), BLK), :],
                                         preferred_element_type=F32)
                    new_runs.append(runs[rc] + cs[:, BLK:])
                return (), (accs, tuple(new_runs))

            def stage_acc(b, _, state):
                accs, runs = state
                return None, (tuple(accs[rc] + pv_scr[rc] for rc in chunks), runs)

            def left_region(state):
                return _software_pipeline([stage_scores, stage_cumsum, stage_weights, stage_acc],
                                          qb * ATT_CHUNKS, lambda b: b, state)

            accs, runs = lax.cond(qb > 0, left_region, lambda s: s, (tuple(accs), tuple(runs)))
            for rc in chunks:
                o_ref[0, pl.ds(r0 + rc * BLK, BLK), :] = accs[rc].astype(BF16)
                lt_ref[0, pl.ds(r0 + rc * BLK, BLK), :] = runs[rc]
            return 0

        lax.fori_loop(0, nq, qblock, 0)

    head = pl.BlockSpec((1, l, dh), lambda h: (h, 0, 0))
    return pl.pallas_call(
        body, name=name, grid=(nh,),
        in_specs=[head, head, head],
        out_specs=[head, pl.BlockSpec((1, l, BLK), lambda h: (h, 0, 0))],
        out_shape=[jax.ShapeDtypeStruct((nh, l, dh), BF16), jax.ShapeDtypeStruct((nh, l, BLK), F32)],
        scratch_shapes=[pltpu.VMEM((ATT_CHUNKS, BLK, BLK), F32), pltpu.VMEM((ATT_CHUNKS, BLK, 2 * BLK), F32),
                        pltpu.VMEM((ATT_CHUNKS, BLK, dh), F32)],
        compiler_params=_cparams("parallel"),
    )(q, k, v)


def _attn_bwd(q, k, v, do, ltot, *, name):
    nh, l, dh = q.shape
    assert l % ATT_BQ == 0
    nq = l // ATT_BQ
    qscale = HEAD_DIM ** -0.5
    chunks = range(ATT_CHUNKS)

    def body(q_ref, k_ref, v_ref, do_ref, lt_ref, dq_ref, dk_ref, dv_ref,
             z_scr, cs_scr, da_scr, cd_scr, dqp_scr, dkp_scr, dvp_scr):
        t_incl = _tri_and_ones(lambda r, c: r <= c)
        t_excl = _tri_and_ones(lambda r, c: r < c)
        dmask = (lax.broadcasted_iota(jnp.int32, (BLK, BLK), 1)
                 < lax.broadcasted_iota(jnp.int32, (BLK, BLK), 0))
        dk_ref[...] = jnp.zeros_like(dk_ref)
        dv_ref[...] = jnp.zeros_like(dv_ref)

        def logits(qc, c0, mask):
            z = lax.dot_general(qc, k_ref[0, pl.ds(c0, BLK), :], _DIMS["nt"], preferred_element_type=F32)
            lb, lm = _logits(z, mask)
            return lb, lm.astype(BF16)

        def weights(lb, lm_b, doc, ltc, c0, mask, pre_lm):
            cs = _cumsum_mm(lm_b, t_incl)
            da = lax.dot_general(doc, v_ref[0, pl.ds(c0, BLK), :], _DIMS["nt"], preferred_element_type=F32)
            a = jnp.exp(lb + (ltc - pre_lm - cs[:, :BLK]))
            if mask is not None:
                a = jnp.where(mask, a, 0.0)
            dl = a * da
            return (jnp.exp(lb), dl, dl.astype(BF16), a.astype(BF16)), pre_lm + cs[:, BLK:]

        def logit_grad(beta, dl, dl_b, mask, pre_dl):
            cd = _cumsum_mm(dl_b, t_excl)
            dz = dl - beta * (dl + pre_dl + cd[:, :BLK])
            if mask is not None:
                dz = jnp.where(mask, dz, 0.0)
            return (dz * qscale).astype(BF16), pre_dl + cd[:, BLK:]

        def key_grads(c0, dz_parts, a_parts, q_rows, do_rows):
            dz_all = dz_parts[0] if len(dz_parts) == 1 else jnp.concatenate(dz_parts, axis=0)
            a_all = a_parts[0] if len(a_parts) == 1 else jnp.concatenate(a_parts, axis=0)
            dk_ref[0, pl.ds(c0, BLK), :] += lax.dot_general(dz_all, q_rows, _DIMS["tn"], preferred_element_type=F32)
            dv_ref[0, pl.ds(c0, BLK), :] += lax.dot_general(a_all, do_rows, _DIMS["tn"], preferred_element_type=F32)

        def qblock(qb, _):
            r0 = pl.multiple_of(qb * ATT_BQ, ATT_BQ)
            q_raw = q_ref[0, pl.ds(r0, ATT_BQ), :]
            do_all = do_ref[0, pl.ds(r0, ATT_BQ), :]
            qs = [q_raw[rc * BLK:(rc + 1) * BLK] * jnp.asarray(qscale, BF16) for rc in chunks]
            dos = [do_all[rc * BLK:(rc + 1) * BLK] for rc in chunks]
            lts = [lt_ref[0, pl.ds(r0 + rc * BLK, BLK), :] for rc in chunks]

            def col0(b):
                return pl.multiple_of(b * BLK, BLK)

            def stage_scores(b, _, state):
                kj = k_ref[0, pl.ds(col0(b), BLK), :]
                for rc in chunks:
                    z_scr[rc] = lax.dot_general(qs[rc], kj, _DIMS["nt"], preferred_element_type=F32)
                return (), state

            def stage_prefix(b, _, state):
                vj = v_ref[0, pl.ds(col0(b), BLK), :]
                lbs = []
                for rc in chunks:
                    lb, lm = _logits(z_scr[rc], None)
                    cs_scr[rc] = _cumsum_mm(lm.astype(BF16), t_incl)
                    da_scr[rc] = lax.dot_general(dos[rc], vj, _DIMS["nt"], preferred_element_type=F32)
                    lbs.append(lb)
                return tuple(lbs), state

            def stage_weights(b, lbs, state):
                dqs, pls, pds = state
                out, new_pls = [], []
                for rc in chunks:
                    cs = cs_scr[rc]
                    a = jnp.exp(lbs[rc] + (lts[rc] - pls[rc] - cs[:, :BLK]))
                    dl = a * da_scr[rc]
                    cd_scr[rc] = _cumsum_mm(dl.astype(BF16), t_excl)
                    out.append((jnp.exp(lbs[rc]), dl, a.astype(BF16)))
                    new_pls.append(pls[rc] + cs[:, BLK:])
                return tuple(out), (dqs, tuple(new_pls), pds)

            def stage_products(b, ys, state):
                dqs, pls, pds = state
                c0 = col0(b)
                kj = k_ref[0, pl.ds(c0, BLK), :]
                dz_bs, new_pds = [], []
                for rc in chunks:
                    beta, dl, _ = ys[rc]
                    cd = cd_scr[rc]
                    dz_b = ((dl - beta * (dl + pds[rc] + cd[:, :BLK])) * qscale).astype(BF16)
                    dqp_scr[rc] = jnp.dot(dz_b, kj, preferred_element_type=F32)
                    dz_bs.append(dz_b)
                    new_pds.append(pds[rc] + cd[:, BLK:])
                dkp_scr[...] = lax.dot_general(jnp.concatenate(dz_bs, axis=0), q_raw, _DIMS["tn"],
                                               preferred_element_type=F32)
                dvp_scr[...] = lax.dot_general(jnp.concatenate([y[2] for y in ys], axis=0), do_all, _DIMS["tn"],
                                               preferred_element_type=F32)
                return (), (dqs, pls, tuple(new_pds))

            def stage_acc(b, _, state):
                dqs, pls, pds = state
                c0 = col0(b)
                dk_ref[0, pl.ds(c0, BLK), :] += dkp_scr[...]
                dv_ref[0, pl.ds(c0, BLK), :] += dvp_scr[...]
                return None, (tuple(dqs[rc] + dqp_scr[rc] for rc in chunks), pls, pds)

            stages = [stage_scores, stage_prefix, stage_weights, stage_products, stage_acc]
            zero = jnp.zeros((BLK, BLK), F32)
            state = ((jnp.zeros((BLK, dh), F32),) * ATT_CHUNKS, (zero,) * ATT_CHUNKS, (zero,) * ATT_CHUNKS)

            def pipelined(state):
                return _software_pipeline(stages, qb * ATT_CHUNKS, lambda b: b, state)

            def one_by_one(state):
                def block(b, state):
                    x = None
                    for stage in stages:
                        x, state = stage(b, x, state)
                    return state
                return lax.fori_loop(0, qb * ATT_CHUNKS, block, state)

            dqs, pls, pds = lax.cond(qb * ATT_CHUNKS >= len(stages) - 1, pipelined, one_by_one, state)
            dqs, pls, pds = list(dqs), list(pls), list(pds)
            for dj in chunks:
                c0 = r0 + dj * BLK
                kj = k_ref[0, pl.ds(c0, BLK), :]
                dz_parts, a_parts = [], []
                for rc in range(dj, ATT_CHUNKS):
                    mask = dmask if rc == dj else None
                    ys, pls[rc] = weights(*logits(qs[rc], c0, mask), dos[rc], lts[rc], c0, mask, pls[rc])
                    dz_b, pds[rc] = logit_grad(*ys[:3], mask, pds[rc])
                    a_b = ys[3]
                    dqs[rc] = dqs[rc] + jnp.dot(dz_b, kj, preferred_element_type=F32)
                    dz_parts.append(dz_b)
                    a_parts.append(a_b)
                key_grads(c0, dz_parts, a_parts, q_raw[dj * BLK:], do_all[dj * BLK:])
            for rc in chunks:
                dq_ref[0, pl.ds(r0 + rc * BLK, BLK), :] = dqs[rc]
            return 0

        lax.fori_loop(0, nq, qblock, 0)

    head = pl.BlockSpec((1, l, dh), lambda h: (h, 0, 0))
    sq = pltpu.VMEM((ATT_CHUNKS, BLK, BLK), F32)
    sw = pltpu.VMEM((ATT_CHUNKS, BLK, 2 * BLK), F32)
    return pl.pallas_call(
        body, name=name, grid=(nh,),
        in_specs=[head, head, head, head, pl.BlockSpec((1, l, BLK), lambda h: (h, 0, 0))],
        out_specs=[head, head, head],
        out_shape=[jax.ShapeDtypeStruct((nh, l, dh), F32)] * 3,
        scratch_shapes=[sq, sw, sq, sw, pltpu.VMEM((ATT_CHUNKS, BLK, dh), F32),
                        pltpu.VMEM((BLK, dh), F32), pltpu.VMEM((BLK, dh), F32)],
        compiler_params=_cparams("parallel"),
    )(q, k, v, do, ltot)


def _add_to_bf16(a, b, *, name):
    n, r, c = a.shape
    tr = _row_tile(r, 1536, 16)
    blk = pl.BlockSpec((1, tr, c), lambda i, j: (i, j, 0))

    def body(a_ref, b_ref, o_ref):
        o_ref[...] = (a_ref[...] + b_ref[...].astype(F32)).astype(BF16)

    return pl.pallas_call(body, name=name, grid=(n, r // tr), in_specs=[blk, blk], out_specs=blk,
                          out_shape=jax.ShapeDtypeStruct(a.shape, BF16),
                          compiler_params=_cparams("parallel", "parallel"))(a, b)


def _sum_slots(p, *, name):
    n, r, c = p.shape
    tr = _row_tile(r, 1536, 16)

    def body(p_ref, o_ref):
        acc = p_ref[0].astype(F32)
        for s in range(1, n):
            acc = acc + p_ref[s].astype(F32)
        o_ref[...] = acc

    return pl.pallas_call(body, name=name, grid=(r // tr,),
                          in_specs=[pl.BlockSpec((n, tr, c), lambda j: (0, j, 0))],
                          out_specs=pl.BlockSpec((tr, c), lambda j: (j, 0)),
                          out_shape=jax.ShapeDtypeStruct((r, c), F32),
                          compiler_params=_cparams("parallel"))(p)


def _adamw(w, g, m, v, *, name):
    r, c = w.shape
    tr = _row_tile(r, 512, 8)
    blk = pl.BlockSpec((tr, c), lambda i: (i, 0))

    def body(w_ref, g_ref, m_ref, v_ref, d_ref, mo_ref, vo_ref):
        gv = g_ref[...]
        mn = ADAM_B1 * m_ref[...] + (1.0 - ADAM_B1) * gv
        vn = ADAM_B2 * v_ref[...] + (1.0 - ADAM_B2) * (gv * gv)
        m_hat = mn / (1.0 - ADAM_B1 ** ADAM_STEP)
        v_hat = vn / (1.0 - ADAM_B2 ** ADAM_STEP)
        d_ref[...] = -ADAM_LR * (m_hat / (jnp.sqrt(v_hat) + ADAM_EPS) + ADAM_WD * w_ref[...])
        mo_ref[...] = mn
        vo_ref[...] = vn

    return pl.pallas_call(body, name=name, grid=(r // tr,), in_specs=[blk] * 4, out_specs=[blk] * 3,
                          out_shape=[jax.ShapeDtypeStruct((r, c), F32)] * 3,
                          compiler_params=_cparams("parallel"))(w, g, m, v)


_ANY = pl.BlockSpec(memory_space=pl.ANY)


def _place():
    x, y, c = lax.axis_index("x"), lax.axis_index("y"), lax.axis_index("c")
    chips = [(1 - x, y), (x, 1 - y), (1 - x, 1 - y)]
    return x, y, c, chips


def _gather_copy(k, src, dst, to, send_sems, recv_sems):
    return pltpu.make_async_remote_copy(src_ref=src, dst_ref=dst, send_sem=send_sems.at[k],
                                        recv_sem=recv_sems.at[k], device_id=to, device_id_type=MESH)


def _gather_begin(w_ref, out_ref, send_sems, recv_sems):
    x, y, c, chips = _place()
    s = 2 * x + y
    for j, chip in enumerate(chips):
        _gather_copy(j, w_ref.at[c], out_ref.at[s, c], (*chip, c), send_sems, recv_sems).start()


def _gather_end(w_ref, out_ref, stage_ref, send_sems, recv_sems):
    x, y, c, chips = _place()
    s = 2 * x + y
    sibling = (x, y, 1 - c)
    slots = [2 * px + py for px, py in chips]
    passed = []
    for j, sp in enumerate(slots):
        _gather_copy(j, w_ref.at[c], out_ref.at[sp, c], sibling, send_sems, recv_sems).wait_recv()
        fwd = _gather_copy(3 + j, out_ref.at[sp, c], out_ref.at[sp, c], sibling, send_sems, recv_sems)
        fwd.start()
        passed.append(fwd)
    for h in range(2):
        pltpu.sync_copy(w_ref.at[h], stage_ref)
        pltpu.sync_copy(stage_ref, out_ref.at[s, h])
    for j, sp in enumerate(slots):
        _gather_copy(3 + j, w_ref.at[c], out_ref.at[sp, 1 - c], sibling, send_sems, recv_sems).wait_recv()
    for j, chip in enumerate(chips):
        _gather_copy(j, w_ref.at[c], out_ref.at[s, c], (*chip, c), send_sems, recv_sems).wait_send()
    for fwd in passed:
        fwd.wait_send()


def _gather_out_and_scratch(wsh):
    _, r, cdim = wsh.shape
    return (jax.ShapeDtypeStruct((N_CHIPS, 2, r, cdim), wsh.dtype),
            [pltpu.VMEM((r, cdim), wsh.dtype), pltpu.SemaphoreType.DMA((6,)), pltpu.SemaphoreType.DMA((6,))])


def _gather_shards(wsh, *, name):
    def body(w_ref, out_ref, stage_ref, send_sems, recv_sems):
        _gather_begin(w_ref, out_ref, send_sems, recv_sems)
        _gather_end(w_ref, out_ref, stage_ref, send_sems, recv_sems)

    out_shape, scratch = _gather_out_and_scratch(wsh)
    return pl.pallas_call(
        body, name=name, in_specs=[_ANY], out_specs=_ANY, out_shape=out_shape, scratch_shapes=scratch,
        compiler_params=pltpu.CompilerParams(vmem_limit_bytes=VMEM_LIMIT),
    )(wsh)


def _to_sibling(a, *, name):
    def body(a_ref, out_ref, send_sem, recv_sem):
        x, y, c, _ = _place()
        cp = pltpu.make_async_remote_copy(src_ref=a_ref, dst_ref=out_ref, send_sem=send_sem, recv_sem=recv_sem,
                                          device_id=(x, y, 1 - c), device_id_type=MESH)
        cp.start()
        cp.wait()

    return pl.pallas_call(
        body, name=name, in_specs=[_ANY], out_specs=_ANY,
        out_shape=jax.ShapeDtypeStruct(a.shape, a.dtype),
        scratch_shapes=[pltpu.SemaphoreType.DMA, pltpu.SemaphoreType.DMA],
    )(a)


def _scatter_rows(p, *, name):
    def body(p_ref, out_ref, stage_ref, send_sems, recv_sems):
        x, y, c, chips = _place()
        s = 2 * x + y
        sends = []
        for j, (px, py) in enumerate(chips):
            sp = 2 * px + py
            cp = pltpu.make_async_remote_copy(src_ref=p_ref.at[sp], dst_ref=out_ref.at[s], send_sem=send_sems.at[j],
                                              recv_sem=recv_sems.at[j], device_id=(px, py, c), device_id_type=MESH)
            cp.start()
            sends.append(cp)
        pltpu.sync_copy(p_ref.at[s], stage_ref)
        pltpu.sync_copy(stage_ref, out_ref.at[s])
        for j, (px, py) in enumerate(chips):
            sp = 2 * px + py
            pltpu.make_async_remote_copy(src_ref=p_ref.at[sp], dst_ref=out_ref.at[sp], send_sem=send_sems.at[j],
                                         recv_sem=recv_sems.at[j], device_id=(px, py, c),
                                         device_id_type=MESH).wait_recv()
        for cp in sends:
            cp.wait_send()

    return pl.pallas_call(
        body, name=name, in_specs=[_ANY], out_specs=_ANY,
        out_shape=jax.ShapeDtypeStruct(p.shape, p.dtype),
        scratch_shapes=[pltpu.VMEM(p.shape[1:], p.dtype), pltpu.SemaphoreType.DMA((3,)),
                        pltpu.SemaphoreType.DMA((3,))],
        compiler_params=pltpu.CompilerParams(vmem_limit_bytes=VMEM_LIMIT),
    )(p)


def _all_devices(a, reduce, *, name):
    r, cdim = a.shape

    def body(a_ref, out_ref, *scratch):
        if reduce:
            buf, send_sems, recv_sems = scratch
        else:
            buf = out_ref
            send_sems, recv_sems = scratch
        x, y, c, _ = _place()
        me = 4 * x + 2 * y + c
        buf[me] = a_ref[...]
        peers = []
        for k in range(1, N_DEV):
            dx, dy, dc = (k >> 2) & 1, (k >> 1) & 1, k & 1
            peers.append((x ^ dx, y ^ dy, c ^ dc))
        sends = []
        for k, peer in enumerate(peers):
            cp = pltpu.make_async_remote_copy(src_ref=a_ref, dst_ref=buf.at[me], send_sem=send_sems.at[k],
                                              recv_sem=recv_sems.at[k], device_id=peer, device_id_type=MESH)
            cp.start()
            sends.append(cp)
        for k, (px, py, pc) in enumerate(peers):
            pltpu.make_async_remote_copy(src_ref=a_ref, dst_ref=buf.at[4 * px + 2 * py + pc],
                                         send_sem=send_sems.at[k], recv_sem=recv_sems.at[k],
                                         device_id=(px, py, pc), device_id_type=MESH).wait_recv()
        for cp in sends:
            cp.wait_send()
        if reduce:
            acc = buf[0]
            for k in range(1, N_DEV):
                acc = acc + buf[k]
            out_ref[...] = acc

    vm = pl.BlockSpec(memory_space=pltpu.VMEM)
    sems = [pltpu.SemaphoreType.DMA((N_DEV - 1,)), pltpu.SemaphoreType.DMA((N_DEV - 1,))]
    if reduce:
        out_shape = jax.ShapeDtypeStruct((r, cdim), F32)
        scratch = [pltpu.VMEM((N_DEV, r, cdim), F32)] + sems
    else:
        out_shape = jax.ShapeDtypeStruct((N_DEV, r, cdim), F32)
        scratch = sems
    return pl.pallas_call(
        body, name=name, in_specs=[vm], out_specs=vm, out_shape=out_shape, scratch_shapes=scratch,
        compiler_params=pltpu.CompilerParams(vmem_limit_bytes=VMEM_LIMIT),
    )(a)


def _flat_rows(parts, cols):
    flat = jnp.concatenate([p.reshape(-1) for p in parts])
    padded = -(-flat.size // (8 * cols)) * (8 * cols)
    return jnp.pad(flat, (0, padded - flat.size)).reshape(-1, cols)


def kernel(x, meta_tokens, mix_norm, ffn_norm, pool_w, pool_scale, kv_norm, w_kv, w_q, w_o, ffn_w_up, ffn_conv_w, ffn_conv_b, ffn_w_down, final_norm, loss_target, m_meta_tokens, m_mix_norm, m_ffn_norm, m_pool_w, m_pool_scale, m_kv_norm, m_w_kv, m_w_q, m_w_o, m_ffn_w_up, m_ffn_conv_w, m_ffn_conv_b, m_ffn_w_down, m_final_norm, v_meta_tokens, v_mix_norm, v_ffn_norm, v_pool_w, v_pool_scale, v_kv_norm, v_w_kv, v_w_q, v_w_o, v_ffn_w_up, v_ffn_conv_w, v_ffn_conv_b, v_ffn_w_down, v_final_norm):
    seq, d = x.shape[1], x.shape[2]
    l_real = N_META + seq
    lp = -(-l_real // ATT_BQ) * ATT_BQ
    f2 = ffn_w_up.shape[2] * N_CHIPS
    f = f2 // 2
    gd = d // N_GROUPS
    chip = 2 * lax.axis_index("x") + lax.axis_index("y")
    core = lax.axis_index("c")

    big_parts = [pool_w[0], w_kv, w_q[0], w_o[0], ffn_w_up, ffn_w_down]
    sizes = [p.size for p in big_parts]
    n_big = sum(sizes)
    cw = 1024
    assert n_big % (2 * 16 * cw) == 0
    rows_half = n_big // (2 * cw)
    offs = [0]
    for sz in sizes:
        offs.append(offs[-1] + sz)

    def shard_halves(parts):
        return _flat_rows([p.astype(BF16) for p in parts], cw).reshape(2, -1, cw)

    def unpack(gathered, parts):
        flat, out, first = gathered.reshape(N_CHIPS, -1), [], 0
        for p in parts:
            out.append(flat[:, first:first + p.size].reshape((N_CHIPS,) + p.shape))
            first += p.size
        return out

    now_parts = [pool_w[0], ffn_w_up[0], ffn_w_down[0]]
    attn_parts = [w_kv, w_q[0], w_o[0], ffn_w_down[1]]
    late_parts = [ffn_w_up[1]]
    wp_s, wup0_s, wdn0_s = unpack(_gather_shards(shard_halves(now_parts), name="gather_weights"), now_parts)
    wp_b = wp_s.transpose(1, 0, 2, 3).reshape(N_GROUPS, gd, gd)
    wup_b = [wup0_s.transpose(1, 0, 2).reshape(d, f2), None]
    wdn_b = [wdn0_s.reshape(f, d), None]

    small_parts = [meta_tokens, pool_scale, ffn_conv_w]
    ssizes = [p.size for p in small_parts]
    small = _flat_rows(small_parts, LANES)
    sall = _all_devices(small, False, name="gather_small")[::2].reshape(N_CHIPS, -1)
    meta_f = sall[:, :ssizes[0]].reshape(N_CHIPS, N_META, d // N_CHIPS).transpose(1, 0, 2).reshape(N_META, d)
    scale_f = sall[:, ssizes[0]:ssizes[0] + ssizes[1]].reshape(1, d)
    cw_f = sall[:, ssizes[0] + ssizes[1]:sum(ssizes)].reshape(N_CHIPS, 2, 3, f2 // N_CHIPS).transpose(1, 2, 0, 3).reshape(2, 3, f2)

    mix0, mix1 = mix_norm[0:1], mix_norm[1:2]
    fn0, fn1 = ffn_norm[0:1], ffn_norm[1:2]
    kvn_g = kv_norm.reshape(1, d)
    fin_g = final_norm.reshape(1, d)

    pad = lp - l_real
    h0 = jnp.concatenate([meta_f, x[0], jnp.zeros((pad, d), F32)], axis=0)
    tgt = jnp.pad(loss_target[0], ((N_META, pad), (0, 0)))

    _, r0 = _norm_fwd(h0, [mix0], name="norm_h0")
    diff_b = _pool_diff_fwd(h0, r0, mix0, name="pool_diff")
    h1 = _pool_mix_fwd(h0, diff_b, wp_b, scale_f, name="pool_mix")

    def ffn_fwd(h_in, layer, tag, gather_behind_up=None, gather_behind_gate=None):
        (fb,), rf = _norm_fwd(h_in, [ffn_norm[layer:layer + 1]], name=f"norm_ffn{tag}")
        u = _mm(fb, wup_b[layer], "nn", tm=1408, tn=512, name=f"ffn_up{tag}", gather=gather_behind_up)
        gathered = []
        if gather_behind_up is not None:
            u, g = u
            gathered.append(g)
        a = _convgate_fwd(u, cw_f[layer], ffn_conv_b[layer:layer + 1], name=f"convgate{tag}",
                          gather=gather_behind_gate)
        if gather_behind_gate is not None:
            a, g = a
            gathered.append(g)
        h_out = _mm(a, wdn_b[layer], "nn", tm=384, tn=1024, res=h_in, name=f"ffn_down{tag}")
        return h_out, (fb, rf, u, a), gathered

    h2, ffn0_saved, (attn_gathered, late_gathered) = ffn_fwd(h1, 0, "0", shard_halves(attn_parts),
                                                             shard_halves(late_parts))
    wkv_s, wq_s, wo_s, wdn1_s = unpack(attn_gathered, attn_parts)
    (wup1_s,) = unpack(late_gathered, late_parts)
    wkv_b = wkv_s.transpose(1, 0, 2).reshape(d, 2 * d)
    wq_b = wq_s.reshape(d, d)
    wo_b = wo_s.reshape(d, d)
    wup_b[1] = wup1_s.transpose(1, 0, 2).reshape(d, f2)
    wdn_b[1] = wdn1_s.reshape(f, d)
    (kvn_b, n1_b), r2 = _norm_fwd(h2, [kvn_g, mix1], name="norm_h2")
    kv_b = _mm(kvn_b, wkv_b, "nn", tm=1408, tn=512, out_dtype=BF16, name="kv_proj")
    q_b = _mm(n1_b, wq_b, "nn", tm=1408, tn=512, out_dtype=BF16, name="q_proj")

    def heads(t):
        return t.reshape(lp, N_HEADS, HEAD_DIM).transpose(1, 0, 2)

    def unheads(t):
        return t.transpose(1, 0, 2).reshape(lp, N_HEADS * HEAD_DIM)

    qh, kh, vh = heads(q_b), heads(kv_b[:, :d]), heads(kv_b[:, d:])
    oh, ltot = _attn_fwd(qh, kh, vh, name="attn_fwd")
    o_b = unheads(oh)
    h3 = _mm(o_b, wo_b, "nn", tm=384, tn=1024, res=h2, name="o_proj")
    h4, ffn1_saved, _ = ffn_fwd(h3, 1, "1")

    loss8, dh4, dh4_b, dfin = _loss_bwd(h4, fin_g, tgt, seq, name="loss")

    def ffn_bwd(h_in, layer, saved, dh_out, dh_out_b, tag):
        fb, rf, u, a = saved
        dwdn = _mm(a, dh_out_b, "tn", tm=256, tn=1024, name=f"d_wdown{tag}")
        da = _mm(dh_out_b, wdn_b[layer], "nt", tm=384, tn=f, name=f"d_act{tag}")
        du_g, du_v, dcw_g, dcw_v, dcb_g, dcb_v = _convgate_bwd(u, cw_f[layer], ffn_conv_b[layer:layer + 1], da,
                                                               name=f"convgate_bwd{tag}")
        dcw = jnp.concatenate([dcw_g, dcw_v], axis=1)
        dcb = jnp.concatenate([dcb_g, dcb_v], axis=1)
        half_chips = N_CHIPS // 2
        dwup = jnp.concatenate(
            [_mm(fb, du_g, "tn", tm=512, tn=f2 // N_CHIPS, out_shards=half_chips, name=f"d_wup_gate{tag}"),
             _mm(fb, du_v, "tn", tm=512, tn=f2 // N_CHIPS, out_shards=half_chips, name=f"d_wup_val{tag}")], axis=0)
        df = _mm(du_g, wup_b[layer][:, :f], "nt", tm=384, tn=1024, name=f"d_ffn_in_gate{tag}")
        df = _mm(du_v, wup_b[layer][:, f:], "nt", tm=384, tn=1024, res=df, name=f"d_ffn_in_val{tag}")
        dh_in, dh_in_b, (dfn,) = _norm_bwd(h_in, rf, [ffn_norm[layer:layer + 1]], [df], dh_out,
                                          name=f"norm_ffn_bwd{tag}")
        return dh_in, dh_in_b, dwup, dwdn, dcw, dcb, dfn

    dh3, dh3_b, dwup1, dwdn1, dcw1, dcb1, dfn1 = ffn_bwd(h3, 1, ffn1_saved, dh4, dh4_b, "1")

    dwo = _mm(o_b, dh3_b, "tn", tm=1024, tn=512, name="d_wo")
    do_b = _mm(dh3_b, wo_b, "nt", tm=384, tn=1024, out_dtype=BF16, name="d_attn_out")
    dqh, dkh, dvh = _attn_bwd(qh, kh, vh, heads(do_b), ltot, name="attn_bwd")
    dq_b = unheads(dqh).astype(BF16)
    dkv_b = jnp.concatenate([unheads(dkh), unheads(dvh)], axis=1).astype(BF16)
    dwq = _mm(n1_b, dq_b, "tn", tm=1024, tn=512, name="d_wq")
    dwkv = _mm(kvn_b, dkv_b, "tn", tm=1024, tn=512, out_shards=N_CHIPS, name="d_wkv")
    dn1 = _mm(dq_b, wq_b, "nt", tm=384, tn=1024, name="d_n1")
    dkvn = _mm(dkv_b, wkv_b, "nt", tm=384, tn=1024, name="d_kvn")
    dh2, dh2_b, (dmix1, dkvg) = _norm_bwd(h2, r2, [mix1, kvn_g], [dn1, dkvn], dh3, name="norm_h2_bwd")

    dh1, dh1_b, dwup0, dwdn0, dcw0, dcb0, dfn0 = ffn_bwd(h1, 0, ffn0_saved, dh2, dh2_b, "0")

    ddiff, dwp, dscale = _pool_mix_bwd(diff_b, wp_b, scale_f, dh1, name="pool_mix_bwd")
    dn0 = _pool_diff_bwd(ddiff, name="pool_diff_bwd")
    dh0, _, (dmix0,) = _norm_bwd(h0, r0, [mix0], [dn0], dh1, name="norm_h0_bwd")

    grad_x = dh0[N_META:l_real][None]
    dmeta = dh0[:N_META]

    g_parts = [
        dwp.reshape(N_GROUPS, N_CHIPS, gd // N_CHIPS, gd).transpose(1, 0, 2, 3),
        dwkv,
        dwq.reshape(N_CHIPS, d // N_CHIPS, d),
        dwo.reshape(N_CHIPS, d // N_CHIPS, d),
        dwup0, dwup1,
        dwdn0.reshape(N_CHIPS, f // N_CHIPS, d), dwdn1.reshape(N_CHIPS, f // N_CHIPS, d),
    ]
    g_all = jnp.concatenate([p.reshape(N_CHIPS, -1) for p in g_parts], axis=1).reshape(N_CHIPS, 2, rows_half, cw)
    g_mine = lax.dynamic_index_in_dim(g_all, core, axis=1, keepdims=False)
    g_other = lax.dynamic_index_in_dim(g_all, 1 - core, axis=1, keepdims=False)
    from_sibling = _to_sibling(g_other.astype(BF16), name="grads_to_sibling")
    chip_part = _add_to_bf16(g_mine, from_sibling, name="grads_chip_sum")
    half = _sum_slots(_scatter_rows(chip_part, name="grads_scatter"), name="grads_sum")
    other_half = _to_sibling(half, name="grads_join")
    g_shard = jnp.stack([jnp.where(core == 0, half, other_half),
                         jnp.where(core == 0, other_half, half)]).reshape(n_big)

    def gbig(i, shape):
        return g_shard[offs[i]:offs[i + 1]].reshape(shape)

    g_pool_w = gbig(0, pool_w.shape)
    g_w_kv = gbig(1, w_kv.shape)
    g_w_q = gbig(2, w_q.shape)
    g_w_o = gbig(3, w_o.shape)
    g_w_up = gbig(4, ffn_w_up.shape)
    g_w_dn = gbig(5, ffn_w_down.shape)

    dcw = jnp.stack([dcw0, dcw1])
    dcb = jnp.concatenate([dcb0, dcb1], axis=0)
    sg_parts = [jnp.concatenate([dmix0, dmix1], axis=0), jnp.concatenate([dfn0, dfn1], axis=0), dkvg, dfin, dcb,
                dmeta, dscale, dcw, loss8]
    sg_sizes = [p.size for p in sg_parts]
    sg = _all_devices(_flat_rows(sg_parts, LANES), True, name="reduce_small").reshape(-1)
    sg_offs = [0]
    for sz in sg_sizes:
        sg_offs.append(sg_offs[-1] + sz)

    def gsmall(i, shape):
        return sg[sg_offs[i]:sg_offs[i + 1]].reshape(shape)

    g_mix = gsmall(0, mix_norm.shape)
    g_ffn_norm = gsmall(1, ffn_norm.shape)
    g_kv_norm = gsmall(2, kv_norm.shape)
    g_final = gsmall(3, final_norm.shape)
    g_conv_b = gsmall(4, ffn_conv_b.shape)
    csh = d // N_CHIPS
    g_meta = lax.dynamic_slice_in_dim(gsmall(5, (N_META, d)), chip * csh, csh, axis=1)
    g_scale = lax.dynamic_slice_in_dim(gsmall(6, (1, d)), chip * csh, csh, axis=1)
    fsh = f2 // N_CHIPS
    g_conv_w = lax.dynamic_slice_in_dim(gsmall(7, (2, 3, f2)), chip * fsh, fsh, axis=2)
    loss = gsmall(8, (8 * LANES,))[0]

    weights = [meta_tokens, mix_norm, ffn_norm, pool_w, pool_scale, kv_norm, w_kv, w_q, w_o, ffn_w_up, ffn_conv_w,
               ffn_conv_b, ffn_w_down, final_norm]
    grads = [g_meta, g_mix, g_ffn_norm, g_pool_w, g_scale, g_kv_norm, g_w_kv, g_w_q, g_w_o, g_w_up, g_conv_w,
             g_conv_b, g_w_dn, g_final]
    ms = [m_meta_tokens, m_mix_norm, m_ffn_norm, m_pool_w, m_pool_scale, m_kv_norm, m_w_kv, m_w_q, m_w_o,
          m_ffn_w_up, m_ffn_conv_w, m_ffn_conv_b, m_ffn_w_down, m_final_norm]
    vs = [v_meta_tokens, v_mix_norm, v_ffn_norm, v_pool_w, v_pool_scale, v_kv_norm, v_w_kv, v_w_q, v_w_o,
          v_ffn_w_up, v_ffn_conv_w, v_ffn_conv_b, v_ffn_w_down, v_final_norm]
    names = ["meta", "mix", "ffnnorm", "poolw", "poolscale", "kvnorm", "wkv", "wq", "wo", "wup", "convw", "convb",
             "wdown", "final"]
    deltas, new_ms, new_vs = [], [], []
    for w, g, m, v, nm in zip(weights, grads, ms, vs, names):
        cols = w.shape[-1]
        if w.size % (8 * LANES) == 0 and w.ndim == 1:
            cols = LANES
        view = (w.size // cols, cols)
        dl, mn, vn = _adamw(w.reshape(view), g.reshape(view), m.reshape(view), v.reshape(view), name=f"adamw_{nm}")
        deltas.append(dl.reshape(w.shape))
        new_ms.append(mn.reshape(w.shape))
        new_vs.append(vn.reshape(w.shape))

    return (loss, grad_x, *grads, *deltas, *new_ms, *new_vs)
```

```python
import jax
import jax.numpy as jnp
from jax import lax
from jax.experimental import pallas as pl
from jax.experimental.pallas import tpu as pltpu

F32 = jnp.float32
BF16 = jnp.bfloat16

N_META = 16
N_HEADS = 16
HEAD_DIM = 64
POOL_WINDOWS = (2, 4, 8, 16)
N_GROUPS = 4
RMS_EPS = 1e-6
ADAM_LR = 0.001
ADAM_B1 = 0.9
ADAM_B2 = 0.999
ADAM_EPS = 1e-08
ADAM_WD = 0.01
ADAM_STEP = 10

LANES = 128
BLK = 128
ATT_CHUNKS = 3
ATT_BQ = ATT_CHUNKS * BLK
CONV_ROWS = 128
N_CHIPS = 4
N_DEV = 8
VMEM_LIMIT = 56 * 1024 * 1024
MESH = pl.DeviceIdType.MESH


def _cparams(*sem):
    return pltpu.CompilerParams(dimension_semantics=tuple(sem) if sem else None,
                                vmem_limit_bytes=VMEM_LIMIT)


def _tile(n, pref):
    best = None
    t = LANES
    while t <= min(n, pref):
        if n % t == 0:
            best = t
        t += LANES
    assert best is not None, (n, pref)
    return best


def _row_tile(r, pref, mult):
    best = r
    for t in range(mult, min(r, pref) + 1, mult):
        if r % t == 0:
            best = t
    return best


_DIMS = {
    "nn": (((1,), (0,)), ((), ())),
    "nt": (((1,), (1,)), ((), ())),
    "tn": (((0,), (0,)), ((), ())),
}


def _mm(a, b, dims, *, tm, tn, name, out_dtype=F32, res=None, out_shards=1, gather=None):
    if dims == "tn":
        k, m = a.shape
    else:
        m, k = a.shape
    n = b.shape[0] if dims == "nt" else b.shape[1]
    tm = _tile(m, tm)
    tn = _tile(n // out_shards, tn)
    a_spec = pl.BlockSpec((k, tm), lambda i, j: (0, i)) if dims == "tn" else pl.BlockSpec((tm, k), lambda i, j: (i, 0))
    b_spec = pl.BlockSpec((tn, k), lambda i, j: (j, 0)) if dims == "nt" else pl.BlockSpec((k, tn), lambda i, j: (0, j))
    o_spec = pl.BlockSpec((tm, tn), lambda i, j: (i, j))
    out_shape = jax.ShapeDtypeStruct((m, n), out_dtype)
    if out_shards > 1:
        assert res is None
        per = n // out_shards // tn
        o_spec = pl.BlockSpec((None, tm, tn), lambda i, j: (j // per, i, j % per))
        out_shape = jax.ShapeDtypeStruct((out_shards, m, n // out_shards), out_dtype)
    dn = _DIMS[dims]

    grid = (m // tm, n // tn)

    def body(*refs):
        refs = list(refs)
        a_ref, b_ref = refs[:2]
        r_ref = refs[2] if res is not None else None
        o_ref = refs[2 + (res is not None) + (gather is not None)]
        acc = lax.dot_general(a_ref[...], b_ref[...], dn, preferred_element_type=F32)
        if res is not None:
            acc = acc + r_ref[...]
        o_ref[...] = acc.astype(out_dtype)
        if gather is not None:
            w_ref, g_ref, stage_ref, send_sems, recv_sems = refs[2 + (res is not None)], *refs[-4:]
            step = pl.program_id(0) * grid[1] + pl.program_id(1)

            @pl.when(step == 0)
            def _():
                _gather_begin(w_ref, g_ref, send_sems, recv_sems)

            @pl.when(step == grid[0] * grid[1] - 1)
            def _():
                _gather_end(w_ref, g_ref, stage_ref, send_sems, recv_sems)

    ins = [a, b] + ([] if res is None else [res])
    specs = [a_spec, b_spec] + ([] if res is None else [o_spec])
    if gather is None:
        return pl.pallas_call(
            body, name=name, grid=grid, in_specs=specs, out_specs=o_spec, out_shape=out_shape,
            compiler_params=_cparams("parallel", "arbitrary"),
        )(*ins)
    g_shape, scratch = _gather_out_and_scratch(gather)
    return pl.pallas_call(
        body, name=name, grid=grid, in_specs=specs + [_ANY], out_specs=[o_spec, _ANY],
        out_shape=[out_shape, g_shape], scratch_shapes=scratch,
        compiler_params=_cparams("arbitrary", "arbitrary"),
    )(*ins, gather)


def _norm_fwd(x, gains, *, name):
    l, d = x.shape
    tr = _tile(l, 384)
    ng = len(gains)

    def body(*refs):
        x_ref = refs[0]
        g_refs = refs[1:1 + ng]
        o_refs = refs[1 + ng:1 + 2 * ng]
        r_ref = refs[1 + 2 * ng]
        xv = x_ref[...]
        r = lax.rsqrt(jnp.mean(xv * xv, axis=-1, keepdims=True) + RMS_EPS)
        xn = xv * r
        for g_ref, o_ref in zip(g_refs, o_refs):
            o_ref[...] = (xn * g_ref[...]).astype(BF16)
        r_ref[...] = r

    row = pl.BlockSpec((tr, d), lambda i: (i, 0))
    gspec = pl.BlockSpec((1, d), lambda i: (0, 0))
    outs = pl.pallas_call(
        body, name=name, grid=(l // tr,),
        in_specs=[row] + [gspec] * ng,
        out_specs=[row] * ng + [pl.BlockSpec((tr, 1), lambda i: (i, 0))],
        out_shape=[jax.ShapeDtypeStruct((l, d), BF16)] * ng + [jax.ShapeDtypeStruct((l, 1), F32)],
        compiler_params=_cparams("parallel"),
    )(x, *gains)
    return list(outs[:ng]), outs[ng]


def _rows8(v):
    r, d = v.shape
    return jnp.sum(v.reshape(r // 8, 8, d), axis=0)


def _norm_bwd(x, r, gains, dns, dres, *, name):
    l, d = x.shape
    tr = _tile(l, 384)
    ng = len(gains)
    nsteps = l // tr

    def body(*refs):
        x_ref, r_ref, dres_ref = refs[0], refs[1], refs[2]
        g_refs = refs[3:3 + ng]
        dn_refs = refs[3 + ng:3 + 2 * ng]
        dx_ref, dxb_ref = refs[3 + 2 * ng], refs[4 + 2 * ng]
        dg_refs = refs[5 + 2 * ng:5 + 3 * ng]
        acc_refs = refs[5 + 3 * ng:5 + 4 * ng]
        i = pl.program_id(0)

        @pl.when(i == 0)
        def _():
            for acc in acc_refs:
                acc[...] = jnp.zeros_like(acc)

        rv = r_ref[...]
        xn = x_ref[...] * rv
        total = dres_ref[...]
        for g_ref, dn_ref, acc in zip(g_refs, dn_refs, acc_refs):
            dn = dn_ref[...]
            acc[...] += _rows8(dn * xn)
            dxn = dn * g_ref[...]
            total = total + rv * (dxn - xn * jnp.mean(dxn * xn, axis=-1, keepdims=True))
        dx_ref[...] = total
        dxb_ref[...] = total.astype(BF16)

        @pl.when(i == nsteps - 1)
        def _():
            for dg_ref, acc in zip(dg_refs, acc_refs):
                dg_ref[...] = jnp.sum(acc[...], axis=0, keepdims=True)

    row = pl.BlockSpec((tr, d), lambda i: (i, 0))
    gspec = pl.BlockSpec((1, d), lambda i: (0, 0))
    outs = pl.pallas_call(
        body, name=name, grid=(nsteps,),
        in_specs=[row, pl.BlockSpec((tr, 1), lambda i: (i, 0)), row] + [gspec] * ng + [row] * ng,
        out_specs=[row, row] + [gspec] * ng,
        out_shape=[jax.ShapeDtypeStruct((l, d), F32), jax.ShapeDtypeStruct((l, d), BF16)]
        + [jax.ShapeDtypeStruct((1, d), F32)] * ng,
        scratch_shapes=[pltpu.VMEM((8, d), F32)] * ng,
        compiler_params=_cparams("arbitrary"),
    )(x, r, dres, *gains, *dns)
    return outs[0], outs[1], list(outs[2:])


def _loss_bwd(h, gain, tgt, seq, *, name):
    l, d = h.shape
    tr = _tile(l, 384)
    nsteps = l // tr

    def body(h_ref, g_ref, t_ref, loss_ref, dh_ref, dhb_ref, dg_ref, lacc, gacc):
        i = pl.program_id(0)

        @pl.when(i == 0)
        def _():
            lacc[...] = jnp.zeros_like(lacc)
            gacc[...] = jnp.zeros_like(gacc)

        xv = h_ref[...]
        g = g_ref[...]
        r = lax.rsqrt(jnp.mean(xv * xv, axis=-1, keepdims=True) + RMS_EPS)
        xn = xv * r
        rows = i * tr + lax.broadcasted_iota(jnp.int32, (tr, 1), 0)
        valid = (rows >= N_META) & (rows < N_META + seq)
        e = jnp.where(valid, xn * g - t_ref[...], 0.0)
        lacc[...] += _rows8(e * e)
        dy = e * (1.0 / d)
        gacc[...] += _rows8(dy * xn)
        dxn = dy * g
        dx = r * (dxn - xn * jnp.mean(dxn * xn, axis=-1, keepdims=True))
        dh_ref[...] = dx
        dhb_ref[...] = dx.astype(BF16)

        @pl.when(i == nsteps - 1)
        def _():
            loss_ref[...] = jnp.full((8, LANES), 0.5 / d * jnp.sum(lacc[...]), F32)
            dg_ref[...] = jnp.sum(gacc[...], axis=0, keepdims=True)

    row = pl.BlockSpec((tr, d), lambda i: (i, 0))
    gspec = pl.BlockSpec((1, d), lambda i: (0, 0))
    return pl.pallas_call(
        body, name=name, grid=(nsteps,),
        in_specs=[row, gspec, row],
        out_specs=[pl.BlockSpec((8, LANES), lambda i: (0, 0)), row, row, gspec],
        out_shape=[jax.ShapeDtypeStruct((8, LANES), F32), jax.ShapeDtypeStruct((l, d), F32),
                   jax.ShapeDtypeStruct((l, d), BF16), jax.ShapeDtypeStruct((1, d), F32)],
        scratch_shapes=[pltpu.VMEM((8, d), F32), pltpu.VMEM((8, d), F32)],
        compiler_params=_cparams("arbitrary"),
    )(h, gain, tgt)


def _shift_down(v, k, rows):
    return jnp.where(rows >= k, pltpu.roll(v, k, 0), 0.0)


def _shift_up(v, k, rows):
    l = v.shape[0]
    return jnp.where(rows < l - k, pltpu.roll(v, l - k, 0), 0.0)


def _pool_diff(n, w, rows):
    s = n
    for k in (1, 2, 4, 8):
        s = s + jnp.where(k < w, _shift_down(s, k, rows), 0.0)
    cnt = jnp.minimum(rows + 1, w).astype(F32)
    return s / cnt - n, cnt


def _pool_diff_fwd(h, r, gain, *, name):
    l, d = h.shape
    per_group = d // N_GROUPS // LANES

    def body(h_ref, r_ref, g_ref, o_ref):
        w = jnp.left_shift(2, pl.program_id(0) // per_group)
        rows = lax.broadcasted_iota(jnp.int32, (l, 1), 0)
        n = h_ref[...] * r_ref[...] * g_ref[...]
        diff, _ = _pool_diff(n, w, rows)
        o_ref[...] = diff.astype(BF16)

    col = pl.BlockSpec((l, LANES), lambda j: (0, j))
    return pl.pallas_call(
        body, name=name, grid=(d // LANES,),
        in_specs=[col, pl.BlockSpec((l, 1), lambda j: (0, 0)), pl.BlockSpec((1, LANES), lambda j: (0, j))],
        out_specs=col, out_shape=jax.ShapeDtypeStruct((l, d), BF16),
        compiler_params=_cparams("parallel"),
    )(h, r, gain)


def _pool_diff_bwd(ddiff, *, name):
    l, d = ddiff.shape
    per_group = d // N_GROUPS // LANES

    def body(dd_ref, o_ref):
        w = jnp.left_shift(2, pl.program_id(0) // per_group)
        rows = lax.broadcasted_iota(jnp.int32, (l, 1), 0)
        dd = dd_ref[...]
        s = dd / jnp.minimum(rows + 1, w).astype(F32)
        for k in (1, 2, 4, 8):
            s = s + jnp.where(k < w, _shift_up(s, k, rows), 0.0)
        o_ref[...] = s - dd

    col = pl.BlockSpec((l, LANES), lambda j: (0, j))
    return pl.pallas_call(
        body, name=name, grid=(d // LANES,), in_specs=[col], out_specs=col,
        out_shape=jax.ShapeDtypeStruct((l, d), F32), compiler_params=_cparams("parallel"),
    )(ddiff)


def _pool_mix_fwd(h, diff_b, w_pool, scale, *, name):
    l, d = h.shape
    gd = d // N_GROUPS
    tr = _tile(l, 1408)

    def body(h_ref, d_ref, w_ref, s_ref, o_ref):
        y = jnp.dot(d_ref[...], w_ref[0], preferred_element_type=F32)
        o_ref[...] = h_ref[...] + y * s_ref[...]

    blk = pl.BlockSpec((tr, gd), lambda i, g: (i, g))
    vec = pl.BlockSpec((1, gd), lambda i, g: (0, g))
    return pl.pallas_call(
        body, name=name, grid=(l // tr, N_GROUPS),
        in_specs=[blk, blk, pl.BlockSpec((1, gd, gd), lambda i, g: (g, 0, 0)), vec],
        out_specs=blk, out_shape=jax.ShapeDtypeStruct((l, d), F32),
        compiler_params=_cparams("parallel", "parallel"),
    )(h, diff_b, w_pool, scale)


def _pool_mix_bwd(diff_b, w_pool, scale, dh1, *, name):
    l, d = dh1.shape
    gd = d // N_GROUPS
    tr = _tile(l, 1408)
    nsteps = l // tr

    def body(d_ref, w_ref, s_ref, dy_ref, dd_ref, dw_ref, ds_ref, sacc):
        i = pl.program_id(1)

        @pl.when(i == 0)
        def _():
            dw_ref[...] = jnp.zeros_like(dw_ref)
            sacc[...] = jnp.zeros_like(sacc)

        diff_b_, wg, dy = d_ref[...], w_ref[0], dy_ref[...]
        yy = jnp.dot(diff_b_, wg, preferred_element_type=F32)
        sacc[...] += _rows8(dy * yy)
        dyy_b = (dy * s_ref[...]).astype(BF16)
        dw_ref[0] += lax.dot_general(diff_b_, dyy_b, _DIMS["tn"], preferred_element_type=F32)
        dd_ref[...] = lax.dot_general(dyy_b, wg, _DIMS["nt"], preferred_element_type=F32)

        @pl.when(i == nsteps - 1)
        def _():
            ds_ref[...] = jnp.sum(sacc[...], axis=0, keepdims=True)

    blk = pl.BlockSpec((tr, gd), lambda g, i: (i, g))
    vec = pl.BlockSpec((1, gd), lambda g, i: (0, g))
    wspec = pl.BlockSpec((1, gd, gd), lambda g, i: (g, 0, 0))
    return pl.pallas_call(
        body, name=name, grid=(N_GROUPS, nsteps),
        in_specs=[blk, wspec, vec, blk],
        out_specs=[blk, wspec, vec],
        out_shape=[jax.ShapeDtypeStruct((l, d), F32), jax.ShapeDtypeStruct((N_GROUPS, gd, gd), F32),
                   jax.ShapeDtypeStruct((1, d), F32)],
        scratch_shapes=[pltpu.VMEM((8, gd), F32)],
        compiler_params=_cparams("parallel", "arbitrary"),
    )(diff_b, w_pool, scale, dh1)


def _conv_chunk(cur, prev, w, b, rowi):
    s1 = jnp.where(rowi < 1, pltpu.roll(prev, 1, 0), pltpu.roll(cur, 1, 0))
    s2 = jnp.where(rowi < 2, pltpu.roll(prev, 2, 0), pltpu.roll(cur, 2, 0))
    return b + w[0] * s2 + w[1] * s1 + w[2] * cur, s1, s2


def _bcast_rows(ref, rows):
    v = ref[...]
    return [jnp.broadcast_to(v[k:k + 1], (rows, v.shape[1])) for k in range(v.shape[0])]


def _convgate_fwd(u, cw, cb, *, name, gather=None):
    l, f2 = u.shape
    f = f2 // 2
    tc = _tile(f, LANES)
    nc = f // tc
    rows = CONV_ROWS
    assert l % rows == 0

    def body(ug_ref, uv_ref, wg_ref, wv_ref, bg_ref, bv_ref, *rest):
        a_ref = rest[0] if gather is None else rest[1]
        if gather is not None:
            w_ref, _, g_ref, stage_ref, send_sems, recv_sems = rest

            @pl.when(pl.program_id(0) == 0)
            def _():
                _gather_begin(w_ref, g_ref, send_sems, recv_sems)

        rowi = lax.broadcasted_iota(jnp.int32, (rows, tc), 0)
        wg, wv = _bcast_rows(wg_ref, rows), _bcast_rows(wv_ref, rows)
        bg, bv = _bcast_rows(bg_ref, rows)[0], _bcast_rows(bv_ref, rows)[0]

        def chunk(i, carry):
            pg, pv = carry
            r = pl.multiple_of(i * rows, rows)
            cg, cv = ug_ref[pl.ds(r, rows), :], uv_ref[pl.ds(r, rows), :]
            gate, _, _ = _conv_chunk(cg, pg, wg, bg, rowi)
            val, _, _ = _conv_chunk(cv, pv, wv, bv, rowi)
            a_ref[pl.ds(r, rows), :] = (gate * jax.nn.sigmoid(gate) * val).astype(BF16)
            return cg, cv

        zero = jnp.zeros((rows, tc), F32)
        lax.fori_loop(0, l // rows, chunk, (zero, zero))
        if gather is not None:
            @pl.when(pl.program_id(0) == nc - 1)
            def _():
                _gather_end(w_ref, g_ref, stage_ref, send_sems, recv_sems)

    def spec(rows_, off):
        return pl.BlockSpec((rows_, tc), lambda j: (0, j + off))

    in_specs = [spec(l, 0), spec(l, nc), spec(3, 0), spec(3, nc), spec(1, 0), spec(1, nc)]
    out_shape = jax.ShapeDtypeStruct((l, f), BF16)
    if gather is None:
        return pl.pallas_call(
            body, name=name, grid=(nc,), in_specs=in_specs, out_specs=spec(l, 0), out_shape=out_shape,
            compiler_params=_cparams("parallel"),
        )(u, u, cw, cw, cb, cb)
    g_shape, scratch = _gather_out_and_scratch(gather)
    return pl.pallas_call(
        body, name=name, grid=(nc,), in_specs=in_specs + [_ANY], out_specs=[spec(l, 0), _ANY],
        out_shape=[out_shape, g_shape], scratch_shapes=scratch,
        compiler_params=_cparams("arbitrary"),
    )(u, u, cw, cw, cb, cb, gather)


def _convgate_bwd(u, cw, cb, da, *, name, scatter=None):
    l, f2 = u.shape
    f = f2 // 2
    tc = _tile(f, LANES)
    nc = f // tc
    rows = CONV_ROWS
    assert l % rows == 0
    nchunks = l // rows

    def body(ug_ref, uv_ref, wg_ref, wv_ref, bg_ref, bv_ref, da_ref, *rest):
        if scatter is None:
            dug_ref, duv_ref, dcwg_ref, dcwv_ref, dcbg_ref, dcbv_ref = rest
        else:
            p_ref, dug_ref, duv_ref, dcwg_ref, dcwv_ref, dcbg_ref, dcbv_ref, s_ref, stage_ref, ssems, rsems = rest

            @pl.when(pl.program_id(0) == 0)
            def _():
                _scatter_begin(p_ref, s_ref, ssems, rsems)

        rowi = lax.broadcasted_iota(jnp.int32, (rows, tc), 0)
        wg, wv = _bcast_rows(wg_ref, rows), _bcast_rows(wv_ref, rows)
        bg, bv = _bcast_rows(bg_ref, rows)[0], _bcast_rows(bv_ref, rows)[0]

        def tap_sums(acc, dc, taps):
            return (acc[0] + _rows8(dc),) + tuple(a + _rows8(dc * t) for a, t in zip(acc[1:], taps))

        def input_grad(dc, dc_next, w):
            up1 = jnp.where(rowi >= rows - 1, pltpu.roll(dc_next, rows - 1, 0), pltpu.roll(dc, rows - 1, 0))
            up2 = jnp.where(rowi >= rows - 2, pltpu.roll(dc_next, rows - 2, 0), pltpu.roll(dc, rows - 2, 0))
            return (w[2] * dc + w[1] * up1 + w[0] * up2).astype(BF16)

        def chunk(ii, carry):
            dcg_next, dcv_next, acc_g, acc_v = carry
            i = nchunks - 1 - ii
            r = pl.multiple_of(i * rows, rows)
            rp = pl.multiple_of(jnp.maximum(i - 1, 0) * rows, rows)
            cg, cv = ug_ref[pl.ds(r, rows), :], uv_ref[pl.ds(r, rows), :]
            pg = jnp.where(i > 0, ug_ref[pl.ds(rp, rows), :], 0.0)
            pv = jnp.where(i > 0, uv_ref[pl.ds(rp, rows), :], 0.0)
            gate, g1, g2 = _conv_chunk(cg, pg, wg, bg, rowi)
            val, v1, v2 = _conv_chunk(cv, pv, wv, bv, rowi)
            sg = jax.nn.sigmoid(gate)
            dav = da_ref[pl.ds(r, rows), :]
            dcg = dav * val * (sg * (1.0 + gate * (1.0 - sg)))
            dcv = dav * (gate * sg)
            acc_g = tap_sums(acc_g, dcg, (g2, g1, cg))
            acc_v = tap_sums(acc_v, dcv, (v2, v1, cv))
            dug_ref[pl.ds(r, rows), :] = input_grad(dcg, dcg_next, wg)
            duv_ref[pl.ds(r, rows), :] = input_grad(dcv, dcv_next, wv)
            return dcg, dcv, acc_g, acc_v

        zero = jnp.zeros((rows, tc), F32)
        zero8 = (jnp.zeros((8, tc), F32),) * 4
        _, _, acc_g, acc_v = lax.fori_loop(0, nchunks, chunk, (zero, zero, zero8, zero8))
        for acc, dcw_ref, dcb_ref in ((acc_g, dcwg_ref, dcbg_ref), (acc_v, dcwv_ref, dcbv_ref)):
            dcb_ref[...] = jnp.sum(acc[0], axis=0, keepdims=True)
            for k in range(3):
                dcw_ref[k:k + 1, :] = jnp.sum(acc[1 + k], axis=0, keepdims=True)
        if scatter is not None:
            @pl.when(pl.program_id(0) == nc - 1)
            def _():
                _scatter_end(p_ref, s_ref, stage_ref, ssems, rsems)

    def spec(rows_, off):
        return pl.BlockSpec((rows_, tc), lambda j: (0, j + off))

    in_specs = [spec(l, 0), spec(l, nc), spec(3, 0), spec(3, nc), spec(1, 0), spec(1, nc), spec(l, 0)]
    out_specs = [spec(l, 0), spec(l, 0), spec(3, 0), spec(3, 0), spec(1, 0), spec(1, 0)]
    out_shape = ([jax.ShapeDtypeStruct((l, f), BF16)] * 2 + [jax.ShapeDtypeStruct((3, f), F32)] * 2
                 + [jax.ShapeDtypeStruct((1, f), F32)] * 2)
    if scatter is None:
        return pl.pallas_call(
            body, name=name, grid=(nc,), in_specs=in_specs, out_specs=out_specs, out_shape=out_shape,
            compiler_params=_cparams("parallel"),
        )(u, u, cw, cw, cb, cb, da)
    s_shape, scratch = _scatter_out_and_scratch(scatter)
    return pl.pallas_call(
        body, name=name, grid=(nc,), in_specs=in_specs + [_ANY], out_specs=out_specs + [_ANY],
        out_shape=out_shape + [s_shape], scratch_shapes=scratch,
        compiler_params=_cparams("arbitrary"),
    )(u, u, cw, cw, cb, cb, da, scatter)


def _cumsum_mm(v_b, t2):
    return jnp.dot(v_b, t2, preferred_element_type=F32)


def _tri_and_ones(tri_fn):
    row = lax.broadcasted_iota(jnp.int32, (BLK, 2 * BLK), 0)
    col = lax.broadcasted_iota(jnp.int32, (BLK, 2 * BLK), 1)
    return jnp.where((col >= BLK) | tri_fn(row, col), 1.0, 0.0).astype(BF16)


def _logits(z, mask):
    sp = jnp.log(1.0 + jnp.exp(-jnp.abs(z)))
    lb = jnp.minimum(z, 0.0) - sp
    lm = lb - z
    if mask is not None:
        lm = jnp.where(mask, lm, 0.0)
    return lb, lm


def _software_pipeline(stages, n, block_of, state):
    ns = len(stages)
    inflight = [None] * (ns - 1)
    for t in range(ns - 1):
        new = list(inflight)
        for s in range(t, -1, -1):
            y, state = stages[s](block_of(t - s), None if s == 0 else inflight[s - 1], state)
            new[s] = y
        inflight = new

    def steady(i, carry):
        inflight, state = carry
        new = [None] * (ns - 1)
        for s in range(ns - 1, -1, -1):
            y, state = stages[s](block_of(i + ns - 1 - s), None if s == 0 else inflight[s - 1], state)
            if s < ns - 1:
                new[s] = y
        return tuple(new), state

    inflight, state = lax.fori_loop(0, n - (ns - 1), steady, (tuple(inflight), state))
    inflight = list(inflight)
    for e in range(1, ns):
        new = list(inflight)
        for s in range(ns - 1, e - 1, -1):
            y, state = stages[s](block_of(n - 1 + e - s), inflight[s - 1], state)
            if s < ns - 1:
                new[s] = y
        inflight = new
    return state


def _attn_fwd(q, k, v, *, name):
    nh, l, dh = q.shape
    assert l % ATT_BQ == 0
    nq = l // ATT_BQ
    qscale = HEAD_DIM ** -0.5
    chunks = range(ATT_CHUNKS)

    def body(q_ref, k_ref, v_ref, o_ref, lt_ref, z_scr, cs_scr, pv_scr):
        t_later = _tri_and_ones(lambda r, c: r > c)
        dmask = (lax.broadcasted_iota(jnp.int32, (BLK, BLK), 1)
                 < lax.broadcasted_iota(jnp.int32, (BLK, BLK), 0))

        def logits(qc, c0, mask):
            z = lax.dot_general(qc, k_ref[0, pl.ds(c0, BLK), :], _DIMS["nt"], preferred_element_type=F32)
            lb, lm = _logits(z, mask)
            return lb, lm.astype(BF16)

        def weights(lb, lm_b, mask, run):
            cs = _cumsum_mm(lm_b, t_later)
            a = jnp.exp(lb + cs[:, :BLK] + run)
            if mask is not None:
                a = jnp.where(mask, a, 0.0)
            return a.astype(BF16), run + cs[:, BLK:]

        def accumulate(a_b, c0, acc):
            return acc + jnp.dot(a_b, v_ref[0, pl.ds(c0, BLK), :], preferred_element_type=F32)

        def qblock(qb, _):
            r0 = pl.multiple_of(qb * ATT_BQ, ATT_BQ)
            qs = [q_ref[0, pl.ds(r0 + rc * BLK, BLK), :] * jnp.asarray(qscale, BF16) for rc in chunks]
            accs = [jnp.zeros((BLK, dh), F32)] * ATT_CHUNKS
            runs = [jnp.zeros((BLK, BLK), F32)] * ATT_CHUNKS
            for dj in range(ATT_CHUNKS - 1, -1, -1):
                c0 = r0 + dj * BLK
                for rc in range(dj, ATT_CHUNKS):
                    mask = dmask if rc == dj else None
                    a_b, runs[rc] = weights(*logits(qs[rc], c0, mask), mask, runs[rc])
                    accs[rc] = accumulate(a_b, c0, accs[rc])

            def col0(b):
                return pl.multiple_of(r0 - (b + 1) * BLK, BLK)

            def stage_scores(b, _, state):
                for rc in chunks:
                    z_scr[rc] = lax.dot_general(qs[rc], k_ref[0, pl.ds(col0(b), BLK), :], _DIMS["nt"],
                                                preferred_element_type=F32)
                return (), state

            def stage_cumsum(b, _, state):
                lbs = []
                for rc in chunks:
                    lb, lm = _logits(z_scr[rc], None)
                    cs_scr[rc] = _cumsum_mm(lm.astype(BF16), t_later)
                    lbs.append(lb)
                return tuple(lbs), state

            def stage_weights(b, lbs, state):
                accs, runs = state
                new_runs = []
                for rc in chunks:
                    cs = cs_scr[rc]
                    a = jnp.exp(lbs[rc] + cs[:, :BLK] + runs[rc])
                    pv_scr[rc] = jnp.dot(a.astype(BF16), v_ref[0, pl.ds(col0(b), BLK), :],
                                         preferred_element_type=F32)
                    new_runs.append(runs[rc] + cs[:, BLK:])
                return (), (accs, tuple(new_runs))

            def stage_acc(b, _, state):
                accs, runs = state
                return None, (tuple(accs[rc] + pv_scr[rc] for rc in chunks), runs)

            def left_region(state):
                return _software_pipeline([stage_scores, stage_cumsum, stage_weights, stage_acc],
                                          qb * ATT_CHUNKS, lambda b: b, state)

            accs, runs = lax.cond(qb > 0, left_region, lambda s: s, (tuple(accs), tuple(runs)))
            for rc in chunks:
                o_ref[0, pl.ds(r0 + rc * BLK, BLK), :] = accs[rc].astype(BF16)
                lt_ref[0, pl.ds(r0 + rc * BLK, BLK), :] = runs[rc]
            return 0

        lax.fori_loop(0, nq, qblock, 0)

    head = pl.BlockSpec((1, l, dh), lambda h: (h, 0, 0))
    return pl.pallas_call(
        body, name=name, grid=(nh,),
        in_specs=[head, head, head],
        out_specs=[head, pl.BlockSpec((1, l, BLK), lambda h: (h, 0, 0))],
        out_shape=[jax.ShapeDtypeStruct((nh, l, dh), BF16), jax.ShapeDtypeStruct((nh, l, BLK), F32)],
        scratch_shapes=[pltpu.VMEM((ATT_CHUNKS, BLK, BLK), F32), pltpu.VMEM((ATT_CHUNKS, BLK, 2 * BLK), F32),
                        pltpu.VMEM((ATT_CHUNKS, BLK, dh), F32)],
        compiler_params=_cparams("parallel"),
    )(q, k, v)


def _attn_bwd(q, k, v, do, ltot, *, name):
    nh, l, dh = q.shape
    assert l % ATT_BQ == 0
    nq = l // ATT_BQ
    qscale = HEAD_DIM ** -0.5
    chunks = range(ATT_CHUNKS)

    def body(q_ref, k_ref, v_ref, do_ref, lt_ref, dq_ref, dk_ref, dv_ref,
             z_scr, cs_scr, da_scr, cd_scr, dqp_scr, dkp_scr, dvp_scr):
        t_incl = _tri_and_ones(lambda r, c: r <= c)
        t_excl = _tri_and_ones(lambda r, c: r < c)
        dmask = (lax.broadcasted_iota(jnp.int32, (BLK, BLK), 1)
                 < lax.broadcasted_iota(jnp.int32, (BLK, BLK), 0))
        dk_ref[...] = jnp.zeros_like(dk_ref)
        dv_ref[...] = jnp.zeros_like(dv_ref)

        def logits(qc, c0, mask):
            z = lax.dot_general(qc, k_ref[0, pl.ds(c0, BLK), :], _DIMS["nt"], preferred_element_type=F32)
            lb, lm = _logits(z, mask)
            return lb, lm.astype(BF16)

        def weights(lb, lm_b, doc, ltc, c0, mask, pre_lm):
            cs = _cumsum_mm(lm_b, t_incl)
            da = lax.dot_general(doc, v_ref[0, pl.ds(c0, BLK), :], _DIMS["nt"], preferred_element_type=F32)
            a = jnp.exp(lb + (ltc - pre_lm - cs[:, :BLK]))
            if mask is not None:
                a = jnp.where(mask, a, 0.0)
            dl = a * da
            return (jnp.exp(lb), dl, dl.astype(BF16), a.astype(BF16)), pre_lm + cs[:, BLK:]

        def logit_grad(beta, dl, dl_b, mask, pre_dl):
            cd = _cumsum_mm(dl_b, t_excl)
            dz = dl - beta * (dl + pre_dl + cd[:, :BLK])
            if mask is not None:
                dz = jnp.where(mask, dz, 0.0)
            return (dz * qscale).astype(BF16), pre_dl + cd[:, BLK:]

        def key_grads(c0, dz_parts, a_parts, q_rows, do_rows):
            dz_all = dz_parts[0] if len(dz_parts) == 1 else jnp.concatenate(dz_parts, axis=0)
            a_all = a_parts[0] if len(a_parts) == 1 else jnp.concatenate(a_parts, axis=0)
            dk_ref[0, pl.ds(c0, BLK), :] += lax.dot_general(dz_all, q_rows, _DIMS["tn"], preferred_element_type=F32)
            dv_ref[0, pl.ds(c0, BLK), :] += lax.dot_general(a_all, do_rows, _DIMS["tn"], preferred_element_type=F32)

        def qblock(qb, _):
            r0 = pl.multiple_of(qb * ATT_BQ, ATT_BQ)
            q_raw = q_ref[0, pl.ds(r0, ATT_BQ), :]
            do_all = do_ref[0, pl.ds(r0, ATT_BQ), :]
            qs = [q_raw[rc * BLK:(rc + 1) * BLK] * jnp.asarray(qscale, BF16) for rc in chunks]
            dos = [do_all[rc * BLK:(rc + 1) * BLK] for rc in chunks]
            lts = [lt_ref[0, pl.ds(r0 + rc * BLK, BLK), :] for rc in chunks]

            def col0(b):
                return pl.multiple_of(b * BLK, BLK)

            def stage_scores(b, _, state):
                kj = k_ref[0, pl.ds(col0(b), BLK), :]
                for rc in chunks:
                    z_scr[rc] = lax.dot_general(qs[rc], kj, _DIMS["nt"], preferred_element_type=F32)
                return (), state

            def stage_prefix(b, _, state):
                vj = v_ref[0, pl.ds(col0(b), BLK), :]
                lbs = []
                for rc in chunks:
                    lb, lm = _logits(z_scr[rc], None)
                    cs_scr[rc] = _cumsum_mm(lm.astype(BF16), t_incl)
                    da_scr[rc] = lax.dot_general(dos[rc], vj, _DIMS["nt"], preferred_element_type=F32)
                    lbs.append(lb)
                return tuple(lbs), state

            def stage_weights(b, lbs, state):
                dqs, pls, pds = state
                out, new_pls = [], []
                for rc in chunks:
                    cs = cs_scr[rc]
                    a = jnp.exp(lbs[rc] + (lts[rc] - pls[rc] - cs[:, :BLK]))
                    dl = a * da_scr[rc]
                    cd_scr[rc] = _cumsum_mm(dl.astype(BF16), t_excl)
                    out.append((jnp.exp(lbs[rc]), dl, a.astype(BF16)))
                    new_pls.append(pls[rc] + cs[:, BLK:])
                return tuple(out), (dqs, tuple(new_pls), pds)

            def stage_products(b, ys, state):
                dqs, pls, pds = state
                c0 = col0(b)
                kj = k_ref[0, pl.ds(c0, BLK), :]
                dz_bs, new_pds = [], []
                for rc in chunks:
                    beta, dl, _ = ys[rc]
                    cd = cd_scr[rc]
                    dz_b = ((dl - beta * (dl + pds[rc] + cd[:, :BLK])) * qscale).astype(BF16)
                    dqp_scr[rc] = jnp.dot(dz_b, kj, preferred_element_type=F32)
                    dz_bs.append(dz_b)
                    new_pds.append(pds[rc] + cd[:, BLK:])
                dkp_scr[...] = lax.dot_general(jnp.concatenate(dz_bs, axis=0), q_raw, _DIMS["tn"],
                                               preferred_element_type=F32)
                dvp_scr[...] = lax.dot_general(jnp.concatenate([y[2] for y in ys], axis=0), do_all, _DIMS["tn"],
                                               preferred_element_type=F32)
                return (), (dqs, pls, tuple(new_pds))

            def stage_acc(b, _, state):
                dqs, pls, pds = state
                c0 = col0(b)
                dk_ref[0, pl.ds(c0, BLK), :] += dkp_scr[...]
                dv_ref[0, pl.ds(c0, BLK), :] += dvp_scr[...]
                return None, (tuple(dqs[rc] + dqp_scr[rc] for rc in chunks), pls, pds)

            stages = [stage_scores, stage_prefix, stage_weights, stage_products, stage_acc]
            zero = jnp.zeros((BLK, BLK), F32)
            state = ((jnp.zeros((BLK, dh), F32),) * ATT_CHUNKS, (zero,) * ATT_CHUNKS, (zero,) * ATT_CHUNKS)

            def pipelined(state):
                return _software_pipeline(stages, qb * ATT_CHUNKS, lambda b: b, state)

            def one_by_one(state):
                def block(b, state):
                    x = None
                    for stage in stages:
                        x, state = stage(b, x, state)
                    return state
                return lax.fori_loop(0, qb * ATT_CHUNKS, block, state)

            dqs, pls, pds = lax.cond(qb * ATT_CHUNKS >= len(stages) - 1, pipelined, one_by_one, state)
            dqs, pls, pds = list(dqs), list(pls), list(pds)
            for dj in chunks:
                c0 = r0 + dj * BLK
                kj = k_ref[0, pl.ds(c0, BLK), :]
                dz_parts, a_parts = [], []
                for rc in range(dj, ATT_CHUNKS):
                    mask = dmask if rc == dj else None
                    ys, pls[rc] = weights(*logits(qs[rc], c0, mask), dos[rc], lts[rc], c0, mask, pls[rc])
                    dz_b, pds[rc] = logit_grad(*ys[:3], mask, pds[rc])
                    a_b = ys[3]
                    dqs[rc] = dqs[rc] + jnp.dot(dz_b, kj, preferred_element_type=F32)
                    dz_parts.append(dz_b)
                    a_parts.append(a_b)
                key_grads(c0, dz_parts, a_parts, q_raw[dj * BLK:], do_all[dj * BLK:])
            for rc in chunks:
                dq_ref[0, pl.ds(r0 + rc * BLK, BLK), :] = dqs[rc]
            return 0

        lax.fori_loop(0, nq, qblock, 0)

    head = pl.BlockSpec((1, l, dh), lambda h: (h, 0, 0))
    sq = pltpu.VMEM((ATT_CHUNKS, BLK, BLK), F32)
    sw = pltpu.VMEM((ATT_CHUNKS, BLK, 2 * BLK), F32)
    return pl.pallas_call(
        body, name=name, grid=(nh,),
        in_specs=[head, head, head, head, pl.BlockSpec((1, l, BLK), lambda h: (h, 0, 0))],
        out_specs=[head, head, head],
        out_shape=[jax.ShapeDtypeStruct((nh, l, dh), F32)] * 3,
        scratch_shapes=[sq, sw, sq, sw, pltpu.VMEM((ATT_CHUNKS, BLK, dh), F32),
                        pltpu.VMEM((BLK, dh), F32), pltpu.VMEM((BLK, dh), F32)],
        compiler_params=_cparams("parallel"),
    )(q, k, v, do, ltot)


def _add_to_bf16(a, b, *, name):
    n, r, c = a.shape
    tr = _row_tile(r, 1536, 16)
    blk = pl.BlockSpec((1, tr, c), lambda i, j: (i, j, 0))

    def body(a_ref, b_ref, o_ref):
        o_ref[...] = (a_ref[...] + b_ref[...].astype(F32)).astype(BF16)

    return pl.pallas_call(body, name=name, grid=(n, r // tr), in_specs=[blk, blk], out_specs=blk,
                          out_shape=jax.ShapeDtypeStruct(a.shape, BF16),
                          compiler_params=_cparams("parallel", "parallel"))(a, b)


def _sum_slots(p, *, name):
    n, r, c = p.shape
    tr = _row_tile(r, 1536, 16)

    def body(p_ref, o_ref):
        acc = p_ref[0].astype(F32)
        for s in range(1, n):
            acc = acc + p_ref[s].astype(F32)
        o_ref[...] = acc

    return pl.pallas_call(body, name=name, grid=(r // tr,),
                          in_specs=[pl.BlockSpec((n, tr, c), lambda j: (0, j, 0))],
                          out_specs=pl.BlockSpec((tr, c), lambda j: (j, 0)),
                          out_shape=jax.ShapeDtypeStruct((r, c), F32),
                          compiler_params=_cparams("parallel"))(p)


def _adamw(w, g, m, v, *, name):
    r, c = w.shape
    tr = _row_tile(r, 512, 8)
    blk = pl.BlockSpec((tr, c), lambda i: (i, 0))

    def body(w_ref, g_ref, m_ref, v_ref, d_ref, mo_ref, vo_ref):
        gv = g_ref[...]
        mn = ADAM_B1 * m_ref[...] + (1.0 - ADAM_B1) * gv
        vn = ADAM_B2 * v_ref[...] + (1.0 - ADAM_B2) * (gv * gv)
        m_hat = mn / (1.0 - ADAM_B1 ** ADAM_STEP)
        v_hat = vn / (1.0 - ADAM_B2 ** ADAM_STEP)
        d_ref[...] = -ADAM_LR * (m_hat / (jnp.sqrt(v_hat) + ADAM_EPS) + ADAM_WD * w_ref[...])
        mo_ref[...] = mn
        vo_ref[...] = vn

    return pl.pallas_call(body, name=name, grid=(r // tr,), in_specs=[blk] * 4, out_specs=[blk] * 3,
                          out_shape=[jax.ShapeDtypeStruct((r, c), F32)] * 3,
                          compiler_params=_cparams("parallel"))(w, g, m, v)


_ANY = pl.BlockSpec(memory_space=pl.ANY)


def _place():
    x, y, c = lax.axis_index("x"), lax.axis_index("y"), lax.axis_index("c")
    chips = [(1 - x, y), (x, 1 - y), (1 - x, 1 - y)]
    return x, y, c, chips


def _gather_copy(k, src, dst, to, send_sems, recv_sems):
    return pltpu.make_async_remote_copy(src_ref=src, dst_ref=dst, send_sem=send_sems.at[k],
                                        recv_sem=recv_sems.at[k], device_id=to, device_id_type=MESH)


def _gather_begin(w_ref, out_ref, send_sems, recv_sems):
    x, y, c, chips = _place()
    s = 2 * x + y
    for j, chip in enumerate(chips):
        _gather_copy(j, w_ref.at[c], out_ref.at[s, c], (*chip, c), send_sems, recv_sems).start()


def _gather_end(w_ref, out_ref, stage_ref, send_sems, recv_sems):
    x, y, c, chips = _place()
    s = 2 * x + y
    sibling = (x, y, 1 - c)
    slots = [2 * px + py for px, py in chips]
    passed = []
    for j, sp in enumerate(slots):
        _gather_copy(j, w_ref.at[c], out_ref.at[sp, c], sibling, send_sems, recv_sems).wait_recv()
        fwd = _gather_copy(3 + j, out_ref.at[sp, c], out_ref.at[sp, c], sibling, send_sems, recv_sems)
        fwd.start()
        passed.append(fwd)
    for h in range(2):
        pltpu.sync_copy(w_ref.at[h], stage_ref)
        pltpu.sync_copy(stage_ref, out_ref.at[s, h])
    for j, sp in enumerate(slots):
        _gather_copy(3 + j, w_ref.at[c], out_ref.at[sp, 1 - c], sibling, send_sems, recv_sems).wait_recv()
    for j, chip in enumerate(chips):
        _gather_copy(j, w_ref.at[c], out_ref.at[s, c], (*chip, c), send_sems, recv_sems).wait_send()
    for fwd in passed:
        fwd.wait_send()


def _gather_out_and_scratch(wsh):
    _, r, cdim = wsh.shape
    return (jax.ShapeDtypeStruct((N_CHIPS, 2, r, cdim), wsh.dtype),
            [pltpu.VMEM((r, cdim), wsh.dtype), pltpu.SemaphoreType.DMA((6,)), pltpu.SemaphoreType.DMA((6,))])


def _gather_shards(wsh, *, name):
    def body(w_ref, out_ref, stage_ref, send_sems, recv_sems):
        _gather_begin(w_ref, out_ref, send_sems, recv_sems)
        _gather_end(w_ref, out_ref, stage_ref, send_sems, recv_sems)

    out_shape, scratch = _gather_out_and_scratch(wsh)
    return pl.pallas_call(
        body, name=name, in_specs=[_ANY], out_specs=_ANY, out_shape=out_shape, scratch_shapes=scratch,
        compiler_params=pltpu.CompilerParams(vmem_limit_bytes=VMEM_LIMIT),
    )(wsh)


def _to_sibling(a, *, name):
    def body(a_ref, out_ref, send_sem, recv_sem):
        x, y, c, _ = _place()
        cp = pltpu.make_async_remote_copy(src_ref=a_ref, dst_ref=out_ref, send_sem=send_sem, recv_sem=recv_sem,
                                          device_id=(x, y, 1 - c), device_id_type=MESH)
        cp.start()
        cp.wait()

    return pl.pallas_call(
        body, name=name, in_specs=[_ANY], out_specs=_ANY,
        out_shape=jax.ShapeDtypeStruct(a.shape, a.dtype),
        scratch_shapes=[pltpu.SemaphoreType.DMA, pltpu.SemaphoreType.DMA],
    )(a)


def _scatter_copy(j, p_ref, out_ref, send_sems, recv_sems, receive):
    x, y, c, chips = _place()
    px, py = chips[j]
    return pltpu.make_async_remote_copy(
        src_ref=p_ref.at[2 * px + py], dst_ref=out_ref.at[(2 * px + py) if receive else (2 * x + y)],
        send_sem=send_sems.at[j], recv_sem=recv_sems.at[j], device_id=(px, py, c), device_id_type=MESH)


def _scatter_begin(p_ref, out_ref, send_sems, recv_sems):
    for j in range(N_CHIPS - 1):
        _scatter_copy(j, p_ref, out_ref, send_sems, recv_sems, False).start()


def _scatter_end(p_ref, out_ref, stage_ref, send_sems, recv_sems):
    x, y, _, _ = _place()
    s = 2 * x + y
    pltpu.sync_copy(p_ref.at[s], stage_ref)
    pltpu.sync_copy(stage_ref, out_ref.at[s])
    for j in range(N_CHIPS - 1):
        _scatter_copy(j, p_ref, out_ref, send_sems, recv_sems, True).wait_recv()
    for j in range(N_CHIPS - 1):
        _scatter_copy(j, p_ref, out_ref, send_sems, recv_sems, False).wait_send()


def _scatter_out_and_scratch(p):
    return (jax.ShapeDtypeStruct(p.shape, p.dtype),
            [pltpu.VMEM(p.shape[1:], p.dtype), pltpu.SemaphoreType.DMA((N_CHIPS - 1,)),
             pltpu.SemaphoreType.DMA((N_CHIPS - 1,))])


def _scatter_rows(p, *, name):
    def body(p_ref, out_ref, stage_ref, send_sems, recv_sems):
        _scatter_begin(p_ref, out_ref, send_sems, recv_sems)
        _scatter_end(p_ref, out_ref, stage_ref, send_sems, recv_sems)

    out_shape, scratch = _scatter_out_and_scratch(p)
    return pl.pallas_call(
        body, name=name, in_specs=[_ANY], out_specs=_ANY, out_shape=out_shape, scratch_shapes=scratch,
        compiler_params=pltpu.CompilerParams(vmem_limit_bytes=VMEM_LIMIT),
    )(p)


def _all_devices(a, reduce, *, name):
    r, cdim = a.shape

    def body(a_ref, out_ref, *scratch):
        if reduce:
            buf, send_sems, recv_sems = scratch
        else:
            buf = out_ref
            send_sems, recv_sems = scratch
        x, y, c, _ = _place()
        me = 4 * x + 2 * y + c
        buf[me] = a_ref[...]
        peers = []
        for k in range(1, N_DEV):
            dx, dy, dc = (k >> 2) & 1, (k >> 1) & 1, k & 1
            peers.append((x ^ dx, y ^ dy, c ^ dc))
        sends = []
        for k, peer in enumerate(peers):
            cp = pltpu.make_async_remote_copy(src_ref=a_ref, dst_ref=buf.at[me], send_sem=send_sems.at[k],
                                              recv_sem=recv_sems.at[k], device_id=peer, device_id_type=MESH)
            cp.start()
            sends.append(cp)
        for k, (px, py, pc) in enumerate(peers):
            pltpu.make_async_remote_copy(src_ref=a_ref, dst_ref=buf.at[4 * px + 2 * py + pc],
                                         send_sem=send_sems.at[k], recv_sem=recv_sems.at[k],
                                         device_id=(px, py, pc), device_id_type=MESH).wait_recv()
        for cp in sends:
            cp.wait_send()
        if reduce:
            acc = buf[0]
            for k in range(1, N_DEV):
                acc = acc + buf[k]
            out_ref[...] = acc

    vm = pl.BlockSpec(memory_space=pltpu.VMEM)
    sems = [pltpu.SemaphoreType.DMA((N_DEV - 1,)), pltpu.SemaphoreType.DMA((N_DEV - 1,))]
    if reduce:
        out_shape = jax.ShapeDtypeStruct((r, cdim), F32)
        scratch = [pltpu.VMEM((N_DEV, r, cdim), F32)] + sems
    else:
        out_shape = jax.ShapeDtypeStruct((N_DEV, r, cdim), F32)
        scratch = sems
    return pl.pallas_call(
        body, name=name, in_specs=[vm], out_specs=vm, out_shape=out_shape, scratch_shapes=scratch,
        compiler_params=pltpu.CompilerParams(vmem_limit_bytes=VMEM_LIMIT),
    )(a)


def _flat_rows(parts, cols):
    flat = jnp.concatenate([p.reshape(-1) for p in parts])
    padded = -(-flat.size // (8 * cols)) * (8 * cols)
    return jnp.pad(flat, (0, padded - flat.size)).reshape(-1, cols)


def kernel(x, meta_tokens, mix_norm, ffn_norm, pool_w, pool_scale, kv_norm, w_kv, w_q, w_o, ffn_w_up, ffn_conv_w, ffn_conv_b, ffn_w_down, final_norm, loss_target, m_meta_tokens, m_mix_norm, m_ffn_norm, m_pool_w, m_pool_scale, m_kv_norm, m_w_kv, m_w_q, m_w_o, m_ffn_w_up, m_ffn_conv_w, m_ffn_conv_b, m_ffn_w_down, m_final_norm, v_meta_tokens, v_mix_norm, v_ffn_norm, v_pool_w, v_pool_scale, v_kv_norm, v_w_kv, v_w_q, v_w_o, v_ffn_w_up, v_ffn_conv_w, v_ffn_conv_b, v_ffn_w_down, v_final_norm):
    seq, d = x.shape[1], x.shape[2]
    l_real = N_META + seq
    lp = -(-l_real // ATT_BQ) * ATT_BQ
    f2 = ffn_w_up.shape[2] * N_CHIPS
    f = f2 // 2
    gd = d // N_GROUPS
    chip = 2 * lax.axis_index("x") + lax.axis_index("y")
    core = lax.axis_index("c")

    cw = 1024

    def shard_halves(parts):
        return _flat_rows([p.astype(BF16) for p in parts], cw).reshape(2, -1, cw)

    def unpack(gathered, parts):
        flat, out, first = gathered.reshape(N_CHIPS, -1), [], 0
        for p in parts:
            out.append(flat[:, first:first + p.size].reshape((N_CHIPS,) + p.shape))
            first += p.size
        return out

    now_parts = [pool_w[0], ffn_w_up[0], ffn_w_down[0]]
    attn_parts = [w_kv, w_q[0], w_o[0], ffn_w_down[1]]
    late_parts = [ffn_w_up[1]]
    wp_s, wup0_s, wdn0_s = unpack(_gather_shards(shard_halves(now_parts), name="gather_weights"), now_parts)
    wp_b = wp_s.transpose(1, 0, 2, 3).reshape(N_GROUPS, gd, gd)
    wup_b = [wup0_s.transpose(1, 0, 2).reshape(d, f2), None]
    wdn_b = [wdn0_s.reshape(f, d), None]

    small_parts = [meta_tokens, pool_scale, ffn_conv_w]
    ssizes = [p.size for p in small_parts]
    small = _flat_rows(small_parts, LANES)
    sall = _all_devices(small, False, name="gather_small")[::2].reshape(N_CHIPS, -1)
    meta_f = sall[:, :ssizes[0]].reshape(N_CHIPS, N_META, d // N_CHIPS).transpose(1, 0, 2).reshape(N_META, d)
    scale_f = sall[:, ssizes[0]:ssizes[0] + ssizes[1]].reshape(1, d)
    cw_f = sall[:, ssizes[0] + ssizes[1]:sum(ssizes)].reshape(N_CHIPS, 2, 3, f2 // N_CHIPS).transpose(1, 2, 0, 3).reshape(2, 3, f2)

    mix0, mix1 = mix_norm[0:1], mix_norm[1:2]
    fn0, fn1 = ffn_norm[0:1], ffn_norm[1:2]
    kvn_g = kv_norm.reshape(1, d)
    fin_g = final_norm.reshape(1, d)

    pad = lp - l_real
    h0 = jnp.concatenate([meta_f, x[0], jnp.zeros((pad, d), F32)], axis=0)
    tgt = jnp.pad(loss_target[0], ((N_META, pad), (0, 0)))

    _, r0 = _norm_fwd(h0, [mix0], name="norm_h0")
    diff_b = _pool_diff_fwd(h0, r0, mix0, name="pool_diff")
    h1 = _pool_mix_fwd(h0, diff_b, wp_b, scale_f, name="pool_mix")

    def ffn_fwd(h_in, layer, tag, gather_behind_up=None, gather_behind_gate=None):
        (fb,), rf = _norm_fwd(h_in, [ffn_norm[layer:layer + 1]], name=f"norm_ffn{tag}")
        u = _mm(fb, wup_b[layer], "nn", tm=1408, tn=512, name=f"ffn_up{tag}", gather=gather_behind_up)
        gathered = []
        if gather_behind_up is not None:
            u, g = u
            gathered.append(g)
        a = _convgate_fwd(u, cw_f[layer], ffn_conv_b[layer:layer + 1], name=f"convgate{tag}",
                          gather=gather_behind_gate)
        if gather_behind_gate is not None:
            a, g = a
            gathered.append(g)
        h_out = _mm(a, wdn_b[layer], "nn", tm=384, tn=1024, res=h_in, name=f"ffn_down{tag}")
        return h_out, (fb, rf, u, a), gathered

    h2, ffn0_saved, (attn_gathered, late_gathered) = ffn_fwd(h1, 0, "0", shard_halves(attn_parts),
                                                             shard_halves(late_parts))
    wkv_s, wq_s, wo_s, wdn1_s = unpack(attn_gathered, attn_parts)
    (wup1_s,) = unpack(late_gathered, late_parts)
    wkv_b = wkv_s.transpose(1, 0, 2).reshape(d, 2 * d)
    wq_b = wq_s.reshape(d, d)
    wo_b = wo_s.reshape(d, d)
    wup_b[1] = wup1_s.transpose(1, 0, 2).reshape(d, f2)
    wdn_b[1] = wdn1_s.reshape(f, d)
    (kvn_b, n1_b), r2 = _norm_fwd(h2, [kvn_g, mix1], name="norm_h2")
    kv_b = _mm(kvn_b, wkv_b, "nn", tm=1408, tn=512, out_dtype=BF16, name="kv_proj")
    q_b = _mm(n1_b, wq_b, "nn", tm=1408, tn=512, out_dtype=BF16, name="q_proj")

    def heads(t):
        return t.reshape(lp, N_HEADS, HEAD_DIM).transpose(1, 0, 2)

    def unheads(t):
        return t.transpose(1, 0, 2).reshape(lp, N_HEADS * HEAD_DIM)

    qh, kh, vh = heads(q_b), heads(kv_b[:, :d]), heads(kv_b[:, d:])
    oh, ltot = _attn_fwd(qh, kh, vh, name="attn_fwd")
    o_b = unheads(oh)
    h3 = _mm(o_b, wo_b, "nn", tm=384, tn=1024, res=h2, name="o_proj")
    h4, ffn1_saved, _ = ffn_fwd(h3, 1, "1")

    loss8, dh4, dh4_b, dfin = _loss_bwd(h4, fin_g, tgt, seq, name="loss")

    def ffn_bwd(h_in, layer, saved, dh_out, dh_out_b, tag, scatter_behind_gate=None):
        fb, rf, u, a = saved
        dwdn = _mm(a, dh_out_b, "tn", tm=256, tn=1024, name=f"d_wdown{tag}")
        da = _mm(dh_out_b, wdn_b[layer], "nt", tm=384, tn=f, name=f"d_act{tag}")
        du_g, du_v, dcw_g, dcw_v, dcb_g, dcb_v, *scattered = _convgate_bwd(
            u, cw_f[layer], ffn_conv_b[layer:layer + 1], da, name=f"convgate_bwd{tag}", scatter=scatter_behind_gate)
        dcw = jnp.concatenate([dcw_g, dcw_v], axis=1)
        dcb = jnp.concatenate([dcb_g, dcb_v], axis=1)
        half_chips = N_CHIPS // 2
        dwup = jnp.concatenate(
            [_mm(fb, du_g, "tn", tm=512, tn=f2 // N_CHIPS, out_shards=half_chips, name=f"d_wup_gate{tag}"),
             _mm(fb, du_v, "tn", tm=512, tn=f2 // N_CHIPS, out_shards=half_chips, name=f"d_wup_val{tag}")], axis=0)
        df = _mm(du_g, wup_b[layer][:, :f], "nt", tm=384, tn=1024, name=f"d_ffn_in_gate{tag}")
        df = _mm(du_v, wup_b[layer][:, f:], "nt", tm=384, tn=1024, res=df, name=f"d_ffn_in_val{tag}")
        dh_in, dh_in_b, (dfn,) = _norm_bwd(h_in, rf, [ffn_norm[layer:layer + 1]], [df], dh_out,
                                          name=f"norm_ffn_bwd{tag}")
        return (dh_in, dh_in_b, dwup, dwdn, dcw, dcb, dfn) + tuple(scattered)

    dh3, dh3_b, dwup1, dwdn1, dcw1, dcb1, dfn1 = ffn_bwd(h3, 1, ffn1_saved, dh4, dh4_b, "1")

    dwo = _mm(o_b, dh3_b, "tn", tm=1024, tn=512, name="d_wo")
    do_b = _mm(dh3_b, wo_b, "nt", tm=384, tn=1024, out_dtype=BF16, name="d_attn_out")
    dqh, dkh, dvh = _attn_bwd(qh, kh, vh, heads(do_b), ltot, name="attn_bwd")
    dq_b = unheads(dqh).astype(BF16)
    dkv_b = jnp.concatenate([unheads(dkh), unheads(dvh)], axis=1).astype(BF16)
    dwq = _mm(n1_b, dq_b, "tn", tm=1024, tn=512, name="d_wq")
    dwkv = _mm(kvn_b, dkv_b, "tn", tm=1024, tn=512, out_shards=N_CHIPS, name="d_wkv")
    dn1 = _mm(dq_b, wq_b, "nt", tm=384, tn=1024, name="d_n1")
    dkvn = _mm(dkv_b, wkv_b, "nt", tm=384, tn=1024, name="d_kvn")
    dh2, dh2_b, (dmix1, dkvg) = _norm_bwd(h2, r2, [mix1, kvn_g], [dn1, dkvn], dh3, name="norm_h2_bwd")

    def chip_partials(parts, tag):
        g = jnp.concatenate([p.reshape(N_CHIPS, -1) for p in parts], axis=1).reshape(N_CHIPS, 2, -1, cw)
        g_mine = lax.dynamic_index_in_dim(g, core, axis=1, keepdims=False)
        g_other = lax.dynamic_index_in_dim(g, 1 - core, axis=1, keepdims=False)
        from_sibling = _to_sibling(g_other.astype(BF16), name=f"grads_to_sibling{tag}")
        return _add_to_bf16(g_mine, from_sibling, name=f"grads_chip_sum{tag}")

    def reduced_shard(slots, tag):
        half = _sum_slots(slots, name=f"grads_sum{tag}")
        other_half = _to_sibling(half, name=f"grads_join{tag}")
        return jnp.stack([jnp.where(core == 0, half, other_half), jnp.where(core == 0, other_half, half)]).reshape(-1)

    def unpack_shard(flat, shapes):
        out, first = [], 0
        for shape in shapes:
            size = 1
            for n in shape:
                size *= n
            out.append(flat[first:first + size].reshape(shape))
            first += size
        return out

    early = chip_partials([dwkv, dwq.reshape(N_CHIPS, d // N_CHIPS, d), dwo.reshape(N_CHIPS, d // N_CHIPS, d),
                           dwup1, dwdn1.reshape(N_CHIPS, f // N_CHIPS, d)], "_early")
    dh1, dh1_b, dwup0, dwdn0, dcw0, dcb0, dfn0, early_slots = ffn_bwd(h1, 0, ffn0_saved, dh2, dh2_b, "0", early)
    g_w_kv, g_w_q, g_w_o, g_w_up1, g_w_dn1 = unpack_shard(
        reduced_shard(early_slots, "_early"),
        [w_kv.shape, w_q.shape, w_o.shape, ffn_w_up.shape[1:], ffn_w_down.shape[1:]])

    ddiff, dwp, dscale = _pool_mix_bwd(diff_b, wp_b, scale_f, dh1, name="pool_mix_bwd")
    dn0 = _pool_diff_bwd(ddiff, name="pool_diff_bwd")
    dh0, _, (dmix0,) = _norm_bwd(h0, r0, [mix0], [dn0], dh1, name="norm_h0_bwd")

    grad_x = dh0[N_META:l_real][None]
    dmeta = dh0[:N_META]

    late = chip_partials([dwp.reshape(N_GROUPS, N_CHIPS, gd // N_CHIPS, gd).transpose(1, 0, 2, 3), dwup0,
                          dwdn0.reshape(N_CHIPS, f // N_CHIPS, d)], "_late")
    g_pool_w, g_w_up0, g_w_dn0 = unpack_shard(
        reduced_shard(_scatter_rows(late, name="grads_scatter_late"), "_late"),
        [pool_w.shape, ffn_w_up.shape[1:], ffn_w_down.shape[1:]])
    g_w_up = jnp.stack([g_w_up0, g_w_up1])
    g_w_dn = jnp.stack([g_w_dn0, g_w_dn1])

    dcw = jnp.stack([dcw0, dcw1])
    dcb = jnp.concatenate([dcb0, dcb1], axis=0)
    sg_parts = [jnp.concatenate([dmix0, dmix1], axis=0), jnp.concatenate([dfn0, dfn1], axis=0), dkvg, dfin, dcb,
                dmeta, dscale, dcw, loss8]
    sg_sizes = [p.size for p in sg_parts]
    sg = _all_devices(_flat_rows(sg_parts, LANES), True, name="reduce_small").reshape(-1)
    sg_offs = [0]
    for sz in sg_sizes:
        sg_offs.append(sg_offs[-1] + sz)

    def gsmall(i, shape):
        return sg[sg_offs[i]:sg_offs[i + 1]].reshape(shape)

    g_mix = gsmall(0, mix_norm.shape)
    g_ffn_norm = gsmall(1, ffn_norm.shape)
    g_kv_norm = gsmall(2, kv_norm.shape)
    g_final = gsmall(3, final_norm.shape)
    g_conv_b = gsmall(4, ffn_conv_b.shape)
    csh = d // N_CHIPS
    g_meta = lax.dynamic_slice_in_dim(gsmall(5, (N_META, d)), chip * csh, csh, axis=1)
    g_scale = lax.dynamic_slice_in_dim(gsmall(6, (1, d)), chip * csh, csh, axis=1)
    fsh = f2 // N_CHIPS
    g_conv_w = lax.dynamic_slice_in_dim(gsmall(7, (2, 3, f2)), chip * fsh, fsh, axis=2)
    loss = gsmall(8, (8 * LANES,))[0]

    weights = [meta_tokens, mix_norm, ffn_norm, pool_w, pool_scale, kv_norm, w_kv, w_q, w_o, ffn_w_up, ffn_conv_w,
               ffn_conv_b, ffn_w_down, final_norm]
    grads = [g_meta, g_mix, g_ffn_norm, g_pool_w, g_scale, g_kv_norm, g_w_kv, g_w_q, g_w_o, g_w_up, g_conv_w,
             g_conv_b, g_w_dn, g_final]
    ms = [m_meta_tokens, m_mix_norm, m_ffn_norm, m_pool_w, m_pool_scale, m_kv_norm, m_w_kv, m_w_q, m_w_o,
          m_ffn_w_up, m_ffn_conv_w, m_ffn_conv_b, m_ffn_w_down, m_final_norm]
    vs = [v_meta_tokens, v_mix_norm, v_ffn_norm, v_pool_w, v_pool_scale, v_kv_norm, v_w_kv, v_w_q, v_w_o,
          v_ffn_w_up, v_ffn_conv_w, v_ffn_conv_b, v_ffn_w_down, v_final_norm]
    names = ["meta", "mix", "ffnnorm", "poolw", "poolscale", "kvnorm", "wkv", "wq", "wo", "wup", "convw", "convb",
             "wdown", "final"]
    deltas, new_ms, new_vs = [], [], []
    for w, g, m, v, nm in zip(weights, grads, ms, vs, names):
        cols = w.shape[-1]
        if w.size % (8 * LANES) == 0 and w.ndim == 1:
            cols = LANES
        view = (w.size // cols, cols)
        dl, mn, vn = _adamw(w.reshape(view), g.reshape(view), m.reshape(view), v.reshape(view), name=f"adamw_{nm}")
        deltas.append(dl.reshape(w.shape))
        new_ms.append(mn.reshape(w.shape))
        new_vs.append(vn.reshape(w.shape))

    return (loss, grad_x, *grads, *deltas, *new_ms, *new_vs)
```

```python
import jax
import jax.numpy as jnp
from jax import lax
from jax.experimental import pallas as pl
from jax.experimental.pallas import tpu as pltpu

F32 = jnp.float32
BF16 = jnp.bfloat16

N_META = 16
N_HEADS = 16
HEAD_DIM = 64
POOL_WINDOWS = (2, 4, 8, 16)
N_GROUPS = 4
RMS_EPS = 1e-6
ADAM_LR = 0.001
ADAM_B1 = 0.9
ADAM_B2 = 0.999
ADAM_EPS = 1e-08
ADAM_WD = 0.01
ADAM_STEP = 10

LANES = 128
BLK = 128
ATT_CHUNKS = 3
ATT_BQ = ATT_CHUNKS * BLK
CONV_ROWS = 128
N_CHIPS = 4
N_DEV = 8
VMEM_LIMIT = 56 * 1024 * 1024
MESH = pl.DeviceIdType.MESH


def _cparams(*sem):
    return pltpu.CompilerParams(dimension_semantics=tuple(sem) if sem else None,
                                vmem_limit_bytes=VMEM_LIMIT)


def _tile(n, pref):
    best = None
    t = LANES
    while t <= min(n, pref):
        if n % t == 0:
            best = t
        t += LANES
    assert best is not None, (n, pref)
    return best


def _row_tile(r, pref, mult):
    best = r
    for t in range(mult, min(r, pref) + 1, mult):
        if r % t == 0:
            best = t
    return best


_DIMS = {
    "nn": (((1,), (0,)), ((), ())),
    "nt": (((1,), (1,)), ((), ())),
    "tn": (((0,), (0,)), ((), ())),
}


def _mm(a, b, dims, *, tm, tn, name, out_dtype=F32, res=None, out_shards=1, gather=None):
    if dims == "tn":
        k, m = a.shape
    else:
        m, k = a.shape
    n = b.shape[0] if dims == "nt" else b.shape[1]
    tm = _tile(m, tm)
    tn = _tile(n // out_shards, tn)
    a_spec = pl.BlockSpec((k, tm), lambda i, j: (0, i)) if dims == "tn" else pl.BlockSpec((tm, k), lambda i, j: (i, 0))
    b_spec = pl.BlockSpec((tn, k), lambda i, j: (j, 0)) if dims == "nt" else pl.BlockSpec((k, tn), lambda i, j: (0, j))
    o_spec = pl.BlockSpec((tm, tn), lambda i, j: (i, j))
    out_shape = jax.ShapeDtypeStruct((m, n), out_dtype)
    if out_shards > 1:
        assert res is None
        per = n // out_shards // tn
        o_spec = pl.BlockSpec((None, tm, tn), lambda i, j: (j // per, i, j % per))
        out_shape = jax.ShapeDtypeStruct((out_shards, m, n // out_shards), out_dtype)
    dn = _DIMS[dims]

    grid = (m // tm, n // tn)

    def body(*refs):
        refs = list(refs)
        a_ref, b_ref = refs[:2]
        r_ref = refs[2] if res is not None else None
        o_ref = refs[2 + (res is not None) + (gather is not None)]
        acc = lax.dot_general(a_ref[...], b_ref[...], dn, preferred_element_type=F32)
        if res is not None:
            acc = acc + r_ref[...]
        o_ref[...] = acc.astype(out_dtype)
        if gather is not None:
            w_ref, g_ref, stage_ref, send_sems, recv_sems = refs[2 + (res is not None)], *refs[-4:]
            step = pl.program_id(0) * grid[1] + pl.program_id(1)

            @pl.when(step == 0)
            def _():
                _gather_begin(w_ref, g_ref, send_sems, recv_sems)

            @pl.when(step == grid[0] * grid[1] - 1)
            def _():
                _gather_end(w_ref, g_ref, stage_ref, send_sems, recv_sems)

    ins = [a, b] + ([] if res is None else [res])
    specs = [a_spec, b_spec] + ([] if res is None else [o_spec])
    if gather is None:
        return pl.pallas_call(
            body, name=name, grid=grid, in_specs=specs, out_specs=o_spec, out_shape=out_shape,
            compiler_params=_cparams("parallel", "arbitrary"),
        )(*ins)
    g_shape, scratch = _gather_out_and_scratch(gather)
    return pl.pallas_call(
        body, name=name, grid=grid, in_specs=specs + [_ANY], out_specs=[o_spec, _ANY],
        out_shape=[out_shape, g_shape], scratch_shapes=scratch,
        compiler_params=_cparams("arbitrary", "arbitrary"),
    )(*ins, gather)


def _norm_fwd(x, gains, *, name):
    l, d = x.shape
    tr = _tile(l, 384)
    ng = len(gains)

    def body(*refs):
        x_ref = refs[0]
        g_refs = refs[1:1 + ng]
        o_refs = refs[1 + ng:1 + 2 * ng]
        r_ref = refs[1 + 2 * ng]
        xv = x_ref[...]
        r = lax.rsqrt(jnp.mean(xv * xv, axis=-1, keepdims=True) + RMS_EPS)
        xn = xv * r
        for g_ref, o_ref in zip(g_refs, o_refs):
            o_ref[...] = (xn * g_ref[...]).astype(BF16)
        r_ref[...] = r

    row = pl.BlockSpec((tr, d), lambda i: (i, 0))
    gspec = pl.BlockSpec((1, d), lambda i: (0, 0))
    outs = pl.pallas_call(
        body, name=name, grid=(l // tr,),
        in_specs=[row] + [gspec] * ng,
        out_specs=[row] * ng + [pl.BlockSpec((tr, 1), lambda i: (i, 0))],
        out_shape=[jax.ShapeDtypeStruct((l, d), BF16)] * ng + [jax.ShapeDtypeStruct((l, 1), F32)],
        compiler_params=_cparams("parallel"),
    )(x, *gains)
    return list(outs[:ng]), outs[ng]


def _rows8(v):
    r, d = v.shape
    return jnp.sum(v.reshape(r // 8, 8, d), axis=0)


def _norm_bwd(x, r, gains, dns, dres, *, name, send=None):
    l, d = x.shape
    tr = _tile(l, 384)
    ng = len(gains)
    nsteps = l // tr

    def body(*refs):
        refs = list(refs)
        if send is not None:
            send_ref, got_ref, sems = refs.pop(3 + 2 * ng), refs.pop(5 + 3 * ng), (refs.pop(), refs.pop())[::-1]
        x_ref, r_ref, dres_ref = refs[0], refs[1], refs[2]
        g_refs = refs[3:3 + ng]
        dn_refs = refs[3 + ng:3 + 2 * ng]
        dx_ref, dxb_ref = refs[3 + 2 * ng], refs[4 + 2 * ng]
        dg_refs = refs[5 + 2 * ng:5 + 3 * ng]
        acc_refs = refs[5 + 3 * ng:5 + 4 * ng]
        i = pl.program_id(0)
        if send is not None:
            x_, y_, c_, _ = _place()
            exchange = pltpu.make_async_remote_copy(src_ref=send_ref, dst_ref=got_ref, send_sem=sems[0],
                                                    recv_sem=sems[1], device_id=(x_, y_, 1 - c_), device_id_type=MESH)

        @pl.when(i == 0)
        def _():
            for acc in acc_refs:
                acc[...] = jnp.zeros_like(acc)
            if send is not None:
                exchange.start()

        rv = r_ref[...]
        xn = x_ref[...] * rv
        total = dres_ref[...]
        for g_ref, dn_ref, acc in zip(g_refs, dn_refs, acc_refs):
            dn = dn_ref[...]
            acc[...] += _rows8(dn * xn)
            dxn = dn * g_ref[...]
            total = total + rv * (dxn - xn * jnp.mean(dxn * xn, axis=-1, keepdims=True))
        dx_ref[...] = total
        dxb_ref[...] = total.astype(BF16)

        @pl.when(i == nsteps - 1)
        def _():
            for dg_ref, acc in zip(dg_refs, acc_refs):
                dg_ref[...] = jnp.sum(acc[...], axis=0, keepdims=True)
            if send is not None:
                exchange.wait()

    row = pl.BlockSpec((tr, d), lambda i: (i, 0))
    gspec = pl.BlockSpec((1, d), lambda i: (0, 0))
    extra_in, extra_out, extra_scratch, extra_args = [], [], [], []
    if send is not None:
        extra_in, extra_args = [_ANY], [send]
        extra_out = [(_ANY, jax.ShapeDtypeStruct(send.shape, send.dtype))]
        extra_scratch = [pltpu.SemaphoreType.DMA, pltpu.SemaphoreType.DMA]
    outs = pl.pallas_call(
        body, name=name, grid=(nsteps,),
        in_specs=[row, pl.BlockSpec((tr, 1), lambda i: (i, 0)), row] + [gspec] * ng + [row] * ng + extra_in,
        out_specs=[row, row] + [gspec] * ng + [o[0] for o in extra_out],
        out_shape=[jax.ShapeDtypeStruct((l, d), F32), jax.ShapeDtypeStruct((l, d), BF16)]
        + [jax.ShapeDtypeStruct((1, d), F32)] * ng + [o[1] for o in extra_out],
        scratch_shapes=[pltpu.VMEM((8, d), F32)] * ng + extra_scratch,
        compiler_params=_cparams("arbitrary"),
    )(x, r, dres, *gains, *dns, *extra_args)
    if send is not None:
        return outs[0], outs[1], list(outs[2:2 + ng]), outs[2 + ng]
    return outs[0], outs[1], list(outs[2:])


def _loss_bwd(h, gain, tgt, seq, *, name):
    l, d = h.shape
    tr = _tile(l, 384)
    nsteps = l // tr

    def body(h_ref, g_ref, t_ref, loss_ref, dh_ref, dhb_ref, dg_ref, lacc, gacc):
        i = pl.program_id(0)

        @pl.when(i == 0)
        def _():
            lacc[...] = jnp.zeros_like(lacc)
            gacc[...] = jnp.zeros_like(gacc)

        xv = h_ref[...]
        g = g_ref[...]
        r = lax.rsqrt(jnp.mean(xv * xv, axis=-1, keepdims=True) + RMS_EPS)
        xn = xv * r
        rows = i * tr + lax.broadcasted_iota(jnp.int32, (tr, 1), 0)
        valid = (rows >= N_META) & (rows < N_META + seq)
        e = jnp.where(valid, xn * g - t_ref[...], 0.0)
        lacc[...] += _rows8(e * e)
        dy = e * (1.0 / d)
        gacc[...] += _rows8(dy * xn)
        dxn = dy * g
        dx = r * (dxn - xn * jnp.mean(dxn * xn, axis=-1, keepdims=True))
        dh_ref[...] = dx
        dhb_ref[...] = dx.astype(BF16)

        @pl.when(i == nsteps - 1)
        def _():
            loss_ref[...] = jnp.full((8, LANES), 0.5 / d * jnp.sum(lacc[...]), F32)
            dg_ref[...] = jnp.sum(gacc[...], axis=0, keepdims=True)

    row = pl.BlockSpec((tr, d), lambda i: (i, 0))
    gspec = pl.BlockSpec((1, d), lambda i: (0, 0))
    return pl.pallas_call(
        body, name=name, grid=(nsteps,),
        in_specs=[row, gspec, row],
        out_specs=[pl.BlockSpec((8, LANES), lambda i: (0, 0)), row, row, gspec],
        out_shape=[jax.ShapeDtypeStruct((8, LANES), F32), jax.ShapeDtypeStruct((l, d), F32),
                   jax.ShapeDtypeStruct((l, d), BF16), jax.ShapeDtypeStruct((1, d), F32)],
        scratch_shapes=[pltpu.VMEM((8, d), F32), pltpu.VMEM((8, d), F32)],
        compiler_params=_cparams("arbitrary"),
    )(h, gain, tgt)


def _shift_down(v, k, rows):
    return jnp.where(rows >= k, pltpu.roll(v, k, 0), 0.0)


def _shift_up(v, k, rows):
    l = v.shape[0]
    return jnp.where(rows < l - k, pltpu.roll(v, l - k, 0), 0.0)


def _pool_diff(n, w, rows):
    s = n
    for k in (1, 2, 4, 8):
        s = s + jnp.where(k < w, _shift_down(s, k, rows), 0.0)
    cnt = jnp.minimum(rows + 1, w).astype(F32)
    return s / cnt - n, cnt


def _pool_diff_fwd(h, r, gain, *, name):
    l, d = h.shape
    per_group = d // N_GROUPS // LANES

    def body(h_ref, r_ref, g_ref, o_ref):
        w = jnp.left_shift(2, pl.program_id(0) // per_group)
        rows = lax.broadcasted_iota(jnp.int32, (l, 1), 0)
        n = h_ref[...] * r_ref[...] * g_ref[...]
        diff, _ = _pool_diff(n, w, rows)
        o_ref[...] = diff.astype(BF16)

    col = pl.BlockSpec((l, LANES), lambda j: (0, j))
    return pl.pallas_call(
        body, name=name, grid=(d // LANES,),
        in_specs=[col, pl.BlockSpec((l, 1), lambda j: (0, 0)), pl.BlockSpec((1, LANES), lambda j: (0, j))],
        out_specs=col, out_shape=jax.ShapeDtypeStruct((l, d), BF16),
        compiler_params=_cparams("parallel"),
    )(h, r, gain)


def _pool_diff_bwd(ddiff, *, name):
    l, d = ddiff.shape
    per_group = d // N_GROUPS // LANES

    def body(dd_ref, o_ref):
        w = jnp.left_shift(2, pl.program_id(0) // per_group)
        rows = lax.broadcasted_iota(jnp.int32, (l, 1), 0)
        dd = dd_ref[...]
        s = dd / jnp.minimum(rows + 1, w).astype(F32)
        for k in (1, 2, 4, 8):
            s = s + jnp.where(k < w, _shift_up(s, k, rows), 0.0)
        o_ref[...] = s - dd

    col = pl.BlockSpec((l, LANES), lambda j: (0, j))
    return pl.pallas_call(
        body, name=name, grid=(d // LANES,), in_specs=[col], out_specs=col,
        out_shape=jax.ShapeDtypeStruct((l, d), F32), compiler_params=_cparams("parallel"),
    )(ddiff)


def _pool_mix_fwd(h, diff_b, w_pool, scale, *, name):
    l, d = h.shape
    gd = d // N_GROUPS
    tr = _tile(l, 1408)

    def body(h_ref, d_ref, w_ref, s_ref, o_ref):
        y = jnp.dot(d_ref[...], w_ref[0], preferred_element_type=F32)
        o_ref[...] = h_ref[...] + y * s_ref[...]

    blk = pl.BlockSpec((tr, gd), lambda i, g: (i, g))
    vec = pl.BlockSpec((1, gd), lambda i, g: (0, g))
    return pl.pallas_call(
        body, name=name, grid=(l // tr, N_GROUPS),
        in_specs=[blk, blk, pl.BlockSpec((1, gd, gd), lambda i, g: (g, 0, 0)), vec],
        out_specs=blk, out_shape=jax.ShapeDtypeStruct((l, d), F32),
        compiler_params=_cparams("parallel", "parallel"),
    )(h, diff_b, w_pool, scale)


def _pool_mix_bwd(diff_b, w_pool, scale, dh1, *, name):
    l, d = dh1.shape
    gd = d // N_GROUPS
    tr = _tile(l, 1408)
    nsteps = l // tr

    def body(d_ref, w_ref, s_ref, dy_ref, dd_ref, dw_ref, ds_ref, sacc):
        i = pl.program_id(1)

        @pl.when(i == 0)
        def _():
            dw_ref[...] = jnp.zeros_like(dw_ref)
            sacc[...] = jnp.zeros_like(sacc)

        diff_b_, wg, dy = d_ref[...], w_ref[0], dy_ref[...]
        yy = jnp.dot(diff_b_, wg, preferred_element_type=F32)
        sacc[...] += _rows8(dy * yy)
        dyy_b = (dy * s_ref[...]).astype(BF16)
        dw_ref[0] += lax.dot_general(diff_b_, dyy_b, _DIMS["tn"], preferred_element_type=F32)
        dd_ref[...] = lax.dot_general(dyy_b, wg, _DIMS["nt"], preferred_element_type=F32)

        @pl.when(i == nsteps - 1)
        def _():
            ds_ref[...] = jnp.sum(sacc[...], axis=0, keepdims=True)

    blk = pl.BlockSpec((tr, gd), lambda g, i: (i, g))
    vec = pl.BlockSpec((1, gd), lambda g, i: (0, g))
    wspec = pl.BlockSpec((1, gd, gd), lambda g, i: (g, 0, 0))
    return pl.pallas_call(
        body, name=name, grid=(N_GROUPS, nsteps),
        in_specs=[blk, wspec, vec, blk],
        out_specs=[blk, wspec, vec],
        out_shape=[jax.ShapeDtypeStruct((l, d), F32), jax.ShapeDtypeStruct((N_GROUPS, gd, gd), F32),
                   jax.ShapeDtypeStruct((1, d), F32)],
        scratch_shapes=[pltpu.VMEM((8, gd), F32)],
        compiler_params=_cparams("parallel", "arbitrary"),
    )(diff_b, w_pool, scale, dh1)


def _conv_chunk(cur, prev, w, b, rowi):
    s1 = jnp.where(rowi < 1, pltpu.roll(prev, 1, 0), pltpu.roll(cur, 1, 0))
    s2 = jnp.where(rowi < 2, pltpu.roll(prev, 2, 0), pltpu.roll(cur, 2, 0))
    return b + w[0] * s2 + w[1] * s1 + w[2] * cur, s1, s2


def _bcast_rows(ref, rows):
    v = ref[...]
    return [jnp.broadcast_to(v[k:k + 1], (rows, v.shape[1])) for k in range(v.shape[0])]


def _convgate_fwd(u, cw, cb, *, name, gather=None):
    l, f2 = u.shape
    f = f2 // 2
    tc = _tile(f, LANES)
    nc = f // tc
    rows = CONV_ROWS
    assert l % rows == 0

    def body(ug_ref, uv_ref, wg_ref, wv_ref, bg_ref, bv_ref, *rest):
        a_ref = rest[0] if gather is None else rest[1]
        if gather is not None:
            w_ref, _, g_ref, stage_ref, send_sems, recv_sems = rest

            @pl.when(pl.program_id(0) == 0)
            def _():
                _gather_begin(w_ref, g_ref, send_sems, recv_sems)

        rowi = lax.broadcasted_iota(jnp.int32, (rows, tc), 0)
        wg, wv = _bcast_rows(wg_ref, rows), _bcast_rows(wv_ref, rows)
        bg, bv = _bcast_rows(bg_ref, rows)[0], _bcast_rows(bv_ref, rows)[0]

        def chunk(i, carry):
            pg, pv = carry
            r = pl.multiple_of(i * rows, rows)
            cg, cv = ug_ref[pl.ds(r, rows), :], uv_ref[pl.ds(r, rows), :]
            gate, _, _ = _conv_chunk(cg, pg, wg, bg, rowi)
            val, _, _ = _conv_chunk(cv, pv, wv, bv, rowi)
            a_ref[pl.ds(r, rows), :] = (gate * jax.nn.sigmoid(gate) * val).astype(BF16)
            return cg, cv

        zero = jnp.zeros((rows, tc), F32)
        lax.fori_loop(0, l // rows, chunk, (zero, zero))
        if gather is not None:
            @pl.when(pl.program_id(0) == nc - 1)
            def _():
                _gather_end(w_ref, g_ref, stage_ref, send_sems, recv_sems)

    def spec(rows_, off):
        return pl.BlockSpec((rows_, tc), lambda j: (0, j + off))

    in_specs = [spec(l, 0), spec(l, nc), spec(3, 0), spec(3, nc), spec(1, 0), spec(1, nc)]
    out_shape = jax.ShapeDtypeStruct((l, f), BF16)
    if gather is None:
        return pl.pallas_call(
            body, name=name, grid=(nc,), in_specs=in_specs, out_specs=spec(l, 0), out_shape=out_shape,
            compiler_params=_cparams("parallel"),
        )(u, u, cw, cw, cb, cb)
    g_shape, scratch = _gather_out_and_scratch(gather)
    return pl.pallas_call(
        body, name=name, grid=(nc,), in_specs=in_specs + [_ANY], out_specs=[spec(l, 0), _ANY],
        out_shape=[out_shape, g_shape], scratch_shapes=scratch,
        compiler_params=_cparams("arbitrary"),
    )(u, u, cw, cw, cb, cb, gather)


def _convgate_bwd(u, cw, cb, da, *, name, scatter=None):
    l, f2 = u.shape
    f = f2 // 2
    tc = _tile(f, LANES)
    nc = f // tc
    rows = CONV_ROWS
    assert l % rows == 0
    nchunks = l // rows

    def body(ug_ref, uv_ref, wg_ref, wv_ref, bg_ref, bv_ref, da_ref, *rest):
        if scatter is None:
            dug_ref, duv_ref, dcwg_ref, dcwv_ref, dcbg_ref, dcbv_ref = rest
        else:
            p_ref, dug_ref, duv_ref, dcwg_ref, dcwv_ref, dcbg_ref, dcbv_ref, s_ref, stage_ref, ssems, rsems = rest

            @pl.when(pl.program_id(0) == 0)
            def _():
                _scatter_begin(p_ref, s_ref, ssems, rsems)

        rowi = lax.broadcasted_iota(jnp.int32, (rows, tc), 0)
        wg, wv = _bcast_rows(wg_ref, rows), _bcast_rows(wv_ref, rows)
        bg, bv = _bcast_rows(bg_ref, rows)[0], _bcast_rows(bv_ref, rows)[0]

        def tap_sums(acc, dc, taps):
            return (acc[0] + _rows8(dc),) + tuple(a + _rows8(dc * t) for a, t in zip(acc[1:], taps))

        def input_grad(dc, dc_next, w):
            up1 = jnp.where(rowi >= rows - 1, pltpu.roll(dc_next, rows - 1, 0), pltpu.roll(dc, rows - 1, 0))
            up2 = jnp.where(rowi >= rows - 2, pltpu.roll(dc_next, rows - 2, 0), pltpu.roll(dc, rows - 2, 0))
            return (w[2] * dc + w[1] * up1 + w[0] * up2).astype(BF16)

        def chunk(ii, carry):
            dcg_next, dcv_next, acc_g, acc_v = carry
            i = nchunks - 1 - ii
            r = pl.multiple_of(i * rows, rows)
            rp = pl.multiple_of(jnp.maximum(i - 1, 0) * rows, rows)
            cg, cv = ug_ref[pl.ds(r, rows), :], uv_ref[pl.ds(r, rows), :]
            pg = jnp.where(i > 0, ug_ref[pl.ds(rp, rows), :], 0.0)
            pv = jnp.where(i > 0, uv_ref[pl.ds(rp, rows), :], 0.0)
            gate, g1, g2 = _conv_chunk(cg, pg, wg, bg, rowi)
            val, v1, v2 = _conv_chunk(cv, pv, wv, bv, rowi)
            sg = jax.nn.sigmoid(gate)
            dav = da_ref[pl.ds(r, rows), :]
            dcg = dav * val * (sg * (1.0 + gate * (1.0 - sg)))
            dcv = dav * (gate * sg)
            acc_g = tap_sums(acc_g, dcg, (g2, g1, cg))
            acc_v = tap_sums(acc_v, dcv, (v2, v1, cv))
            dug_ref[pl.ds(r, rows), :] = input_grad(dcg, dcg_next, wg)
            duv_ref[pl.ds(r, rows), :] = input_grad(dcv, dcv_next, wv)
            return dcg, dcv, acc_g, acc_v

        zero = jnp.zeros((rows, tc), F32)
        zero8 = (jnp.zeros((8, tc), F32),) * 4
        _, _, acc_g, acc_v = lax.fori_loop(0, nchunks, chunk, (zero, zero, zero8, zero8))
        for acc, dcw_ref, dcb_ref in ((acc_g, dcwg_ref, dcbg_ref), (acc_v, dcwv_ref, dcbv_ref)):
            dcb_ref[...] = jnp.sum(acc[0], axis=0, keepdims=True)
            for k in range(3):
                dcw_ref[k:k + 1, :] = jnp.sum(acc[1 + k], axis=0, keepdims=True)
        if scatter is not None:
            @pl.when(pl.program_id(0) == nc - 1)
            def _():
                _scatter_end(p_ref, s_ref, stage_ref, ssems, rsems)

    def spec(rows_, off):
        return pl.BlockSpec((rows_, tc), lambda j: (0, j + off))

    in_specs = [spec(l, 0), spec(l, nc), spec(3, 0), spec(3, nc), spec(1, 0), spec(1, nc), spec(l, 0)]
    out_specs = [spec(l, 0), spec(l, 0), spec(3, 0), spec(3, 0), spec(1, 0), spec(1, 0)]
    out_shape = ([jax.ShapeDtypeStruct((l, f), BF16)] * 2 + [jax.ShapeDtypeStruct((3, f), F32)] * 2
                 + [jax.ShapeDtypeStruct((1, f), F32)] * 2)
    if scatter is None:
        return pl.pallas_call(
            body, name=name, grid=(nc,), in_specs=in_specs, out_specs=out_specs, out_shape=out_shape,
            compiler_params=_cparams("parallel"),
        )(u, u, cw, cw, cb, cb, da)
    s_shape, scratch = _scatter_out_and_scratch(scatter)
    return pl.pallas_call(
        body, name=name, grid=(nc,), in_specs=in_specs + [_ANY], out_specs=out_specs + [_ANY],
        out_shape=out_shape + [s_shape], scratch_shapes=scratch,
        compiler_params=_cparams("arbitrary"),
    )(u, u, cw, cw, cb, cb, da, scatter)


def _cumsum_mm(v_b, t2):
    return jnp.dot(v_b, t2, preferred_element_type=F32)


def _tri_and_ones(tri_fn):
    row = lax.broadcasted_iota(jnp.int32, (BLK, 2 * BLK), 0)
    col = lax.broadcasted_iota(jnp.int32, (BLK, 2 * BLK), 1)
    return jnp.where((col >= BLK) | tri_fn(row, col), 1.0, 0.0).astype(BF16)


def _logits(z, mask):
    sp = jnp.log(1.0 + jnp.exp(-jnp.abs(z)))
    lb = jnp.minimum(z, 0.0) - sp
    lm = lb - z
    if mask is not None:
        lm = jnp.where(mask, lm, 0.0)
    return lb, lm


def _software_pipeline(stages, n, block_of, state):
    ns = len(stages)
    inflight = [None] * (ns - 1)
    for t in range(ns - 1):
        new = list(inflight)
        for s in range(t, -1, -1):
            y, state = stages[s](block_of(t - s), None if s == 0 else inflight[s - 1], state)
            new[s] = y
        inflight = new

    def steady(i, carry):
        inflight, state = carry
        new = [None] * (ns - 1)
        for s in range(ns - 1, -1, -1):
            y, state = stages[s](block_of(i + ns - 1 - s), None if s == 0 else inflight[s - 1], state)
            if s < ns - 1:
                new[s] = y
        return tuple(new), state

    inflight, state = lax.fori_loop(0, n - (ns - 1), steady, (tuple(inflight), state))
    inflight = list(inflight)
    for e in range(1, ns):
        new = list(inflight)
        for s in range(ns - 1, e - 1, -1):
            y, state = stages[s](block_of(n - 1 + e - s), inflight[s - 1], state)
            if s < ns - 1:
                new[s] = y
        inflight = new
    return state


def _attn_fwd(q, k, v, *, name):
    nh, l, dh = q.shape
    assert l % ATT_BQ == 0
    nq = l // ATT_BQ
    qscale = HEAD_DIM ** -0.5
    chunks = range(ATT_CHUNKS)

    def body(q_ref, k_ref, v_ref, o_ref, lt_ref, z_scr, cs_scr, pv_scr):
        t_later = _tri_and_ones(lambda r, c: r > c)
        dmask = (lax.broadcasted_iota(jnp.int32, (BLK, BLK), 1)
                 < lax.broadcasted_iota(jnp.int32, (BLK, BLK), 0))

        def logits(qc, c0, mask):
            z = lax.dot_general(qc, k_ref[0, pl.ds(c0, BLK), :], _DIMS["nt"], preferred_element_type=F32)
            lb, lm = _logits(z, mask)
            return lb, lm.astype(BF16)

        def weights(lb, lm_b, mask, run):
            cs = _cumsum_mm(lm_b, t_later)
            a = jnp.exp(lb + cs[:, :BLK] + run)
            if mask is not None:
                a = jnp.where(mask, a, 0.0)
            return a.astype(BF16), run + cs[:, BLK:]

        def accumulate(a_b, c0, acc):
            return acc + jnp.dot(a_b, v_ref[0, pl.ds(c0, BLK), :], preferred_element_type=F32)

        def qblock(qb, _):
            r0 = pl.multiple_of(qb * ATT_BQ, ATT_BQ)
            qs = [q_ref[0, pl.ds(r0 + rc * BLK, BLK), :] * jnp.asarray(qscale, BF16) for rc in chunks]
            accs = [jnp.zeros((BLK, dh), F32)] * ATT_CHUNKS
            runs = [jnp.zeros((BLK, BLK), F32)] * ATT_CHUNKS
            for dj in range(ATT_CHUNKS - 1, -1, -1):
                c0 = r0 + dj * BLK
                for rc in range(dj, ATT_CHUNKS):
                    mask = dmask if rc == dj else None
                    a_b, runs[rc] = weights(*logits(qs[rc], c0, mask), mask, runs[rc])
                    accs[rc] = accumulate(a_b, c0, accs[rc])

            def col0(b):
                return pl.multiple_of(r0 - (b + 1) * BLK, BLK)

            def stage_scores(b, _, state):
                for rc in chunks:
                    z_scr[rc] = lax.dot_general(qs[rc], k_ref[0, pl.ds(col0(b), BLK), :], _DIMS["nt"],
                                                preferred_element_type=F32)
                return (), state

            def stage_cumsum(b, _, state):
                lbs = []
                for rc in chunks:
                    lb, lm = _logits(z_scr[rc], None)
                    cs_scr[rc] = _cumsum_mm(lm.astype(BF16), t_later)
                    lbs.append(lb)
                return tuple(lbs), state

            def stage_weights(b, lbs, state):
                accs, runs = state
                new_runs = []
                for rc in chunks:
                    cs = cs_scr[rc]
                    a = jnp.exp(lbs[rc] + cs[:, :BLK] + runs[rc])
                    pv_scr[rc] = jnp.dot(a.astype(BF16), v_ref[0, pl.ds(col0(b), BLK), :],
                                         preferred_element_type=F32)
                    new_runs.append(runs[rc] + cs[:, BLK:])
                return (), (accs, tuple(new_runs))

            def stage_acc(b, _, state):
                accs, runs = state
                return None, (tuple(accs[rc] + pv_scr[rc] for rc in chunks), runs)

            def left_region(state):
                return _software_pipeline([stage_scores, stage_cumsum, stage_weights, stage_acc],
                                          qb * ATT_CHUNKS, lambda b: b, state)

            accs, runs = lax.cond(qb > 0, left_region, lambda s: s, (tuple(accs), tuple(runs)))
            for rc in chunks:
                o_ref[0, pl.ds(r0 + rc * BLK, BLK), :] = accs[rc].astype(BF16)
                lt_ref[0, pl.ds(r0 + rc * BLK, BLK), :] = runs[rc]
            return 0

        lax.fori_loop(0, nq, qblock, 0)

    head = pl.BlockSpec((1, l, dh), lambda h: (h, 0, 0))
    return pl.pallas_call(
        body, name=name, grid=(nh,),
        in_specs=[head, head, head],
        out_specs=[head, pl.BlockSpec((1, l, BLK), lambda h: (h, 0, 0))],
        out_shape=[jax.ShapeDtypeStruct((nh, l, dh), BF16), jax.ShapeDtypeStruct((nh, l, BLK), F32)],
        scratch_shapes=[pltpu.VMEM((ATT_CHUNKS, BLK, BLK), F32), pltpu.VMEM((ATT_CHUNKS, BLK, 2 * BLK), F32),
                        pltpu.VMEM((ATT_CHUNKS, BLK, dh), F32)],
        compiler_params=_cparams("parallel"),
    )(q, k, v)


def _attn_bwd(q, k, v, do, ltot, *, name):
    nh, l, dh = q.shape
    assert l % ATT_BQ == 0
    nq = l // ATT_BQ
    qscale = HEAD_DIM ** -0.5
    chunks = range(ATT_CHUNKS)

    def body(q_ref, k_ref, v_ref, do_ref, lt_ref, dq_ref, dk_ref, dv_ref,
             z_scr, cs_scr, da_scr, cd_scr, dqp_scr, dkp_scr, dvp_scr):
        t_incl = _tri_and_ones(lambda r, c: r <= c)
        t_excl = _tri_and_ones(lambda r, c: r < c)
        dmask = (lax.broadcasted_iota(jnp.int32, (BLK, BLK), 1)
                 < lax.broadcasted_iota(jnp.int32, (BLK, BLK), 0))
        dk_ref[...] = jnp.zeros_like(dk_ref)
        dv_ref[...] = jnp.zeros_like(dv_ref)

        def logits(qc, c0, mask):
            z = lax.dot_general(qc, k_ref[0, pl.ds(c0, BLK), :], _DIMS["nt"], preferred_element_type=F32)
            lb, lm = _logits(z, mask)
            return lb, lm.astype(BF16)

        def weights(lb, lm_b, doc, ltc, c0, mask, pre_lm):
            cs = _cumsum_mm(lm_b, t_incl)
            da = lax.dot_general(doc, v_ref[0, pl.ds(c0, BLK), :], _DIMS["nt"], preferred_element_type=F32)
            a = jnp.exp(lb + (ltc - pre_lm - cs[:, :BLK]))
            if mask is not None:
                a = jnp.where(mask, a, 0.0)
            dl = a * da
            return (jnp.exp(lb), dl, dl.astype(BF16), a.astype(BF16)), pre_lm + cs[:, BLK:]

        def logit_grad(beta, dl, dl_b, mask, pre_dl):
            cd = _cumsum_mm(dl_b, t_excl)
            dz = dl - beta * (dl + pre_dl + cd[:, :BLK])
            if mask is not None:
                dz = jnp.where(mask, dz, 0.0)
            return (dz * qscale).astype(BF16), pre_dl + cd[:, BLK:]

        def key_grads(c0, dz_parts, a_parts, q_rows, do_rows):
            dz_all = dz_parts[0] if len(dz_parts) == 1 else jnp.concatenate(dz_parts, axis=0)
            a_all = a_parts[0] if len(a_parts) == 1 else jnp.concatenate(a_parts, axis=0)
            dk_ref[0, pl.ds(c0, BLK), :] += lax.dot_general(dz_all, q_rows, _DIMS["tn"], preferred_element_type=F32)
            dv_ref[0, pl.ds(c0, BLK), :] += lax.dot_general(a_all, do_rows, _DIMS["tn"], preferred_element_type=F32)

        def qblock(qb, _):
            r0 = pl.multiple_of(qb * ATT_BQ, ATT_BQ)
            q_raw = q_ref[0, pl.ds(r0, ATT_BQ), :]
            do_all = do_ref[0, pl.ds(r0, ATT_BQ), :]
            qs = [q_raw[rc * BLK:(rc + 1) * BLK] * jnp.asarray(qscale, BF16) for rc in chunks]
            dos = [do_all[rc * BLK:(rc + 1) * BLK] for rc in chunks]
            lts = [lt_ref[0, pl.ds(r0 + rc * BLK, BLK), :] for rc in chunks]

            def col0(b):
                return pl.multiple_of(b * BLK, BLK)

            def stage_scores(b, _, state):
                kj = k_ref[0, pl.ds(col0(b), BLK), :]
                for rc in chunks:
                    z_scr[rc] = lax.dot_general(qs[rc], kj, _DIMS["nt"], preferred_element_type=F32)
                return (), state

            def stage_prefix(b, _, state):
                vj = v_ref[0, pl.ds(col0(b), BLK), :]
                lbs = []
                for rc in chunks:
                    lb, lm = _logits(z_scr[rc], None)
                    cs_scr[rc] = _cumsum_mm(lm.astype(BF16), t_incl)
                    da_scr[rc] = lax.dot_general(dos[rc], vj, _DIMS["nt"], preferred_element_type=F32)
                    lbs.append(lb)
                return tuple(lbs), state

            def stage_weights(b, lbs, state):
                dqs, pls, pds = state
                out, new_pls = [], []
                for rc in chunks:
                    cs = cs_scr[rc]
                    a = jnp.exp(lbs[rc] + (lts[rc] - pls[rc] - cs[:, :BLK]))
                    dl = a * da_scr[rc]
                    cd_scr[rc] = _cumsum_mm(dl.astype(BF16), t_excl)
                    out.append((jnp.exp(lbs[rc]), dl, a.astype(BF16)))
                    new_pls.append(pls[rc] + cs[:, BLK:])
                return tuple(out), (dqs, tuple(new_pls), pds)

            def stage_products(b, ys, state):
                dqs, pls, pds = state
                c0 = col0(b)
                kj = k_ref[0, pl.ds(c0, BLK), :]
                dz_bs, new_pds = [], []
                for rc in chunks:
                    beta, dl, _ = ys[rc]
                    cd = cd_scr[rc]
                    dz_b = ((dl - beta * (dl + pds[rc] + cd[:, :BLK])) * qscale).astype(BF16)
                    dqp_scr[rc] = jnp.dot(dz_b, kj, preferred_element_type=F32)
                    dz_bs.append(dz_b)
                    new_pds.append(pds[rc] + cd[:, BLK:])
                dkp_scr[...] = lax.dot_general(jnp.concatenate(dz_bs, axis=0), q_raw, _DIMS["tn"],
                                               preferred_element_type=F32)
                dvp_scr[...] = lax.dot_general(jnp.concatenate([y[2] for y in ys], axis=0), do_all, _DIMS["tn"],
                                               preferred_element_type=F32)
                return (), (dqs, pls, tuple(new_pds))

            def stage_acc(b, _, state):
                dqs, pls, pds = state
                c0 = col0(b)
                dk_ref[0, pl.ds(c0, BLK), :] += dkp_scr[...]
                dv_ref[0, pl.ds(c0, BLK), :] += dvp_scr[...]
                return None, (tuple(dqs[rc] + dqp_scr[rc] for rc in chunks), pls, pds)

            stages = [stage_scores, stage_prefix, stage_weights, stage_products, stage_acc]
            zero = jnp.zeros((BLK, BLK), F32)
            state = ((jnp.zeros((BLK, dh), F32),) * ATT_CHUNKS, (zero,) * ATT_CHUNKS, (zero,) * ATT_CHUNKS)

            def pipelined(state):
                return _software_pipeline(stages, qb * ATT_CHUNKS, lambda b: b, state)

            def one_by_one(state):
                def block(b, state):
                    x = None
                    for stage in stages:
                        x, state = stage(b, x, state)
                    return state
                return lax.fori_loop(0, qb * ATT_CHUNKS, block, state)

            dqs, pls, pds = lax.cond(qb * ATT_CHUNKS >= len(stages) - 1, pipelined, one_by_one, state)
            dqs, pls, pds = list(dqs), list(pls), list(pds)
            for dj in chunks:
                c0 = r0 + dj * BLK
                kj = k_ref[0, pl.ds(c0, BLK), :]
                dz_parts, a_parts = [], []
                for rc in range(dj, ATT_CHUNKS):
                    mask = dmask if rc == dj else None
                    ys, pls[rc] = weights(*logits(qs[rc], c0, mask), dos[rc], lts[rc], c0, mask, pls[rc])
                    dz_b, pds[rc] = logit_grad(*ys[:3], mask, pds[rc])
                    a_b = ys[3]
                    dqs[rc] = dqs[rc] + jnp.dot(dz_b, kj, preferred_element_type=F32)
                    dz_parts.append(dz_b)
                    a_parts.append(a_b)
                key_grads(c0, dz_parts, a_parts, q_raw[dj * BLK:], do_all[dj * BLK:])
            for rc in chunks:
                dq_ref[0, pl.ds(r0 + rc * BLK, BLK), :] = dqs[rc]
            return 0

        lax.fori_loop(0, nq, qblock, 0)

    head = pl.BlockSpec((1, l, dh), lambda h: (h, 0, 0))
    sq = pltpu.VMEM((ATT_CHUNKS, BLK, BLK), F32)
    sw = pltpu.VMEM((ATT_CHUNKS, BLK, 2 * BLK), F32)
    return pl.pallas_call(
        body, name=name, grid=(nh,),
        in_specs=[head, head, head, head, pl.BlockSpec((1, l, BLK), lambda h: (h, 0, 0))],
        out_specs=[head, head, head],
        out_shape=[jax.ShapeDtypeStruct((nh, l, dh), F32)] * 3,
        scratch_shapes=[sq, sw, sq, sw, pltpu.VMEM((ATT_CHUNKS, BLK, dh), F32),
                        pltpu.VMEM((BLK, dh), F32), pltpu.VMEM((BLK, dh), F32)],
        compiler_params=_cparams("parallel"),
    )(q, k, v, do, ltot)


def _add_to_bf16(a, b, *, name):
    n, r, c = a.shape
    tr = _row_tile(r, 1536, 16)
    blk = pl.BlockSpec((1, tr, c), lambda i, j: (i, j, 0))

    def body(a_ref, b_ref, o_ref):
        o_ref[...] = (a_ref[...] + b_ref[...].astype(F32)).astype(BF16)

    return pl.pallas_call(body, name=name, grid=(n, r // tr), in_specs=[blk, blk], out_specs=blk,
                          out_shape=jax.ShapeDtypeStruct(a.shape, BF16),
                          compiler_params=_cparams("parallel", "parallel"))(a, b)


def _sum_slots(p, *, name):
    n, r, c = p.shape
    tr = _row_tile(r, 1536, 16)

    def body(p_ref, o_ref):
        acc = p_ref[0].astype(F32)
        for s in range(1, n):
            acc = acc + p_ref[s].astype(F32)
        o_ref[...] = acc

    return pl.pallas_call(body, name=name, grid=(r // tr,),
                          in_specs=[pl.BlockSpec((n, tr, c), lambda j: (0, j, 0))],
                          out_specs=pl.BlockSpec((tr, c), lambda j: (j, 0)),
                          out_shape=jax.ShapeDtypeStruct((r, c), F32),
                          compiler_params=_cparams("parallel"))(p)


def _adamw(w, g, m, v, *, name):
    r, c = w.shape
    tr = _row_tile(r, 512, 8)
    blk = pl.BlockSpec((tr, c), lambda i: (i, 0))

    def body(w_ref, g_ref, m_ref, v_ref, d_ref, mo_ref, vo_ref):
        gv = g_ref[...]
        mn = ADAM_B1 * m_ref[...] + (1.0 - ADAM_B1) * gv
        vn = ADAM_B2 * v_ref[...] + (1.0 - ADAM_B2) * (gv * gv)
        m_hat = mn / (1.0 - ADAM_B1 ** ADAM_STEP)
        v_hat = vn / (1.0 - ADAM_B2 ** ADAM_STEP)
        d_ref[...] = -ADAM_LR * (m_hat / (jnp.sqrt(v_hat) + ADAM_EPS) + ADAM_WD * w_ref[...])
        mo_ref[...] = mn
        vo_ref[...] = vn

    return pl.pallas_call(body, name=name, grid=(r // tr,), in_specs=[blk] * 4, out_specs=[blk] * 3,
                          out_shape=[jax.ShapeDtypeStruct((r, c), F32)] * 3,
                          compiler_params=_cparams("parallel"))(w, g, m, v)


_ANY = pl.BlockSpec(memory_space=pl.ANY)


def _place():
    x, y, c = lax.axis_index("x"), lax.axis_index("y"), lax.axis_index("c")
    chips = [(1 - x, y), (x, 1 - y), (1 - x, 1 - y)]
    return x, y, c, chips


def _gather_copy(k, src, dst, to, send_sems, recv_sems):
    return pltpu.make_async_remote_copy(src_ref=src, dst_ref=dst, send_sem=send_sems.at[k],
                                        recv_sem=recv_sems.at[k], device_id=to, device_id_type=MESH)


def _gather_begin(w_ref, out_ref, send_sems, recv_sems):
    x, y, c, chips = _place()
    s = 2 * x + y
    for j, chip in enumerate(chips):
        _gather_copy(j, w_ref.at[c], out_ref.at[s, c], (*chip, c), send_sems, recv_sems).start()


def _gather_end(w_ref, out_ref, stage_ref, send_sems, recv_sems):
    x, y, c, chips = _place()
    s = 2 * x + y
    sibling = (x, y, 1 - c)
    slots = [2 * px + py for px, py in chips]
    passed = []
    for j, sp in enumerate(slots):
        _gather_copy(j, w_ref.at[c], out_ref.at[sp, c], sibling, send_sems, recv_sems).wait_recv()
        fwd = _gather_copy(3 + j, out_ref.at[sp, c], out_ref.at[sp, c], sibling, send_sems, recv_sems)
        fwd.start()
        passed.append(fwd)
    for h in range(2):
        pltpu.sync_copy(w_ref.at[h], stage_ref)
        pltpu.sync_copy(stage_ref, out_ref.at[s, h])
    for j, sp in enumerate(slots):
        _gather_copy(3 + j, w_ref.at[c], out_ref.at[sp, 1 - c], sibling, send_sems, recv_sems).wait_recv()
    for j, chip in enumerate(chips):
        _gather_copy(j, w_ref.at[c], out_ref.at[s, c], (*chip, c), send_sems, recv_sems).wait_send()
    for fwd in passed:
        fwd.wait_send()


def _gather_out_and_scratch(wsh):
    _, r, cdim = wsh.shape
    return (jax.ShapeDtypeStruct((N_CHIPS, 2, r, cdim), wsh.dtype),
            [pltpu.VMEM((r, cdim), wsh.dtype), pltpu.SemaphoreType.DMA((6,)), pltpu.SemaphoreType.DMA((6,))])


def _gather_shards(wsh, *, name):
    def body(w_ref, out_ref, stage_ref, send_sems, recv_sems):
        _gather_begin(w_ref, out_ref, send_sems, recv_sems)
        _gather_end(w_ref, out_ref, stage_ref, send_sems, recv_sems)

    out_shape, scratch = _gather_out_and_scratch(wsh)
    return pl.pallas_call(
        body, name=name, in_specs=[_ANY], out_specs=_ANY, out_shape=out_shape, scratch_shapes=scratch,
        compiler_params=pltpu.CompilerParams(vmem_limit_bytes=VMEM_LIMIT),
    )(wsh)


def _to_sibling(a, *, name):
    def body(a_ref, out_ref, send_sem, recv_sem):
        x, y, c, _ = _place()
        cp = pltpu.make_async_remote_copy(src_ref=a_ref, dst_ref=out_ref, send_sem=send_sem, recv_sem=recv_sem,
                                          device_id=(x, y, 1 - c), device_id_type=MESH)
        cp.start()
        cp.wait()

    return pl.pallas_call(
        body, name=name, in_specs=[_ANY], out_specs=_ANY,
        out_shape=jax.ShapeDtypeStruct(a.shape, a.dtype),
        scratch_shapes=[pltpu.SemaphoreType.DMA, pltpu.SemaphoreType.DMA],
    )(a)


def _scatter_copy(j, p_ref, out_ref, send_sems, recv_sems, receive):
    x, y, c, chips = _place()
    px, py = chips[j]
    return pltpu.make_async_remote_copy(
        src_ref=p_ref.at[2 * px + py], dst_ref=out_ref.at[(2 * px + py) if receive else (2 * x + y)],
        send_sem=send_sems.at[j], recv_sem=recv_sems.at[j], device_id=(px, py, c), device_id_type=MESH)


def _scatter_begin(p_ref, out_ref, send_sems, recv_sems):
    for j in range(N_CHIPS - 1):
        _scatter_copy(j, p_ref, out_ref, send_sems, recv_sems, False).start()


def _scatter_end(p_ref, out_ref, stage_ref, send_sems, recv_sems):
    x, y, _, _ = _place()
    s = 2 * x + y
    pltpu.sync_copy(p_ref.at[s], stage_ref)
    pltpu.sync_copy(stage_ref, out_ref.at[s])
    for j in range(N_CHIPS - 1):
        _scatter_copy(j, p_ref, out_ref, send_sems, recv_sems, True).wait_recv()
    for j in range(N_CHIPS - 1):
        _scatter_copy(j, p_ref, out_ref, send_sems, recv_sems, False).wait_send()


def _scatter_out_and_scratch(p):
    return (jax.ShapeDtypeStruct(p.shape, p.dtype),
            [pltpu.VMEM(p.shape[1:], p.dtype), pltpu.SemaphoreType.DMA((N_CHIPS - 1,)),
             pltpu.SemaphoreType.DMA((N_CHIPS - 1,))])


def _scatter_rows(p, *, name):
    def body(p_ref, out_ref, stage_ref, send_sems, recv_sems):
        _scatter_begin(p_ref, out_ref, send_sems, recv_sems)
        _scatter_end(p_ref, out_ref, stage_ref, send_sems, recv_sems)

    out_shape, scratch = _scatter_out_and_scratch(p)
    return pl.pallas_call(
        body, name=name, in_specs=[_ANY], out_specs=_ANY, out_shape=out_shape, scratch_shapes=scratch,
        compiler_params=pltpu.CompilerParams(vmem_limit_bytes=VMEM_LIMIT),
    )(p)


def _all_devices(a, reduce, *, name):
    r, cdim = a.shape

    def body(a_ref, out_ref, *scratch):
        if reduce:
            buf, send_sems, recv_sems = scratch
        else:
            buf = out_ref
            send_sems, recv_sems = scratch
        x, y, c, _ = _place()
        me = 4 * x + 2 * y + c
        buf[me] = a_ref[...]
        peers = []
        for k in range(1, N_DEV):
            dx, dy, dc = (k >> 2) & 1, (k >> 1) & 1, k & 1
            peers.append((x ^ dx, y ^ dy, c ^ dc))
        sends = []
        for k, peer in enumerate(peers):
            cp = pltpu.make_async_remote_copy(src_ref=a_ref, dst_ref=buf.at[me], send_sem=send_sems.at[k],
                                              recv_sem=recv_sems.at[k], device_id=peer, device_id_type=MESH)
            cp.start()
            sends.append(cp)
        for k, (px, py, pc) in enumerate(peers):
            pltpu.make_async_remote_copy(src_ref=a_ref, dst_ref=buf.at[4 * px + 2 * py + pc],
                                         send_sem=send_sems.at[k], recv_sem=recv_sems.at[k],
                                         device_id=(px, py, pc), device_id_type=MESH).wait_recv()
        for cp in sends:
            cp.wait_send()
        if reduce:
            acc = buf[0]
            for k in range(1, N_DEV):
                acc = acc + buf[k]
            out_ref[...] = acc

    vm = pl.BlockSpec(memory_space=pltpu.VMEM)
    sems = [pltpu.SemaphoreType.DMA((N_DEV - 1,)), pltpu.SemaphoreType.DMA((N_DEV - 1,))]
    if reduce:
        out_shape = jax.ShapeDtypeStruct((r, cdim), F32)
        scratch = [pltpu.VMEM((N_DEV, r, cdim), F32)] + sems
    else:
        out_shape = jax.ShapeDtypeStruct((N_DEV, r, cdim), F32)
        scratch = sems
    return pl.pallas_call(
        body, name=name, in_specs=[vm], out_specs=vm, out_shape=out_shape, scratch_shapes=scratch,
        compiler_params=pltpu.CompilerParams(vmem_limit_bytes=VMEM_LIMIT),
    )(a)


def _flat_rows(parts, cols):
    flat = jnp.concatenate([p.reshape(-1) for p in parts])
    padded = -(-flat.size // (8 * cols)) * (8 * cols)
    return jnp.pad(flat, (0, padded - flat.size)).reshape(-1, cols)


def kernel(x, meta_tokens, mix_norm, ffn_norm, pool_w, pool_scale, kv_norm, w_kv, w_q, w_o, ffn_w_up, ffn_conv_w, ffn_conv_b, ffn_w_down, final_norm, loss_target, m_meta_tokens, m_mix_norm, m_ffn_norm, m_pool_w, m_pool_scale, m_kv_norm, m_w_kv, m_w_q, m_w_o, m_ffn_w_up, m_ffn_conv_w, m_ffn_conv_b, m_ffn_w_down, m_final_norm, v_meta_tokens, v_mix_norm, v_ffn_norm, v_pool_w, v_pool_scale, v_kv_norm, v_w_kv, v_w_q, v_w_o, v_ffn_w_up, v_ffn_conv_w, v_ffn_conv_b, v_ffn_w_down, v_final_norm):
    seq, d = x.shape[1], x.shape[2]
    l_real = N_META + seq
    lp = -(-l_real // ATT_BQ) * ATT_BQ
    f2 = ffn_w_up.shape[2] * N_CHIPS
    f = f2 // 2
    gd = d // N_GROUPS
    chip = 2 * lax.axis_index("x") + lax.axis_index("y")
    core = lax.axis_index("c")

    cw = 1024

    def shard_halves(parts):
        return _flat_rows([p.astype(BF16) for p in parts], cw).reshape(2, -1, cw)

    def unpack(gathered, parts):
        flat, out, first = gathered.reshape(N_CHIPS, -1), [], 0
        for p in parts:
            out.append(flat[:, first:first + p.size].reshape((N_CHIPS,) + p.shape))
            first += p.size
        return out

    now_parts = [pool_w[0], ffn_w_up[0], ffn_w_down[0]]
    attn_parts = [w_kv, w_q[0], w_o[0]]
    late_parts = [ffn_w_up[1]]
    down_parts = [ffn_w_down[1]]
    wp_s, wup0_s, wdn0_s = unpack(_gather_shards(shard_halves(now_parts), name="gather_weights"), now_parts)
    wp_b = wp_s.transpose(1, 0, 2, 3).reshape(N_GROUPS, gd, gd)
    wup_b = [wup0_s.transpose(1, 0, 2).reshape(d, f2), None]
    wdn_b = [wdn0_s.reshape(f, d), None]

    small_parts = [meta_tokens, pool_scale, ffn_conv_w]
    ssizes = [p.size for p in small_parts]
    small = _flat_rows(small_parts, LANES)
    sall = _all_devices(small, False, name="gather_small")[::2].reshape(N_CHIPS, -1)
    meta_f = sall[:, :ssizes[0]].reshape(N_CHIPS, N_META, d // N_CHIPS).transpose(1, 0, 2).reshape(N_META, d)
    scale_f = sall[:, ssizes[0]:ssizes[0] + ssizes[1]].reshape(1, d)
    cw_f = sall[:, ssizes[0] + ssizes[1]:sum(ssizes)].reshape(N_CHIPS, 2, 3, f2 // N_CHIPS).transpose(1, 2, 0, 3).reshape(2, 3, f2)

    mix0, mix1 = mix_norm[0:1], mix_norm[1:2]
    fn0, fn1 = ffn_norm[0:1], ffn_norm[1:2]
    kvn_g = kv_norm.reshape(1, d)
    fin_g = final_norm.reshape(1, d)

    pad = lp - l_real
    h0 = jnp.concatenate([meta_f, x[0], jnp.zeros((pad, d), F32)], axis=0)
    tgt = jnp.pad(loss_target[0], ((N_META, pad), (0, 0)))

    _, r0 = _norm_fwd(h0, [mix0], name="norm_h0")
    diff_b = _pool_diff_fwd(h0, r0, mix0, name="pool_diff")
    h1 = _pool_mix_fwd(h0, diff_b, wp_b, scale_f, name="pool_mix")

    def ffn_fwd(h_in, layer, tag, gather_behind_up=None, gather_behind_gate=None, gather_behind_down=None):
        (fb,), rf = _norm_fwd(h_in, [ffn_norm[layer:layer + 1]], name=f"norm_ffn{tag}")
        u = _mm(fb, wup_b[layer], "nn", tm=1408, tn=512, name=f"ffn_up{tag}", gather=gather_behind_up)
        gathered = []
        if gather_behind_up is not None:
            u, g = u
            gathered.append(g)
        a = _convgate_fwd(u, cw_f[layer], ffn_conv_b[layer:layer + 1], name=f"convgate{tag}",
                          gather=gather_behind_gate)
        if gather_behind_gate is not None:
            a, g = a
            gathered.append(g)
        h_out = _mm(a, wdn_b[layer], "nn", tm=384, tn=1024, res=h_in, name=f"ffn_down{tag}",
                    gather=gather_behind_down)
        if gather_behind_down is not None:
            h_out, g = h_out
            gathered.append(g)
        return h_out, (fb, rf, u, a), gathered

    h2, ffn0_saved, (attn_gathered, late_gathered, down_gathered) = ffn_fwd(
        h1, 0, "0", shard_halves(attn_parts), shard_halves(late_parts), shard_halves(down_parts))
    wkv_s, wq_s, wo_s = unpack(attn_gathered, attn_parts)
    (wup1_s,) = unpack(late_gathered, late_parts)
    (wdn1_s,) = unpack(down_gathered, down_parts)
    wkv_b = wkv_s.transpose(1, 0, 2).reshape(d, 2 * d)
    wq_b = wq_s.reshape(d, d)
    wo_b = wo_s.reshape(d, d)
    wup_b[1] = wup1_s.transpose(1, 0, 2).reshape(d, f2)
    wdn_b[1] = wdn1_s.reshape(f, d)
    (kvn_b, n1_b), r2 = _norm_fwd(h2, [kvn_g, mix1], name="norm_h2")
    kv_b = _mm(kvn_b, wkv_b, "nn", tm=1408, tn=512, out_dtype=BF16, name="kv_proj")
    q_b = _mm(n1_b, wq_b, "nn", tm=1408, tn=512, out_dtype=BF16, name="q_proj")

    def heads(t):
        return t.reshape(lp, N_HEADS, HEAD_DIM).transpose(1, 0, 2)

    def unheads(t):
        return t.transpose(1, 0, 2).reshape(lp, N_HEADS * HEAD_DIM)

    qh, kh, vh = heads(q_b), heads(kv_b[:, :d]), heads(kv_b[:, d:])
    oh, ltot = _attn_fwd(qh, kh, vh, name="attn_fwd")
    o_b = unheads(oh)
    h3 = _mm(o_b, wo_b, "nn", tm=384, tn=1024, res=h2, name="o_proj")
    h4, ffn1_saved, _ = ffn_fwd(h3, 1, "1")

    loss8, dh4, dh4_b, dfin = _loss_bwd(h4, fin_g, tgt, seq, name="loss")

    def ffn_bwd(h_in, layer, saved, dh_out, dh_out_b, tag, scatter_behind_gate=None):
        fb, rf, u, a = saved
        dwdn = _mm(a, dh_out_b, "tn", tm=256, tn=1024, name=f"d_wdown{tag}")
        da = _mm(dh_out_b, wdn_b[layer], "nt", tm=384, tn=f, name=f"d_act{tag}")
        du_g, du_v, dcw_g, dcw_v, dcb_g, dcb_v, *scattered = _convgate_bwd(
            u, cw_f[layer], ffn_conv_b[layer:layer + 1], da, name=f"convgate_bwd{tag}", scatter=scatter_behind_gate)
        dcw = jnp.concatenate([dcw_g, dcw_v], axis=1)
        dcb = jnp.concatenate([dcb_g, dcb_v], axis=1)
        half_chips = N_CHIPS // 2
        dwup = jnp.concatenate(
            [_mm(fb, du_g, "tn", tm=512, tn=f2 // N_CHIPS, out_shards=half_chips, name=f"d_wup_gate{tag}"),
             _mm(fb, du_v, "tn", tm=512, tn=f2 // N_CHIPS, out_shards=half_chips, name=f"d_wup_val{tag}")], axis=0)
        df = _mm(du_g, wup_b[layer][:, :f], "nt", tm=384, tn=1024, name=f"d_ffn_in_gate{tag}")
        df = _mm(du_v, wup_b[layer][:, f:], "nt", tm=384, tn=1024, res=df, name=f"d_ffn_in_val{tag}")
        dh_in, dh_in_b, (dfn,) = _norm_bwd(h_in, rf, [ffn_norm[layer:layer + 1]], [df], dh_out,
                                          name=f"norm_ffn_bwd{tag}")
        return (dh_in, dh_in_b, dwup, dwdn, dcw, dcb, dfn) + tuple(scattered)

    dh3, dh3_b, dwup1, dwdn1, dcw1, dcb1, dfn1 = ffn_bwd(h3, 1, ffn1_saved, dh4, dh4_b, "1")

    dwo = _mm(o_b, dh3_b, "tn", tm=1024, tn=512, name="d_wo")
    do_b = _mm(dh3_b, wo_b, "nt", tm=384, tn=1024, out_dtype=BF16, name="d_attn_out")
    dqh, dkh, dvh = _attn_bwd(qh, kh, vh, heads(do_b), ltot, name="attn_bwd")
    dq_b = unheads(dqh).astype(BF16)
    dkv_b = jnp.concatenate([unheads(dkh), unheads(dvh)], axis=1).astype(BF16)
    dwq = _mm(n1_b, dq_b, "tn", tm=1024, tn=512, name="d_wq")
    dwkv = _mm(kvn_b, dkv_b, "tn", tm=1024, tn=512, out_shards=N_CHIPS, name="d_wkv")
    dn1 = _mm(dq_b, wq_b, "nt", tm=384, tn=1024, name="d_n1")
    dkvn = _mm(dkv_b, wkv_b, "nt", tm=384, tn=1024, name="d_kvn")
    def core_halves(parts):
        g = jnp.concatenate([p.reshape(N_CHIPS, -1) for p in parts], axis=1).reshape(N_CHIPS, 2, -1, cw)
        return (lax.dynamic_index_in_dim(g, core, axis=1, keepdims=False),
                lax.dynamic_index_in_dim(g, 1 - core, axis=1, keepdims=False).astype(BF16))

    def reduced_shard(slots, tag):
        half = _sum_slots(slots, name=f"grads_sum{tag}")
        other_half = _to_sibling(half, name=f"grads_join{tag}")
        return jnp.stack([jnp.where(core == 0, half, other_half), jnp.where(core == 0, other_half, half)]).reshape(-1)

    def unpack_shard(flat, shapes):
        out, first = [], 0
        for shape in shapes:
            size = 1
            for n in shape:
                size *= n
            out.append(flat[first:first + size].reshape(shape))
            first += size
        return out

    early_mine, early_other = core_halves([dwkv, dwq.reshape(N_CHIPS, d // N_CHIPS, d),
                                           dwo.reshape(N_CHIPS, d // N_CHIPS, d), dwup1,
                                           dwdn1.reshape(N_CHIPS, f // N_CHIPS, d)])
    dh2, dh2_b, (dmix1, dkvg), early_from_sibling = _norm_bwd(h2, r2, [mix1, kvn_g], [dn1, dkvn], dh3,
                                                             name="norm_h2_bwd", send=early_other)
    early = _add_to_bf16(early_mine, early_from_sibling, name="grads_chip_sum_early")
    dh1, dh1_b, dwup0, dwdn0, dcw0, dcb0, dfn0, early_slots = ffn_bwd(h1, 0, ffn0_saved, dh2, dh2_b, "0", early)
    g_w_kv, g_w_q, g_w_o, g_w_up1, g_w_dn1 = unpack_shard(
        reduced_shard(early_slots, "_early"),
        [w_kv.shape, w_q.shape, w_o.shape, ffn_w_up.shape[1:], ffn_w_down.shape[1:]])

    ddiff, dwp, dscale = _pool_mix_bwd(diff_b, wp_b, scale_f, dh1, name="pool_mix_bwd")
    dn0 = _pool_diff_bwd(ddiff, name="pool_diff_bwd")
    dh0, _, (dmix0,) = _norm_bwd(h0, r0, [mix0], [dn0], dh1, name="norm_h0_bwd")

    grad_x = dh0[N_META:l_real][None]
    dmeta = dh0[:N_META]

    late_mine, late_other = core_halves([dwp.reshape(N_GROUPS, N_CHIPS, gd // N_CHIPS, gd).transpose(1, 0, 2, 3),
                                         dwup0, dwdn0.reshape(N_CHIPS, f // N_CHIPS, d)])
    late = _add_to_bf16(late_mine, _to_sibling(late_other, name="grads_to_sibling_late"), name="grads_chip_sum_late")
    g_pool_w, g_w_up0, g_w_dn0 = unpack_shard(
        reduced_shard(_scatter_rows(late, name="grads_scatter_late"), "_late"),
        [pool_w.shape, ffn_w_up.shape[1:], ffn_w_down.shape[1:]])
    g_w_up = jnp.stack([g_w_up0, g_w_up1])
    g_w_dn = jnp.stack([g_w_dn0, g_w_dn1])

    dcw = jnp.stack([dcw0, dcw1])
    dcb = jnp.concatenate([dcb0, dcb1], axis=0)
    sg_parts = [jnp.concatenate([dmix0, dmix1], axis=0), jnp.concatenate([dfn0, dfn1], axis=0), dkvg, dfin, dcb,
                dmeta, dscale, dcw, loss8]
    sg_sizes = [p.size for p in sg_parts]
    sg = _all_devices(_flat_rows(sg_parts, LANES), True, name="reduce_small").reshape(-1)
    sg_offs = [0]
    for sz in sg_sizes:
        sg_offs.append(sg_offs[-1] + sz)

    def gsmall(i, shape):
        return sg[sg_offs[i]:sg_offs[i + 1]].reshape(shape)

    g_mix = gsmall(0, mix_norm.shape)
    g_ffn_norm = gsmall(1, ffn_norm.shape)
    g_kv_norm = gsmall(2, kv_norm.shape)
    g_final = gsmall(3, final_norm.shape)
    g_conv_b = gsmall(4, ffn_conv_b.shape)
    csh = d // N_CHIPS
    g_meta = lax.dynamic_slice_in_dim(gsmall(5, (N_META, d)), chip * csh, csh, axis=1)
    g_scale = lax.dynamic_slice_in_dim(gsmall(6, (1, d)), chip * csh, csh, axis=1)
    fsh = f2 // N_CHIPS
    g_conv_w = lax.dynamic_slice_in_dim(gsmall(7, (2, 3, f2)), chip * fsh, fsh, axis=2)
    loss = gsmall(8, (8 * LANES,))[0]

    weights = [meta_tokens, mix_norm, ffn_norm, pool_w, pool_scale, kv_norm, w_kv, w_q, w_o, ffn_w_up, ffn_conv_w,
               ffn_conv_b, ffn_w_down, final_norm]
    grads = [g_meta, g_mix, g_ffn_norm, g_pool_w, g_scale, g_kv_norm, g_w_kv, g_w_q, g_w_o, g_w_up, g_conv_w,
             g_conv_b, g_w_dn, g_final]
    ms = [m_meta_tokens, m_mix_norm, m_ffn_norm, m_pool_w, m_pool_scale, m_kv_norm, m_w_kv, m_w_q, m_w_o,
          m_ffn_w_up, m_ffn_conv_w, m_ffn_conv_b, m_ffn_w_down, m_final_norm]
    vs = [v_meta_tokens, v_mix_norm, v_ffn_norm, v_pool_w, v_pool_scale, v_kv_norm, v_w_kv, v_w_q, v_w_o,
          v_ffn_w_up, v_ffn_conv_w, v_ffn_conv_b, v_ffn_w_down, v_final_norm]
    names = ["meta", "mix", "ffnnorm", "poolw", "poolscale", "kvnorm", "wkv", "wq", "wo", "wup", "convw", "convb",
             "wdown", "final"]
    deltas, new_ms, new_vs = [], [], []
    for w, g, m, v, nm in zip(weights, grads, ms, vs, names):
        cols = w.shape[-1]
        if w.size % (8 * LANES) == 0 and w.ndim == 1:
            cols = LANES
        view = (w.size // cols, cols)
        dl, mn, vn = _adamw(w.reshape(view), g.reshape(view), m.reshape(view), v.reshape(view), name=f"adamw_{nm}")
        deltas.append(dl.reshape(w.shape))
        new_ms.append(mn.reshape(w.shape))
        new_vs.append(vn.reshape(w.shape))

    return (loss, grad_x, *grads, *deltas, *new_ms, *new_vs)
```

```python
import jax
import jax.numpy as jnp
from jax import lax
from jax.experimental import pallas as pl
from jax.experimental.pallas import tpu as pltpu

F32 = jnp.float32
BF16 = jnp.bfloat16

N_META = 16
N_HEADS = 16
HEAD_DIM = 64
POOL_WINDOWS = (2, 4, 8, 16)
N_GROUPS = 4
RMS_EPS = 1e-6
ADAM_LR = 0.001
ADAM_B1 = 0.9
ADAM_B2 = 0.999
ADAM_EPS = 1e-08
ADAM_WD = 0.01
ADAM_STEP = 10

LANES = 128
BLK = 128
ATT_CHUNKS = 3
ATT_BQ = ATT_CHUNKS * BLK
CONV_ROWS = 128
N_CHIPS = 4
N_DEV = 8
VMEM_LIMIT = 56 * 1024 * 1024
MESH = pl.DeviceIdType.MESH


def _cparams(*sem):
    return pltpu.CompilerParams(dimension_semantics=tuple(sem) if sem else None,
                                vmem_limit_bytes=VMEM_LIMIT)


def _tile(n, pref):
    best = None
    t = LANES
    while t <= min(n, pref):
        if n % t == 0:
            best = t
        t += LANES
    assert best is not None, (n, pref)
    return best


def _row_tile(r, pref, mult):
    best = r
    for t in range(mult, min(r, pref) + 1, mult):
        if r % t == 0:
            best = t
    return best


_DIMS = {
    "nn": (((1,), (0,)), ((), ())),
    "nt": (((1,), (1,)), ((), ())),
    "tn": (((0,), (0,)), ((), ())),
}


def _mm(a, b, dims, *, tm, tn, name, out_dtype=F32, res=None, out_shards=1, gather=None):
    if dims == "tn":
        k, m = a.shape
    else:
        m, k = a.shape
    n = b.shape[0] if dims == "nt" else b.shape[1]
    tm = _tile(m, tm)
    tn = _tile(n // out_shards, tn)
    a_spec = pl.BlockSpec((k, tm), lambda i, j: (0, i)) if dims == "tn" else pl.BlockSpec((tm, k), lambda i, j: (i, 0))
    b_spec = pl.BlockSpec((tn, k), lambda i, j: (j, 0)) if dims == "nt" else pl.BlockSpec((k, tn), lambda i, j: (0, j))
    o_spec = pl.BlockSpec((tm, tn), lambda i, j: (i, j))
    out_shape = jax.ShapeDtypeStruct((m, n), out_dtype)
    if out_shards > 1:
        assert res is None
        per = n // out_shards // tn
        o_spec = pl.BlockSpec((None, tm, tn), lambda i, j: (j // per, i, j % per))
        out_shape = jax.ShapeDtypeStruct((out_shards, m, n // out_shards), out_dtype)
    dn = _DIMS[dims]

    grid = (m // tm, n // tn)

    def body(*refs):
        refs = list(refs)
        a_ref, b_ref = refs[:2]
        r_ref = refs[2] if res is not None else None
        o_ref = refs[2 + (res is not None) + (gather is not None)]
        acc = lax.dot_general(a_ref[...], b_ref[...], dn, preferred_element_type=F32)
        if res is not None:
            acc = acc + r_ref[...]
        o_ref[...] = acc.astype(out_dtype)
        if gather is not None:
            w_ref, g_ref, stage_ref, send_sems, recv_sems = refs[2 + (res is not None)], *refs[-4:]
            step = pl.program_id(0) * grid[1] + pl.program_id(1)

            @pl.when(step == 0)
            def _():
                _gather_begin(w_ref, g_ref, send_sems, recv_sems)

            @pl.when(step == grid[0] * grid[1] - 1)
            def _():
                _gather_end(w_ref, g_ref, stage_ref, send_sems, recv_sems)

    ins = [a, b] + ([] if res is None else [res])
    specs = [a_spec, b_spec] + ([] if res is None else [o_spec])
    if gather is None:
        return pl.pallas_call(
            body, name=name, grid=grid, in_specs=specs, out_specs=o_spec, out_shape=out_shape,
            compiler_params=_cparams("parallel", "arbitrary"),
        )(*ins)
    g_shape, scratch = _gather_out_and_scratch(gather)
    return pl.pallas_call(
        body, name=name, grid=grid, in_specs=specs + [_ANY], out_specs=[o_spec, _ANY],
        out_shape=[out_shape, g_shape], scratch_shapes=scratch,
        compiler_params=_cparams("arbitrary", "arbitrary"),
    )(*ins, gather)


def _norm_fwd(x, gains, *, name):
    l, d = x.shape
    tr = _tile(l, 384)
    ng = len(gains)

    def body(*refs):
        x_ref = refs[0]
        g_refs = refs[1:1 + ng]
        o_refs = refs[1 + ng:1 + 2 * ng]
        r_ref = refs[1 + 2 * ng]
        xv = x_ref[...]
        r = lax.rsqrt(jnp.mean(xv * xv, axis=-1, keepdims=True) + RMS_EPS)
        xn = xv * r
        for g_ref, o_ref in zip(g_refs, o_refs):
            o_ref[...] = (xn * g_ref[...]).astype(BF16)
        r_ref[...] = r

    row = pl.BlockSpec((tr, d), lambda i: (i, 0))
    gspec = pl.BlockSpec((1, d), lambda i: (0, 0))
    outs = pl.pallas_call(
        body, name=name, grid=(l // tr,),
        in_specs=[row] + [gspec] * ng,
        out_specs=[row] * ng + [pl.BlockSpec((tr, 1), lambda i: (i, 0))],
        out_shape=[jax.ShapeDtypeStruct((l, d), BF16)] * ng + [jax.ShapeDtypeStruct((l, 1), F32)],
        compiler_params=_cparams("parallel"),
    )(x, *gains)
    return list(outs[:ng]), outs[ng]


def _rows8(v):
    r, d = v.shape
    return jnp.sum(v.reshape(r // 8, 8, d), axis=0)


def _norm_bwd(x, r, gains, dns, dres, *, name, send=None):
    l, d = x.shape
    tr = _tile(l, 384)
    ng = len(gains)
    nsteps = l // tr

    def body(*refs):
        refs = list(refs)
        if send is not None:
            send_ref, got_ref, sems = refs.pop(3 + 2 * ng), refs.pop(5 + 3 * ng), (refs.pop(), refs.pop())[::-1]
        x_ref, r_ref, dres_ref = refs[0], refs[1], refs[2]
        g_refs = refs[3:3 + ng]
        dn_refs = refs[3 + ng:3 + 2 * ng]
        dx_ref, dxb_ref = refs[3 + 2 * ng], refs[4 + 2 * ng]
        dg_refs = refs[5 + 2 * ng:5 + 3 * ng]
        acc_refs = refs[5 + 3 * ng:5 + 4 * ng]
        i = pl.program_id(0)
        if send is not None:
            x_, y_, c_, _ = _place()
            exchange = pltpu.make_async_remote_copy(src_ref=send_ref, dst_ref=got_ref, send_sem=sems[0],
                                                    recv_sem=sems[1], device_id=(x_, y_, 1 - c_), device_id_type=MESH)

        @pl.when(i == 0)
        def _():
            for acc in acc_refs:
                acc[...] = jnp.zeros_like(acc)
            if send is not None:
                exchange.start()

        rv = r_ref[...]
        xn = x_ref[...] * rv
        total = dres_ref[...]
        for g_ref, dn_ref, acc in zip(g_refs, dn_refs, acc_refs):
            dn = dn_ref[...]
            acc[...] += _rows8(dn * xn)
            dxn = dn * g_ref[...]
            total = total + rv * (dxn - xn * jnp.mean(dxn * xn, axis=-1, keepdims=True))
        dx_ref[...] = total
        dxb_ref[...] = total.astype(BF16)

        @pl.when(i == nsteps - 1)
        def _():
            for dg_ref, acc in zip(dg_refs, acc_refs):
                dg_ref[...] = jnp.sum(acc[...], axis=0, keepdims=True)
            if send is not None:
                exchange.wait()

    row = pl.BlockSpec((tr, d), lambda i: (i, 0))
    gspec = pl.BlockSpec((1, d), lambda i: (0, 0))
    extra_in, extra_out, extra_scratch, extra_args = [], [], [], []
    if send is not None:
        extra_in, extra_args = [_ANY], [send]
        extra_out = [(_ANY, jax.ShapeDtypeStruct(send.shape, send.dtype))]
        extra_scratch = [pltpu.SemaphoreType.DMA, pltpu.SemaphoreType.DMA]
    outs = pl.pallas_call(
        body, name=name, grid=(nsteps,),
        in_specs=[row, pl.BlockSpec((tr, 1), lambda i: (i, 0)), row] + [gspec] * ng + [row] * ng + extra_in,
        out_specs=[row, row] + [gspec] * ng + [o[0] for o in extra_out],
        out_shape=[jax.ShapeDtypeStruct((l, d), F32), jax.ShapeDtypeStruct((l, d), BF16)]
        + [jax.ShapeDtypeStruct((1, d), F32)] * ng + [o[1] for o in extra_out],
        scratch_shapes=[pltpu.VMEM((8, d), F32)] * ng + extra_scratch,
        compiler_params=_cparams("arbitrary"),
    )(x, r, dres, *gains, *dns, *extra_args)
    if send is not None:
        return outs[0], outs[1], list(outs[2:2 + ng]), outs[2 + ng]
    return outs[0], outs[1], list(outs[2:])


def _loss_bwd(h, gain, tgt, seq, *, name):
    l, d = h.shape
    tr = _tile(l, 384)
    nsteps = l // tr

    def body(h_ref, g_ref, t_ref, loss_ref, dh_ref, dhb_ref, dg_ref, lacc, gacc):
        i = pl.program_id(0)

        @pl.when(i == 0)
        def _():
            lacc[...] = jnp.zeros_like(lacc)
            gacc[...] = jnp.zeros_like(gacc)

        xv = h_ref[...]
        g = g_ref[...]
        r = lax.rsqrt(jnp.mean(xv * xv, axis=-1, keepdims=True) + RMS_EPS)
        xn = xv * r
        rows = i * tr + lax.broadcasted_iota(jnp.int32, (tr, 1), 0)
        valid = (rows >= N_META) & (rows < N_META + seq)
        e = jnp.where(valid, xn * g - t_ref[...], 0.0)
        lacc[...] += _rows8(e * e)
        dy = e * (1.0 / d)
        gacc[...] += _rows8(dy * xn)
        dxn = dy * g
        dx = r * (dxn - xn * jnp.mean(dxn * xn, axis=-1, keepdims=True))
        dh_ref[...] = dx
        dhb_ref[...] = dx.astype(BF16)

        @pl.when(i == nsteps - 1)
        def _():
            loss_ref[...] = jnp.full((8, LANES), 0.5 / d * jnp.sum(lacc[...]), F32)
            dg_ref[...] = jnp.sum(gacc[...], axis=0, keepdims=True)

    row = pl.BlockSpec((tr, d), lambda i: (i, 0))
    gspec = pl.BlockSpec((1, d), lambda i: (0, 0))
    return pl.pallas_call(
        body, name=name, grid=(nsteps,),
        in_specs=[row, gspec, row],
        out_specs=[pl.BlockSpec((8, LANES), lambda i: (0, 0)), row, row, gspec],
        out_shape=[jax.ShapeDtypeStruct((8, LANES), F32), jax.ShapeDtypeStruct((l, d), F32),
                   jax.ShapeDtypeStruct((l, d), BF16), jax.ShapeDtypeStruct((1, d), F32)],
        scratch_shapes=[pltpu.VMEM((8, d), F32), pltpu.VMEM((8, d), F32)],
        compiler_params=_cparams("arbitrary"),
    )(h, gain, tgt)


def _shift_down(v, k, rows):
    return jnp.where(rows >= k, pltpu.roll(v, k, 0), 0.0)


def _shift_up(v, k, rows):
    l = v.shape[0]
    return jnp.where(rows < l - k, pltpu.roll(v, l - k, 0), 0.0)


def _pool_diff(n, w, rows):
    s = n
    for k in (1, 2, 4, 8):
        s = s + jnp.where(k < w, _shift_down(s, k, rows), 0.0)
    cnt = jnp.minimum(rows + 1, w).astype(F32)
    return s / cnt - n, cnt


def _pool_diff_fwd(h, r, gain, *, name):
    l, d = h.shape
    per_group = d // N_GROUPS // LANES

    def body(h_ref, r_ref, g_ref, o_ref):
        w = jnp.left_shift(2, pl.program_id(0) // per_group)
        rows = lax.broadcasted_iota(jnp.int32, (l, 1), 0)
        n = h_ref[...] * r_ref[...] * g_ref[...]
        diff, _ = _pool_diff(n, w, rows)
        o_ref[...] = diff.astype(BF16)

    col = pl.BlockSpec((l, LANES), lambda j: (0, j))
    return pl.pallas_call(
        body, name=name, grid=(d // LANES,),
        in_specs=[col, pl.BlockSpec((l, 1), lambda j: (0, 0)), pl.BlockSpec((1, LANES), lambda j: (0, j))],
        out_specs=col, out_shape=jax.ShapeDtypeStruct((l, d), BF16),
        compiler_params=_cparams("parallel"),
    )(h, r, gain)


def _pool_diff_bwd(ddiff, *, name):
    l, d = ddiff.shape
    per_group = d // N_GROUPS // LANES

    def body(dd_ref, o_ref):
        w = jnp.left_shift(2, pl.program_id(0) // per_group)
        rows = lax.broadcasted_iota(jnp.int32, (l, 1), 0)
        dd = dd_ref[...]
        s = dd / jnp.minimum(rows + 1, w).astype(F32)
        for k in (1, 2, 4, 8):
            s = s + jnp.where(k < w, _shift_up(s, k, rows), 0.0)
        o_ref[...] = s - dd

    col = pl.BlockSpec((l, LANES), lambda j: (0, j))
    return pl.pallas_call(
        body, name=name, grid=(d // LANES,), in_specs=[col], out_specs=col,
        out_shape=jax.ShapeDtypeStruct((l, d), F32), compiler_params=_cparams("parallel"),
    )(ddiff)


def _pool_mix_fwd(h, diff_b, w_pool, scale, *, name):
    l, d = h.shape
    gd = d // N_GROUPS
    tr = _tile(l, 1408)

    def body(h_ref, d_ref, w_ref, s_ref, o_ref):
        y = jnp.dot(d_ref[...], w_ref[0], preferred_element_type=F32)
        o_ref[...] = h_ref[...] + y * s_ref[...]

    blk = pl.BlockSpec((tr, gd), lambda i, g: (i, g))
    vec = pl.BlockSpec((1, gd), lambda i, g: (0, g))
    return pl.pallas_call(
        body, name=name, grid=(l // tr, N_GROUPS),
        in_specs=[blk, blk, pl.BlockSpec((1, gd, gd), lambda i, g: (g, 0, 0)), vec],
        out_specs=blk, out_shape=jax.ShapeDtypeStruct((l, d), F32),
        compiler_params=_cparams("parallel", "parallel"),
    )(h, diff_b, w_pool, scale)


def _pool_mix_bwd(diff_b, w_pool, scale, dh1, *, name):
    l, d = dh1.shape
    gd = d // N_GROUPS
    tr = _tile(l, 1408)
    nsteps = l // tr

    def body(d_ref, w_ref, s_ref, dy_ref, dd_ref, dw_ref, ds_ref, sacc):
        i = pl.program_id(1)

        @pl.when(i == 0)
        def _():
            dw_ref[...] = jnp.zeros_like(dw_ref)
            sacc[...] = jnp.zeros_like(sacc)

        diff_b_, wg, dy = d_ref[...], w_ref[0], dy_ref[...]
        yy = jnp.dot(diff_b_, wg, preferred_element_type=F32)
        sacc[...] += _rows8(dy * yy)
        dyy_b = (dy * s_ref[...]).astype(BF16)
        dw_ref[0] += lax.dot_general(diff_b_, dyy_b, _DIMS["tn"], preferred_element_type=F32)
        dd_ref[...] = lax.dot_general(dyy_b, wg, _DIMS["nt"], preferred_element_type=F32)

        @pl.when(i == nsteps - 1)
        def _():
            ds_ref[...] = jnp.sum(sacc[...], axis=0, keepdims=True)

    blk = pl.BlockSpec((tr, gd), lambda g, i: (i, g))
    vec = pl.BlockSpec((1, gd), lambda g, i: (0, g))
    wspec = pl.BlockSpec((1, gd, gd), lambda g, i: (g, 0, 0))
    return pl.pallas_call(
        body, name=name, grid=(N_GROUPS, nsteps),
        in_specs=[blk, wspec, vec, blk],
        out_specs=[blk, wspec, vec],
        out_shape=[jax.ShapeDtypeStruct((l, d), F32), jax.ShapeDtypeStruct((N_GROUPS, gd, gd), F32),
                   jax.ShapeDtypeStruct((1, d), F32)],
        scratch_shapes=[pltpu.VMEM((8, gd), F32)],
        compiler_params=_cparams("parallel", "arbitrary"),
    )(diff_b, w_pool, scale, dh1)


def _conv_chunk(cur, prev, w, b, rowi):
    s1 = jnp.where(rowi < 1, pltpu.roll(prev, 1, 0), pltpu.roll(cur, 1, 0))
    s2 = jnp.where(rowi < 2, pltpu.roll(prev, 2, 0), pltpu.roll(cur, 2, 0))
    return b + w[0] * s2 + w[1] * s1 + w[2] * cur, s1, s2


def _bcast_rows(ref, rows):
    v = ref[...]
    return [jnp.broadcast_to(v[k:k + 1], (rows, v.shape[1])) for k in range(v.shape[0])]


def _convgate_fwd(u, cw, cb, *, name, gather=None):
    l, f2 = u.shape
    f = f2 // 2
    tc = _tile(f, LANES)
    nc = f // tc
    rows = CONV_ROWS
    assert l % rows == 0

    def body(ug_ref, uv_ref, wg_ref, wv_ref, bg_ref, bv_ref, *rest):
        a_ref = rest[0] if gather is None else rest[1]
        if gather is not None:
            w_ref, _, g_ref, stage_ref, send_sems, recv_sems = rest

            @pl.when(pl.program_id(0) == 0)
            def _():
                _gather_begin(w_ref, g_ref, send_sems, recv_sems)

        rowi = lax.broadcasted_iota(jnp.int32, (rows, tc), 0)
        wg, wv = _bcast_rows(wg_ref, rows), _bcast_rows(wv_ref, rows)
        bg, bv = _bcast_rows(bg_ref, rows)[0], _bcast_rows(bv_ref, rows)[0]

        def chunk(i, carry):
            pg, pv = carry
            r = pl.multiple_of(i * rows, rows)
            cg, cv = ug_ref[pl.ds(r, rows), :], uv_ref[pl.ds(r, rows), :]
            gate, _, _ = _conv_chunk(cg, pg, wg, bg, rowi)
            val, _, _ = _conv_chunk(cv, pv, wv, bv, rowi)
            a_ref[pl.ds(r, rows), :] = (gate * jax.nn.sigmoid(gate) * val).astype(BF16)
            return cg, cv

        zero = jnp.zeros((rows, tc), F32)
        lax.fori_loop(0, l // rows, chunk, (zero, zero))
        if gather is not None:
            @pl.when(pl.program_id(0) == nc - 1)
            def _():
                _gather_end(w_ref, g_ref, stage_ref, send_sems, recv_sems)

    def spec(rows_, off):
        return pl.BlockSpec((rows_, tc), lambda j: (0, j + off))

    in_specs = [spec(l, 0), spec(l, nc), spec(3, 0), spec(3, nc), spec(1, 0), spec(1, nc)]
    out_shape = jax.ShapeDtypeStruct((l, f), BF16)
    if gather is None:
        return pl.pallas_call(
            body, name=name, grid=(nc,), in_specs=in_specs, out_specs=spec(l, 0), out_shape=out_shape,
            compiler_params=_cparams("parallel"),
        )(u, u, cw, cw, cb, cb)
    g_shape, scratch = _gather_out_and_scratch(gather)
    return pl.pallas_call(
        body, name=name, grid=(nc,), in_specs=in_specs + [_ANY], out_specs=[spec(l, 0), _ANY],
        out_shape=[out_shape, g_shape], scratch_shapes=scratch,
        compiler_params=_cparams("arbitrary"),
    )(u, u, cw, cw, cb, cb, gather)


def _convgate_bwd(u, cw, cb, da, *, name, scatter=None):
    l, f2 = u.shape
    f = f2 // 2
    tc = _tile(f, LANES)
    nc = f // tc
    rows = CONV_ROWS
    assert l % rows == 0
    nchunks = l // rows

    def body(ug_ref, uv_ref, wg_ref, wv_ref, bg_ref, bv_ref, da_ref, *rest):
        if scatter is None:
            dug_ref, duv_ref, dcwg_ref, dcwv_ref, dcbg_ref, dcbv_ref = rest
        else:
            p_ref, dug_ref, duv_ref, dcwg_ref, dcwv_ref, dcbg_ref, dcbv_ref, s_ref, stage_ref, ssems, rsems = rest

            @pl.when(pl.program_id(0) == 0)
            def _():
                _scatter_begin(p_ref, s_ref, ssems, rsems)

        rowi = lax.broadcasted_iota(jnp.int32, (rows, tc), 0)
        wg, wv = _bcast_rows(wg_ref, rows), _bcast_rows(wv_ref, rows)
        bg, bv = _bcast_rows(bg_ref, rows)[0], _bcast_rows(bv_ref, rows)[0]

        def tap_sums(acc, dc, taps):
            return (acc[0] + _rows8(dc),) + tuple(a + _rows8(dc * t) for a, t in zip(acc[1:], taps))

        def input_grad(dc, dc_next, w):
            up1 = jnp.where(rowi >= rows - 1, pltpu.roll(dc_next, rows - 1, 0), pltpu.roll(dc, rows - 1, 0))
            up2 = jnp.where(rowi >= rows - 2, pltpu.roll(dc_next, rows - 2, 0), pltpu.roll(dc, rows - 2, 0))
            return (w[2] * dc + w[1] * up1 + w[0] * up2).astype(BF16)

        def chunk(ii, carry):
            dcg_next, dcv_next, acc_g, acc_v = carry
            i = nchunks - 1 - ii
            r = pl.multiple_of(i * rows, rows)
            rp = pl.multiple_of(jnp.maximum(i - 1, 0) * rows, rows)
            cg, cv = ug_ref[pl.ds(r, rows), :], uv_ref[pl.ds(r, rows), :]
            pg = jnp.where(i > 0, ug_ref[pl.ds(rp, rows), :], 0.0)
            pv = jnp.where(i > 0, uv_ref[pl.ds(rp, rows), :], 0.0)
            gate, g1, g2 = _conv_chunk(cg, pg, wg, bg, rowi)
            val, v1, v2 = _conv_chunk(cv, pv, wv, bv, rowi)
            sg = jax.nn.sigmoid(gate)
            dav = da_ref[pl.ds(r, rows), :]
            dcg = dav * val * (sg * (1.0 + gate * (1.0 - sg)))
            dcv = dav * (gate * sg)
            acc_g = tap_sums(acc_g, dcg, (g2, g1, cg))
            acc_v = tap_sums(acc_v, dcv, (v2, v1, cv))
            dug_ref[pl.ds(r, rows), :] = input_grad(dcg, dcg_next, wg)
            duv_ref[pl.ds(r, rows), :] = input_grad(dcv, dcv_next, wv)
            return dcg, dcv, acc_g, acc_v

        zero = jnp.zeros((rows, tc), F32)
        zero8 = (jnp.zeros((8, tc), F32),) * 4
        _, _, acc_g, acc_v = lax.fori_loop(0, nchunks, chunk, (zero, zero, zero8, zero8))
        for acc, dcw_ref, dcb_ref in ((acc_g, dcwg_ref, dcbg_ref), (acc_v, dcwv_ref, dcbv_ref)):
            dcb_ref[...] = jnp.sum(acc[0], axis=0, keepdims=True)
            for k in range(3):
                dcw_ref[k:k + 1, :] = jnp.sum(acc[1 + k], axis=0, keepdims=True)
        if scatter is not None:
            @pl.when(pl.program_id(0) == nc - 1)
            def _():
                _scatter_end(p_ref, s_ref, stage_ref, ssems, rsems)

    def spec(rows_, off):
        return pl.BlockSpec((rows_, tc), lambda j: (0, j + off))

    in_specs = [spec(l, 0), spec(l, nc), spec(3, 0), spec(3, nc), spec(1, 0), spec(1, nc), spec(l, 0)]
    out_specs = [spec(l, 0), spec(l, 0), spec(3, 0), spec(3, 0), spec(1, 0), spec(1, 0)]
    out_shape = ([jax.ShapeDtypeStruct((l, f), BF16)] * 2 + [jax.ShapeDtypeStruct((3, f), F32)] * 2
                 + [jax.ShapeDtypeStruct((1, f), F32)] * 2)
    if scatter is None:
        return pl.pallas_call(
            body, name=name, grid=(nc,), in_specs=in_specs, out_specs=out_specs, out_shape=out_shape,
            compiler_params=_cparams("parallel"),
        )(u, u, cw, cw, cb, cb, da)
    s_shape, scratch = _scatter_out_and_scratch(scatter)
    return pl.pallas_call(
        body, name=name, grid=(nc,), in_specs=in_specs + [_ANY], out_specs=out_specs + [_ANY],
        out_shape=out_shape + [s_shape], scratch_shapes=scratch,
        compiler_params=_cparams("arbitrary"),
    )(u, u, cw, cw, cb, cb, da, scatter)


def _cumsum_mm(v_b, t2):
    return jnp.dot(v_b, t2, preferred_element_type=F32)


def _tri_and_ones(tri_fn):
    row = lax.broadcasted_iota(jnp.int32, (BLK, 2 * BLK), 0)
    col = lax.broadcasted_iota(jnp.int32, (BLK, 2 * BLK), 1)
    return jnp.where((col >= BLK) | tri_fn(row, col), 1.0, 0.0).astype(BF16)


def _logits(z, mask):
    sp = jnp.log(1.0 + jnp.exp(-jnp.abs(z)))
    lb = jnp.minimum(z, 0.0) - sp
    lm = lb - z
    if mask is not None:
        lm = jnp.where(mask, lm, 0.0)
    return lb, lm


def _block_start(index, size):
    return index * size if isinstance(index, int) else pl.multiple_of(index * size, size)


def _peeled_loop(body, first_looped, n):
    for i in range(min(first_looped, n)):
        body(i)
    if n > first_looped:
        def looped(i, carry):
            body(i)
            return carry
        lax.fori_loop(first_looped, n, looped, 0)


def _software_pipeline(stages, n, block_of, state):
    ns = len(stages)
    inflight = [None] * (ns - 1)
    for t in range(ns - 1):
        new = list(inflight)
        for s in range(t, -1, -1):
            y, state = stages[s](block_of(t - s), None if s == 0 else inflight[s - 1], state)
            new[s] = y
        inflight = new

    def steady(i, carry):
        inflight, state = carry
        new = [None] * (ns - 1)
        for s in range(ns - 1, -1, -1):
            y, state = stages[s](block_of(i + ns - 1 - s), None if s == 0 else inflight[s - 1], state)
            if s < ns - 1:
                new[s] = y
        return tuple(new), state

    inflight, state = lax.fori_loop(0, n - (ns - 1), steady, (tuple(inflight), state))
    inflight = list(inflight)
    for e in range(1, ns):
        new = list(inflight)
        for s in range(ns - 1, e - 1, -1):
            y, state = stages[s](block_of(n - 1 + e - s), inflight[s - 1], state)
            if s < ns - 1:
                new[s] = y
        inflight = new
    return state


def _attn_fwd(q, k, v, *, name):
    nh, l, dh = q.shape
    assert l % ATT_BQ == 0
    nq = l // ATT_BQ
    qscale = HEAD_DIM ** -0.5
    chunks = range(ATT_CHUNKS)

    def body(q_ref, k_ref, v_ref, o_ref, lt_ref, z_scr, cs_scr, pv_scr):
        t_later = _tri_and_ones(lambda r, c: r > c)
        dmask = (lax.broadcasted_iota(jnp.int32, (BLK, BLK), 1)
                 < lax.broadcasted_iota(jnp.int32, (BLK, BLK), 0))

        def logits(qc, c0, mask):
            z = lax.dot_general(qc, k_ref[0, pl.ds(c0, BLK), :], _DIMS["nt"], preferred_element_type=F32)
            lb, lm = _logits(z, mask)
            return lb, lm.astype(BF16)

        def weights(lb, lm_b, mask, run):
            cs = _cumsum_mm(lm_b, t_later)
            a = jnp.exp(lb + cs[:, :BLK] + run)
            if mask is not None:
                a = jnp.where(mask, a, 0.0)
            return a.astype(BF16), run + cs[:, BLK:]

        def accumulate(a_b, c0, acc):
            return acc + jnp.dot(a_b, v_ref[0, pl.ds(c0, BLK), :], preferred_element_type=F32)

        def qblock(qb):
            r0 = _block_start(qb, ATT_BQ)
            qs = [q_ref[0, pl.ds(r0 + rc * BLK, BLK), :] * jnp.asarray(qscale, BF16) for rc in chunks]
            accs = [jnp.zeros((BLK, dh), F32)] * ATT_CHUNKS
            runs = [jnp.zeros((BLK, BLK), F32)] * ATT_CHUNKS
            for dj in range(ATT_CHUNKS - 1, -1, -1):
                c0 = r0 + dj * BLK
                for rc in range(dj, ATT_CHUNKS):
                    mask = dmask if rc == dj else None
                    a_b, runs[rc] = weights(*logits(qs[rc], c0, mask), mask, runs[rc])
                    accs[rc] = accumulate(a_b, c0, accs[rc])

            def col0(b):
                return pl.multiple_of(r0 - (b + 1) * BLK, BLK)

            def stage_scores(b, _, state):
                for rc in chunks:
                    z_scr[rc] = lax.dot_general(qs[rc], k_ref[0, pl.ds(col0(b), BLK), :], _DIMS["nt"],
                                                preferred_element_type=F32)
                return (), state

            def stage_cumsum(b, _, state):
                lbs = []
                for rc in chunks:
                    lb, lm = _logits(z_scr[rc], None)
                    cs_scr[rc] = _cumsum_mm(lm.astype(BF16), t_later)
                    lbs.append(lb)
                return tuple(lbs), state

            def stage_weights(b, lbs, state):
                accs, runs = state
                new_runs = []
                for rc in chunks:
                    cs = cs_scr[rc]
                    a = jnp.exp(lbs[rc] + cs[:, :BLK] + runs[rc])
                    pv_scr[rc] = jnp.dot(a.astype(BF16), v_ref[0, pl.ds(col0(b), BLK), :],
                                         preferred_element_type=F32)
                    new_runs.append(runs[rc] + cs[:, BLK:])
                return (), (accs, tuple(new_runs))

            def stage_acc(b, _, state):
                accs, runs = state
                return None, (tuple(accs[rc] + pv_scr[rc] for rc in chunks), runs)

            def left_region(state):
                return _software_pipeline([stage_scores, stage_cumsum, stage_weights, stage_acc],
                                          qb * ATT_CHUNKS, lambda b: b, state)

            accs, runs = tuple(accs), tuple(runs)
            if not (isinstance(qb, int) and qb == 0):
                accs, runs = left_region((accs, runs))
            for rc in chunks:
                o_ref[0, pl.ds(r0 + rc * BLK, BLK), :] = accs[rc].astype(BF16)
                lt_ref[0, pl.ds(r0 + rc * BLK, BLK), :] = runs[rc]

        _peeled_loop(qblock, 1, nq)

    head = pl.BlockSpec((1, l, dh), lambda h: (h, 0, 0))
    return pl.pallas_call(
        body, name=name, grid=(nh,),
        in_specs=[head, head, head],
        out_specs=[head, pl.BlockSpec((1, l, BLK), lambda h: (h, 0, 0))],
        out_shape=[jax.ShapeDtypeStruct((nh, l, dh), BF16), jax.ShapeDtypeStruct((nh, l, BLK), F32)],
        scratch_shapes=[pltpu.VMEM((ATT_CHUNKS, BLK, BLK), F32), pltpu.VMEM((ATT_CHUNKS, BLK, 2 * BLK), F32),
                        pltpu.VMEM((ATT_CHUNKS, BLK, dh), F32)],
        compiler_params=_cparams("parallel"),
    )(q, k, v)


def _attn_bwd(q, k, v, do, ltot, *, name):
    nh, l, dh = q.shape
    assert l % ATT_BQ == 0
    nq = l // ATT_BQ
    qscale = HEAD_DIM ** -0.5
    chunks = range(ATT_CHUNKS)

    def body(q_ref, k_ref, v_ref, do_ref, lt_ref, dq_ref, dk_ref, dv_ref,
             z_scr, cs_scr, da_scr, cd_scr, dqp_scr, dkp_scr, dvp_scr):
        t_incl = _tri_and_ones(lambda r, c: r <= c)
        t_excl = _tri_and_ones(lambda r, c: r < c)
        dmask = (lax.broadcasted_iota(jnp.int32, (BLK, BLK), 1)
                 < lax.broadcasted_iota(jnp.int32, (BLK, BLK), 0))
        dk_ref[...] = jnp.zeros_like(dk_ref)
        dv_ref[...] = jnp.zeros_like(dv_ref)

        def logits(qc, c0, mask):
            z = lax.dot_general(qc, k_ref[0, pl.ds(c0, BLK), :], _DIMS["nt"], preferred_element_type=F32)
            lb, lm = _logits(z, mask)
            return lb, lm.astype(BF16)

        def weights(lb, lm_b, doc, ltc, c0, mask, pre_lm):
            cs = _cumsum_mm(lm_b, t_incl)
            da = lax.dot_general(doc, v_ref[0, pl.ds(c0, BLK), :], _DIMS["nt"], preferred_element_type=F32)
            a = jnp.exp(lb + (ltc - pre_lm - cs[:, :BLK]))
            if mask is not None:
                a = jnp.where(mask, a, 0.0)
            dl = a * da
            return (jnp.exp(lb), dl, dl.astype(BF16), a.astype(BF16)), pre_lm + cs[:, BLK:]

        def logit_grad(beta, dl, dl_b, mask, pre_dl):
            cd = _cumsum_mm(dl_b, t_excl)
            dz = dl - beta * (dl + pre_dl + cd[:, :BLK])
            if mask is not None:
                dz = jnp.where(mask, dz, 0.0)
            return (dz * qscale).astype(BF16), pre_dl + cd[:, BLK:]

        def key_grads(c0, dz_parts, a_parts, q_rows, do_rows):
            dz_all = dz_parts[0] if len(dz_parts) == 1 else jnp.concatenate(dz_parts, axis=0)
            a_all = a_parts[0] if len(a_parts) == 1 else jnp.concatenate(a_parts, axis=0)
            dk_ref[0, pl.ds(c0, BLK), :] += lax.dot_general(dz_all, q_rows, _DIMS["tn"], preferred_element_type=F32)
            dv_ref[0, pl.ds(c0, BLK), :] += lax.dot_general(a_all, do_rows, _DIMS["tn"], preferred_element_type=F32)

        def qblock(qb):
            r0 = _block_start(qb, ATT_BQ)
            q_raw = q_ref[0, pl.ds(r0, ATT_BQ), :]
            do_all = do_ref[0, pl.ds(r0, ATT_BQ), :]
            qs = [q_raw[rc * BLK:(rc + 1) * BLK] * jnp.asarray(qscale, BF16) for rc in chunks]
            dos = [do_all[rc * BLK:(rc + 1) * BLK] for rc in chunks]
            lts = [lt_ref[0, pl.ds(r0 + rc * BLK, BLK), :] for rc in chunks]

            def col0(b):
                return pl.multiple_of(b * BLK, BLK)

            def stage_scores(b, _, state):
                kj = k_ref[0, pl.ds(col0(b), BLK), :]
                for rc in chunks:
                    z_scr[rc] = lax.dot_general(qs[rc], kj, _DIMS["nt"], preferred_element_type=F32)
                return (), state

            def stage_prefix(b, _, state):
                vj = v_ref[0, pl.ds(col0(b), BLK), :]
                lbs = []
                for rc in chunks:
                    lb, lm = _logits(z_scr[rc], None)
                    cs_scr[rc] = _cumsum_mm(lm.astype(BF16), t_incl)
                    da_scr[rc] = lax.dot_general(dos[rc], vj, _DIMS["nt"], preferred_element_type=F32)
                    lbs.append(lb)
                return tuple(lbs), state

            def stage_weights(b, lbs, state):
                dqs, pls, pds = state
                out, new_pls = [], []
                for rc in chunks:
                    cs = cs_scr[rc]
                    a = jnp.exp(lbs[rc] + (lts[rc] - pls[rc] - cs[:, :BLK]))
                    dl = a * da_scr[rc]
                    cd_scr[rc] = _cumsum_mm(dl.astype(BF16), t_excl)
                    out.append((jnp.exp(lbs[rc]), dl, a.astype(BF16)))
                    new_pls.append(pls[rc] + cs[:, BLK:])
                return tuple(out), (dqs, tuple(new_pls), pds)

            def stage_products(b, ys, state):
                dqs, pls, pds = state
                c0 = col0(b)
                kj = k_ref[0, pl.ds(c0, BLK), :]
                dz_bs, new_pds = [], []
                for rc in chunks:
                    beta, dl, _ = ys[rc]
                    cd = cd_scr[rc]
                    dz_b = ((dl - beta * (dl + pds[rc] + cd[:, :BLK])) * qscale).astype(BF16)
                    dqp_scr[rc] = jnp.dot(dz_b, kj, preferred_element_type=F32)
                    dz_bs.append(dz_b)
                    new_pds.append(pds[rc] + cd[:, BLK:])
                dkp_scr[...] = lax.dot_general(jnp.concatenate(dz_bs, axis=0), q_raw, _DIMS["tn"],
                                               preferred_element_type=F32)
                dvp_scr[...] = lax.dot_general(jnp.concatenate([y[2] for y in ys], axis=0), do_all, _DIMS["tn"],
                                               preferred_element_type=F32)
                return (), (dqs, pls, tuple(new_pds))

            def stage_acc(b, _, state):
                dqs, pls, pds = state
                c0 = col0(b)
                dk_ref[0, pl.ds(c0, BLK), :] += dkp_scr[...]
                dv_ref[0, pl.ds(c0, BLK), :] += dvp_scr[...]
                return None, (tuple(dqs[rc] + dqp_scr[rc] for rc in chunks), pls, pds)

            stages = [stage_scores, stage_prefix, stage_weights, stage_products, stage_acc]
            zero = jnp.zeros((BLK, BLK), F32)
            state = ((jnp.zeros((BLK, dh), F32),) * ATT_CHUNKS, (zero,) * ATT_CHUNKS, (zero,) * ATT_CHUNKS)

            def pipelined(state):
                return _software_pipeline(stages, qb * ATT_CHUNKS, lambda b: b, state)

            def one_by_one(state):
                def block(b, state):
                    x = None
                    for stage in stages:
                        x, state = stage(b, x, state)
                    return state
                return lax.fori_loop(0, qb * ATT_CHUNKS, block, state)

            if isinstance(qb, int):
                dqs, pls, pds = one_by_one(state) if qb > 0 else state
            else:
                dqs, pls, pds = pipelined(state)
            dqs, pls, pds = list(dqs), list(pls), list(pds)
            for dj in chunks:
                c0 = r0 + dj * BLK
                kj = k_ref[0, pl.ds(c0, BLK), :]
                dz_parts, a_parts = [], []
                for rc in range(dj, ATT_CHUNKS):
                    mask = dmask if rc == dj else None
                    ys, pls[rc] = weights(*logits(qs[rc], c0, mask), dos[rc], lts[rc], c0, mask, pls[rc])
                    dz_b, pds[rc] = logit_grad(*ys[:3], mask, pds[rc])
                    a_b = ys[3]
                    dqs[rc] = dqs[rc] + jnp.dot(dz_b, kj, preferred_element_type=F32)
                    dz_parts.append(dz_b)
                    a_parts.append(a_b)
                key_grads(c0, dz_parts, a_parts, q_raw[dj * BLK:], do_all[dj * BLK:])
            for rc in chunks:
                dq_ref[0, pl.ds(r0 + rc * BLK, BLK), :] = dqs[rc]

        _peeled_loop(qblock, -(-4 // ATT_CHUNKS), nq)

    head = pl.BlockSpec((1, l, dh), lambda h: (h, 0, 0))
    sq = pltpu.VMEM((ATT_CHUNKS, BLK, BLK), F32)
    sw = pltpu.VMEM((ATT_CHUNKS, BLK, 2 * BLK), F32)
    return pl.pallas_call(
        body, name=name, grid=(nh,),
        in_specs=[head, head, head, head, pl.BlockSpec((1, l, BLK), lambda h: (h, 0, 0))],
        out_specs=[head, head, head],
        out_shape=[jax.ShapeDtypeStruct((nh, l, dh), F32)] * 3,
        scratch_shapes=[sq, sw, sq, sw, pltpu.VMEM((ATT_CHUNKS, BLK, dh), F32),
                        pltpu.VMEM((BLK, dh), F32), pltpu.VMEM((BLK, dh), F32)],
        compiler_params=_cparams("parallel"),
    )(q, k, v, do, ltot)


def _add_to_bf16(a, b, *, name):
    n, r, c = a.shape
    tr = _row_tile(r, 1536, 16)
    blk = pl.BlockSpec((1, tr, c), lambda i, j: (i, j, 0))

    def body(a_ref, b_ref, o_ref):
        o_ref[...] = (a_ref[...] + b_ref[...].astype(F32)).astype(BF16)

    return pl.pallas_call(body, name=name, grid=(n, r // tr), in_specs=[blk, blk], out_specs=blk,
                          out_shape=jax.ShapeDtypeStruct(a.shape, BF16),
                          compiler_params=_cparams("parallel", "parallel"))(a, b)


def _sum_slots(p, *, name):
    n, r, c = p.shape
    tr = _row_tile(r, 1536, 16)

    def body(p_ref, o_ref):
        acc = p_ref[0].astype(F32)
        for s in range(1, n):
            acc = acc + p_ref[s].astype(F32)
        o_ref[...] = acc

    return pl.pallas_call(body, name=name, grid=(r // tr,),
                          in_specs=[pl.BlockSpec((n, tr, c), lambda j: (0, j, 0))],
                          out_specs=pl.BlockSpec((tr, c), lambda j: (j, 0)),
                          out_shape=jax.ShapeDtypeStruct((r, c), F32),
                          compiler_params=_cparams("parallel"))(p)


def _adamw(w, g, m, v, *, name):
    r, c = w.shape
    tr = _row_tile(r, 512, 8)
    blk = pl.BlockSpec((tr, c), lambda i: (i, 0))

    def body(w_ref, g_ref, m_ref, v_ref, d_ref, mo_ref, vo_ref):
        gv = g_ref[...]
        mn = ADAM_B1 * m_ref[...] + (1.0 - ADAM_B1) * gv
        vn = ADAM_B2 * v_ref[...] + (1.0 - ADAM_B2) * (gv * gv)
        m_hat = mn / (1.0 - ADAM_B1 ** ADAM_STEP)
        v_hat = vn / (1.0 - ADAM_B2 ** ADAM_STEP)
        d_ref[...] = -ADAM_LR * (m_hat / (jnp.sqrt(v_hat) + ADAM_EPS) + ADAM_WD * w_ref[...])
        mo_ref[...] = mn
        vo_ref[...] = vn

    return pl.pallas_call(body, name=name, grid=(r // tr,), in_specs=[blk] * 4, out_specs=[blk] * 3,
                          out_shape=[jax.ShapeDtypeStruct((r, c), F32)] * 3,
                          compiler_params=_cparams("parallel"))(w, g, m, v)


_ANY = pl.BlockSpec(memory_space=pl.ANY)


def _place():
    x, y, c = lax.axis_index("x"), lax.axis_index("y"), lax.axis_index("c")
    chips = [(1 - x, y), (x, 1 - y), (1 - x, 1 - y)]
    return x, y, c, chips


def _gather_copy(k, src, dst, to, send_sems, recv_sems):
    return pltpu.make_async_remote_copy(src_ref=src, dst_ref=dst, send_sem=send_sems.at[k],
                                        recv_sem=recv_sems.at[k], device_id=to, device_id_type=MESH)


def _gather_begin(w_ref, out_ref, send_sems, recv_sems):
    x, y, c, chips = _place()
    s = 2 * x + y
    for j, chip in enumerate(chips):
        _gather_copy(j, w_ref.at[c], out_ref.at[s, c], (*chip, c), send_sems, recv_sems).start()


def _gather_end(w_ref, out_ref, stage_ref, send_sems, recv_sems):
    x, y, c, chips = _place()
    s = 2 * x + y
    sibling = (x, y, 1 - c)
    slots = [2 * px + py for px, py in chips]
    passed = []
    for j, sp in enumerate(slots):
        _gather_copy(j, w_ref.at[c], out_ref.at[sp, c], sibling, send_sems, recv_sems).wait_recv()
        fwd = _gather_copy(3 + j, out_ref.at[sp, c], out_ref.at[sp, c], sibling, send_sems, recv_sems)
        fwd.start()
        passed.append(fwd)
    for h in range(2):
        pltpu.sync_copy(w_ref.at[h], stage_ref)
        pltpu.sync_copy(stage_ref, out_ref.at[s, h])
    for j, sp in enumerate(slots):
        _gather_copy(3 + j, w_ref.at[c], out_ref.at[sp, 1 - c], sibling, send_sems, recv_sems).wait_recv()
    for j, chip in enumerate(chips):
        _gather_copy(j, w_ref.at[c], out_ref.at[s, c], (*chip, c), send_sems, recv_sems).wait_send()
    for fwd in passed:
        fwd.wait_send()


def _gather_out_and_scratch(wsh):
    _, r, cdim = wsh.shape
    return (jax.ShapeDtypeStruct((N_CHIPS, 2, r, cdim), wsh.dtype),
            [pltpu.VMEM((r, cdim), wsh.dtype), pltpu.SemaphoreType.DMA((6,)), pltpu.SemaphoreType.DMA((6,))])


def _gather_shards(wsh, *, name):
    def body(w_ref, out_ref, stage_ref, send_sems, recv_sems):
        _gather_begin(w_ref, out_ref, send_sems, recv_sems)
        _gather_end(w_ref, out_ref, stage_ref, send_sems, recv_sems)

    out_shape, scratch = _gather_out_and_scratch(wsh)
    return pl.pallas_call(
        body, name=name, in_specs=[_ANY], out_specs=_ANY, out_shape=out_shape, scratch_shapes=scratch,
        compiler_params=pltpu.CompilerParams(vmem_limit_bytes=VMEM_LIMIT),
    )(wsh)


def _to_sibling(a, *, name):
    def body(a_ref, out_ref, send_sem, recv_sem):
        x, y, c, _ = _place()
        cp = pltpu.make_async_remote_copy(src_ref=a_ref, dst_ref=out_ref, send_sem=send_sem, recv_sem=recv_sem,
                                          device_id=(x, y, 1 - c), device_id_type=MESH)
        cp.start()
        cp.wait()

    return pl.pallas_call(
        body, name=name, in_specs=[_ANY], out_specs=_ANY,
        out_shape=jax.ShapeDtypeStruct(a.shape, a.dtype),
        scratch_shapes=[pltpu.SemaphoreType.DMA, pltpu.SemaphoreType.DMA],
    )(a)


def _scatter_copy(j, p_ref, out_ref, send_sems, recv_sems, receive):
    x, y, c, chips = _place()
    px, py = chips[j]
    return pltpu.make_async_remote_copy(
        src_ref=p_ref.at[2 * px + py], dst_ref=out_ref.at[(2 * px + py) if receive else (2 * x + y)],
        send_sem=send_sems.at[j], recv_sem=recv_sems.at[j], device_id=(px, py, c), device_id_type=MESH)


def _scatter_begin(p_ref, out_ref, send_sems, recv_sems):
    for j in range(N_CHIPS - 1):
        _scatter_copy(j, p_ref, out_ref, send_sems, recv_sems, False).start()


def _scatter_end(p_ref, out_ref, stage_ref, send_sems, recv_sems):
    x, y, _, _ = _place()
    s = 2 * x + y
    pltpu.sync_copy(p_ref.at[s], stage_ref)
    pltpu.sync_copy(stage_ref, out_ref.at[s])
    for j in range(N_CHIPS - 1):
        _scatter_copy(j, p_ref, out_ref, send_sems, recv_sems, True).wait_recv()
    for j in range(N_CHIPS - 1):
        _scatter_copy(j, p_ref, out_ref, send_sems, recv_sems, False).wait_send()


def _scatter_out_and_scratch(p):
    return (jax.ShapeDtypeStruct(p.shape, p.dtype),
            [pltpu.VMEM(p.shape[1:], p.dtype), pltpu.SemaphoreType.DMA((N_CHIPS - 1,)),
             pltpu.SemaphoreType.DMA((N_CHIPS - 1,))])


def _scatter_rows(p, *, name):
    def body(p_ref, out_ref, stage_ref, send_sems, recv_sems):
        _scatter_begin(p_ref, out_ref, send_sems, recv_sems)
        _scatter_end(p_ref, out_ref, stage_ref, send_sems, recv_sems)

    out_shape, scratch = _scatter_out_and_scratch(p)
    return pl.pallas_call(
        body, name=name, in_specs=[_ANY], out_specs=_ANY, out_shape=out_shape, scratch_shapes=scratch,
        compiler_params=pltpu.CompilerParams(vmem_limit_bytes=VMEM_LIMIT),
    )(p)


def _all_devices(a, reduce, *, name):
    r, cdim = a.shape

    def body(a_ref, out_ref, *scratch):
        if reduce:
            buf, send_sems, recv_sems = scratch
        else:
            buf = out_ref
            send_sems, recv_sems = scratch
        x, y, c, _ = _place()
        me = 4 * x + 2 * y + c
        buf[me] = a_ref[...]
        peers = []
        for k in range(1, N_DEV):
            dx, dy, dc = (k >> 2) & 1, (k >> 1) & 1, k & 1
            peers.append((x ^ dx, y ^ dy, c ^ dc))
        sends = []
        for k, peer in enumerate(peers):
            cp = pltpu.make_async_remote_copy(src_ref=a_ref, dst_ref=buf.at[me], send_sem=send_sems.at[k],
                                              recv_sem=recv_sems.at[k], device_id=peer, device_id_type=MESH)
            cp.start()
            sends.append(cp)
        for k, (px, py, pc) in enumerate(peers):
            pltpu.make_async_remote_copy(src_ref=a_ref, dst_ref=buf.at[4 * px + 2 * py + pc],
                                         send_sem=send_sems.at[k], recv_sem=recv_sems.at[k],
                                         device_id=(px, py, pc), device_id_type=MESH).wait_recv()
        for cp in sends:
            cp.wait_send()
        if reduce:
            acc = buf[0]
            for k in range(1, N_DEV):
                acc = acc + buf[k]
            out_ref[...] = acc

    vm = pl.BlockSpec(memory_space=pltpu.VMEM)
    sems = [pltpu.SemaphoreType.DMA((N_DEV - 1,)), pltpu.SemaphoreType.DMA((N_DEV - 1,))]
    if reduce:
        out_shape = jax.ShapeDtypeStruct((r, cdim), F32)
        scratch = [pltpu.VMEM((N_DEV, r, cdim), F32)] + sems
    else:
        out_shape = jax.ShapeDtypeStruct((N_DEV, r, cdim), F32)
        scratch = sems
    return pl.pallas_call(
        body, name=name, in_specs=[vm], out_specs=vm, out_shape=out_shape, scratch_shapes=scratch,
        compiler_params=pltpu.CompilerParams(vmem_limit_bytes=VMEM_LIMIT),
    )(a)


def _flat_rows(parts, cols):
    flat = jnp.concatenate([p.reshape(-1) for p in parts])
    padded = -(-flat.size // (8 * cols)) * (8 * cols)
    return jnp.pad(flat, (0, padded - flat.size)).reshape(-1, cols)


def kernel(x, meta_tokens, mix_norm, ffn_norm, pool_w, pool_scale, kv_norm, w_kv, w_q, w_o, ffn_w_up, ffn_conv_w, ffn_conv_b, ffn_w_down, final_norm, loss_target, m_meta_tokens, m_mix_norm, m_ffn_norm, m_pool_w, m_pool_scale, m_kv_norm, m_w_kv, m_w_q, m_w_o, m_ffn_w_up, m_ffn_conv_w, m_ffn_conv_b, m_ffn_w_down, m_final_norm, v_meta_tokens, v_mix_norm, v_ffn_norm, v_pool_w, v_pool_scale, v_kv_norm, v_w_kv, v_w_q, v_w_o, v_ffn_w_up, v_ffn_conv_w, v_ffn_conv_b, v_ffn_w_down, v_final_norm):
    seq, d = x.shape[1], x.shape[2]
    l_real = N_META + seq
    lp = -(-l_real // ATT_BQ) * ATT_BQ
    f2 = ffn_w_up.shape[2] * N_CHIPS
    f = f2 // 2
    gd = d // N_GROUPS
    chip = 2 * lax.axis_index("x") + lax.axis_index("y")
    core = lax.axis_index("c")

    cw = 1024

    def shard_halves(parts):
        return _flat_rows([p.astype(BF16) for p in parts], cw).reshape(2, -1, cw)

    def unpack(gathered, parts):
        flat, out, first = gathered.reshape(N_CHIPS, -1), [], 0
        for p in parts:
            out.append(flat[:, first:first + p.size].reshape((N_CHIPS,) + p.shape))
            first += p.size
        return out

    now_parts = [pool_w[0], ffn_w_up[0], ffn_w_down[0]]
    attn_parts = [w_kv, w_q[0], w_o[0]]
    late_parts = [ffn_w_up[1]]
    down_parts = [ffn_w_down[1]]
    wp_s, wup0_s, wdn0_s = unpack(_gather_shards(shard_halves(now_parts), name="gather_weights"), now_parts)
    wp_b = wp_s.transpose(1, 0, 2, 3).reshape(N_GROUPS, gd, gd)
    wup_b = [wup0_s.transpose(1, 0, 2).reshape(d, f2), None]
    wdn_b = [wdn0_s.reshape(f, d), None]

    small_parts = [meta_tokens, pool_scale, ffn_conv_w]
    ssizes = [p.size for p in small_parts]
    small = _flat_rows(small_parts, LANES)
    sall = _all_devices(small, False, name="gather_small")[::2].reshape(N_CHIPS, -1)
    meta_f = sall[:, :ssizes[0]].reshape(N_CHIPS, N_META, d // N_CHIPS).transpose(1, 0, 2).reshape(N_META, d)
    scale_f = sall[:, ssizes[0]:ssizes[0] + ssizes[1]].reshape(1, d)
    cw_f = sall[:, ssizes[0] + ssizes[1]:sum(ssizes)].reshape(N_CHIPS, 2, 3, f2 // N_CHIPS).transpose(1, 2, 0, 3).reshape(2, 3, f2)

    mix0, mix1 = mix_norm[0:1], mix_norm[1:2]
    fn0, fn1 = ffn_norm[0:1], ffn_norm[1:2]
    kvn_g = kv_norm.reshape(1, d)
    fin_g = final_norm.reshape(1, d)

    pad = lp - l_real
    h0 = jnp.concatenate([meta_f, x[0], jnp.zeros((pad, d), F32)], axis=0)
    tgt = jnp.pad(loss_target[0], ((N_META, pad), (0, 0)))

    _, r0 = _norm_fwd(h0, [mix0], name="norm_h0")
    diff_b = _pool_diff_fwd(h0, r0, mix0, name="pool_diff")
    h1 = _pool_mix_fwd(h0, diff_b, wp_b, scale_f, name="pool_mix")

    def ffn_fwd(h_in, layer, tag, gather_behind_up=None, gather_behind_gate=None, gather_behind_down=None):
        (fb,), rf = _norm_fwd(h_in, [ffn_norm[layer:layer + 1]], name=f"norm_ffn{tag}")
        u = _mm(fb, wup_b[layer], "nn", tm=1408, tn=512, name=f"ffn_up{tag}", gather=gather_behind_up)
        gathered = []
        if gather_behind_up is not None:
            u, g = u
            gathered.append(g)
        a = _convgate_fwd(u, cw_f[layer], ffn_conv_b[layer:layer + 1], name=f"convgate{tag}",
                          gather=gather_behind_gate)
        if gather_behind_gate is not None:
            a, g = a
            gathered.append(g)
        h_out = _mm(a, wdn_b[layer], "nn", tm=384, tn=1024, res=h_in, name=f"ffn_down{tag}",
                    gather=gather_behind_down)
        if gather_behind_down is not None:
            h_out, g = h_out
            gathered.append(g)
        return h_out, (fb, rf, u, a), gathered

    h2, ffn0_saved, (attn_gathered, late_gathered, down_gathered) = ffn_fwd(
        h1, 0, "0", shard_halves(attn_parts), shard_halves(late_parts), shard_halves(down_parts))
    wkv_s, wq_s, wo_s = unpack(attn_gathered, attn_parts)
    (wup1_s,) = unpack(late_gathered, late_parts)
    (wdn1_s,) = unpack(down_gathered, down_parts)
    wkv_b = wkv_s.transpose(1, 0, 2).reshape(d, 2 * d)
    wq_b = wq_s.reshape(d, d)
    wo_b = wo_s.reshape(d, d)
    wup_b[1] = wup1_s.transpose(1, 0, 2).reshape(d, f2)
    wdn_b[1] = wdn1_s.reshape(f, d)
    (kvn_b, n1_b), r2 = _norm_fwd(h2, [kvn_g, mix1], name="norm_h2")
    kv_b = _mm(kvn_b, wkv_b, "nn", tm=1408, tn=512, out_dtype=BF16, name="kv_proj")
    q_b = _mm(n1_b, wq_b, "nn", tm=1408, tn=512, out_dtype=BF16, name="q_proj")

    def heads(t):
        return t.reshape(lp, N_HEADS, HEAD_DIM).transpose(1, 0, 2)

    def unheads(t):
        return t.transpose(1, 0, 2).reshape(lp, N_HEADS * HEAD_DIM)

    qh, kh, vh = heads(q_b), heads(kv_b[:, :d]), heads(kv_b[:, d:])
    oh, ltot = _attn_fwd(qh, kh, vh, name="attn_fwd")
    o_b = unheads(oh)
    h3 = _mm(o_b, wo_b, "nn", tm=384, tn=1024, res=h2, name="o_proj")
    h4, ffn1_saved, _ = ffn_fwd(h3, 1, "1")

    loss8, dh4, dh4_b, dfin = _loss_bwd(h4, fin_g, tgt, seq, name="loss")

    def ffn_bwd(h_in, layer, saved, dh_out, dh_out_b, tag, scatter_behind_gate=None):
        fb, rf, u, a = saved
        dwdn = _mm(a, dh_out_b, "tn", tm=256, tn=1024, name=f"d_wdown{tag}")
        da = _mm(dh_out_b, wdn_b[layer], "nt", tm=384, tn=f, name=f"d_act{tag}")
        du_g, du_v, dcw_g, dcw_v, dcb_g, dcb_v, *scattered = _convgate_bwd(
            u, cw_f[layer], ffn_conv_b[layer:layer + 1], da, name=f"convgate_bwd{tag}", scatter=scatter_behind_gate)
        dcw = jnp.concatenate([dcw_g, dcw_v], axis=1)
        dcb = jnp.concatenate([dcb_g, dcb_v], axis=1)
        half_chips = N_CHIPS // 2
        dwup = jnp.concatenate(
            [_mm(fb, du_g, "tn", tm=512, tn=f2 // N_CHIPS, out_shards=half_chips, name=f"d_wup_gate{tag}"),
             _mm(fb, du_v, "tn", tm=512, tn=f2 // N_CHIPS, out_shards=half_chips, name=f"d_wup_val{tag}")], axis=0)
        df = _mm(du_g, wup_b[layer][:, :f], "nt", tm=384, tn=1024, name=f"d_ffn_in_gate{tag}")
        df = _mm(du_v, wup_b[layer][:, f:], "nt", tm=384, tn=1024, res=df, name=f"d_ffn_in_val{tag}")
        dh_in, dh_in_b, (dfn,) = _norm_bwd(h_in, rf, [ffn_norm[layer:layer + 1]], [df], dh_out,
                                          name=f"norm_ffn_bwd{tag}")
        return (dh_in, dh_in_b, dwup, dwdn, dcw, dcb, dfn) + tuple(scattered)

    dh3, dh3_b, dwup1, dwdn1, dcw1, dcb1, dfn1 = ffn_bwd(h3, 1, ffn1_saved, dh4, dh4_b, "1")

    dwo = _mm(o_b, dh3_b, "tn", tm=1024, tn=512, name="d_wo")
    do_b = _mm(dh3_b, wo_b, "nt", tm=384, tn=1024, out_dtype=BF16, name="d_attn_out")
    dqh, dkh, dvh = _attn_bwd(qh, kh, vh, heads(do_b), ltot, name="attn_bwd")
    dq_b = unheads(dqh).astype(BF16)
    dkv_b = jnp.concatenate([unheads(dkh), unheads(dvh)], axis=1).astype(BF16)
    dwq = _mm(n1_b, dq_b, "tn", tm=1024, tn=512, name="d_wq")
    dwkv = _mm(kvn_b, dkv_b, "tn", tm=1024, tn=512, out_shards=N_CHIPS, name="d_wkv")
    dn1 = _mm(dq_b, wq_b, "nt", tm=384, tn=1024, name="d_n1")
    dkvn = _mm(dkv_b, wkv_b, "nt", tm=384, tn=1024, name="d_kvn")
    def core_halves(parts):
        g = jnp.concatenate([p.reshape(N_CHIPS, -1) for p in parts], axis=1).reshape(N_CHIPS, 2, -1, cw)
        return (lax.dynamic_index_in_dim(g, core, axis=1, keepdims=False),
                lax.dynamic_index_in_dim(g, 1 - core, axis=1, keepdims=False).astype(BF16))

    def reduced_shard(slots, tag):
        half = _sum_slots(slots, name=f"grads_sum{tag}")
        other_half = _to_sibling(half, name=f"grads_join{tag}")
        return jnp.stack([jnp.where(core == 0, half, other_half), jnp.where(core == 0, other_half, half)]).reshape(-1)

    def unpack_shard(flat, shapes):
        out, first = [], 0
        for shape in shapes:
            size = 1
            for n in shape:
                size *= n
            out.append(flat[first:first + size].reshape(shape))
            first += size
        return out

    early_mine, early_other = core_halves([dwkv, dwq.reshape(N_CHIPS, d // N_CHIPS, d),
                                           dwo.reshape(N_CHIPS, d // N_CHIPS, d), dwup1,
                                           dwdn1.reshape(N_CHIPS, f // N_CHIPS, d)])
    dh2, dh2_b, (dmix1, dkvg), early_from_sibling = _norm_bwd(h2, r2, [mix1, kvn_g], [dn1, dkvn], dh3,
                                                             name="norm_h2_bwd", send=early_other)
    early = _add_to_bf16(early_mine, early_from_sibling, name="grads_chip_sum_early")
    dh1, dh1_b, dwup0, dwdn0, dcw0, dcb0, dfn0, early_slots = ffn_bwd(h1, 0, ffn0_saved, dh2, dh2_b, "0", early)
    g_w_kv, g_w_q, g_w_o, g_w_up1, g_w_dn1 = unpack_shard(
        reduced_shard(early_slots, "_early"),
        [w_kv.shape, w_q.shape, w_o.shape, ffn_w_up.shape[1:], ffn_w_down.shape[1:]])

    ddiff, dwp, dscale = _pool_mix_bwd(diff_b, wp_b, scale_f, dh1, name="pool_mix_bwd")
    dn0 = _pool_diff_bwd(ddiff, name="pool_diff_bwd")
    dh0, _, (dmix0,) = _norm_bwd(h0, r0, [mix0], [dn0], dh1, name="norm_h0_bwd")

    grad_x = dh0[N_META:l_real][None]
    dmeta = dh0[:N_META]

    late_mine, late_other = core_halves([dwp.reshape(N_GROUPS, N_CHIPS, gd // N_CHIPS, gd).transpose(1, 0, 2, 3),
                                         dwup0, dwdn0.reshape(N_CHIPS, f // N_CHIPS, d)])
    late = _add_to_bf16(late_mine, _to_sibling(late_other, name="grads_to_sibling_late"), name="grads_chip_sum_late")
    g_pool_w, g_w_up0, g_w_dn0 = unpack_shard(
        reduced_shard(_scatter_rows(late, name="grads_scatter_late"), "_late"),
        [pool_w.shape, ffn_w_up.shape[1:], ffn_w_down.shape[1:]])
    g_w_up = jnp.stack([g_w_up0, g_w_up1])
    g_w_dn = jnp.stack([g_w_dn0, g_w_dn1])

    dcw = jnp.stack([dcw0, dcw1])
    dcb = jnp.concatenate([dcb0, dcb1], axis=0)
    sg_parts = [jnp.concatenate([dmix0, dmix1], axis=0), jnp.concatenate([dfn0, dfn1], axis=0), dkvg, dfin, dcb,
                dmeta, dscale, dcw, loss8]
    sg_sizes = [p.size for p in sg_parts]
    sg = _all_devices(_flat_rows(sg_parts, LANES), True, name="reduce_small").reshape(-1)
    sg_offs = [0]
    for sz in sg_sizes:
        sg_offs.append(sg_offs[-1] + sz)

    def gsmall(i, shape):
        return sg[sg_offs[i]:sg_offs[i + 1]].reshape(shape)

    g_mix = gsmall(0, mix_norm.shape)
    g_ffn_norm = gsmall(1, ffn_norm.shape)
    g_kv_norm = gsmall(2, kv_norm.shape)
    g_final = gsmall(3, final_norm.shape)
    g_conv_b = gsmall(4, ffn_conv_b.shape)
    csh = d // N_CHIPS
    g_meta = lax.dynamic_slice_in_dim(gsmall(5, (N_META, d)), chip * csh, csh, axis=1)
    g_scale = lax.dynamic_slice_in_dim(gsmall(6, (1, d)), chip * csh, csh, axis=1)
    fsh = f2 // N_CHIPS
    g_conv_w = lax.dynamic_slice_in_dim(gsmall(7, (2, 3, f2)), chip * fsh, fsh, axis=2)
    loss = gsmall(8, (8 * LANES,))[0]

    weights = [meta_tokens, mix_norm, ffn_norm, pool_w, pool_scale, kv_norm, w_kv, w_q, w_o, ffn_w_up, ffn_conv_w,
               ffn_conv_b, ffn_w_down, final_norm]
    grads = [g_meta, g_mix, g_ffn_norm, g_pool_w, g_scale, g_kv_norm, g_w_kv, g_w_q, g_w_o, g_w_up, g_conv_w,
             g_conv_b, g_w_dn, g_final]
    ms = [m_meta_tokens, m_mix_norm, m_ffn_norm, m_pool_w, m_pool_scale, m_kv_norm, m_w_kv, m_w_q, m_w_o,
          m_ffn_w_up, m_ffn_conv_w, m_ffn_conv_b, m_ffn_w_down, m_final_norm]
    vs = [v_meta_tokens, v_mix_norm, v_ffn_norm, v_pool_w, v_pool_scale, v_kv_norm, v_w_kv, v_w_q, v_w_o,
          v_ffn_w_up, v_ffn_conv_w, v_ffn_conv_b, v_ffn_w_down, v_final_norm]
    names = ["meta", "mix", "ffnnorm", "poolw", "poolscale", "kvnorm", "wkv", "wq", "wo", "wup", "convw", "convb",
             "wdown", "final"]
    deltas, new_ms, new_vs = [], [], []
    for w, g, m, v, nm in zip(weights, grads, ms, vs, names):
        cols = w.shape[-1]
        if w.size % (8 * LANES) == 0 and w.ndim == 1:
            cols = LANES
        view = (w.size // cols, cols)
        dl, mn, vn = _adamw(w.reshape(view), g.reshape(view), m.reshape(view), v.reshape(view), name=f"adamw_{nm}")
        deltas.append(dl.reshape(w.shape))
        new_ms.append(mn.reshape(w.shape))
        new_vs.append(vn.reshape(w.shape))

    return (loss, grad_x, *grads, *deltas, *new_ms, *new_vs)
```

```python
import jax
import jax.numpy as jnp
from jax import lax
from jax.experimental import pallas as pl
from jax.experimental.pallas import tpu as pltpu

F32 = jnp.float32
BF16 = jnp.bfloat16

N_META = 16
N_HEADS = 16
HEAD_DIM = 64
POOL_WINDOWS = (2, 4, 8, 16)
N_GROUPS = 4
RMS_EPS = 1e-6
ADAM_LR = 0.001
ADAM_B1 = 0.9
ADAM_B2 = 0.999
ADAM_EPS = 1e-08
ADAM_WD = 0.01
ADAM_STEP = 10

LANES = 128
BLK = 128
ATT_CHUNKS = 3
ATT_BQ = ATT_CHUNKS * BLK
CONV_ROWS = 128
N_CHIPS = 4
N_DEV = 8
VMEM_LIMIT = 56 * 1024 * 1024
MESH = pl.DeviceIdType.MESH


def _cparams(*sem):
    return pltpu.CompilerParams(dimension_semantics=tuple(sem) if sem else None,
                                vmem_limit_bytes=VMEM_LIMIT)


def _tile(n, pref):
    best = None
    t = LANES
    while t <= min(n, pref):
        if n % t == 0:
            best = t
        t += LANES
    assert best is not None, (n, pref)
    return best


def _row_tile(r, pref, mult):
    best = r
    for t in range(mult, min(r, pref) + 1, mult):
        if r % t == 0:
            best = t
    return best


_DIMS = {
    "nn": (((1,), (0,)), ((), ())),
    "nt": (((1,), (1,)), ((), ())),
    "tn": (((0,), (0,)), ((), ())),
}


def _mm(a, b, dims, *, tm, tn, name, out_dtype=F32, res=None, out_shards=1, gather=None):
    if dims == "tn":
        k, m = a.shape
    else:
        m, k = a.shape
    n = b.shape[0] if dims == "nt" else b.shape[1]
    tm = _tile(m, tm)
    tn = _tile(n // out_shards, tn)
    a_spec = pl.BlockSpec((k, tm), lambda i, j: (0, i)) if dims == "tn" else pl.BlockSpec((tm, k), lambda i, j: (i, 0))
    b_spec = pl.BlockSpec((tn, k), lambda i, j: (j, 0)) if dims == "nt" else pl.BlockSpec((k, tn), lambda i, j: (0, j))
    o_spec = pl.BlockSpec((tm, tn), lambda i, j: (i, j))
    out_shape = jax.ShapeDtypeStruct((m, n), out_dtype)
    if out_shards > 1:
        assert res is None
        per = n // out_shards // tn
        o_spec = pl.BlockSpec((None, tm, tn), lambda i, j: (j // per, i, j % per))
        out_shape = jax.ShapeDtypeStruct((out_shards, m, n // out_shards), out_dtype)
    dn = _DIMS[dims]

    grid = (m // tm, n // tn)

    def body(*refs):
        refs = list(refs)
        a_ref, b_ref = refs[:2]
        r_ref = refs[2] if res is not None else None
        o_ref = refs[2 + (res is not None) + (gather is not None)]
        acc = lax.dot_general(a_ref[...], b_ref[...], dn, preferred_element_type=F32)
        if res is not None:
            acc = acc + r_ref[...]
        o_ref[...] = acc.astype(out_dtype)
        if gather is not None:
            w_ref, g_ref, stage_ref, send_sems, recv_sems = refs[2 + (res is not None)], *refs[-4:]
            step = pl.program_id(0) * grid[1] + pl.program_id(1)

            @pl.when(step == 0)
            def _():
                _gather_begin(w_ref, g_ref, send_sems, recv_sems)

            @pl.when(step == grid[0] * grid[1] - 1)
            def _():
                _gather_end(w_ref, g_ref, stage_ref, send_sems, recv_sems)

    ins = [a, b] + ([] if res is None else [res])
    specs = [a_spec, b_spec] + ([] if res is None else [o_spec])
    if gather is None:
        return pl.pallas_call(
            body, name=name, grid=grid, in_specs=specs, out_specs=o_spec, out_shape=out_shape,
            compiler_params=_cparams("parallel", "arbitrary"),
        )(*ins)
    g_shape, scratch = _gather_out_and_scratch(gather)
    return pl.pallas_call(
        body, name=name, grid=grid, in_specs=specs + [_ANY], out_specs=[o_spec, _ANY],
        out_shape=[out_shape, g_shape], scratch_shapes=scratch,
        compiler_params=_cparams("arbitrary", "arbitrary"),
    )(*ins, gather)


def _norm_fwd(x, gains, *, name):
    l, d = x.shape
    tr = _tile(l, 384)
    ng = len(gains)

    def body(*refs):
        x_ref = refs[0]
        g_refs = refs[1:1 + ng]
        o_refs = refs[1 + ng:1 + 2 * ng]
        r_ref = refs[1 + 2 * ng]
        xv = x_ref[...]
        r = lax.rsqrt(jnp.mean(xv * xv, axis=-1, keepdims=True) + RMS_EPS)
        xn = xv * r
        for g_ref, o_ref in zip(g_refs, o_refs):
            o_ref[...] = (xn * g_ref[...]).astype(BF16)
        r_ref[...] = r

    row = pl.BlockSpec((tr, d), lambda i: (i, 0))
    gspec = pl.BlockSpec((1, d), lambda i: (0, 0))
    outs = pl.pallas_call(
        body, name=name, grid=(l // tr,),
        in_specs=[row] + [gspec] * ng,
        out_specs=[row] * ng + [pl.BlockSpec((tr, 1), lambda i: (i, 0))],
        out_shape=[jax.ShapeDtypeStruct((l, d), BF16)] * ng + [jax.ShapeDtypeStruct((l, 1), F32)],
        compiler_params=_cparams("parallel"),
    )(x, *gains)
    return list(outs[:ng]), outs[ng]


def _rows8(v):
    r, d = v.shape
    return jnp.sum(v.reshape(r // 8, 8, d), axis=0)


def _norm_bwd(x, r, gains, dns, dres, *, name, send=None):
    l, d = x.shape
    tr = _tile(l, 384)
    ng = len(gains)
    nsteps = l // tr

    def body(*refs):
        refs = list(refs)
        if send is not None:
            send_ref, got_ref, sems = refs.pop(3 + 2 * ng), refs.pop(5 + 3 * ng), (refs.pop(), refs.pop())[::-1]
        x_ref, r_ref, dres_ref = refs[0], refs[1], refs[2]
        g_refs = refs[3:3 + ng]
        dn_refs = refs[3 + ng:3 + 2 * ng]
        dx_ref, dxb_ref = refs[3 + 2 * ng], refs[4 + 2 * ng]
        dg_refs = refs[5 + 2 * ng:5 + 3 * ng]
        acc_refs = refs[5 + 3 * ng:5 + 4 * ng]
        i = pl.program_id(0)
        if send is not None:
            x_, y_, c_, _ = _place()
            exchange = pltpu.make_async_remote_copy(src_ref=send_ref, dst_ref=got_ref, send_sem=sems[0],
                                                    recv_sem=sems[1], device_id=(x_, y_, 1 - c_), device_id_type=MESH)

        @pl.when(i == 0)
        def _():
            for acc in acc_refs:
                acc[...] = jnp.zeros_like(acc)
            if send is not None:
                exchange.start()

        rv = r_ref[...]
        xn = x_ref[...] * rv
        total = dres_ref[...]
        for g_ref, dn_ref, acc in zip(g_refs, dn_refs, acc_refs):
            dn = dn_ref[...]
            acc[...] += _rows8(dn * xn)
            dxn = dn * g_ref[...]
            total = total + rv * (dxn - xn * jnp.mean(dxn * xn, axis=-1, keepdims=True))
        dx_ref[...] = total
        dxb_ref[...] = total.astype(BF16)

        @pl.when(i == nsteps - 1)
        def _():
            for dg_ref, acc in zip(dg_refs, acc_refs):
                dg_ref[...] = jnp.sum(acc[...], axis=0, keepdims=True)
            if send is not None:
                exchange.wait()

    row = pl.BlockSpec((tr, d), lambda i: (i, 0))
    gspec = pl.BlockSpec((1, d), lambda i: (0, 0))
    extra_in, extra_out, extra_scratch, extra_args = [], [], [], []
    if send is not None:
        extra_in, extra_args = [_ANY], [send]
        extra_out = [(_ANY, jax.ShapeDtypeStruct(send.shape, send.dtype))]
        extra_scratch = [pltpu.SemaphoreType.DMA, pltpu.SemaphoreType.DMA]
    outs = pl.pallas_call(
        body, name=name, grid=(nsteps,),
        in_specs=[row, pl.BlockSpec((tr, 1), lambda i: (i, 0)), row] + [gspec] * ng + [row] * ng + extra_in,
        out_specs=[row, row] + [gspec] * ng + [o[0] for o in extra_out],
        out_shape=[jax.ShapeDtypeStruct((l, d), F32), jax.ShapeDtypeStruct((l, d), BF16)]
        + [jax.ShapeDtypeStruct((1, d), F32)] * ng + [o[1] for o in extra_out],
        scratch_shapes=[pltpu.VMEM((8, d), F32)] * ng + extra_scratch,
        compiler_params=_cparams("arbitrary"),
    )(x, r, dres, *gains, *dns, *extra_args)
    if send is not None:
        return outs[0], outs[1], list(outs[2:2 + ng]), outs[2 + ng]
    return outs[0], outs[1], list(outs[2:])


def _loss_bwd(h, gain, tgt, seq, *, name):
    l, d = h.shape
    tr = _tile(l, 384)
    nsteps = l // tr

    def body(h_ref, g_ref, t_ref, loss_ref, dh_ref, dhb_ref, dg_ref, lacc, gacc):
        i = pl.program_id(0)

        @pl.when(i == 0)
        def _():
            lacc[...] = jnp.zeros_like(lacc)
            gacc[...] = jnp.zeros_like(gacc)

        xv = h_ref[...]
        g = g_ref[...]
        r = lax.rsqrt(jnp.mean(xv * xv, axis=-1, keepdims=True) + RMS_EPS)
        xn = xv * r
        rows = i * tr + lax.broadcasted_iota(jnp.int32, (tr, 1), 0)
        valid = (rows >= N_META) & (rows < N_META + seq)
        e = jnp.where(valid, xn * g - t_ref[...], 0.0)
        lacc[...] += _rows8(e * e)
        dy = e * (1.0 / d)
        gacc[...] += _rows8(dy * xn)
        dxn = dy * g
        dx = r * (dxn - xn * jnp.mean(dxn * xn, axis=-1, keepdims=True))
        dh_ref[...] = dx
        dhb_ref[...] = dx.astype(BF16)

        @pl.when(i == nsteps - 1)
        def _():
            loss_ref[...] = jnp.full((8, LANES), 0.5 / d * jnp.sum(lacc[...]), F32)
            dg_ref[...] = jnp.sum(gacc[...], axis=0, keepdims=True)

    row = pl.BlockSpec((tr, d), lambda i: (i, 0))
    gspec = pl.BlockSpec((1, d), lambda i: (0, 0))
    return pl.pallas_call(
        body, name=name, grid=(nsteps,),
        in_specs=[row, gspec, row],
        out_specs=[pl.BlockSpec((8, LANES), lambda i: (0, 0)), row, row, gspec],
        out_shape=[jax.ShapeDtypeStruct((8, LANES), F32), jax.ShapeDtypeStruct((l, d), F32),
                   jax.ShapeDtypeStruct((l, d), BF16), jax.ShapeDtypeStruct((1, d), F32)],
        scratch_shapes=[pltpu.VMEM((8, d), F32), pltpu.VMEM((8, d), F32)],
        compiler_params=_cparams("arbitrary"),
    )(h, gain, tgt)


def _shift_down(v, k, rows):
    return jnp.where(rows >= k, pltpu.roll(v, k, 0), 0.0)


def _shift_up(v, k, rows):
    l = v.shape[0]
    return jnp.where(rows < l - k, pltpu.roll(v, l - k, 0), 0.0)


def _pool_diff(n, w, rows):
    s = n
    for k in (1, 2, 4, 8):
        s = s + jnp.where(k < w, _shift_down(s, k, rows), 0.0)
    cnt = jnp.minimum(rows + 1, w).astype(F32)
    return s / cnt - n, cnt


def _pool_diff_fwd(h, r, gain, *, name):
    l, d = h.shape
    per_group = d // N_GROUPS // LANES

    def body(h_ref, r_ref, g_ref, o_ref):
        w = jnp.left_shift(2, pl.program_id(0) // per_group)
        rows = lax.broadcasted_iota(jnp.int32, (l, 1), 0)
        n = h_ref[...] * r_ref[...] * g_ref[...]
        diff, _ = _pool_diff(n, w, rows)
        o_ref[...] = diff.astype(BF16)

    col = pl.BlockSpec((l, LANES), lambda j: (0, j))
    return pl.pallas_call(
        body, name=name, grid=(d // LANES,),
        in_specs=[col, pl.BlockSpec((l, 1), lambda j: (0, 0)), pl.BlockSpec((1, LANES), lambda j: (0, j))],
        out_specs=col, out_shape=jax.ShapeDtypeStruct((l, d), BF16),
        compiler_params=_cparams("parallel"),
    )(h, r, gain)


def _pool_diff_bwd(ddiff, *, name):
    l, d = ddiff.shape
    per_group = d // N_GROUPS // LANES

    def body(dd_ref, o_ref):
        w = jnp.left_shift(2, pl.program_id(0) // per_group)
        rows = lax.broadcasted_iota(jnp.int32, (l, 1), 0)
        dd = dd_ref[...]
        s = dd / jnp.minimum(rows + 1, w).astype(F32)
        for k in (1, 2, 4, 8):
            s = s + jnp.where(k < w, _shift_up(s, k, rows), 0.0)
        o_ref[...] = s - dd

    col = pl.BlockSpec((l, LANES), lambda j: (0, j))
    return pl.pallas_call(
        body, name=name, grid=(d // LANES,), in_specs=[col], out_specs=col,
        out_shape=jax.ShapeDtypeStruct((l, d), F32), compiler_params=_cparams("parallel"),
    )(ddiff)


def _pool_mix_fwd(h, diff_b, w_pool, scale, *, name):
    l, d = h.shape
    gd = d // N_GROUPS
    tr = _tile(l, 1408)

    def body(h_ref, d_ref, w_ref, s_ref, o_ref):
        y = jnp.dot(d_ref[...], w_ref[0], preferred_element_type=F32)
        o_ref[...] = h_ref[...] + y * s_ref[...]

    blk = pl.BlockSpec((tr, gd), lambda i, g: (i, g))
    vec = pl.BlockSpec((1, gd), lambda i, g: (0, g))
    return pl.pallas_call(
        body, name=name, grid=(l // tr, N_GROUPS),
        in_specs=[blk, blk, pl.BlockSpec((1, gd, gd), lambda i, g: (g, 0, 0)), vec],
        out_specs=blk, out_shape=jax.ShapeDtypeStruct((l, d), F32),
        compiler_params=_cparams("parallel", "parallel"),
    )(h, diff_b, w_pool, scale)


def _pool_mix_bwd(diff_b, w_pool, scale, dh1, *, name):
    l, d = dh1.shape
    gd = d // N_GROUPS
    tr = _tile(l, 1408)
    nsteps = l // tr

    def body(d_ref, w_ref, s_ref, dy_ref, dd_ref, dw_ref, ds_ref, sacc):
        i = pl.program_id(1)

        @pl.when(i == 0)
        def _():
            dw_ref[...] = jnp.zeros_like(dw_ref)
            sacc[...] = jnp.zeros_like(sacc)

        diff_b_, wg, dy = d_ref[...], w_ref[0], dy_ref[...]
        yy = jnp.dot(diff_b_, wg, preferred_element_type=F32)
        sacc[...] += _rows8(dy * yy)
        dyy_b = (dy * s_ref[...]).astype(BF16)
        dw_ref[0] += lax.dot_general(diff_b_, dyy_b, _DIMS["tn"], preferred_element_type=F32)
        dd_ref[...] = lax.dot_general(dyy_b, wg, _DIMS["nt"], preferred_element_type=F32)

        @pl.when(i == nsteps - 1)
        def _():
            ds_ref[...] = jnp.sum(sacc[...], axis=0, keepdims=True)

    blk = pl.BlockSpec((tr, gd), lambda g, i: (i, g))
    vec = pl.BlockSpec((1, gd), lambda g, i: (0, g))
    wspec = pl.BlockSpec((1, gd, gd), lambda g, i: (g, 0, 0))
    return pl.pallas_call(
        body, name=name, grid=(N_GROUPS, nsteps),
        in_specs=[blk, wspec, vec, blk],
        out_specs=[blk, wspec, vec],
        out_shape=[jax.ShapeDtypeStruct((l, d), F32), jax.ShapeDtypeStruct((N_GROUPS, gd, gd), F32),
                   jax.ShapeDtypeStruct((1, d), F32)],
        scratch_shapes=[pltpu.VMEM((8, gd), F32)],
        compiler_params=_cparams("parallel", "arbitrary"),
    )(diff_b, w_pool, scale, dh1)


def _conv_chunk(cur, prev, w, b, rowi):
    s1 = jnp.where(rowi < 1, pltpu.roll(prev, 1, 0), pltpu.roll(cur, 1, 0))
    s2 = jnp.where(rowi < 2, pltpu.roll(prev, 2, 0), pltpu.roll(cur, 2, 0))
    return b + w[0] * s2 + w[1] * s1 + w[2] * cur, s1, s2


def _bcast_rows(ref, rows):
    v = ref[...]
    return [jnp.broadcast_to(v[k:k + 1], (rows, v.shape[1])) for k in range(v.shape[0])]


def _convgate_fwd(u, cw, cb, *, name, gather=None):
    l, f2 = u.shape
    f = f2 // 2
    tc = _tile(f, LANES)
    nc = f // tc
    rows = CONV_ROWS
    assert l % rows == 0

    def body(ug_ref, uv_ref, wg_ref, wv_ref, bg_ref, bv_ref, *rest):
        a_ref = rest[0] if gather is None else rest[1]
        if gather is not None:
            w_ref, _, g_ref, stage_ref, send_sems, recv_sems = rest

            @pl.when(pl.program_id(0) == 0)
            def _():
                _gather_begin(w_ref, g_ref, send_sems, recv_sems)

        rowi = lax.broadcasted_iota(jnp.int32, (rows, tc), 0)
        wg, wv = _bcast_rows(wg_ref, rows), _bcast_rows(wv_ref, rows)
        bg, bv = _bcast_rows(bg_ref, rows)[0], _bcast_rows(bv_ref, rows)[0]

        def chunk(i, carry):
            pg, pv = carry
            r = pl.multiple_of(i * rows, rows)
            cg, cv = ug_ref[pl.ds(r, rows), :], uv_ref[pl.ds(r, rows), :]
            gate, _, _ = _conv_chunk(cg, pg, wg, bg, rowi)
            val, _, _ = _conv_chunk(cv, pv, wv, bv, rowi)
            a_ref[pl.ds(r, rows), :] = (gate * jax.nn.sigmoid(gate) * val).astype(BF16)
            return cg, cv

        zero = jnp.zeros((rows, tc), F32)
        lax.fori_loop(0, l // rows, chunk, (zero, zero))
        if gather is not None:
            @pl.when(pl.program_id(0) == nc - 1)
            def _():
                _gather_end(w_ref, g_ref, stage_ref, send_sems, recv_sems)

    def spec(rows_, off):
        return pl.BlockSpec((rows_, tc), lambda j: (0, j + off))

    in_specs = [spec(l, 0), spec(l, nc), spec(3, 0), spec(3, nc), spec(1, 0), spec(1, nc)]
    out_shape = jax.ShapeDtypeStruct((l, f), BF16)
    if gather is None:
        return pl.pallas_call(
            body, name=name, grid=(nc,), in_specs=in_specs, out_specs=spec(l, 0), out_shape=out_shape,
            compiler_params=_cparams("parallel"),
        )(u, u, cw, cw, cb, cb)
    g_shape, scratch = _gather_out_and_scratch(gather)
    return pl.pallas_call(
        body, name=name, grid=(nc,), in_specs=in_specs + [_ANY], out_specs=[spec(l, 0), _ANY],
        out_shape=[out_shape, g_shape], scratch_shapes=scratch,
        compiler_params=_cparams("arbitrary"),
    )(u, u, cw, cw, cb, cb, gather)


def _convgate_bwd(u, cw, cb, da, *, name, scatter=None):
    l, f2 = u.shape
    f = f2 // 2
    tc = _tile(f, LANES)
    nc = f // tc
    rows = CONV_ROWS
    assert l % rows == 0
    nchunks = l // rows

    def body(ug_ref, uv_ref, wg_ref, wv_ref, bg_ref, bv_ref, da_ref, *rest):
        if scatter is None:
            dug_ref, duv_ref, dcwg_ref, dcwv_ref, dcbg_ref, dcbv_ref = rest
        else:
            p_ref, dug_ref, duv_ref, dcwg_ref, dcwv_ref, dcbg_ref, dcbv_ref, s_ref, stage_ref, ssems, rsems = rest

            @pl.when(pl.program_id(0) == 0)
            def _():
                _scatter_begin(p_ref, s_ref, ssems, rsems)

        rowi = lax.broadcasted_iota(jnp.int32, (rows, tc), 0)
        wg, wv = _bcast_rows(wg_ref, rows), _bcast_rows(wv_ref, rows)
        bg, bv = _bcast_rows(bg_ref, rows)[0], _bcast_rows(bv_ref, rows)[0]

        def tap_sums(acc, dc, taps):
            return (acc[0] + _rows8(dc),) + tuple(a + _rows8(dc * t) for a, t in zip(acc[1:], taps))

        def input_grad(dc, dc_next, w):
            up1 = jnp.where(rowi >= rows - 1, pltpu.roll(dc_next, rows - 1, 0), pltpu.roll(dc, rows - 1, 0))
            up2 = jnp.where(rowi >= rows - 2, pltpu.roll(dc_next, rows - 2, 0), pltpu.roll(dc, rows - 2, 0))
            return (w[2] * dc + w[1] * up1 + w[0] * up2).astype(BF16)

        def chunk(ii, carry):
            dcg_next, dcv_next, acc_g, acc_v = carry
            i = nchunks - 1 - ii
            r = pl.multiple_of(i * rows, rows)
            rp = pl.multiple_of(jnp.maximum(i - 1, 0) * rows, rows)
            cg, cv = ug_ref[pl.ds(r, rows), :], uv_ref[pl.ds(r, rows), :]
            pg = jnp.where(i > 0, ug_ref[pl.ds(rp, rows), :], 0.0)
            pv = jnp.where(i > 0, uv_ref[pl.ds(rp, rows), :], 0.0)
            gate, g1, g2 = _conv_chunk(cg, pg, wg, bg, rowi)
            val, v1, v2 = _conv_chunk(cv, pv, wv, bv, rowi)
            sg = jax.nn.sigmoid(gate)
            dav = da_ref[pl.ds(r, rows), :]
            dcg = dav * val * (sg * (1.0 + gate * (1.0 - sg)))
            dcv = dav * (gate * sg)
            acc_g = tap_sums(acc_g, dcg, (g2, g1, cg))
            acc_v = tap_sums(acc_v, dcv, (v2, v1, cv))
            dug_ref[pl.ds(r, rows), :] = input_grad(dcg, dcg_next, wg)
            duv_ref[pl.ds(r, rows), :] = input_grad(dcv, dcv_next, wv)
            return dcg, dcv, acc_g, acc_v

        zero = jnp.zeros((rows, tc), F32)
        zero8 = (jnp.zeros((8, tc), F32),) * 4
        _, _, acc_g, acc_v = lax.fori_loop(0, nchunks, chunk, (zero, zero, zero8, zero8))
        for acc, dcw_ref, dcb_ref in ((acc_g, dcwg_ref, dcbg_ref), (acc_v, dcwv_ref, dcbv_ref)):
            dcb_ref[...] = jnp.sum(acc[0], axis=0, keepdims=True)
            for k in range(3):
                dcw_ref[k:k + 1, :] = jnp.sum(acc[1 + k], axis=0, keepdims=True)
        if scatter is not None:
            @pl.when(pl.program_id(0) == nc - 1)
            def _():
                _scatter_end(p_ref, s_ref, stage_ref, ssems, rsems)

    def spec(rows_, off):
        return pl.BlockSpec((rows_, tc), lambda j: (0, j + off))

    in_specs = [spec(l, 0), spec(l, nc), spec(3, 0), spec(3, nc), spec(1, 0), spec(1, nc), spec(l, 0)]
    out_specs = [spec(l, 0), spec(l, 0), spec(3, 0), spec(3, 0), spec(1, 0), spec(1, 0)]
    out_shape = ([jax.ShapeDtypeStruct((l, f), BF16)] * 2 + [jax.ShapeDtypeStruct((3, f), F32)] * 2
                 + [jax.ShapeDtypeStruct((1, f), F32)] * 2)
    if scatter is None:
        return pl.pallas_call(
            body, name=name, grid=(nc,), in_specs=in_specs, out_specs=out_specs, out_shape=out_shape,
            compiler_params=_cparams("parallel"),
        )(u, u, cw, cw, cb, cb, da)
    s_shape, scratch = _scatter_out_and_scratch(scatter)
    return pl.pallas_call(
        body, name=name, grid=(nc,), in_specs=in_specs + [_ANY], out_specs=out_specs + [_ANY],
        out_shape=out_shape + [s_shape], scratch_shapes=scratch,
        compiler_params=_cparams("arbitrary"),
    )(u, u, cw, cw, cb, cb, da, scatter)


def _cumsum_mm(v_b, t2):
    return jnp.dot(v_b, t2, preferred_element_type=F32)


def _tri_and_ones(tri_fn):
    row = lax.broadcasted_iota(jnp.int32, (BLK, 2 * BLK), 0)
    col = lax.broadcasted_iota(jnp.int32, (BLK, 2 * BLK), 1)
    return jnp.where((col >= BLK) | tri_fn(row, col), 1.0, 0.0).astype(BF16)


def _logits(z, mask):
    sp = jnp.log(1.0 + jnp.exp(-jnp.abs(z)))
    lb = jnp.minimum(z, 0.0) - sp
    lm = lb - z
    if mask is not None:
        lm = jnp.where(mask, lm, 0.0)
    return lb, lm


def _block_start(index, size):
    return index * size if isinstance(index, int) else pl.multiple_of(index * size, size)


def _peeled_loop(body, first_looped, n):
    for i in range(min(first_looped, n)):
        body(i)
    if n > first_looped:
        def looped(i, carry):
            body(i)
            return carry
        lax.fori_loop(first_looped, n, looped, 0)


def _software_pipeline(stages, n, block_of, state):
    ns = len(stages)
    inflight = [None] * (ns - 1)
    for t in range(ns - 1):
        new = list(inflight)
        for s in range(t, -1, -1):
            y, state = stages[s](block_of(t - s), None if s == 0 else inflight[s - 1], state)
            new[s] = y
        inflight = new

    def steady(i, carry):
        inflight, state = carry
        new = [None] * (ns - 1)
        for s in range(ns - 1, -1, -1):
            y, state = stages[s](block_of(i + ns - 1 - s), None if s == 0 else inflight[s - 1], state)
            if s < ns - 1:
                new[s] = y
        return tuple(new), state

    inflight, state = lax.fori_loop(0, n - (ns - 1), steady, (tuple(inflight), state))
    inflight = list(inflight)
    for e in range(1, ns):
        new = list(inflight)
        for s in range(ns - 1, e - 1, -1):
            y, state = stages[s](block_of(n - 1 + e - s), inflight[s - 1], state)
            if s < ns - 1:
                new[s] = y
        inflight = new
    return state


def _attn_fwd(q, k, v, *, name):
    nh, l, dh = q.shape
    assert l % ATT_BQ == 0
    nq = l // ATT_BQ
    qscale = HEAD_DIM ** -0.5
    chunks = range(ATT_CHUNKS)

    def body(q_ref, k_ref, v_ref, o_ref, lt_ref, z_scr, cs_scr, pv_scr):
        t_later = _tri_and_ones(lambda r, c: r > c)
        dmask = (lax.broadcasted_iota(jnp.int32, (BLK, BLK), 1)
                 < lax.broadcasted_iota(jnp.int32, (BLK, BLK), 0))

        def logits(qc, c0, mask):
            z = lax.dot_general(qc, k_ref[0, pl.ds(c0, BLK), :], _DIMS["nt"], preferred_element_type=F32)
            lb, lm = _logits(z, mask)
            return lb, lm.astype(BF16)

        def weights(lb, lm_b, mask, run):
            cs = _cumsum_mm(lm_b, t_later)
            a = jnp.exp(lb + cs[:, :BLK] + run)
            if mask is not None:
                a = jnp.where(mask, a, 0.0)
            return a.astype(BF16), run + cs[:, BLK:]

        def accumulate(a_b, c0, acc):
            return acc + jnp.dot(a_b, v_ref[0, pl.ds(c0, BLK), :], preferred_element_type=F32)

        def qblock(qb):
            r0 = _block_start(qb, ATT_BQ)
            qs = [q_ref[0, pl.ds(r0 + rc * BLK, BLK), :] * jnp.asarray(qscale, BF16) for rc in chunks]
            accs = [jnp.zeros((BLK, dh), F32)] * ATT_CHUNKS
            runs = [jnp.zeros((BLK, BLK), F32)] * ATT_CHUNKS
            for dj in range(ATT_CHUNKS - 1, -1, -1):
                c0 = r0 + dj * BLK
                for rc in range(dj, ATT_CHUNKS):
                    mask = dmask if rc == dj else None
                    a_b, runs[rc] = weights(*logits(qs[rc], c0, mask), mask, runs[rc])
                    accs[rc] = accumulate(a_b, c0, accs[rc])

            def col0(b):
                return pl.multiple_of(r0 - (b + 1) * BLK, BLK)

            def stage_scores(b, _, state):
                for rc in chunks:
                    z_scr[rc] = lax.dot_general(qs[rc], k_ref[0, pl.ds(col0(b), BLK), :], _DIMS["nt"],
                                                preferred_element_type=F32)
                return (), state

            def stage_cumsum(b, _, state):
                lbs = []
                for rc in chunks:
                    lb, lm = _logits(z_scr[rc], None)
                    cs_scr[rc] = _cumsum_mm(lm.astype(BF16), t_later)
                    lbs.append(lb)
                return tuple(lbs), state

            def stage_weights(b, lbs, state):
                accs, runs = state
                new_runs = []
                for rc in chunks:
                    cs = cs_scr[rc]
                    a = jnp.exp(lbs[rc] + cs[:, :BLK] + runs[rc])
                    pv_scr[rc] = jnp.dot(a.astype(BF16), v_ref[0, pl.ds(col0(b), BLK), :],
                                         preferred_element_type=F32)
                    new_runs.append(runs[rc] + cs[:, BLK:])
                return (), (accs, tuple(new_runs))

            def stage_acc(b, _, state):
                accs, runs = state
                return None, (tuple(accs[rc] + pv_scr[rc] for rc in chunks), runs)

            def left_region(state):
                return _software_pipeline([stage_scores, stage_cumsum, stage_weights, stage_acc],
                                          qb * ATT_CHUNKS, lambda b: b, state)

            accs, runs = tuple(accs), tuple(runs)
            if not (isinstance(qb, int) and qb == 0):
                accs, runs = left_region((accs, runs))
            for rc in chunks:
                o_ref[0, pl.ds(r0 + rc * BLK, BLK), :] = accs[rc].astype(BF16)
                lt_ref[0, pl.ds(r0 + rc * BLK, BLK), :] = runs[rc]

        _peeled_loop(qblock, 2, nq)

    head = pl.BlockSpec((1, l, dh), lambda h: (h, 0, 0))
    return pl.pallas_call(
        body, name=name, grid=(nh,),
        in_specs=[head, head, head],
        out_specs=[head, pl.BlockSpec((1, l, BLK), lambda h: (h, 0, 0))],
        out_shape=[jax.ShapeDtypeStruct((nh, l, dh), BF16), jax.ShapeDtypeStruct((nh, l, BLK), F32)],
        scratch_shapes=[pltpu.VMEM((ATT_CHUNKS, BLK, BLK), F32), pltpu.VMEM((ATT_CHUNKS, BLK, 2 * BLK), F32),
                        pltpu.VMEM((ATT_CHUNKS, BLK, dh), F32)],
        compiler_params=_cparams("parallel"),
    )(q, k, v)


def _attn_bwd(q, k, v, do, ltot, *, name):
    nh, l, dh = q.shape
    assert l % ATT_BQ == 0
    nq = l // ATT_BQ
    qscale = HEAD_DIM ** -0.5
    chunks = range(ATT_CHUNKS)

    def body(q_ref, k_ref, v_ref, do_ref, lt_ref, dq_ref, dk_ref, dv_ref,
             z_scr, cs_scr, da_scr, cd_scr, dqp_scr, dkp_scr, dvp_scr):
        t_incl = _tri_and_ones(lambda r, c: r <= c)
        t_excl = _tri_and_ones(lambda r, c: r < c)
        dmask = (lax.broadcasted_iota(jnp.int32, (BLK, BLK), 1)
                 < lax.broadcasted_iota(jnp.int32, (BLK, BLK), 0))
        dk_ref[...] = jnp.zeros_like(dk_ref)
        dv_ref[...] = jnp.zeros_like(dv_ref)

        def logits(qc, c0, mask):
            z = lax.dot_general(qc, k_ref[0, pl.ds(c0, BLK), :], _DIMS["nt"], preferred_element_type=F32)
            lb, lm = _logits(z, mask)
            return lb, lm.astype(BF16)

        def weights(lb, lm_b, doc, ltc, c0, mask, pre_lm):
            cs = _cumsum_mm(lm_b, t_incl)
            da = lax.dot_general(doc, v_ref[0, pl.ds(c0, BLK), :], _DIMS["nt"], preferred_element_type=F32)
            a = jnp.exp(lb + (ltc - pre_lm - cs[:, :BLK]))
            if mask is not None:
                a = jnp.where(mask, a, 0.0)
            dl = a * da
            return (jnp.exp(lb), dl, dl.astype(BF16), a.astype(BF16)), pre_lm + cs[:, BLK:]

        def logit_grad(beta, dl, dl_b, mask, pre_dl):
            cd = _cumsum_mm(dl_b, t_excl)
            dz = dl - beta * (dl + pre_dl + cd[:, :BLK])
            if mask is not None:
                dz = jnp.where(mask, dz, 0.0)
            return (dz * qscale).astype(BF16), pre_dl + cd[:, BLK:]

        def key_grads(c0, dz_parts, a_parts, q_rows, do_rows):
            dz_all = dz_parts[0] if len(dz_parts) == 1 else jnp.concatenate(dz_parts, axis=0)
            a_all = a_parts[0] if len(a_parts) == 1 else jnp.concatenate(a_parts, axis=0)
            dk_ref[0, pl.ds(c0, BLK), :] += lax.dot_general(dz_all, q_rows, _DIMS["tn"], preferred_element_type=F32)
            dv_ref[0, pl.ds(c0, BLK), :] += lax.dot_general(a_all, do_rows, _DIMS["tn"], preferred_element_type=F32)

        def qblock(qb):
            r0 = _block_start(qb, ATT_BQ)
            q_raw = q_ref[0, pl.ds(r0, ATT_BQ), :]
            do_all = do_ref[0, pl.ds(r0, ATT_BQ), :]
            qs = [q_raw[rc * BLK:(rc + 1) * BLK] * jnp.asarray(qscale, BF16) for rc in chunks]
            dos = [do_all[rc * BLK:(rc + 1) * BLK] for rc in chunks]
            lts = [lt_ref[0, pl.ds(r0 + rc * BLK, BLK), :] for rc in chunks]

            def col0(b):
                return pl.multiple_of(b * BLK, BLK)

            def stage_scores(b, _, state):
                kj = k_ref[0, pl.ds(col0(b), BLK), :]
                for rc in chunks:
                    z_scr[rc] = lax.dot_general(qs[rc], kj, _DIMS["nt"], preferred_element_type=F32)
                return (), state

            def stage_prefix(b, _, state):
                vj = v_ref[0, pl.ds(col0(b), BLK), :]
                lbs = []
                for rc in chunks:
                    lb, lm = _logits(z_scr[rc], None)
                    cs_scr[rc] = _cumsum_mm(lm.astype(BF16), t_incl)
                    da_scr[rc] = lax.dot_general(dos[rc], vj, _DIMS["nt"], preferred_element_type=F32)
                    lbs.append(lb)
                return tuple(lbs), state

            def stage_weights(b, lbs, state):
                dqs, pls, pds = state
                out, new_pls = [], []
                for rc in chunks:
                    cs = cs_scr[rc]
                    a = jnp.exp(lbs[rc] + (lts[rc] - pls[rc] - cs[:, :BLK]))
                    dl = a * da_scr[rc]
                    cd_scr[rc] = _cumsum_mm(dl.astype(BF16), t_excl)
                    out.append((jnp.exp(lbs[rc]), dl, a.astype(BF16)))
                    new_pls.append(pls[rc] + cs[:, BLK:])
                return tuple(out), (dqs, tuple(new_pls), pds)

            def stage_products(b, ys, state):
                dqs, pls, pds = state
                c0 = col0(b)
                kj = k_ref[0, pl.ds(c0, BLK), :]
                dz_bs, new_pds = [], []
                for rc in chunks:
                    beta, dl, _ = ys[rc]
                    cd = cd_scr[rc]
                    dz_b = ((dl - beta * (dl + pds[rc] + cd[:, :BLK])) * qscale).astype(BF16)
                    dqp_scr[rc] = jnp.dot(dz_b, kj, preferred_element_type=F32)
                    dz_bs.append(dz_b)
                    new_pds.append(pds[rc] + cd[:, BLK:])
                dkp_scr[...] = lax.dot_general(jnp.concatenate(dz_bs, axis=0), q_raw, _DIMS["tn"],
                                               preferred_element_type=F32)
                dvp_scr[...] = lax.dot_general(jnp.concatenate([y[2] for y in ys], axis=0), do_all, _DIMS["tn"],
                                               preferred_element_type=F32)
                return (), (dqs, pls, tuple(new_pds))

            def stage_acc(b, _, state):
                dqs, pls, pds = state
                c0 = col0(b)
                dk_ref[0, pl.ds(c0, BLK), :] += dkp_scr[...]
                dv_ref[0, pl.ds(c0, BLK), :] += dvp_scr[...]
                return None, (tuple(dqs[rc] + dqp_scr[rc] for rc in chunks), pls, pds)

            stages = [stage_scores, stage_prefix, stage_weights, stage_products, stage_acc]
            zero = jnp.zeros((BLK, BLK), F32)
            state = ((jnp.zeros((BLK, dh), F32),) * ATT_CHUNKS, (zero,) * ATT_CHUNKS, (zero,) * ATT_CHUNKS)

            def pipelined(state):
                return _software_pipeline(stages, qb * ATT_CHUNKS, lambda b: b, state)

            def one_by_one(state):
                def block(b, state):
                    x = None
                    for stage in stages:
                        x, state = stage(b, x, state)
                    return state
                return lax.fori_loop(0, qb * ATT_CHUNKS, block, state)

            if isinstance(qb, int):
                dqs, pls, pds = one_by_one(state) if qb > 0 else state
            else:
                dqs, pls, pds = pipelined(state)
            dqs, pls, pds = list(dqs), list(pls), list(pds)
            for dj in chunks:
                c0 = r0 + dj * BLK
                kj = k_ref[0, pl.ds(c0, BLK), :]
                dz_parts, a_parts = [], []
                for rc in range(dj, ATT_CHUNKS):
                    mask = dmask if rc == dj else None
                    ys, pls[rc] = weights(*logits(qs[rc], c0, mask), dos[rc], lts[rc], c0, mask, pls[rc])
                    dz_b, pds[rc] = logit_grad(*ys[:3], mask, pds[rc])
                    a_b = ys[3]
                    dqs[rc] = dqs[rc] + jnp.dot(dz_b, kj, preferred_element_type=F32)
                    dz_parts.append(dz_b)
                    a_parts.append(a_b)
                key_grads(c0, dz_parts, a_parts, q_raw[dj * BLK:], do_all[dj * BLK:])
            for rc in chunks:
                dq_ref[0, pl.ds(r0 + rc * BLK, BLK), :] = dqs[rc]

        _peeled_loop(qblock, -(-4 // ATT_CHUNKS), nq)

    head = pl.BlockSpec((1, l, dh), lambda h: (h, 0, 0))
    sq = pltpu.VMEM((ATT_CHUNKS, BLK, BLK), F32)
    sw = pltpu.VMEM((ATT_CHUNKS, BLK, 2 * BLK), F32)
    return pl.pallas_call(
        body, name=name, grid=(nh,),
        in_specs=[head, head, head, head, pl.BlockSpec((1, l, BLK), lambda h: (h, 0, 0))],
        out_specs=[head, head, head],
        out_shape=[jax.ShapeDtypeStruct((nh, l, dh), F32)] * 3,
        scratch_shapes=[sq, sw, sq, sw, pltpu.VMEM((ATT_CHUNKS, BLK, dh), F32),
                        pltpu.VMEM((BLK, dh), F32), pltpu.VMEM((BLK, dh), F32)],
        compiler_params=_cparams("parallel"),
    )(q, k, v, do, ltot)


def _add_to_bf16(a, b, *, name):
    n, r, c = a.shape
    tr = _row_tile(r, 1536, 16)
    blk = pl.BlockSpec((1, tr, c), lambda i, j: (i, j, 0))

    def body(a_ref, b_ref, o_ref):
        o_ref[...] = (a_ref[...] + b_ref[...].astype(F32)).astype(BF16)

    return pl.pallas_call(body, name=name, grid=(n, r // tr), in_specs=[blk, blk], out_specs=blk,
                          out_shape=jax.ShapeDtypeStruct(a.shape, BF16),
                          compiler_params=_cparams("parallel", "parallel"))(a, b)


def _sum_slots(p, *, name):
    n, r, c = p.shape
    tr = _row_tile(r, 1536, 16)

    def body(p_ref, o_ref):
        acc = p_ref[0].astype(F32)
        for s in range(1, n):
            acc = acc + p_ref[s].astype(F32)
        o_ref[...] = acc

    return pl.pallas_call(body, name=name, grid=(r // tr,),
                          in_specs=[pl.BlockSpec((n, tr, c), lambda j: (0, j, 0))],
                          out_specs=pl.BlockSpec((tr, c), lambda j: (j, 0)),
                          out_shape=jax.ShapeDtypeStruct((r, c), F32),
                          compiler_params=_cparams("parallel"))(p)


def _adamw(w, g, m, v, *, name):
    r, c = w.shape
    tr = _row_tile(r, 512, 8)
    blk = pl.BlockSpec((tr, c), lambda i: (i, 0))

    def body(w_ref, g_ref, m_ref, v_ref, d_ref, mo_ref, vo_ref):
        gv = g_ref[...]
        mn = ADAM_B1 * m_ref[...] + (1.0 - ADAM_B1) * gv
        vn = ADAM_B2 * v_ref[...] + (1.0 - ADAM_B2) * (gv * gv)
        m_hat = mn / (1.0 - ADAM_B1 ** ADAM_STEP)
        v_hat = vn / (1.0 - ADAM_B2 ** ADAM_STEP)
        d_ref[...] = -ADAM_LR * (m_hat / (jnp.sqrt(v_hat) + ADAM_EPS) + ADAM_WD * w_ref[...])
        mo_ref[...] = mn
        vo_ref[...] = vn

    return pl.pallas_call(body, name=name, grid=(r // tr,), in_specs=[blk] * 4, out_specs=[blk] * 3,
                          out_shape=[jax.ShapeDtypeStruct((r, c), F32)] * 3,
                          compiler_params=_cparams("parallel"))(w, g, m, v)


_ANY = pl.BlockSpec(memory_space=pl.ANY)


def _place():
    x, y, c = lax.axis_index("x"), lax.axis_index("y"), lax.axis_index("c")
    chips = [(1 - x, y), (x, 1 - y), (1 - x, 1 - y)]
    return x, y, c, chips


def _gather_copy(k, src, dst, to, send_sems, recv_sems):
    return pltpu.make_async_remote_copy(src_ref=src, dst_ref=dst, send_sem=send_sems.at[k],
                                        recv_sem=recv_sems.at[k], device_id=to, device_id_type=MESH)


def _gather_begin(w_ref, out_ref, send_sems, recv_sems):
    x, y, c, chips = _place()
    s = 2 * x + y
    for j, chip in enumerate(chips):
        _gather_copy(j, w_ref.at[c], out_ref.at[s, c], (*chip, c), send_sems, recv_sems).start()


def _gather_end(w_ref, out_ref, stage_ref, send_sems, recv_sems):
    x, y, c, chips = _place()
    s = 2 * x + y
    sibling = (x, y, 1 - c)
    slots = [2 * px + py for px, py in chips]
    passed = []
    for j, sp in enumerate(slots):
        _gather_copy(j, w_ref.at[c], out_ref.at[sp, c], sibling, send_sems, recv_sems).wait_recv()
        fwd = _gather_copy(3 + j, out_ref.at[sp, c], out_ref.at[sp, c], sibling, send_sems, recv_sems)
        fwd.start()
        passed.append(fwd)
    for h in range(2):
        pltpu.sync_copy(w_ref.at[h], stage_ref)
        pltpu.sync_copy(stage_ref, out_ref.at[s, h])
    for j, sp in enumerate(slots):
        _gather_copy(3 + j, w_ref.at[c], out_ref.at[sp, 1 - c], sibling, send_sems, recv_sems).wait_recv()
    for j, chip in enumerate(chips):
        _gather_copy(j, w_ref.at[c], out_ref.at[s, c], (*chip, c), send_sems, recv_sems).wait_send()
    for fwd in passed:
        fwd.wait_send()


def _gather_out_and_scratch(wsh):
    _, r, cdim = wsh.shape
    return (jax.ShapeDtypeStruct((N_CHIPS, 2, r, cdim), wsh.dtype),
            [pltpu.VMEM((r, cdim), wsh.dtype), pltpu.SemaphoreType.DMA((6,)), pltpu.SemaphoreType.DMA((6,))])


def _gather_shards(wsh, *, name):
    def body(w_ref, out_ref, stage_ref, send_sems, recv_sems):
        _gather_begin(w_ref, out_ref, send_sems, recv_sems)
        _gather_end(w_ref, out_ref, stage_ref, send_sems, recv_sems)

    out_shape, scratch = _gather_out_and_scratch(wsh)
    return pl.pallas_call(
        body, name=name, in_specs=[_ANY], out_specs=_ANY, out_shape=out_shape, scratch_shapes=scratch,
        compiler_params=pltpu.CompilerParams(vmem_limit_bytes=VMEM_LIMIT),
    )(wsh)


def _to_sibling(a, *, name):
    def body(a_ref, out_ref, send_sem, recv_sem):
        x, y, c, _ = _place()
        cp = pltpu.make_async_remote_copy(src_ref=a_ref, dst_ref=out_ref, send_sem=send_sem, recv_sem=recv_sem,
                                          device_id=(x, y, 1 - c), device_id_type=MESH)
        cp.start()
        cp.wait()

    return pl.pallas_call(
        body, name=name, in_specs=[_ANY], out_specs=_ANY,
        out_shape=jax.ShapeDtypeStruct(a.shape, a.dtype),
        scratch_shapes=[pltpu.SemaphoreType.DMA, pltpu.SemaphoreType.DMA],
    )(a)


def _scatter_copy(j, p_ref, out_ref, send_sems, recv_sems, receive):
    x, y, c, chips = _place()
    px, py = chips[j]
    return pltpu.make_async_remote_copy(
        src_ref=p_ref.at[2 * px + py], dst_ref=out_ref.at[(2 * px + py) if receive else (2 * x + y)],
        send_sem=send_sems.at[j], recv_sem=recv_sems.at[j], device_id=(px, py, c), device_id_type=MESH)


def _scatter_begin(p_ref, out_ref, send_sems, recv_sems):
    for j in range(N_CHIPS - 1):
        _scatter_copy(j, p_ref, out_ref, send_sems, recv_sems, False).start()


def _scatter_end(p_ref, out_ref, stage_ref, send_sems, recv_sems):
    x, y, _, _ = _place()
    s = 2 * x + y
    pltpu.sync_copy(p_ref.at[s], stage_ref)
    pltpu.sync_copy(stage_ref, out_ref.at[s])
    for j in range(N_CHIPS - 1):
        _scatter_copy(j, p_ref, out_ref, send_sems, recv_sems, True).wait_recv()
    for j in range(N_CHIPS - 1):
        _scatter_copy(j, p_ref, out_ref, send_sems, recv_sems, False).wait_send()


def _scatter_out_and_scratch(p):
    return (jax.ShapeDtypeStruct(p.shape, p.dtype),
            [pltpu.VMEM(p.shape[1:], p.dtype), pltpu.SemaphoreType.DMA((N_CHIPS - 1,)),
             pltpu.SemaphoreType.DMA((N_CHIPS - 1,))])


def _scatter_rows(p, *, name):
    def body(p_ref, out_ref, stage_ref, send_sems, recv_sems):
        _scatter_begin(p_ref, out_ref, send_sems, recv_sems)
        _scatter_end(p_ref, out_ref, stage_ref, send_sems, recv_sems)

    out_shape, scratch = _scatter_out_and_scratch(p)
    return pl.pallas_call(
        body, name=name, in_specs=[_ANY], out_specs=_ANY, out_shape=out_shape, scratch_shapes=scratch,
        compiler_params=pltpu.CompilerParams(vmem_limit_bytes=VMEM_LIMIT),
    )(p)


def _all_devices(a, reduce, *, name):
    r, cdim = a.shape

    def body(a_ref, out_ref, *scratch):
        if reduce:
            buf, send_sems, recv_sems = scratch
        else:
            buf = out_ref
            send_sems, recv_sems = scratch
        x, y, c, _ = _place()
        me = 4 * x + 2 * y + c
        buf[me] = a_ref[...]
        peers = []
        for k in range(1, N_DEV):
            dx, dy, dc = (k >> 2) & 1, (k >> 1) & 1, k & 1
            peers.append((x ^ dx, y ^ dy, c ^ dc))
        sends = []
        for k, peer in enumerate(peers):
            cp = pltpu.make_async_remote_copy(src_ref=a_ref, dst_ref=buf.at[me], send_sem=send_sems.at[k],
                                              recv_sem=recv_sems.at[k], device_id=peer, device_id_type=MESH)
            cp.start()
            sends.append(cp)
        for k, (px, py, pc) in enumerate(peers):
            pltpu.make_async_remote_copy(src_ref=a_ref, dst_ref=buf.at[4 * px + 2 * py + pc],
                                         send_sem=send_sems.at[k], recv_sem=recv_sems.at[k],
                                         device_id=(px, py, pc), device_id_type=MESH).wait_recv()
        for cp in sends:
            cp.wait_send()
        if reduce:
            acc = buf[0]
            for k in range(1, N_DEV):
                acc = acc + buf[k]
            out_ref[...] = acc

    vm = pl.BlockSpec(memory_space=pltpu.VMEM)
    sems = [pltpu.SemaphoreType.DMA((N_DEV - 1,)), pltpu.SemaphoreType.DMA((N_DEV - 1,))]
    if reduce:
        out_shape = jax.ShapeDtypeStruct((r, cdim), F32)
        scratch = [pltpu.VMEM((N_DEV, r, cdim), F32)] + sems
    else:
        out_shape = jax.ShapeDtypeStruct((N_DEV, r, cdim), F32)
        scratch = sems
    return pl.pallas_call(
        body, name=name, in_specs=[vm], out_specs=vm, out_shape=out_shape, scratch_shapes=scratch,
        compiler_params=pltpu.CompilerParams(vmem_limit_bytes=VMEM_LIMIT),
    )(a)


def _flat_rows(parts, cols):
    flat = jnp.concatenate([p.reshape(-1) for p in parts])
    padded = -(-flat.size // (8 * cols)) * (8 * cols)
    return jnp.pad(flat, (0, padded - flat.size)).reshape(-1, cols)


def kernel(x, meta_tokens, mix_norm, ffn_norm, pool_w, pool_scale, kv_norm, w_kv, w_q, w_o, ffn_w_up, ffn_conv_w, ffn_conv_b, ffn_w_down, final_norm, loss_target, m_meta_tokens, m_mix_norm, m_ffn_norm, m_pool_w, m_pool_scale, m_kv_norm, m_w_kv, m_w_q, m_w_o, m_ffn_w_up, m_ffn_conv_w, m_ffn_conv_b, m_ffn_w_down, m_final_norm, v_meta_tokens, v_mix_norm, v_ffn_norm, v_pool_w, v_pool_scale, v_kv_norm, v_w_kv, v_w_q, v_w_o, v_ffn_w_up, v_ffn_conv_w, v_ffn_conv_b, v_ffn_w_down, v_final_norm):
    seq, d = x.shape[1], x.shape[2]
    l_real = N_META + seq
    lp = -(-l_real // ATT_BQ) * ATT_BQ
    f2 = ffn_w_up.shape[2] * N_CHIPS
    f = f2 // 2
    gd = d // N_GROUPS
    chip = 2 * lax.axis_index("x") + lax.axis_index("y")
    core = lax.axis_index("c")

    cw = 1024

    def shard_halves(parts):
        return _flat_rows([p.astype(BF16) for p in parts], cw).reshape(2, -1, cw)

    def unpack(gathered, parts):
        flat, out, first = gathered.reshape(N_CHIPS, -1), [], 0
        for p in parts:
            out.append(flat[:, first:first + p.size].reshape((N_CHIPS,) + p.shape))
            first += p.size
        return out

    now_parts = [pool_w[0], ffn_w_up[0], ffn_w_down[0]]
    attn_parts = [w_kv, w_q[0], w_o[0]]
    late_parts = [ffn_w_up[1]]
    down_parts = [ffn_w_down[1]]
    wp_s, wup0_s, wdn0_s = unpack(_gather_shards(shard_halves(now_parts), name="gather_weights"), now_parts)
    wp_b = wp_s.transpose(1, 0, 2, 3).reshape(N_GROUPS, gd, gd)
    wup_b = [wup0_s.transpose(1, 0, 2).reshape(d, f2), None]
    wdn_b = [wdn0_s.reshape(f, d), None]

    small_parts = [meta_tokens, pool_scale, ffn_conv_w]
    ssizes = [p.size for p in small_parts]
    small = _flat_rows(small_parts, LANES)
    sall = _all_devices(small, False, name="gather_small")[::2].reshape(N_CHIPS, -1)
    meta_f = sall[:, :ssizes[0]].reshape(N_CHIPS, N_META, d // N_CHIPS).transpose(1, 0, 2).reshape(N_META, d)
    scale_f = sall[:, ssizes[0]:ssizes[0] + ssizes[1]].reshape(1, d)
    cw_f = sall[:, ssizes[0] + ssizes[1]:sum(ssizes)].reshape(N_CHIPS, 2, 3, f2 // N_CHIPS).transpose(1, 2, 0, 3).reshape(2, 3, f2)

    mix0, mix1 = mix_norm[0:1], mix_norm[1:2]
    fn0, fn1 = ffn_norm[0:1], ffn_norm[1:2]
    kvn_g = kv_norm.reshape(1, d)
    fin_g = final_norm.reshape(1, d)

    pad = lp - l_real
    h0 = jnp.concatenate([meta_f, x[0], jnp.zeros((pad, d), F32)], axis=0)
    tgt = jnp.pad(loss_target[0], ((N_META, pad), (0, 0)))

    _, r0 = _norm_fwd(h0, [mix0], name="norm_h0")
    diff_b = _pool_diff_fwd(h0, r0, mix0, name="pool_diff")
    h1 = _pool_mix_fwd(h0, diff_b, wp_b, scale_f, name="pool_mix")

    def ffn_fwd(h_in, layer, tag, gather_behind_up=None, gather_behind_gate=None, gather_behind_down=None):
        (fb,), rf = _norm_fwd(h_in, [ffn_norm[layer:layer + 1]], name=f"norm_ffn{tag}")
        u = _mm(fb, wup_b[layer], "nn", tm=1408, tn=512, name=f"ffn_up{tag}", gather=gather_behind_up)
        gathered = []
        if gather_behind_up is not None:
            u, g = u
            gathered.append(g)
        a = _convgate_fwd(u, cw_f[layer], ffn_conv_b[layer:layer + 1], name=f"convgate{tag}",
                          gather=gather_behind_gate)
        if gather_behind_gate is not None:
            a, g = a
            gathered.append(g)
        h_out = _mm(a, wdn_b[layer], "nn", tm=384, tn=1024, res=h_in, name=f"ffn_down{tag}",
                    gather=gather_behind_down)
        if gather_behind_down is not None:
            h_out, g = h_out
            gathered.append(g)
        return h_out, (fb, rf, u, a), gathered

    h2, ffn0_saved, (attn_gathered, late_gathered, down_gathered) = ffn_fwd(
        h1, 0, "0", shard_halves(attn_parts), shard_halves(late_parts), shard_halves(down_parts))
    wkv_s, wq_s, wo_s = unpack(attn_gathered, attn_parts)
    (wup1_s,) = unpack(late_gathered, late_parts)
    (wdn1_s,) = unpack(down_gathered, down_parts)
    wkv_b = wkv_s.transpose(1, 0, 2).reshape(d, 2 * d)
    wq_b = wq_s.reshape(d, d)
    wo_b = wo_s.reshape(d, d)
    wup_b[1] = wup1_s.transpose(1, 0, 2).reshape(d, f2)
    wdn_b[1] = wdn1_s.reshape(f, d)
    (kvn_b, n1_b), r2 = _norm_fwd(h2, [kvn_g, mix1], name="norm_h2")
    kv_b = _mm(kvn_b, wkv_b, "nn", tm=1408, tn=512, out_dtype=BF16, name="kv_proj")
    q_b = _mm(n1_b, wq_b, "nn", tm=1408, tn=512, out_dtype=BF16, name="q_proj")

    def heads(t):
        return t.reshape(lp, N_HEADS, HEAD_DIM).transpose(1, 0, 2)

    def unheads(t):
        return t.transpose(1, 0, 2).reshape(lp, N_HEADS * HEAD_DIM)

    qh, kh, vh = heads(q_b), heads(kv_b[:, :d]), heads(kv_b[:, d:])
    oh, ltot = _attn_fwd(qh, kh, vh, name="attn_fwd")
    o_b = unheads(oh)
    h3 = _mm(o_b, wo_b, "nn", tm=384, tn=1024, res=h2, name="o_proj")
    h4, ffn1_saved, _ = ffn_fwd(h3, 1, "1")

    loss8, dh4, dh4_b, dfin = _loss_bwd(h4, fin_g, tgt, seq, name="loss")

    def ffn_bwd(h_in, layer, saved, dh_out, dh_out_b, tag, scatter_behind_gate=None):
        fb, rf, u, a = saved
        dwdn = _mm(a, dh_out_b, "tn", tm=256, tn=1024, name=f"d_wdown{tag}")
        da = _mm(dh_out_b, wdn_b[layer], "nt", tm=384, tn=f, name=f"d_act{tag}")
        du_g, du_v, dcw_g, dcw_v, dcb_g, dcb_v, *scattered = _convgate_bwd(
            u, cw_f[layer], ffn_conv_b[layer:layer + 1], da, name=f"convgate_bwd{tag}", scatter=scatter_behind_gate)
        dcw = jnp.concatenate([dcw_g, dcw_v], axis=1)
        dcb = jnp.concatenate([dcb_g, dcb_v], axis=1)
        half_chips = N_CHIPS // 2
        dwup = jnp.concatenate(
            [_mm(fb, du_g, "tn", tm=512, tn=f2 // N_CHIPS, out_shards=half_chips, name=f"d_wup_gate{tag}"),
             _mm(fb, du_v, "tn", tm=512, tn=f2 // N_CHIPS, out_shards=half_chips, name=f"d_wup_val{tag}")], axis=0)
        df = _mm(du_g, wup_b[layer][:, :f], "nt", tm=384, tn=1024, name=f"d_ffn_in_gate{tag}")
        df = _mm(du_v, wup_b[layer][:, f:], "nt", tm=384, tn=1024, res=df, name=f"d_ffn_in_val{tag}")
        dh_in, dh_in_b, (dfn,) = _norm_bwd(h_in, rf, [ffn_norm[layer:layer + 1]], [df], dh_out,
                                          name=f"norm_ffn_bwd{tag}")
        return (dh_in, dh_in_b, dwup, dwdn, dcw, dcb, dfn) + tuple(scattered)

    dh3, dh3_b, dwup1, dwdn1, dcw1, dcb1, dfn1 = ffn_bwd(h3, 1, ffn1_saved, dh4, dh4_b, "1")

    dwo = _mm(o_b, dh3_b, "tn", tm=1024, tn=512, name="d_wo")
    do_b = _mm(dh3_b, wo_b, "nt", tm=384, tn=1024, out_dtype=BF16, name="d_attn_out")
    dqh, dkh, dvh = _attn_bwd(qh, kh, vh, heads(do_b), ltot, name="attn_bwd")
    dq_b = unheads(dqh).astype(BF16)
    dkv_b = jnp.concatenate([unheads(dkh), unheads(dvh)], axis=1).astype(BF16)
    dwq = _mm(n1_b, dq_b, "tn", tm=1024, tn=512, name="d_wq")
    dwkv = _mm(kvn_b, dkv_b, "tn", tm=1024, tn=512, out_shards=N_CHIPS, name="d_wkv")
    dn1 = _mm(dq_b, wq_b, "nt", tm=384, tn=1024, name="d_n1")
    dkvn = _mm(dkv_b, wkv_b, "nt", tm=384, tn=1024, name="d_kvn")
    def core_halves(parts):
        g = jnp.concatenate([p.reshape(N_CHIPS, -1) for p in parts], axis=1).reshape(N_CHIPS, 2, -1, cw)
        return (lax.dynamic_index_in_dim(g, core, axis=1, keepdims=False),
                lax.dynamic_index_in_dim(g, 1 - core, axis=1, keepdims=False).astype(BF16))

    def reduced_shard(slots, tag):
        half = _sum_slots(slots, name=f"grads_sum{tag}")
        other_half = _to_sibling(half, name=f"grads_join{tag}")
        return jnp.stack([jnp.where(core == 0, half, other_half), jnp.where(core == 0, other_half, half)]).reshape(-1)

    def unpack_shard(flat, shapes):
        out, first = [], 0
        for shape in shapes:
            size = 1
            for n in shape:
                size *= n
            out.append(flat[first:first + size].reshape(shape))
            first += size
        return out

    early_mine, early_other = core_halves([dwkv, dwq.reshape(N_CHIPS, d // N_CHIPS, d),
                                           dwo.reshape(N_CHIPS, d // N_CHIPS, d), dwup1,
                                           dwdn1.reshape(N_CHIPS, f // N_CHIPS, d)])
    dh2, dh2_b, (dmix1, dkvg), early_from_sibling = _norm_bwd(h2, r2, [mix1, kvn_g], [dn1, dkvn], dh3,
                                                             name="norm_h2_bwd", send=early_other)
    early = _add_to_bf16(early_mine, early_from_sibling, name="grads_chip_sum_early")
    dh1, dh1_b, dwup0, dwdn0, dcw0, dcb0, dfn0, early_slots = ffn_bwd(h1, 0, ffn0_saved, dh2, dh2_b, "0", early)
    g_w_kv, g_w_q, g_w_o, g_w_up1, g_w_dn1 = unpack_shard(
        reduced_shard(early_slots, "_early"),
        [w_kv.shape, w_q.shape, w_o.shape, ffn_w_up.shape[1:], ffn_w_down.shape[1:]])

    ddiff, dwp, dscale = _pool_mix_bwd(diff_b, wp_b, scale_f, dh1, name="pool_mix_bwd")
    dn0 = _pool_diff_bwd(ddiff, name="pool_diff_bwd")
    dh0, _, (dmix0,) = _norm_bwd(h0, r0, [mix0], [dn0], dh1, name="norm_h0_bwd")

    grad_x = dh0[N_META:l_real][None]
    dmeta = dh0[:N_META]

    late_mine, late_other = core_halves([dwp.reshape(N_GROUPS, N_CHIPS, gd // N_CHIPS, gd).transpose(1, 0, 2, 3),
                                         dwup0, dwdn0.reshape(N_CHIPS, f // N_CHIPS, d)])
    late = _add_to_bf16(late_mine, _to_sibling(late_other, name="grads_to_sibling_late"), name="grads_chip_sum_late")
    g_pool_w, g_w_up0, g_w_dn0 = unpack_shard(
        reduced_shard(_scatter_rows(late, name="grads_scatter_late"), "_late"),
        [pool_w.shape, ffn_w_up.shape[1:], ffn_w_down.shape[1:]])
    g_w_up = jnp.stack([g_w_up0, g_w_up1])
    g_w_dn = jnp.stack([g_w_dn0, g_w_dn1])

    dcw = jnp.stack([dcw0, dcw1])
    dcb = jnp.concatenate([dcb0, dcb1], axis=0)
    sg_parts = [jnp.concatenate([dmix0, dmix1], axis=0), jnp.concatenate([dfn0, dfn1], axis=0), dkvg, dfin, dcb,
                dmeta, dscale, dcw, loss8]
    sg_sizes = [p.size for p in sg_parts]
    sg = _all_devices(_flat_rows(sg_parts, LANES), True, name="reduce_small").reshape(-1)
    sg_offs = [0]
    for sz in sg_sizes:
        sg_offs.append(sg_offs[-1] + sz)

    def gsmall(i, shape):
        return sg[sg_offs[i]:sg_offs[i + 1]].reshape(shape)

    g_mix = gsmall(0, mix_norm.shape)
    g_ffn_norm = gsmall(1, ffn_norm.shape)
    g_kv_norm = gsmall(2, kv_norm.shape)
    g_final = gsmall(3, final_norm.shape)
    g_conv_b = gsmall(4, ffn_conv_b.shape)
    csh = d // N_CHIPS
    g_meta = lax.dynamic_slice_in_dim(gsmall(5, (N_META, d)), chip * csh, csh, axis=1)
    g_scale = lax.dynamic_slice_in_dim(gsmall(6, (1, d)), chip * csh, csh, axis=1)
    fsh = f2 // N_CHIPS
    g_conv_w = lax.dynamic_slice_in_dim(gsmall(7, (2, 3, f2)), chip * fsh, fsh, axis=2)
    loss = gsmall(8, (8 * LANES,))[0]

    weights = [meta_tokens, mix_norm, ffn_norm, pool_w, pool_scale, kv_norm, w_kv, w_q, w_o, ffn_w_up, ffn_conv_w,
               ffn_conv_b, ffn_w_down, final_norm]
    grads = [g_meta, g_mix, g_ffn_norm, g_pool_w, g_scale, g_kv_norm, g_w_kv, g_w_q, g_w_o, g_w_up, g_conv_w,
             g_conv_b, g_w_dn, g_final]
    ms = [m_meta_tokens, m_mix_norm, m_ffn_norm, m_pool_w, m_pool_scale, m_kv_norm, m_w_kv, m_w_q, m_w_o,
          m_ffn_w_up, m_ffn_conv_w, m_ffn_conv_b, m_ffn_w_down, m_final_norm]
    vs = [v_meta_tokens, v_mix_norm, v_ffn_norm, v_pool_w, v_pool_scale, v_kv_norm, v_w_kv, v_w_q, v_w_o,
          v_ffn_w_up, v_ffn_conv_w, v_ffn_conv_b, v_ffn_w_down, v_final_norm]
    names = ["meta", "mix", "ffnnorm", "poolw", "poolscale", "kvnorm", "wkv", "wq", "wo", "wup", "convw", "convb",
             "wdown", "final"]
    deltas, new_ms, new_vs = [], [], []
    for w, g, m, v, nm in zip(weights, grads, ms, vs, names):
        cols = w.shape[-1]
        if w.size % (8 * LANES) == 0 and w.ndim == 1:
            cols = LANES
        view = (w.size // cols, cols)
        dl, mn, vn = _adamw(w.reshape(view), g.reshape(view), m.reshape(view), v.reshape(view), name=f"adamw_{nm}")
        deltas.append(dl.reshape(w.shape))
        new_ms.append(mn.reshape(w.shape))
        new_vs.append(vn.reshape(w.shape))

    return (loss, grad_x, *grads, *deltas, *new_ms, *new_vs)
```
